```python
import math
import jax, jax.numpy as jnp
from jax import lax
import numpy as np

D_MODEL = 1024
BATCH = 2
SEQ = 8192
DEPTH = 1

HEAD_DIM = 64
N_Q_HEADS = 8
N_KV_HEADS = 2
Q_DIM = N_Q_HEADS * HEAD_DIM
KV_DIM = N_KV_HEADS * HEAD_DIM
Q_BLOCK = 128
GM_GROUPS = 8
GM_GROUP_DIM = 64
GM_DIM = GM_GROUPS * GM_GROUP_DIM
GM_CHUNK = 128
GATE_DIM = 2 * D_MODEL
IN_DIM = Q_DIM + 2 * KV_DIM + 2 * GM_DIM + GATE_DIM
D_FF = ((int(math.ceil(8 * D_MODEL / 3)) + 255) // 256) * 256
GRID_W = 64
ROPE_THETA = 10000.0
EPS = 1e-6

kernel_name = "hybrid_gqa_axialrope_gmlp_gated_block"


def rms_norm(x, g):
    xf = x.astype(jnp.float32)
    y = xf * lax.rsqrt(jnp.mean(xf * xf, axis=-1, keepdims=True) + EPS)
    return (y * g.astype(jnp.float32)).astype(x.dtype)


def layer_norm(x, g, b):
    xf = x.astype(jnp.float32)
    mu = jnp.mean(xf, axis=-1, keepdims=True)
    var = jnp.mean(jnp.square(xf - mu), axis=-1, keepdims=True)
    y = (xf - mu) * lax.rsqrt(var + EPS)
    return (y * g.astype(jnp.float32) + b.astype(jnp.float32)).astype(x.dtype)


def rope_half(xh, ang):
    cos = jnp.cos(ang)[None, :, None, :].astype(xh.dtype)
    sin = jnp.sin(ang)[None, :, None, :].astype(xh.dtype)
    x1, x2 = jnp.split(xh, 2, axis=-1)
    return jnp.concatenate([x1 * cos - x2 * sin, x2 * cos + x1 * sin], axis=-1)


def axial_rope(x, row, col):
    half = x.shape[-1] // 2
    inv_freq = 1.0 / (ROPE_THETA ** (jnp.arange(0, half, 2, dtype=jnp.float32) / half))
    ang_r = row.astype(jnp.float32)[:, None] * inv_freq[None, :]
    ang_c = col.astype(jnp.float32)[:, None] * inv_freq[None, :]
    xr, xc = x[..., :half], x[..., half:]
    return jnp.concatenate([rope_half(xr, ang_r), rope_half(xc, ang_c)], axis=-1)


def blocked_gqa(q, k, v):
    B, S, H, HD = q.shape
    G = H // N_KV_HEADS
    nblk = S // Q_BLOCK
    scale = 1.0 / math.sqrt(HD)
    qb = (q * jnp.asarray(scale, q.dtype)).reshape(B, nblk, Q_BLOCK, N_KV_HEADS, G, HD)
    qb = jnp.moveaxis(qb, 1, 0)

    def one_block(qblk):
        s = jnp.einsum('bqkgd,bskd->bkgqs', qblk, k).astype(jnp.float32)
        p = jax.nn.softmax(s, axis=-1).astype(v.dtype)
        return jnp.einsum('bkgqs,bskd->bqkgd', p, v)

    o = lax.map(one_block, qb)
    return jnp.moveaxis(o, 0, 1).reshape(B, S, H * HD)


def gmlp_spatial_gate(uv, ln_g, ln_b, w_s, b_s):
    B, S, _ = uv.shape
    u, v = jnp.split(uv, 2, axis=-1)
    v = v.reshape(B, S, GM_GROUPS, GM_GROUP_DIM)
    v = layer_norm(v, ln_g, ln_b)
    nch = S // GM_CHUNK
    v = v.reshape(B, nch, GM_CHUNK, GM_GROUPS, GM_GROUP_DIM)
    sv = jnp.einsum('gpq,bnqgc->bnpgc', w_s, v) + jnp.transpose(b_s)[None, None, :, :, None]
    return u * sv.reshape(B, S, GM_DIM)


def setup_inputs(seed: int = 0) -> dict:
    key = jax.random.key(seed)
    ks = jax.random.split(key, 20)
    f32 = jnp.float32

    def nrm(k, shape, scale):
        return jax.random.normal(k, shape, f32) * scale

    def gain(k, shape):
        return 1.0 + 0.02 * jax.random.normal(k, shape, f32)

    L = DEPTH
    return {
        "x": jax.random.normal(ks[0], (BATCH, SEQ, D_MODEL), f32),
        "norm_mix_g": gain(ks[1], (L, D_MODEL)),
        "w_in": nrm(ks[2], (L, D_MODEL, IN_DIM), D_MODEL ** -0.5),
        "q_norm_g": gain(ks[3], (L, HEAD_DIM)),
        "k_norm_g": gain(ks[4], (L, HEAD_DIM)),
        "gm_ln_g": gain(ks[5], (L, GM_GROUPS, GM_GROUP_DIM)),
        "gm_ln_b": nrm(ks[6], (L, GM_GROUPS, GM_GROUP_DIM), 0.02),
        "w_s": nrm(ks[7], (L, GM_GROUPS, GM_CHUNK, GM_CHUNK), GM_CHUNK ** -0.5),
        "b_s": gain(ks[8], (L, GM_GROUPS, GM_CHUNK)),
        "w_proj_a": nrm(ks[9], (L, Q_DIM, D_MODEL), Q_DIM ** -0.5),
        "w_proj_b": nrm(ks[10], (L, GM_DIM, D_MODEL), GM_DIM ** -0.5),
        "w_out": nrm(ks[11], (L, D_MODEL, D_MODEL), D_MODEL ** -0.5),
        "norm_ffn_g": gain(ks[12], (L, D_MODEL)),
        "w_gate_up": nrm(ks[13], (L, D_MODEL, 2 * D_FF), D_MODEL ** -0.5),
        "w_down": nrm(ks[14], (L, D_FF, D_MODEL), D_FF ** -0.5),
        "norm_final_g": gain(ks[15], (D_MODEL,)),
    }


def reference(x, norm_mix_g, w_in, q_norm_g, k_norm_g, gm_ln_g, gm_ln_b, w_s, b_s,
              w_proj_a, w_proj_b, w_out, norm_ffn_g, w_gate_up, w_down, norm_final_g):
    B, S, D = x.shape
    rows = S // GRID_W
    row = jnp.repeat(jnp.arange(rows, dtype=jnp.int32), GRID_W)
    col = jnp.tile(jnp.arange(GRID_W, dtype=jnp.int32), rows)
    splits = [Q_DIM, Q_DIM + KV_DIM, Q_DIM + 2 * KV_DIM, Q_DIM + 2 * KV_DIM + 2 * GM_DIM]

    for l in range(DEPTH):
        h = rms_norm(x, norm_mix_g[l])
        proj = jnp.einsum('bsd,de->bse', h, w_in[l])
        q, k, v, uv, gates = jnp.split(proj, splits, axis=-1)

        q = rms_norm(q.reshape(B, S, N_Q_HEADS, HEAD_DIM), q_norm_g[l])
        k = rms_norm(k.reshape(B, S, N_KV_HEADS, HEAD_DIM), k_norm_g[l])
        v = v.reshape(B, S, N_KV_HEADS, HEAD_DIM)
        q = axial_rope(q, row, col)
        k = axial_rope(k, row, col)
        attn = blocked_gqa(q, k, v)

        gm = gmlp_spatial_gate(jax.nn.gelu(uv), gm_ln_g[l], gm_ln_b[l], w_s[l], b_s[l])

        g_a, g_b = jnp.split(jax.nn.sigmoid(gates), 2, axis=-1)
        mix = (g_a * jnp.einsum('bse,ed->bsd', attn, w_proj_a[l])
               + g_b * jnp.einsum('bse,ed->bsd', gm, w_proj_b[l]))
        x = x + jnp.einsum('bsd,de->bse', mix, w_out[l])

        h = rms_norm(x, norm_ffn_g[l])
        gu = jnp.einsum('bsd,df->bsf', h, w_gate_up[l])
        gt, up = jnp.split(gu, 2, axis=-1)
        x = x + jnp.einsum('bsf,fd->bsd', jax.nn.silu(gt) * up, w_down[l])

    return rms_norm(x, norm_final_g)
```

```python
import functools
import math

import jax
import jax.numpy as jnp
from jax import lax
from jax.experimental import pallas as pl
from jax.experimental.pallas import tpu as pltpu

HEAD_DIM = 64
N_Q_HEADS = 8
N_KV_HEADS = 2
Q_DIM = N_Q_HEADS * HEAD_DIM
KV_DIM = N_KV_HEADS * HEAD_DIM
GM_GROUPS = 8
GM_GROUP_DIM = 64
GM_DIM = GM_GROUPS * GM_GROUP_DIM
GM_CHUNK = 128
GRID_W = 64
ROPE_THETA = 10000.0
EPS = 1e-6

LANES = 128
VMEM_LIMIT_BYTES = 56 * 1024 * 1024

TM_IN = 256
TQ = 128
TK = 512
TM_OUT = 256

F32 = jnp.float32
BF16 = jnp.bfloat16


def _dot(a, b):
    return jnp.dot(a, b, preferred_element_type=F32)


def _group_sum(x, gmat):
    hi = x.astype(BF16)
    lo = (x - hi.astype(F32)).astype(BF16)
    return _dot(hi, gmat) + _dot(lo, gmat)


def _rope_partner(x, lane):
    fwd = pltpu.roll(x, LANES - 16, axis=1)
    bwd = pltpu.roll(x, 16, axis=1)
    return jnp.where((lane & 16) == 0, fwd, bwd)


def _inproj_kernel(x_ref, g_ref, w_ref, gmat_ref, cos_ref, sin_ref, qg_ref, kg_ref,
                   lng_ref, lnb_ref,
                   q_ref, k_ref, v_ref, u_ref, vn_ref, gate_ref):
    x = x_ref[...]
    ms = jnp.mean(x * x, axis=-1, keepdims=True)
    h = (x * lax.rsqrt(ms + EPS) * g_ref[...]).astype(BF16)
    gmat = gmat_ref[...]
    cos = cos_ref[...]
    sin = sin_ref[...]
    lane = lax.broadcasted_iota(jnp.int32, (x.shape[0], LANES), 1)
    inv_hd = 1.0 / HEAD_DIM

    def head_norm_rope(col0, gain):
        y = _dot(h, w_ref[:, col0:col0 + LANES])
        ss = _group_sum(y * y, gmat)
        y = y * lax.rsqrt(ss * inv_hd + EPS) * gain
        return y * cos + _rope_partner(y, lane) * sin

    scale = 1.0 / math.sqrt(HEAD_DIM)
    for j in range(Q_DIM // LANES):
        q = head_norm_rope(j * LANES, qg_ref[...]) * scale
        q_ref[:, j * LANES:(j + 1) * LANES] = q.astype(BF16)
    k_ref[...] = head_norm_rope(Q_DIM, kg_ref[...]).astype(BF16)
    v_ref[...] = _dot(h, w_ref[:, Q_DIM + KV_DIM:Q_DIM + 2 * KV_DIM]).astype(BF16)

    uv0 = Q_DIM + 2 * KV_DIM
    u_ref[...] = jax.nn.gelu(_dot(h, w_ref[:, uv0:uv0 + GM_DIM]))
    for j in range(GM_DIM // LANES):
        c0 = uv0 + GM_DIM + j * LANES
        vv = jax.nn.gelu(_dot(h, w_ref[:, c0:c0 + LANES]))
        mu = _group_sum(vv, gmat) * (1.0 / GM_GROUP_DIM)
        d = vv - mu
        var = _group_sum(d * d, gmat) * (1.0 / GM_GROUP_DIM)
        sl = slice(j * LANES, (j + 1) * LANES)
        vn = d * lax.rsqrt(var + EPS) * lng_ref[:, sl] + lnb_ref[:, sl]
        vn_ref[:, sl] = vn.astype(BF16)

    g0 = uv0 + 2 * GM_DIM
    gate_ref[...] = jax.nn.sigmoid(_dot(h, w_ref[:, g0:]))


def _attn_kernel(q_ref, k_ref, v_ref, o_ref, qs_ref):
    tq = q_ref.shape[0]
    n_slab = Q_DIM // LANES
    lane = lax.broadcasted_iota(jnp.int32, (tq, LANES), 1)
    low = lane < HEAD_DIM
    for j in range(n_slab):
        slab = q_ref[:, j * LANES:(j + 1) * LANES]
        zero = jnp.zeros_like(slab)
        qs_ref[j * tq:(j + 1) * tq, :] = jnp.where(low, slab, zero)
        qs_ref[(n_slab + j) * tq:(n_slab + j + 1) * tq, :] = jnp.where(low, zero, slab)
    qs = qs_ref[...]
    rows = qs.shape[0]
    n_chunks = k_ref.shape[0] // TK

    def body(c, carry):
        m, l, acc = carry
        start = pl.multiple_of(c * TK, TK)
        k = k_ref[pl.ds(start, TK), :]
        v = v_ref[pl.ds(start, TK), :]
        s = lax.dot_general(qs, k, (((1,), (1,)), ((), ())), preferred_element_type=F32)
        m_new = jnp.maximum(m, jnp.max(s, axis=1, keepdims=True))
        alpha = jnp.exp(m - m_new)
        p = jnp.exp(s - m_new)
        l = alpha * l + jnp.sum(p, axis=1, keepdims=True)
        acc = alpha * acc + _dot(p.astype(BF16), v)
        return m_new, l, acc

    m0 = jnp.full((rows, 1), -jnp.inf, F32)
    l0 = jnp.zeros((rows, 1), F32)
    acc0 = jnp.zeros((rows, LANES), F32)
    _, l, acc = lax.fori_loop(0, n_chunks, body, (m0, l0, acc0))
    o = acc / l
    for j in range(n_slab):
        o_lo = o[j * tq:(j + 1) * tq, :]
        o_hi = o[(n_slab + j) * tq:(n_slab + j + 1) * tq, :]
        o_ref[:, j * LANES:(j + 1) * LANES] = jnp.where(low, o_lo, o_hi).astype(BF16)


def _merge_ffn_kernel(x_ref, a_ref, u_ref, vn_ref, gate_ref, ws_ref, bs_ref,
                      wpa_ref, wpb_ref, wo_ref, gf_ref, wgu_ref, wd_ref, gfin_ref,
                      o_ref, gm_ref):
    tm = x_ref.shape[0]
    d_model = x_ref.shape[1]
    d_ff = wd_ref.shape[0]
    lane = lax.broadcasted_iota(jnp.int32, (GM_CHUNK, LANES), 1)
    low = lane < GM_GROUP_DIM
    for c in range(tm // GM_CHUNK):
        rs = slice(c * GM_CHUNK, (c + 1) * GM_CHUNK)
        for j in range(GM_DIM // LANES):
            cs = slice(j * LANES, (j + 1) * LANES)
            vblk = vn_ref[rs, cs]
            sv = jnp.where(low, _dot(ws_ref[2 * j], vblk), _dot(ws_ref[2 * j + 1], vblk))
            gm_ref[rs, cs] = (u_ref[rs, cs] * (sv + bs_ref[:, cs])).astype(BF16)

    pa = _dot(a_ref[...], wpa_ref[...])
    pb = _dot(gm_ref[...], wpb_ref[...])
    mix = gate_ref[:, :d_model] * pa + gate_ref[:, d_model:] * pb
    x1 = x_ref[...] + _dot(mix.astype(BF16), wo_ref[...])

    ms = jnp.mean(x1 * x1, axis=-1, keepdims=True)
    h = (x1 * lax.rsqrt(ms + EPS) * gf_ref[...]).astype(BF16)
    gt = _dot(h, wgu_ref[:, :d_ff])
    up = _dot(h, wgu_ref[:, d_ff:])
    act = (jax.nn.silu(gt) * up).astype(BF16)
    x2 = x1 + _dot(act, wd_ref[...])

    ms2 = jnp.mean(x2 * x2, axis=-1, keepdims=True)
    o_ref[...] = x2 * lax.rsqrt(ms2 + EPS) * gfin_ref[...]


def _const_spec(shape):
    nd = len(shape)
    return pl.BlockSpec(shape, lambda *_: (0,) * nd, pipeline_mode=pl.Buffered(1))


def _rope_tables(seq):
    half = HEAD_DIM // 2
    rows = seq // GRID_W
    row = jnp.repeat(jnp.arange(rows, dtype=jnp.int32), GRID_W)
    col = jnp.tile(jnp.arange(GRID_W, dtype=jnp.int32), rows)
    inv_freq = 1.0 / (ROPE_THETA ** (jnp.arange(0, half, 2, dtype=F32) / half))
    ang_r = row.astype(F32)[:, None] * inv_freq[None, :]
    ang_c = col.astype(F32)[:, None] * inv_freq[None, :]
    cr, sr = jnp.cos(ang_r), jnp.sin(ang_r)
    cc, sc = jnp.cos(ang_c), jnp.sin(ang_c)
    cos64 = jnp.concatenate([cr, cr, cc, cc], axis=-1)
    sin64 = jnp.concatenate([-sr, sr, -sc, sc], axis=-1)
    return jnp.tile(cos64, (1, LANES // HEAD_DIM)), jnp.tile(sin64, (1, LANES // HEAD_DIM))


def kernel(x, norm_mix_g, w_in, q_norm_g, k_norm_g, gm_ln_g, gm_ln_b, w_s, b_s, w_proj_a,
           w_proj_b, w_out, norm_ffn_g, w_gate_up, w_down, norm_final_g):
    B, S, D = x.shape
    M = B * S
    in_dim = w_in.shape[-1]
    d_ff = w_down.shape[1]
    assert norm_mix_g.shape[0] == 1, "single-layer block"
    assert S % TK == 0 and S % TQ == 0 and S % TM_IN == 0 and S % TM_OUT == 0
    assert TM_OUT % GM_CHUNK == 0

    half_heads = N_Q_HEADS // 2
    perm = jnp.concatenate([
        jnp.arange(HEAD_DIM) + HEAD_DIM * h
        for j in range(half_heads) for h in (j, j + half_heads)])
    w_in_l = w_in[0]
    w_in_b = jnp.concatenate([w_in_l[:, :Q_DIM][:, perm], w_in_l[:, Q_DIM:]], axis=1).astype(BF16)
    wpa_b = w_proj_a[0][perm, :].astype(BF16)

    cos_t, sin_t = _rope_tables(S)
    reps = LANES // HEAD_DIM
    qg = jnp.tile(q_norm_g[0], reps)[None, :]
    kg = jnp.tile(k_norm_g[0], reps)[None, :]
    lng = gm_ln_g[0].reshape(1, GM_DIM)
    lnb = gm_ln_b[0].reshape(1, GM_DIM)
    gid = jnp.arange(LANES) // HEAD_DIM
    gmat = (gid[:, None] == gid[None, :]).astype(BF16)

    x2d = x.reshape(M, D)
    seq_blocks = S // TM_IN
    cparams = pltpu.CompilerParams(dimension_semantics=("arbitrary",),
                                   vmem_limit_bytes=VMEM_LIMIT_BYTES)

    def row_spec(tm, width):
        return pl.BlockSpec((tm, width), lambda i: (i, 0))

    q, k, v, u, vn, gates = pl.pallas_call(
        _inproj_kernel,
        grid=(M // TM_IN,),
        in_specs=[
            row_spec(TM_IN, D),
            _const_spec((1, D)),
            _const_spec((D, in_dim)),
            _const_spec((LANES, LANES)),
            pl.BlockSpec((TM_IN, LANES), lambda i: (i % seq_blocks, 0)),
            pl.BlockSpec((TM_IN, LANES), lambda i: (i % seq_blocks, 0)),
            _const_spec((1, LANES)),
            _const_spec((1, LANES)),
            _const_spec((1, GM_DIM)),
            _const_spec((1, GM_DIM)),
        ],
        out_specs=[
            row_spec(TM_IN, Q_DIM), row_spec(TM_IN, KV_DIM), row_spec(TM_IN, KV_DIM),
            row_spec(TM_IN, GM_DIM), row_spec(TM_IN, GM_DIM), row_spec(TM_IN, 2 * D),
        ],
        out_shape=[
            jax.ShapeDtypeStruct((M, Q_DIM), BF16),
            jax.ShapeDtypeStruct((M, KV_DIM), BF16),
            jax.ShapeDtypeStruct((M, KV_DIM), BF16),
            jax.ShapeDtypeStruct((M, GM_DIM), F32),
            jax.ShapeDtypeStruct((M, GM_DIM), BF16),
            jax.ShapeDtypeStruct((M, 2 * D), F32),
        ],
        compiler_params=cparams,
        name="inproj",
    )(x2d, norm_mix_g, w_in_b, gmat, cos_t, sin_t, qg, kg, lng, lnb)

    q_blocks = S // TQ
    attn = pl.pallas_call(
        _attn_kernel,
        grid=(B, q_blocks),
        in_specs=[
            pl.BlockSpec((TQ, Q_DIM), lambda b, i: (b * q_blocks + i, 0)),
            pl.BlockSpec((S, KV_DIM), lambda b, i: (b, 0)),
            pl.BlockSpec((S, KV_DIM), lambda b, i: (b, 0)),
        ],
        out_specs=pl.BlockSpec((TQ, Q_DIM), lambda b, i: (b * q_blocks + i, 0)),
        out_shape=jax.ShapeDtypeStruct((M, Q_DIM), BF16),
        scratch_shapes=[pltpu.VMEM((N_Q_HEADS * TQ, LANES), BF16)],
        compiler_params=pltpu.CompilerParams(
            dimension_semantics=("arbitrary", "arbitrary"),
            vmem_limit_bytes=VMEM_LIMIT_BYTES),
        name="gqa_attention",
    )(q, k, v)

    bs_full = jnp.repeat(jnp.transpose(b_s[0]), GM_GROUP_DIM, axis=1)
    out = pl.pallas_call(
        _merge_ffn_kernel,
        grid=(M // TM_OUT,),
        in_specs=[
            row_spec(TM_OUT, D), row_spec(TM_OUT, Q_DIM), row_spec(TM_OUT, GM_DIM),
            row_spec(TM_OUT, GM_DIM), row_spec(TM_OUT, 2 * D),
            _const_spec((GM_GROUPS, GM_CHUNK, GM_CHUNK)),
            _const_spec((GM_CHUNK, GM_DIM)),
            _const_spec((Q_DIM, D)),
            _const_spec((GM_DIM, D)),
            _const_spec((D, D)),
            _const_spec((1, D)),
            _const_spec((D, 2 * d_ff)),
            _const_spec((d_ff, D)),
            _const_spec((1, D)),
        ],
        out_specs=row_spec(TM_OUT, D),
        out_shape=jax.ShapeDtypeStruct((M, D), F32),
        scratch_shapes=[pltpu.VMEM((TM_OUT, GM_DIM), BF16)],
        compiler_params=cparams,
        name="merge_ffn",
    )(x2d, attn, u, vn, gates, w_s[0].astype(BF16), bs_full,
      wpa_b, w_proj_b[0].astype(BF16), w_out[0].astype(BF16), norm_ffn_g,
      w_gate_up[0].astype(BF16), w_down[0].astype(BF16), norm_final_g[None, :])
    return out.reshape(B, S, D)
```

```python
import math

import jax
import jax.numpy as jnp
from jax import lax
from jax.experimental import pallas as pl
from jax.experimental.pallas import tpu as pltpu

HEAD_DIM = 64
N_Q_HEADS = 8
N_KV_HEADS = 2
Q_DIM = N_Q_HEADS * HEAD_DIM
KV_DIM = N_KV_HEADS * HEAD_DIM
GM_GROUPS = 8
GM_GROUP_DIM = 64
GM_DIM = GM_GROUPS * GM_GROUP_DIM
GM_CHUNK = 128
GRID_W = 64
ROPE_THETA = 10000.0
EPS = 1e-6

LANES = 128
VMEM_LIMIT_BYTES = 56 * 1024 * 1024

TM_IN = 256
TQ = 256
TK = 256
L_ROWS = 16
TM_OUT = 256

F32 = jnp.float32
BF16 = jnp.bfloat16


def _dot(a, b):
    return jnp.dot(a, b, preferred_element_type=F32)


def _group_sum(x, gmat):
    hi = x.astype(BF16)
    lo = (x - hi.astype(F32)).astype(BF16)
    return _dot(hi, gmat) + _dot(lo, gmat)


def _rope_partner(x, lane):
    fwd = pltpu.roll(x, LANES - 16, axis=1)
    bwd = pltpu.roll(x, 16, axis=1)
    return jnp.where((lane & 16) == 0, fwd, bwd)


def _inproj_kernel(x_ref, g_ref, w_ref, gmat_ref, cos_ref, sin_ref, qg_ref, kg_ref,
                   lng_ref, lnb_ref,
                   qt_ref, k_ref, vt_ref, u_ref, vn_ref, gate_ref):
    x = x_ref[...]
    tm = x.shape[0]
    ms = jnp.mean(x * x, axis=-1, keepdims=True)
    h = (x * lax.rsqrt(ms + EPS) * g_ref[...]).astype(BF16)
    gmat = gmat_ref[...]
    cos = cos_ref[...]
    sin = sin_ref[...]
    lane = lax.broadcasted_iota(jnp.int32, (tm, LANES), 1)
    inv_hd = 1.0 / HEAD_DIM

    def head_norm_rope(col0, gain):
        y = _dot(h, w_ref[:, col0:col0 + LANES])
        ss = _group_sum(y * y, gmat)
        y = y * lax.rsqrt(ss * inv_hd + EPS) * gain
        return y * cos + _rope_partner(y, lane) * sin

    scale = math.log2(math.e) / math.sqrt(HEAD_DIM)
    row = lax.broadcasted_iota(jnp.int32, (LANES, tm), 0)
    top = row < HEAD_DIM
    half_heads = N_Q_HEADS // 2
    for j in range(Q_DIM // LANES):
        qt = (head_norm_rope(j * LANES, qg_ref[...]) * scale).T
        zero = jnp.zeros_like(qt)
        qt_ref[j] = jnp.where(top, qt, zero).astype(BF16)
        qt_ref[j + half_heads] = jnp.where(top, zero, qt).astype(BF16)
    k_ref[...] = head_norm_rope(Q_DIM, kg_ref[...]).astype(BF16)
    vt = _dot(h, w_ref[:, Q_DIM + KV_DIM:Q_DIM + 2 * KV_DIM]).T.astype(BF16)
    for c in range(tm // TK):
        vt_ref[c] = vt[:, c * TK:(c + 1) * TK]

    uv0 = Q_DIM + 2 * KV_DIM
    u_ref[...] = jax.nn.gelu(_dot(h, w_ref[:, uv0:uv0 + GM_DIM]))
    for j in range(GM_DIM // LANES):
        c0 = uv0 + GM_DIM + j * LANES
        vv = jax.nn.gelu(_dot(h, w_ref[:, c0:c0 + LANES]))
        mu = _group_sum(vv, gmat) * (1.0 / GM_GROUP_DIM)
        d = vv - mu
        var = _group_sum(d * d, gmat) * (1.0 / GM_GROUP_DIM)
        sl = slice(j * LANES, (j + 1) * LANES)
        vn = d * lax.rsqrt(var + EPS) * lng_ref[:, sl] + lnb_ref[:, sl]
        vn_ref[:, sl] = vn.astype(BF16)

    g0 = uv0 + 2 * GM_DIM
    gate_ref[...] = jax.nn.sigmoid(_dot(h, w_ref[:, g0:]))


def _attn_kernel(qt_ref, k_ref, vt_ref, o_ref, sa_ref, sb_ref, m_ref, acc_ref):
    n_heads = qt_ref.shape[0]
    group = n_heads // N_KV_HEADS
    n_chunks = k_ref.shape[0] // TK
    m_ref[...] = jnp.full(m_ref.shape, -jnp.inf, F32)
    acc_ref[...] = jnp.zeros(acc_ref.shape, F32)
    ones = jnp.ones((L_ROWS, TK), BF16)

    def keys(c):
        return k_ref[pl.ds(pl.multiple_of(c * TK, TK), TK), :]

    def step(c, cur_ref, next_ref):
        if next_ref is not None:
            k_next = keys(c + 1)
        vt_c = vt_ref[c]
        for h in range(n_heads):
            g = h // group
            if next_ref is not None:
                next_ref[h] = _dot(k_next, qt_ref[h])
            s = cur_ref[h]
            m_old = m_ref[h]
            m_new = jnp.maximum(m_old, jnp.max(s, axis=0, keepdims=True))
            alpha = jnp.exp2(m_old - m_new)
            p = jnp.exp2(s - m_new).astype(BF16)
            m_ref[h] = m_new
            lhs = jnp.concatenate([vt_c[g * HEAD_DIM:(g + 1) * HEAD_DIM, :], ones], axis=0)
            acc_ref[h] = alpha * acc_ref[h] + _dot(lhs, p)

    k_first = keys(0)
    for h in range(n_heads):
        sa_ref[h] = _dot(k_first, qt_ref[h])

    def body(i, carry):
        step(2 * i, sa_ref, sb_ref)
        step(2 * i + 1, sb_ref, sa_ref)
        return carry

    lax.fori_loop(0, n_chunks // 2 - 1, body, 0)
    step(n_chunks - 2, sa_ref, sb_ref)
    step(n_chunks - 1, sb_ref, None)
    for h in range(n_heads):
        acc = acc_ref[h]
        o = acc[:HEAD_DIM, :] / acc[HEAD_DIM:HEAD_DIM + 1, :]
        o_ref[h * HEAD_DIM:(h + 1) * HEAD_DIM, :] = o.astype(BF16)


def _merge_ffn_kernel(x_ref, at_ref, u_ref, vn_ref, gate_ref, ws_ref, bs_ref,
                      wpa_ref, wpb_ref, wo_ref, gf_ref, wgu_ref, wd_ref, gfin_ref,
                      o_ref, gm_ref):
    tm = x_ref.shape[0]
    d_model = x_ref.shape[1]
    d_ff = wd_ref.shape[0]
    lane = lax.broadcasted_iota(jnp.int32, (GM_CHUNK, LANES), 1)
    low = lane < GM_GROUP_DIM
    for c in range(tm // GM_CHUNK):
        rs = slice(c * GM_CHUNK, (c + 1) * GM_CHUNK)
        for j in range(GM_DIM // LANES):
            cs = slice(j * LANES, (j + 1) * LANES)
            vblk = vn_ref[rs, cs]
            sv = jnp.where(low, _dot(ws_ref[2 * j], vblk), _dot(ws_ref[2 * j + 1], vblk))
            gm_ref[rs, cs] = (u_ref[rs, cs] * (sv + bs_ref[:, cs])).astype(BF16)

    pa = lax.dot_general(at_ref[...], wpa_ref[...], (((0,), (0,)), ((), ())),
                         preferred_element_type=F32)
    pb = _dot(gm_ref[...], wpb_ref[...])
    mix = gate_ref[:, :d_model] * pa + gate_ref[:, d_model:] * pb
    x1 = x_ref[...] + _dot(mix.astype(BF16), wo_ref[...])

    ms = jnp.mean(x1 * x1, axis=-1, keepdims=True)
    h = (x1 * lax.rsqrt(ms + EPS) * gf_ref[...]).astype(BF16)
    gt = _dot(h, wgu_ref[:, :d_ff])
    up = _dot(h, wgu_ref[:, d_ff:])
    act = (jax.nn.silu(gt) * up).astype(BF16)
    x2 = x1 + _dot(act, wd_ref[...])

    ms2 = jnp.mean(x2 * x2, axis=-1, keepdims=True)
    o_ref[...] = x2 * lax.rsqrt(ms2 + EPS) * gfin_ref[...]


def _const_spec(shape):
    nd = len(shape)
    return pl.BlockSpec(shape, lambda *_: (0,) * nd, pipeline_mode=pl.Buffered(1))


def _rope_tables(seq):
    half = HEAD_DIM // 2
    rows = seq // GRID_W
    row = jnp.repeat(jnp.arange(rows, dtype=jnp.int32), GRID_W)
    col = jnp.tile(jnp.arange(GRID_W, dtype=jnp.int32), rows)
    inv_freq = 1.0 / (ROPE_THETA ** (jnp.arange(0, half, 2, dtype=F32) / half))
    ang_r = row.astype(F32)[:, None] * inv_freq[None, :]
    ang_c = col.astype(F32)[:, None] * inv_freq[None, :]
    cr, sr = jnp.cos(ang_r), jnp.sin(ang_r)
    cc, sc = jnp.cos(ang_c), jnp.sin(ang_c)
    cos64 = jnp.concatenate([cr, cr, cc, cc], axis=-1)
    sin64 = jnp.concatenate([-sr, sr, -sc, sc], axis=-1)
    return jnp.tile(cos64, (1, LANES // HEAD_DIM)), jnp.tile(sin64, (1, LANES // HEAD_DIM))


def kernel(x, norm_mix_g, w_in, q_norm_g, k_norm_g, gm_ln_g, gm_ln_b, w_s, b_s, w_proj_a,
           w_proj_b, w_out, norm_ffn_g, w_gate_up, w_down, norm_final_g):
    B, S, D = x.shape
    in_dim = w_in.shape[-1]
    d_ff = w_down.shape[1]
    assert norm_mix_g.shape[0] == 1, "single-layer block"
    assert S % (2 * TK) == 0 and S % TQ == 0 and S % TM_IN == 0 and S % TM_OUT == 0
    assert TM_OUT % GM_CHUNK == 0 and TM_IN % TK == 0

    half_heads = N_Q_HEADS // 2
    perm = jnp.concatenate([
        jnp.arange(HEAD_DIM) + HEAD_DIM * h
        for j in range(half_heads) for h in (j, j + half_heads)])
    w_in_l = w_in[0]
    w_in_b = jnp.concatenate([w_in_l[:, :Q_DIM][:, perm], w_in_l[:, Q_DIM:]], axis=1).astype(BF16)

    cos_t, sin_t = _rope_tables(S)
    reps = LANES // HEAD_DIM
    qg = jnp.tile(q_norm_g[0], reps)[None, :]
    kg = jnp.tile(k_norm_g[0], reps)[None, :]
    lng = gm_ln_g[0].reshape(1, GM_DIM)
    lnb = gm_ln_b[0].reshape(1, GM_DIM)
    gid = jnp.arange(LANES) // HEAD_DIM
    gmat = (gid[:, None] == gid[None, :]).astype(BF16)

    two_axes = ("arbitrary", "arbitrary")
    cparams = pltpu.CompilerParams(dimension_semantics=two_axes,
                                   vmem_limit_bytes=VMEM_LIMIT_BYTES)

    def row_spec(tm, width):
        return pl.BlockSpec((None, tm, width), lambda b, i: (b, i, 0))

    n_in = S // TM_IN
    ck_in = TM_IN // TK
    qt, k, vt, u, vn, gates = pl.pallas_call(
        _inproj_kernel,
        grid=(B, n_in),
        in_specs=[
            row_spec(TM_IN, D),
            _const_spec((1, D)),
            _const_spec((D, in_dim)),
            _const_spec((LANES, LANES)),
            pl.BlockSpec((TM_IN, LANES), lambda b, i: (i, 0)),
            pl.BlockSpec((TM_IN, LANES), lambda b, i: (i, 0)),
            _const_spec((1, LANES)),
            _const_spec((1, LANES)),
            _const_spec((1, GM_DIM)),
            _const_spec((1, GM_DIM)),
        ],
        out_specs=[
            pl.BlockSpec((None, N_Q_HEADS, LANES, TM_IN), lambda b, i: (b, 0, 0, i)),
            row_spec(TM_IN, KV_DIM),
            pl.BlockSpec((None, ck_in, KV_DIM, TK), lambda b, i: (b, i, 0, 0)),
            row_spec(TM_IN, GM_DIM), row_spec(TM_IN, GM_DIM), row_spec(TM_IN, 2 * D),
        ],
        out_shape=[
            jax.ShapeDtypeStruct((B, N_Q_HEADS, LANES, S), BF16),
            jax.ShapeDtypeStruct((B, S, KV_DIM), BF16),
            jax.ShapeDtypeStruct((B, S // TK, KV_DIM, TK), BF16),
            jax.ShapeDtypeStruct((B, S, GM_DIM), F32),
            jax.ShapeDtypeStruct((B, S, GM_DIM), BF16),
            jax.ShapeDtypeStruct((B, S, 2 * D), F32),
        ],
        compiler_params=cparams,
        name="inproj",
    )(x, norm_mix_g, w_in_b, gmat, cos_t, sin_t, qg, kg, lng, lnb)

    attn_t = pl.pallas_call(
        _attn_kernel,
        grid=(B, S // TQ),
        in_specs=[
            pl.BlockSpec((None, N_Q_HEADS, LANES, TQ), lambda b, i: (b, 0, 0, i)),
            pl.BlockSpec((None, S, KV_DIM), lambda b, i: (b, 0, 0)),
            pl.BlockSpec((None, S // TK, KV_DIM, TK), lambda b, i: (b, 0, 0, 0)),
        ],
        out_specs=pl.BlockSpec((None, Q_DIM, TQ), lambda b, i: (b, 0, i)),
        out_shape=jax.ShapeDtypeStruct((B, Q_DIM, S), BF16),
        scratch_shapes=[
            pltpu.VMEM((N_Q_HEADS, TK, TQ), F32),
            pltpu.VMEM((N_Q_HEADS, TK, TQ), F32),
            pltpu.VMEM((N_Q_HEADS, 1, TQ), F32),
            pltpu.VMEM((N_Q_HEADS, HEAD_DIM + L_ROWS, TQ), F32),
        ],
        compiler_params=cparams,
        name="gqa_attention",
    )(qt, k, vt)

    bs_full = jnp.repeat(jnp.transpose(b_s[0]), GM_GROUP_DIM, axis=1)
    out = pl.pallas_call(
        _merge_ffn_kernel,
        grid=(B, S // TM_OUT),
        in_specs=[
            row_spec(TM_OUT, D),
            pl.BlockSpec((None, Q_DIM, TM_OUT), lambda b, i: (b, 0, i)),
            row_spec(TM_OUT, GM_DIM), row_spec(TM_OUT, GM_DIM), row_spec(TM_OUT, 2 * D),
            _const_spec((GM_GROUPS, GM_CHUNK, GM_CHUNK)),
            _const_spec((GM_CHUNK, GM_DIM)),
            _const_spec((Q_DIM, D)),
            _const_spec((GM_DIM, D)),
            _const_spec((D, D)),
            _const_spec((1, D)),
            _const_spec((D, 2 * d_ff)),
            _const_spec((d_ff, D)),
            _const_spec((1, D)),
        ],
        out_specs=row_spec(TM_OUT, D),
        out_shape=jax.ShapeDtypeStruct((B, S, D), F32),
        scratch_shapes=[pltpu.VMEM((TM_OUT, GM_DIM), BF16)],
        compiler_params=cparams,
        name="merge_ffn",
    )(x, attn_t, u, vn, gates, w_s[0].astype(BF16), bs_full,
      w_proj_a[0].astype(BF16), w_proj_b[0].astype(BF16), w_out[0].astype(BF16), norm_ffn_g,
      w_gate_up[0].astype(BF16), w_down[0].astype(BF16), norm_final_g[None, :])
    return out
```

```python
import functools
import math

import jax
import jax.numpy as jnp
from jax import lax
from jax.experimental import pallas as pl
from jax.experimental.pallas import tpu as pltpu

HEAD_DIM = 64
N_Q_HEADS = 8
N_KV_HEADS = 2
Q_DIM = N_Q_HEADS * HEAD_DIM
KV_DIM = N_KV_HEADS * HEAD_DIM
GM_GROUPS = 8
GM_GROUP_DIM = 64
GM_DIM = GM_GROUPS * GM_GROUP_DIM
GM_CHUNK = 128
GRID_W = 64
ROPE_THETA = 10000.0
EPS = 1e-6

LANES = 128
VMEM_LIMIT_BYTES = 56 * 1024 * 1024

TM_IN = 256
TQ = 256
TK = 256
L_ROWS = 16
QK_SCALE_LOG2 = math.log2(math.e) / math.sqrt(HEAD_DIM)
MAX_FIXED_SHIFT = 48.0
TM_OUT = 256

F32 = jnp.float32
BF16 = jnp.bfloat16


def _dot(a, b):
    return jnp.dot(a, b, preferred_element_type=F32)


def _group_sum(x, gmat):
    hi = x.astype(BF16)
    lo = (x - hi.astype(F32)).astype(BF16)
    return _dot(hi, gmat) + _dot(lo, gmat)


def _rope_partner(x, lane):
    fwd = pltpu.roll(x, LANES - 16, axis=1)
    bwd = pltpu.roll(x, 16, axis=1)
    return jnp.where((lane & 16) == 0, fwd, bwd)


def _inproj_kernel(x_ref, g_ref, w_ref, gmat_ref, cos_ref, sin_ref, qg_ref, kg_ref,
                   lng_ref, lnb_ref,
                   qt_ref, k_ref, vt_ref, u_ref, vn_ref, gate_ref):
    x = x_ref[...]
    tm = x.shape[0]
    ms = jnp.mean(x * x, axis=-1, keepdims=True)
    h = (x * lax.rsqrt(ms + EPS) * g_ref[...]).astype(BF16)
    gmat = gmat_ref[...]
    cos = cos_ref[...]
    sin = sin_ref[...]
    lane = lax.broadcasted_iota(jnp.int32, (tm, LANES), 1)
    inv_hd = 1.0 / HEAD_DIM

    def head_norm_rope(col0, gain):
        y = _dot(h, w_ref[:, col0:col0 + LANES])
        ss = _group_sum(y * y, gmat)
        y = y * lax.rsqrt(ss * inv_hd + EPS) * gain
        return y * cos + _rope_partner(y, lane) * sin

    scale = QK_SCALE_LOG2
    row = lax.broadcasted_iota(jnp.int32, (LANES, tm), 0)
    top = row < HEAD_DIM
    half_heads = N_Q_HEADS // 2
    for j in range(Q_DIM // LANES):
        qt = (head_norm_rope(j * LANES, qg_ref[...]) * scale).T
        zero = jnp.zeros_like(qt)
        qt_ref[j] = jnp.where(top, qt, zero).astype(BF16)
        qt_ref[j + half_heads] = jnp.where(top, zero, qt).astype(BF16)
    k_ref[...] = head_norm_rope(Q_DIM, kg_ref[...]).astype(BF16)
    vt = _dot(h, w_ref[:, Q_DIM + KV_DIM:Q_DIM + 2 * KV_DIM]).T.astype(BF16)
    for c in range(tm // TK):
        vt_ref[c] = vt[:, c * TK:(c + 1) * TK]

    uv0 = Q_DIM + 2 * KV_DIM
    u_ref[...] = jax.nn.gelu(_dot(h, w_ref[:, uv0:uv0 + GM_DIM]))
    for j in range(GM_DIM // LANES):
        c0 = uv0 + GM_DIM + j * LANES
        vv = jax.nn.gelu(_dot(h, w_ref[:, c0:c0 + LANES]))
        mu = _group_sum(vv, gmat) * (1.0 / GM_GROUP_DIM)
        d = vv - mu
        var = _group_sum(d * d, gmat) * (1.0 / GM_GROUP_DIM)
        sl = slice(j * LANES, (j + 1) * LANES)
        vn = d * lax.rsqrt(var + EPS) * lng_ref[:, sl] + lnb_ref[:, sl]
        vn_ref[:, sl] = vn.astype(BF16)

    g0 = uv0 + 2 * GM_DIM
    gate_ref[...] = jax.nn.sigmoid(_dot(h, w_ref[:, g0:]))


def _attn_kernel(shift_ref, qt_ref, k_ref, vt_ref, o_ref, sa_ref, sb_ref, m_ref, acc_ref, *,
                 fixed_shift):
    n_heads = qt_ref.shape[0]
    group = n_heads // N_KV_HEADS
    n_chunks = k_ref.shape[0] // TK
    if fixed_shift:
        shift = shift_ref[0]
    else:
        m_ref[...] = jnp.full(m_ref.shape, -jnp.inf, F32)
    acc_ref[...] = jnp.zeros(acc_ref.shape, F32)
    ones = jnp.ones((L_ROWS, TK), BF16)

    def keys(c):
        return k_ref[pl.ds(pl.multiple_of(c * TK, TK), TK), :]

    def step(c, cur_ref, next_ref):
        if next_ref is not None:
            k_next = keys(c + 1)
        vt_c = vt_ref[c]
        for h in range(n_heads):
            g = h // group
            if next_ref is not None:
                next_ref[h] = _dot(k_next, qt_ref[h])
            s = cur_ref[h]
            lhs = jnp.concatenate([vt_c[g * HEAD_DIM:(g + 1) * HEAD_DIM, :], ones], axis=0)
            if fixed_shift:
                p = jnp.exp2(s - shift).astype(BF16)
                acc_ref[h] += _dot(lhs, p)
            else:
                m_old = m_ref[h]
                m_new = jnp.maximum(m_old, jnp.max(s, axis=0, keepdims=True))
                alpha = jnp.exp2(m_old - m_new)
                p = jnp.exp2(s - m_new).astype(BF16)
                m_ref[h] = m_new
                acc_ref[h] = alpha * acc_ref[h] + _dot(lhs, p)

    k_first = keys(0)
    for h in range(n_heads):
        sa_ref[h] = _dot(k_first, qt_ref[h])

    def body(i, carry):
        step(2 * i, sa_ref, sb_ref)
        step(2 * i + 1, sb_ref, sa_ref)
        return carry

    lax.fori_loop(0, n_chunks // 2 - 1, body, 0)
    step(n_chunks - 2, sa_ref, sb_ref)
    step(n_chunks - 1, sb_ref, None)
    for h in range(n_heads):
        acc = acc_ref[h]
        o = acc[:HEAD_DIM, :] / acc[HEAD_DIM:HEAD_DIM + 1, :]
        o_ref[h * HEAD_DIM:(h + 1) * HEAD_DIM, :] = o.astype(BF16)


def _merge_ffn_kernel(x_ref, at_ref, u_ref, vn_ref, gate_ref, ws_ref, bs_ref,
                      wpa_ref, wpb_ref, wo_ref, gf_ref, wgu_ref, wd_ref, gfin_ref,
                      o_ref, gm_ref):
    tm = x_ref.shape[0]
    d_model = x_ref.shape[1]
    d_ff = wd_ref.shape[0]
    lane = lax.broadcasted_iota(jnp.int32, (GM_CHUNK, LANES), 1)
    low = lane < GM_GROUP_DIM
    for c in range(tm // GM_CHUNK):
        rs = slice(c * GM_CHUNK, (c + 1) * GM_CHUNK)
        for j in range(GM_DIM // LANES):
            cs = slice(j * LANES, (j + 1) * LANES)
            vblk = vn_ref[rs, cs]
            sv = jnp.where(low, _dot(ws_ref[2 * j], vblk), _dot(ws_ref[2 * j + 1], vblk))
            gm_ref[rs, cs] = (u_ref[rs, cs] * (sv + bs_ref[:, cs])).astype(BF16)

    pa = lax.dot_general(at_ref[...], wpa_ref[...], (((0,), (0,)), ((), ())),
                         preferred_element_type=F32)
    pb = _dot(gm_ref[...], wpb_ref[...])
    mix = gate_ref[:, :d_model] * pa + gate_ref[:, d_model:] * pb
    x1 = x_ref[...] + _dot(mix.astype(BF16), wo_ref[...])

    ms = jnp.mean(x1 * x1, axis=-1, keepdims=True)
    h = (x1 * lax.rsqrt(ms + EPS) * gf_ref[...]).astype(BF16)
    gt = _dot(h, wgu_ref[:, :d_ff])
    up = _dot(h, wgu_ref[:, d_ff:])
    act = (jax.nn.silu(gt) * up).astype(BF16)
    x2 = x1 + _dot(act, wd_ref[...])

    ms2 = jnp.mean(x2 * x2, axis=-1, keepdims=True)
    o_ref[...] = x2 * lax.rsqrt(ms2 + EPS) * gfin_ref[...]


def _const_spec(shape):
    nd = len(shape)
    return pl.BlockSpec(shape, lambda *_: (0,) * nd, pipeline_mode=pl.Buffered(1))


def _rope_tables(seq):
    half = HEAD_DIM // 2
    rows = seq // GRID_W
    row = jnp.repeat(jnp.arange(rows, dtype=jnp.int32), GRID_W)
    col = jnp.tile(jnp.arange(GRID_W, dtype=jnp.int32), rows)
    inv_freq = 1.0 / (ROPE_THETA ** (jnp.arange(0, half, 2, dtype=F32) / half))
    ang_r = row.astype(F32)[:, None] * inv_freq[None, :]
    ang_c = col.astype(F32)[:, None] * inv_freq[None, :]
    cr, sr = jnp.cos(ang_r), jnp.sin(ang_r)
    cc, sc = jnp.cos(ang_c), jnp.sin(ang_c)
    cos64 = jnp.concatenate([cr, cr, cc, cc], axis=-1)
    sin64 = jnp.concatenate([-sr, sr, -sc, sc], axis=-1)
    return jnp.tile(cos64, (1, LANES // HEAD_DIM)), jnp.tile(sin64, (1, LANES // HEAD_DIM))


def kernel(x, norm_mix_g, w_in, q_norm_g, k_norm_g, gm_ln_g, gm_ln_b, w_s, b_s, w_proj_a,
           w_proj_b, w_out, norm_ffn_g, w_gate_up, w_down, norm_final_g):
    B, S, D = x.shape
    in_dim = w_in.shape[-1]
    d_ff = w_down.shape[1]
    assert norm_mix_g.shape[0] == 1, "single-layer block"
    assert S % (2 * TK) == 0 and S % TQ == 0 and S % TM_IN == 0 and S % TM_OUT == 0
    assert TM_OUT % GM_CHUNK == 0 and TM_IN % TK == 0

    half_heads = N_Q_HEADS // 2
    perm = jnp.concatenate([
        jnp.arange(HEAD_DIM) + HEAD_DIM * h
        for j in range(half_heads) for h in (j, j + half_heads)])
    w_in_l = w_in[0]
    w_in_b = jnp.concatenate([w_in_l[:, :Q_DIM][:, perm], w_in_l[:, Q_DIM:]], axis=1).astype(BF16)

    cos_t, sin_t = _rope_tables(S)
    reps = LANES // HEAD_DIM
    qg = jnp.tile(q_norm_g[0], reps)[None, :]
    kg = jnp.tile(k_norm_g[0], reps)[None, :]
    lng = gm_ln_g[0].reshape(1, GM_DIM)
    lnb = gm_ln_b[0].reshape(1, GM_DIM)
    gid = jnp.arange(LANES) // HEAD_DIM
    gmat = (gid[:, None] == gid[None, :]).astype(BF16)

    two_axes = ("arbitrary", "arbitrary")
    cparams = pltpu.CompilerParams(dimension_semantics=two_axes,
                                   vmem_limit_bytes=VMEM_LIMIT_BYTES)

    def row_spec(tm, width):
        return pl.BlockSpec((None, tm, width), lambda b, i: (b, i, 0))

    n_in = S // TM_IN
    ck_in = TM_IN // TK
    qt, k, vt, u, vn, gates = pl.pallas_call(
        _inproj_kernel,
        grid=(B, n_in),
        in_specs=[
            row_spec(TM_IN, D),
            _const_spec((1, D)),
            _const_spec((D, in_dim)),
            _const_spec((LANES, LANES)),
            pl.BlockSpec((TM_IN, LANES), lambda b, i: (i, 0)),
            pl.BlockSpec((TM_IN, LANES), lambda b, i: (i, 0)),
            _const_spec((1, LANES)),
            _const_spec((1, LANES)),
            _const_spec((1, GM_DIM)),
            _const_spec((1, GM_DIM)),
        ],
        out_specs=[
            pl.BlockSpec((None, N_Q_HEADS, LANES, TM_IN), lambda b, i: (b, 0, 0, i)),
            row_spec(TM_IN, KV_DIM),
            pl.BlockSpec((None, ck_in, KV_DIM, TK), lambda b, i: (b, i, 0, 0)),
            row_spec(TM_IN, GM_DIM), row_spec(TM_IN, GM_DIM), row_spec(TM_IN, 2 * D),
        ],
        out_shape=[
            jax.ShapeDtypeStruct((B, N_Q_HEADS, LANES, S), BF16),
            jax.ShapeDtypeStruct((B, S, KV_DIM), BF16),
            jax.ShapeDtypeStruct((B, S // TK, KV_DIM, TK), BF16),
            jax.ShapeDtypeStruct((B, S, GM_DIM), F32),
            jax.ShapeDtypeStruct((B, S, GM_DIM), BF16),
            jax.ShapeDtypeStruct((B, S, 2 * D), F32),
        ],
        compiler_params=cparams,
        name="inproj",
    )(x, norm_mix_g, w_in_b, gmat, cos_t, sin_t, qg, kg, lng, lnb)

    def attention(fixed_shift):
        return pl.pallas_call(
            functools.partial(_attn_kernel, fixed_shift=fixed_shift),
            grid=(B, S // TQ),
            in_specs=[
                pl.BlockSpec(memory_space=pltpu.SMEM),
                pl.BlockSpec((None, N_Q_HEADS, LANES, TQ), lambda b, i: (b, 0, 0, i)),
                pl.BlockSpec((None, S, KV_DIM), lambda b, i: (b, 0, 0)),
                pl.BlockSpec((None, S // TK, KV_DIM, TK), lambda b, i: (b, 0, 0, 0)),
            ],
            out_specs=pl.BlockSpec((None, Q_DIM, TQ), lambda b, i: (b, 0, i)),
            out_shape=jax.ShapeDtypeStruct((B, Q_DIM, S), BF16),
            scratch_shapes=[
                pltpu.VMEM((N_Q_HEADS, TK, TQ), F32),
                pltpu.VMEM((N_Q_HEADS, TK, TQ), F32),
                pltpu.VMEM((N_Q_HEADS, 1, TQ), F32),
                pltpu.VMEM((N_Q_HEADS, HEAD_DIM + L_ROWS, TQ), F32),
            ],
            compiler_params=cparams,
            name="gqa_attention_shift" if fixed_shift else "gqa_attention_online",
        )

    score_bound = (HEAD_DIM * QK_SCALE_LOG2 * (1.0 + 2.0 ** -6)
                   * jnp.max(jnp.abs(q_norm_g[0])) * jnp.max(jnp.abs(k_norm_g[0])))
    attn_t = lax.cond(
        score_bound <= MAX_FIXED_SHIFT,
        lambda s_, *ops: attention(True)(s_, *ops),
        lambda s_, *ops: attention(False)(s_, *ops),
        score_bound.reshape(1).astype(F32), qt, k, vt)

    bs_full = jnp.repeat(jnp.transpose(b_s[0]), GM_GROUP_DIM, axis=1)
    out = pl.pallas_call(
        _merge_ffn_kernel,
        grid=(B, S // TM_OUT),
        in_specs=[
            row_spec(TM_OUT, D),
            pl.BlockSpec((None, Q_DIM, TM_OUT), lambda b, i: (b, 0, i)),
            row_spec(TM_OUT, GM_DIM), row_spec(TM_OUT, GM_DIM), row_spec(TM_OUT, 2 * D),
            _const_spec((GM_GROUPS, GM_CHUNK, GM_CHUNK)),
            _const_spec((GM_CHUNK, GM_DIM)),
            _const_spec((Q_DIM, D)),
            _const_spec((GM_DIM, D)),
            _const_spec((D, D)),
            _const_spec((1, D)),
            _const_spec((D, 2 * d_ff)),
            _const_spec((d_ff, D)),
            _const_spec((1, D)),
        ],
        out_specs=row_spec(TM_OUT, D),
        out_shape=jax.ShapeDtypeStruct((B, S, D), F32),
        scratch_shapes=[pltpu.VMEM((TM_OUT, GM_DIM), BF16)],
        compiler_params=cparams,
        name="merge_ffn",
    )(x, attn_t, u, vn, gates, w_s[0].astype(BF16), bs_full,
      w_proj_a[0].astype(BF16), w_proj_b[0].astype(BF16), w_out[0].astype(BF16), norm_ffn_g,
      w_gate_up[0].astype(BF16), w_down[0].astype(BF16), norm_final_g[None, :])
    return out
```

```python
import functools
import math

import jax
import jax.numpy as jnp
from jax import lax
from jax.experimental import pallas as pl
from jax.experimental.pallas import tpu as pltpu

HEAD_DIM = 64
N_Q_HEADS = 8
N_KV_HEADS = 2
Q_DIM = N_Q_HEADS * HEAD_DIM
KV_DIM = N_KV_HEADS * HEAD_DIM
GM_GROUPS = 8
GM_GROUP_DIM = 64
GM_DIM = GM_GROUPS * GM_GROUP_DIM
GM_CHUNK = 128
GRID_W = 64
ROPE_THETA = 10000.0
EPS = 1e-6

LANES = 128
VMEM_LIMIT_BYTES = 56 * 1024 * 1024

TM_IN = 256
TQ = 256
TK = 256
L_ROWS = 16
QK_SCALE_LOG2 = math.log2(math.e) / math.sqrt(HEAD_DIM)
MAX_FIXED_SHIFT = 48.0
CHUNKS_PER_BODY = 8
TM_OUT = 256

F32 = jnp.float32
BF16 = jnp.bfloat16


def _dot(a, b):
    return jnp.dot(a, b, preferred_element_type=F32)


def _group_sum(x, gmat):
    hi = x.astype(BF16)
    lo = (x - hi.astype(F32)).astype(BF16)
    return _dot(hi, gmat) + _dot(lo, gmat)


def _rope_partner(x, lane):
    fwd = pltpu.roll(x, LANES - 16, axis=1)
    bwd = pltpu.roll(x, 16, axis=1)
    return jnp.where((lane & 16) == 0, fwd, bwd)


def _inproj_kernel(x_ref, g_ref, w_ref, gmat_ref, cos_ref, sin_ref, qg_ref, kg_ref,
                   lng_ref, lnb_ref,
                   qt_ref, k_ref, vt_ref, u_ref, vn_ref, gate_ref):
    x = x_ref[...]
    tm = x.shape[0]
    ms = jnp.mean(x * x, axis=-1, keepdims=True)
    h = (x * lax.rsqrt(ms + EPS) * g_ref[...]).astype(BF16)
    gmat = gmat_ref[...]
    cos = cos_ref[...]
    sin = sin_ref[...]
    lane = lax.broadcasted_iota(jnp.int32, (tm, LANES), 1)
    inv_hd = 1.0 / HEAD_DIM

    def head_norm_rope(col0, gain):
        y = _dot(h, w_ref[:, col0:col0 + LANES])
        ss = _group_sum(y * y, gmat)
        y = y * lax.rsqrt(ss * inv_hd + EPS) * gain
        return y * cos + _rope_partner(y, lane) * sin

    scale = QK_SCALE_LOG2
    row = lax.broadcasted_iota(jnp.int32, (LANES, tm), 0)
    top = row < HEAD_DIM
    half_heads = N_Q_HEADS // 2
    for j in range(Q_DIM // LANES):
        qt = (head_norm_rope(j * LANES, qg_ref[...]) * scale).T
        zero = jnp.zeros_like(qt)
        qt_ref[j] = jnp.where(top, qt, zero).astype(BF16)
        qt_ref[j + half_heads] = jnp.where(top, zero, qt).astype(BF16)
    k_ref[...] = head_norm_rope(Q_DIM, kg_ref[...]).astype(BF16)
    vt = _dot(h, w_ref[:, Q_DIM + KV_DIM:Q_DIM + 2 * KV_DIM]).T.astype(BF16)
    for c in range(tm // TK):
        vt_ref[c] = vt[:, c * TK:(c + 1) * TK]

    uv0 = Q_DIM + 2 * KV_DIM
    u_ref[...] = jax.nn.gelu(_dot(h, w_ref[:, uv0:uv0 + GM_DIM]))
    for j in range(GM_DIM // LANES):
        c0 = uv0 + GM_DIM + j * LANES
        vv = jax.nn.gelu(_dot(h, w_ref[:, c0:c0 + LANES]))
        mu = _group_sum(vv, gmat) * (1.0 / GM_GROUP_DIM)
        d = vv - mu
        var = _group_sum(d * d, gmat) * (1.0 / GM_GROUP_DIM)
        sl = slice(j * LANES, (j + 1) * LANES)
        vn = d * lax.rsqrt(var + EPS) * lng_ref[:, sl] + lnb_ref[:, sl]
        vn_ref[:, sl] = vn.astype(BF16)

    g0 = uv0 + 2 * GM_DIM
    gate_ref[...] = jax.nn.sigmoid(_dot(h, w_ref[:, g0:]))


def _attn_kernel(shift_ref, qt_ref, k_ref, vt_ref, o_ref, sa_ref, sb_ref, m_ref, acc_ref, *,
                 fixed_shift):
    n_heads = qt_ref.shape[0]
    group = n_heads // N_KV_HEADS
    n_chunks = k_ref.shape[0] // TK
    if fixed_shift:
        shift = shift_ref[0]
    else:
        m_ref[...] = jnp.full(m_ref.shape, -jnp.inf, F32)
    acc_ref[...] = jnp.zeros(acc_ref.shape, F32)
    ones = jnp.ones((L_ROWS, TK), BF16)

    def keys(c):
        return k_ref[pl.ds(pl.multiple_of(c * TK, TK), TK), :]

    def step(c, cur_ref, next_ref):
        if next_ref is not None:
            k_next = keys(c + 1)
        vt_c = vt_ref[c]
        for h in range(n_heads):
            g = h // group
            if next_ref is not None:
                next_ref[h] = _dot(k_next, qt_ref[h])
            s = cur_ref[h]
            lhs = jnp.concatenate([vt_c[g * HEAD_DIM:(g + 1) * HEAD_DIM, :], ones], axis=0)
            if fixed_shift:
                p = jnp.exp2(s - shift).astype(BF16)
                acc_ref[h] += _dot(lhs, p)
            else:
                m_old = m_ref[h]
                m_new = jnp.maximum(m_old, jnp.max(s, axis=0, keepdims=True))
                alpha = jnp.exp2(m_old - m_new)
                p = jnp.exp2(s - m_new).astype(BF16)
                m_ref[h] = m_new
                acc_ref[h] = alpha * acc_ref[h] + _dot(lhs, p)

    k_first = keys(0)
    for h in range(n_heads):
        sa_ref[h] = _dot(k_first, qt_ref[h])

    bufs = (sa_ref, sb_ref)

    def steps(c0, last):
        for j in range(CHUNKS_PER_BODY):
            final = last and j == CHUNKS_PER_BODY - 1
            step(c0 + j, bufs[j % 2], None if final else bufs[(j + 1) % 2])

    def body(i, carry):
        steps(CHUNKS_PER_BODY * i, False)
        return carry

    lax.fori_loop(0, n_chunks // CHUNKS_PER_BODY - 1, body, 0)
    steps(n_chunks - CHUNKS_PER_BODY, True)
    for h in range(n_heads):
        acc = acc_ref[h]
        o = acc[:HEAD_DIM, :] / acc[HEAD_DIM:HEAD_DIM + 1, :]
        o_ref[h * HEAD_DIM:(h + 1) * HEAD_DIM, :] = o.astype(BF16)


def _merge_ffn_kernel(x_ref, at_ref, u_ref, vn_ref, gate_ref, ws_ref, bs_ref,
                      wpa_ref, wpb_ref, wo_ref, gf_ref, wgu_ref, wd_ref, gfin_ref,
                      o_ref, gm_ref):
    tm = x_ref.shape[0]
    d_model = x_ref.shape[1]
    d_ff = wd_ref.shape[0]
    lane = lax.broadcasted_iota(jnp.int32, (GM_CHUNK, LANES), 1)
    low = lane < GM_GROUP_DIM
    for c in range(tm // GM_CHUNK):
        rs = slice(c * GM_CHUNK, (c + 1) * GM_CHUNK)
        for j in range(GM_DIM // LANES):
            cs = slice(j * LANES, (j + 1) * LANES)
            vblk = vn_ref[rs, cs]
            sv = jnp.where(low, _dot(ws_ref[2 * j], vblk), _dot(ws_ref[2 * j + 1], vblk))
            gm_ref[rs, cs] = (u_ref[rs, cs] * (sv + bs_ref[:, cs])).astype(BF16)

    pa = lax.dot_general(at_ref[...], wpa_ref[...], (((0,), (0,)), ((), ())),
                         preferred_element_type=F32)
    pb = _dot(gm_ref[...], wpb_ref[...])
    mix = gate_ref[:, :d_model] * pa + gate_ref[:, d_model:] * pb
    x1 = x_ref[...] + _dot(mix.astype(BF16), wo_ref[...])

    ms = jnp.mean(x1 * x1, axis=-1, keepdims=True)
    h = (x1 * lax.rsqrt(ms + EPS) * gf_ref[...]).astype(BF16)
    gt = _dot(h, wgu_ref[:, :d_ff])
    up = _dot(h, wgu_ref[:, d_ff:])
    act = (jax.nn.silu(gt) * up).astype(BF16)
    x2 = x1 + _dot(act, wd_ref[...])

    ms2 = jnp.mean(x2 * x2, axis=-1, keepdims=True)
    o_ref[...] = x2 * lax.rsqrt(ms2 + EPS) * gfin_ref[...]


def _const_spec(shape):
    nd = len(shape)
    return pl.BlockSpec(shape, lambda *_: (0,) * nd, pipeline_mode=pl.Buffered(1))


def _rope_tables(seq):
    half = HEAD_DIM // 2
    rows = seq // GRID_W
    row = jnp.repeat(jnp.arange(rows, dtype=jnp.int32), GRID_W)
    col = jnp.tile(jnp.arange(GRID_W, dtype=jnp.int32), rows)
    inv_freq = 1.0 / (ROPE_THETA ** (jnp.arange(0, half, 2, dtype=F32) / half))
    ang_r = row.astype(F32)[:, None] * inv_freq[None, :]
    ang_c = col.astype(F32)[:, None] * inv_freq[None, :]
    cr, sr = jnp.cos(ang_r), jnp.sin(ang_r)
    cc, sc = jnp.cos(ang_c), jnp.sin(ang_c)
    cos64 = jnp.concatenate([cr, cr, cc, cc], axis=-1)
    sin64 = jnp.concatenate([-sr, sr, -sc, sc], axis=-1)
    return jnp.tile(cos64, (1, LANES // HEAD_DIM)), jnp.tile(sin64, (1, LANES // HEAD_DIM))


def kernel(x, norm_mix_g, w_in, q_norm_g, k_norm_g, gm_ln_g, gm_ln_b, w_s, b_s, w_proj_a,
           w_proj_b, w_out, norm_ffn_g, w_gate_up, w_down, norm_final_g):
    B, S, D = x.shape
    in_dim = w_in.shape[-1]
    d_ff = w_down.shape[1]
    assert norm_mix_g.shape[0] == 1, "single-layer block"
    assert CHUNKS_PER_BODY % 2 == 0 and S % (CHUNKS_PER_BODY * TK) == 0 and S % TQ == 0 and S % TM_IN == 0 and S % TM_OUT == 0
    assert TM_OUT % GM_CHUNK == 0 and TM_IN % TK == 0

    half_heads = N_Q_HEADS // 2
    perm = jnp.concatenate([
        jnp.arange(HEAD_DIM) + HEAD_DIM * h
        for j in range(half_heads) for h in (j, j + half_heads)])
    w_in_l = w_in[0]
    w_in_b = jnp.concatenate([w_in_l[:, :Q_DIM][:, perm], w_in_l[:, Q_DIM:]], axis=1).astype(BF16)

    cos_t, sin_t = _rope_tables(S)
    reps = LANES // HEAD_DIM
    qg = jnp.tile(q_norm_g[0], reps)[None, :]
    kg = jnp.tile(k_norm_g[0], reps)[None, :]
    lng = gm_ln_g[0].reshape(1, GM_DIM)
    lnb = gm_ln_b[0].reshape(1, GM_DIM)
    gid = jnp.arange(LANES) // HEAD_DIM
    gmat = (gid[:, None] == gid[None, :]).astype(BF16)

    two_axes = ("arbitrary", "arbitrary")
    cparams = pltpu.CompilerParams(dimension_semantics=two_axes,
                                   vmem_limit_bytes=VMEM_LIMIT_BYTES)

    def row_spec(tm, width):
        return pl.BlockSpec((None, tm, width), lambda b, i: (b, i, 0))

    n_in = S // TM_IN
    ck_in = TM_IN // TK
    qt, k, vt, u, vn, gates = pl.pallas_call(
        _inproj_kernel,
        grid=(B, n_in),
        in_specs=[
            row_spec(TM_IN, D),
            _const_spec((1, D)),
            _const_spec((D, in_dim)),
            _const_spec((LANES, LANES)),
            pl.BlockSpec((TM_IN, LANES), lambda b, i: (i, 0)),
            pl.BlockSpec((TM_IN, LANES), lambda b, i: (i, 0)),
            _const_spec((1, LANES)),
            _const_spec((1, LANES)),
            _const_spec((1, GM_DIM)),
            _const_spec((1, GM_DIM)),
        ],
        out_specs=[
            pl.BlockSpec((None, N_Q_HEADS, LANES, TM_IN), lambda b, i: (b, 0, 0, i)),
            row_spec(TM_IN, KV_DIM),
            pl.BlockSpec((None, ck_in, KV_DIM, TK), lambda b, i: (b, i, 0, 0)),
            row_spec(TM_IN, GM_DIM), row_spec(TM_IN, GM_DIM), row_spec(TM_IN, 2 * D),
        ],
        out_shape=[
            jax.ShapeDtypeStruct((B, N_Q_HEADS, LANES, S), BF16),
            jax.ShapeDtypeStruct((B, S, KV_DIM), BF16),
            jax.ShapeDtypeStruct((B, S // TK, KV_DIM, TK), BF16),
            jax.ShapeDtypeStruct((B, S, GM_DIM), F32),
            jax.ShapeDtypeStruct((B, S, GM_DIM), BF16),
            jax.ShapeDtypeStruct((B, S, 2 * D), F32),
        ],
        compiler_params=cparams,
        name="inproj",
    )(x, norm_mix_g, w_in_b, gmat, cos_t, sin_t, qg, kg, lng, lnb)

    def attention(fixed_shift):
        return pl.pallas_call(
            functools.partial(_attn_kernel, fixed_shift=fixed_shift),
            grid=(B, S // TQ),
            in_specs=[
                pl.BlockSpec(memory_space=pltpu.SMEM),
                pl.BlockSpec((None, N_Q_HEADS, LANES, TQ), lambda b, i: (b, 0, 0, i)),
                pl.BlockSpec((None, S, KV_DIM), lambda b, i: (b, 0, 0)),
                pl.BlockSpec((None, S // TK, KV_DIM, TK), lambda b, i: (b, 0, 0, 0)),
            ],
            out_specs=pl.BlockSpec((None, Q_DIM, TQ), lambda b, i: (b, 0, i)),
            out_shape=jax.ShapeDtypeStruct((B, Q_DIM, S), BF16),
            scratch_shapes=[
                pltpu.VMEM((N_Q_HEADS, TK, TQ), F32),
                pltpu.VMEM((N_Q_HEADS, TK, TQ), F32),
                pltpu.VMEM((N_Q_HEADS, 1, TQ), F32),
                pltpu.VMEM((N_Q_HEADS, HEAD_DIM + L_ROWS, TQ), F32),
            ],
            compiler_params=cparams,
            name="gqa_attention_shift" if fixed_shift else "gqa_attention_online",
        )

    score_bound = (HEAD_DIM * QK_SCALE_LOG2 * (1.0 + 2.0 ** -6)
                   * jnp.max(jnp.abs(q_norm_g[0])) * jnp.max(jnp.abs(k_norm_g[0])))
    attn_t = lax.cond(
        score_bound <= MAX_FIXED_SHIFT,
        lambda s_, *ops: attention(True)(s_, *ops),
        lambda s_, *ops: attention(False)(s_, *ops),
        score_bound.reshape(1).astype(F32), qt, k, vt)

    bs_full = jnp.repeat(jnp.transpose(b_s[0]), GM_GROUP_DIM, axis=1)
    out = pl.pallas_call(
        _merge_ffn_kernel,
        grid=(B, S // TM_OUT),
        in_specs=[
            row_spec(TM_OUT, D),
            pl.BlockSpec((None, Q_DIM, TM_OUT), lambda b, i: (b, 0, i)),
            row_spec(TM_OUT, GM_DIM), row_spec(TM_OUT, GM_DIM), row_spec(TM_OUT, 2 * D),
            _const_spec((GM_GROUPS, GM_CHUNK, GM_CHUNK)),
            _const_spec((GM_CHUNK, GM_DIM)),
            _const_spec((Q_DIM, D)),
            _const_spec((GM_DIM, D)),
            _const_spec((D, D)),
            _const_spec((1, D)),
            _const_spec((D, 2 * d_ff)),
            _const_spec((d_ff, D)),
            _const_spec((1, D)),
        ],
        out_specs=row_spec(TM_OUT, D),
        out_shape=jax.ShapeDtypeStruct((B, S, D), F32),
        scratch_shapes=[pltpu.VMEM((TM_OUT, GM_DIM), BF16)],
        compiler_params=cparams,
        name="merge_ffn",
    )(x, attn_t, u, vn, gates, w_s[0].astype(BF16), bs_full,
      w_proj_a[0].astype(BF16), w_proj_b[0].astype(BF16), w_out[0].astype(BF16), norm_ffn_g,
      w_gate_up[0].astype(BF16), w_down[0].astype(BF16), norm_final_g[None, :])
    return out
```

```python
import functools
import math

import jax
import jax.numpy as jnp
from jax import lax
from jax.experimental import pallas as pl
from jax.experimental.pallas import tpu as pltpu

HEAD_DIM = 64
N_Q_HEADS = 8
N_KV_HEADS = 2
Q_DIM = N_Q_HEADS * HEAD_DIM
KV_DIM = N_KV_HEADS * HEAD_DIM
GM_GROUPS = 8
GM_GROUP_DIM = 64
GM_DIM = GM_GROUPS * GM_GROUP_DIM
GM_CHUNK = 128
GRID_W = 64
ROPE_THETA = 10000.0
EPS = 1e-6

LANES = 128
SUBLANES = 8
VMEM_LIMIT_BYTES = 56 * 1024 * 1024

TM_IN = 256
TQ = 256
TK = 256
QK_SCALE_LOG2 = math.log2(math.e) / math.sqrt(HEAD_DIM)
MAX_FIXED_SHIFT = 48.0
CHUNKS_PER_BODY = 8
TM_OUT = 256

F32 = jnp.float32
BF16 = jnp.bfloat16


def _dot(a, b):
    return jnp.dot(a, b, preferred_element_type=F32)


def _group_sum(x, gmat):
    hi = x.astype(BF16)
    lo = (x - hi.astype(F32)).astype(BF16)
    return _dot(hi, gmat) + _dot(lo, gmat)


def _rope_partner(x, lane):
    fwd = pltpu.roll(x, LANES - 16, axis=1)
    bwd = pltpu.roll(x, 16, axis=1)
    return jnp.where((lane & 16) == 0, fwd, bwd)


def _inproj_kernel(x_ref, g_ref, w_ref, gmat_ref, cos_ref, sin_ref, qg_ref, kg_ref,
                   lng_ref, lnb_ref,
                   qt_ref, k_ref, vt_ref, u_ref, vn_ref, gate_ref):
    x = x_ref[...]
    tm = x.shape[0]
    ms = jnp.mean(x * x, axis=-1, keepdims=True)
    h = (x * lax.rsqrt(ms + EPS) * g_ref[...]).astype(BF16)
    gmat = gmat_ref[...]
    cos = cos_ref[...]
    sin = sin_ref[...]
    lane = lax.broadcasted_iota(jnp.int32, (tm, LANES), 1)
    inv_hd = 1.0 / HEAD_DIM

    def head_norm_rope(col0, gain):
        y = _dot(h, w_ref[:, col0:col0 + LANES])
        ss = _group_sum(y * y, gmat)
        y = y * lax.rsqrt(ss * inv_hd + EPS) * gain
        return y * cos + _rope_partner(y, lane) * sin

    scale = QK_SCALE_LOG2
    row = lax.broadcasted_iota(jnp.int32, (LANES, tm), 0)
    top = row < HEAD_DIM
    half_heads = N_Q_HEADS // 2
    for j in range(Q_DIM // LANES):
        qt = (head_norm_rope(j * LANES, qg_ref[...]) * scale).T
        zero = jnp.zeros_like(qt)
        qt_ref[j] = jnp.where(top, qt, zero).astype(BF16)
        qt_ref[j + half_heads] = jnp.where(top, zero, qt).astype(BF16)
    k_ref[...] = head_norm_rope(Q_DIM, kg_ref[...]).astype(BF16)
    vt = _dot(h, w_ref[:, Q_DIM + KV_DIM:Q_DIM + 2 * KV_DIM]).T.astype(BF16)
    for c in range(tm // TK):
        vt_ref[c] = vt[:, c * TK:(c + 1) * TK]

    uv0 = Q_DIM + 2 * KV_DIM
    u_ref[...] = jax.nn.gelu(_dot(h, w_ref[:, uv0:uv0 + GM_DIM]))
    for j in range(GM_DIM // LANES):
        c0 = uv0 + GM_DIM + j * LANES
        vv = jax.nn.gelu(_dot(h, w_ref[:, c0:c0 + LANES]))
        mu = _group_sum(vv, gmat) * (1.0 / GM_GROUP_DIM)
        d = vv - mu
        var = _group_sum(d * d, gmat) * (1.0 / GM_GROUP_DIM)
        sl = slice(j * LANES, (j + 1) * LANES)
        vn = d * lax.rsqrt(var + EPS) * lng_ref[:, sl] + lnb_ref[:, sl]
        vn_ref[:, sl] = vn.astype(BF16)

    g0 = uv0 + 2 * GM_DIM
    gate_ref[...] = jax.nn.sigmoid(_dot(h, w_ref[:, g0:]))


def _sublane_partial_sum(e):
    rows, lanes = e.shape
    return jnp.sum(e.reshape(rows // SUBLANES, SUBLANES, lanes), axis=0)


def _attn_kernel(shift_ref, qt_ref, k_ref, vt_ref, o_ref, sa_ref, sb_ref, m_ref, l_ref, acc_ref,
                 *, fixed_shift):
    n_heads = qt_ref.shape[0]
    group = n_heads // N_KV_HEADS
    n_chunks = k_ref.shape[0] // TK
    if fixed_shift:
        shift = shift_ref[0]
    else:
        m_ref[...] = jnp.full(m_ref.shape, -jnp.inf, F32)
    acc_ref[...] = jnp.zeros(acc_ref.shape, F32)
    l_ref[...] = jnp.zeros(l_ref.shape, F32)

    def keys(c):
        return k_ref[pl.ds(pl.multiple_of(c * TK, TK), TK), :]

    def step(c, cur_ref, next_ref):
        if next_ref is not None:
            k_next = keys(c + 1)
        vt_c = vt_ref[c]
        for h in range(n_heads):
            g = h // group
            if next_ref is not None:
                next_ref[h] = _dot(k_next, qt_ref[h])
            s = cur_ref[h]
            vt_g = vt_c[g * HEAD_DIM:(g + 1) * HEAD_DIM, :]
            if fixed_shift:
                e = jnp.exp2(s - shift)
                l_ref[h] += _sublane_partial_sum(e)
                acc_ref[h] += _dot(vt_g, e.astype(BF16))
            else:
                m_old = m_ref[h]
                m_new = jnp.maximum(m_old, jnp.max(s, axis=0, keepdims=True))
                alpha = jnp.exp2(m_old - m_new)
                e = jnp.exp2(s - m_new)
                m_ref[h] = m_new
                l_ref[h] = alpha * l_ref[h] + _sublane_partial_sum(e)
                acc_ref[h] = alpha * acc_ref[h] + _dot(vt_g, e.astype(BF16))

    k_first = keys(0)
    for h in range(n_heads):
        sa_ref[h] = _dot(k_first, qt_ref[h])

    bufs = (sa_ref, sb_ref)

    def steps(c0, last):
        for j in range(CHUNKS_PER_BODY):
            final = last and j == CHUNKS_PER_BODY - 1
            step(c0 + j, bufs[j % 2], None if final else bufs[(j + 1) % 2])

    def body(i, carry):
        steps(CHUNKS_PER_BODY * i, False)
        return carry

    lax.fori_loop(0, n_chunks // CHUNKS_PER_BODY - 1, body, 0)
    steps(n_chunks - CHUNKS_PER_BODY, True)
    for h in range(n_heads):
        o = acc_ref[h] / jnp.sum(l_ref[h], axis=0, keepdims=True)
        o_ref[h * HEAD_DIM:(h + 1) * HEAD_DIM, :] = o.astype(BF16)


def _merge_ffn_kernel(x_ref, at_ref, u_ref, vn_ref, gate_ref, ws_ref, bs_ref,
                      wpa_ref, wpb_ref, wo_ref, gf_ref, wgu_ref, wd_ref, gfin_ref,
                      o_ref, gm_ref):
    tm = x_ref.shape[0]
    d_model = x_ref.shape[1]
    d_ff = wd_ref.shape[0]
    lane = lax.broadcasted_iota(jnp.int32, (GM_CHUNK, LANES), 1)
    low = lane < GM_GROUP_DIM
    for c in range(tm // GM_CHUNK):
        rs = slice(c * GM_CHUNK, (c + 1) * GM_CHUNK)
        for j in range(GM_DIM // LANES):
            cs = slice(j * LANES, (j + 1) * LANES)
            vblk = vn_ref[rs, cs]
            sv = jnp.where(low, _dot(ws_ref[2 * j], vblk), _dot(ws_ref[2 * j + 1], vblk))
            gm_ref[rs, cs] = (u_ref[rs, cs] * (sv + bs_ref[:, cs])).astype(BF16)

    pa = lax.dot_general(at_ref[...], wpa_ref[...], (((0,), (0,)), ((), ())),
                         preferred_element_type=F32)
    pb = _dot(gm_ref[...], wpb_ref[...])
    mix = gate_ref[:, :d_model] * pa + gate_ref[:, d_model:] * pb
    x1 = x_ref[...] + _dot(mix.astype(BF16), wo_ref[...])

    ms = jnp.mean(x1 * x1, axis=-1, keepdims=True)
    h = (x1 * lax.rsqrt(ms + EPS) * gf_ref[...]).astype(BF16)
    gt = _dot(h, wgu_ref[:, :d_ff])
    up = _dot(h, wgu_ref[:, d_ff:])
    act = (jax.nn.silu(gt) * up).astype(BF16)
    x2 = x1 + _dot(act, wd_ref[...])

    ms2 = jnp.mean(x2 * x2, axis=-1, keepdims=True)
    o_ref[...] = x2 * lax.rsqrt(ms2 + EPS) * gfin_ref[...]


def _const_spec(shape):
    nd = len(shape)
    return pl.BlockSpec(shape, lambda *_: (0,) * nd, pipeline_mode=pl.Buffered(1))


def _rope_tables(seq):
    half = HEAD_DIM // 2
    rows = seq // GRID_W
    row = jnp.repeat(jnp.arange(rows, dtype=jnp.int32), GRID_W)
    col = jnp.tile(jnp.arange(GRID_W, dtype=jnp.int32), rows)
    inv_freq = 1.0 / (ROPE_THETA ** (jnp.arange(0, half, 2, dtype=F32) / half))
    ang_r = row.astype(F32)[:, None] * inv_freq[None, :]
    ang_c = col.astype(F32)[:, None] * inv_freq[None, :]
    cr, sr = jnp.cos(ang_r), jnp.sin(ang_r)
    cc, sc = jnp.cos(ang_c), jnp.sin(ang_c)
    cos64 = jnp.concatenate([cr, cr, cc, cc], axis=-1)
    sin64 = jnp.concatenate([-sr, sr, -sc, sc], axis=-1)
    return jnp.tile(cos64, (1, LANES // HEAD_DIM)), jnp.tile(sin64, (1, LANES // HEAD_DIM))


def kernel(x, norm_mix_g, w_in, q_norm_g, k_norm_g, gm_ln_g, gm_ln_b, w_s, b_s, w_proj_a,
           w_proj_b, w_out, norm_ffn_g, w_gate_up, w_down, norm_final_g):
    B, S, D = x.shape
    in_dim = w_in.shape[-1]
    d_ff = w_down.shape[1]
    assert norm_mix_g.shape[0] == 1, "single-layer block"
    assert CHUNKS_PER_BODY % 2 == 0 and S % (CHUNKS_PER_BODY * TK) == 0 and S % TQ == 0 and S % TM_IN == 0 and S % TM_OUT == 0
    assert TM_OUT % GM_CHUNK == 0 and TM_IN % TK == 0

    half_heads = N_Q_HEADS // 2
    perm = jnp.concatenate([
        jnp.arange(HEAD_DIM) + HEAD_DIM * h
        for j in range(half_heads) for h in (j, j + half_heads)])
    w_in_l = w_in[0]
    w_in_b = jnp.concatenate([w_in_l[:, :Q_DIM][:, perm], w_in_l[:, Q_DIM:]], axis=1).astype(BF16)

    cos_t, sin_t = _rope_tables(S)
    reps = LANES // HEAD_DIM
    qg = jnp.tile(q_norm_g[0], reps)[None, :]
    kg = jnp.tile(k_norm_g[0], reps)[None, :]
    lng = gm_ln_g[0].reshape(1, GM_DIM)
    lnb = gm_ln_b[0].reshape(1, GM_DIM)
    gid = jnp.arange(LANES) // HEAD_DIM
    gmat = (gid[:, None] == gid[None, :]).astype(BF16)

    two_axes = ("arbitrary", "arbitrary")
    cparams = pltpu.CompilerParams(dimension_semantics=two_axes,
                                   vmem_limit_bytes=VMEM_LIMIT_BYTES)

    def row_spec(tm, width):
        return pl.BlockSpec((None, tm, width), lambda b, i: (b, i, 0))

    n_in = S // TM_IN
    ck_in = TM_IN // TK
    qt, k, vt, u, vn, gates = pl.pallas_call(
        _inproj_kernel,
        grid=(B, n_in),
        in_specs=[
            row_spec(TM_IN, D),
            _const_spec((1, D)),
            _const_spec((D, in_dim)),
            _const_spec((LANES, LANES)),
            pl.BlockSpec((TM_IN, LANES), lambda b, i: (i, 0)),
            pl.BlockSpec((TM_IN, LANES), lambda b, i: (i, 0)),
            _const_spec((1, LANES)),
            _const_spec((1, LANES)),
            _const_spec((1, GM_DIM)),
            _const_spec((1, GM_DIM)),
        ],
        out_specs=[
            pl.BlockSpec((None, N_Q_HEADS, LANES, TM_IN), lambda b, i: (b, 0, 0, i)),
            row_spec(TM_IN, KV_DIM),
            pl.BlockSpec((None, ck_in, KV_DIM, TK), lambda b, i: (b, i, 0, 0)),
            row_spec(TM_IN, GM_DIM), row_spec(TM_IN, GM_DIM), row_spec(TM_IN, 2 * D),
        ],
        out_shape=[
            jax.ShapeDtypeStruct((B, N_Q_HEADS, LANES, S), BF16),
            jax.ShapeDtypeStruct((B, S, KV_DIM), BF16),
            jax.ShapeDtypeStruct((B, S // TK, KV_DIM, TK), BF16),
            jax.ShapeDtypeStruct((B, S, GM_DIM), F32),
            jax.ShapeDtypeStruct((B, S, GM_DIM), BF16),
            jax.ShapeDtypeStruct((B, S, 2 * D), F32),
        ],
        compiler_params=cparams,
        name="inproj",
    )(x, norm_mix_g, w_in_b, gmat, cos_t, sin_t, qg, kg, lng, lnb)

    def attention(fixed_shift):
        return pl.pallas_call(
            functools.partial(_attn_kernel, fixed_shift=fixed_shift),
            grid=(B, S // TQ),
            in_specs=[
                pl.BlockSpec(memory_space=pltpu.SMEM),
                pl.BlockSpec((None, N_Q_HEADS, LANES, TQ), lambda b, i: (b, 0, 0, i)),
                pl.BlockSpec((None, S, KV_DIM), lambda b, i: (b, 0, 0)),
                pl.BlockSpec((None, S // TK, KV_DIM, TK), lambda b, i: (b, 0, 0, 0)),
            ],
            out_specs=pl.BlockSpec((None, Q_DIM, TQ), lambda b, i: (b, 0, i)),
            out_shape=jax.ShapeDtypeStruct((B, Q_DIM, S), BF16),
            scratch_shapes=[
                pltpu.VMEM((N_Q_HEADS, TK, TQ), F32),
                pltpu.VMEM((N_Q_HEADS, TK, TQ), F32),
                pltpu.VMEM((N_Q_HEADS, 1, TQ), F32),
                pltpu.VMEM((N_Q_HEADS, SUBLANES, TQ), F32),
                pltpu.VMEM((N_Q_HEADS, HEAD_DIM, TQ), F32),
            ],
            compiler_params=cparams,
            name="gqa_attention_shift" if fixed_shift else "gqa_attention_online",
        )

    score_bound = (HEAD_DIM * QK_SCALE_LOG2 * (1.0 + 2.0 ** -6)
                   * jnp.max(jnp.abs(q_norm_g[0])) * jnp.max(jnp.abs(k_norm_g[0])))
    attn_t = lax.cond(
        score_bound <= MAX_FIXED_SHIFT,
        lambda s_, *ops: attention(True)(s_, *ops),
        lambda s_, *ops: attention(False)(s_, *ops),
        score_bound.reshape(1).astype(F32), qt, k, vt)

    bs_full = jnp.repeat(jnp.transpose(b_s[0]), GM_GROUP_DIM, axis=1)
    out = pl.pallas_call(
        _merge_ffn_kernel,
        grid=(B, S // TM_OUT),
        in_specs=[
            row_spec(TM_OUT, D),
            pl.BlockSpec((None, Q_DIM, TM_OUT), lambda b, i: (b, 0, i)),
            row_spec(TM_OUT, GM_DIM), row_spec(TM_OUT, GM_DIM), row_spec(TM_OUT, 2 * D),
            _const_spec((GM_GROUPS, GM_CHUNK, GM_CHUNK)),
            _const_spec((GM_CHUNK, GM_DIM)),
            _const_spec((Q_DIM, D)),
            _const_spec((GM_DIM, D)),
            _const_spec((D, D)),
            _const_spec((1, D)),
            _const_spec((D, 2 * d_ff)),
            _const_spec((d_ff, D)),
            _const_spec((1, D)),
        ],
        out_specs=row_spec(TM_OUT, D),
        out_shape=jax.ShapeDtypeStruct((B, S, D), F32),
        scratch_shapes=[pltpu.VMEM((TM_OUT, GM_DIM), BF16)],
        compiler_params=cparams,
        name="merge_ffn",
    )(x, attn_t, u, vn, gates, w_s[0].astype(BF16), bs_full,
      w_proj_a[0].astype(BF16), w_proj_b[0].astype(BF16), w_out[0].astype(BF16), norm_ffn_g,
      w_gate_up[0].astype(BF16), w_down[0].astype(BF16), norm_final_g[None, :])
    return out
```

```python
import functools
import math

import jax
import jax.numpy as jnp
from jax import lax
from jax.experimental import pallas as pl
from jax.experimental.pallas import tpu as pltpu

HEAD_DIM = 64
N_Q_HEADS = 8
N_KV_HEADS = 2
Q_DIM = N_Q_HEADS * HEAD_DIM
KV_DIM = N_KV_HEADS * HEAD_DIM
GM_GROUPS = 8
GM_GROUP_DIM = 64
GM_DIM = GM_GROUPS * GM_GROUP_DIM
GM_CHUNK = 128
GRID_W = 64
ROPE_THETA = 10000.0
EPS = 1e-6

LANES = 128
SUBLANES = 8
VMEM_LIMIT_BYTES = 56 * 1024 * 1024

TM_IN = 256
TQ = 256
TK = 256
QK_SCALE_LOG2 = math.log2(math.e) / math.sqrt(HEAD_DIM)
MAX_FIXED_SHIFT = 48.0
CHUNKS_PER_BODY = 8
TM_OUT = 256

F32 = jnp.float32
BF16 = jnp.bfloat16


def _dot(a, b):
    return jnp.dot(a, b, preferred_element_type=F32)


def _group_sum(x, gmat2):
    hi = x.astype(BF16)
    lo = (x - hi.astype(F32)).astype(BF16)
    return _dot(jnp.concatenate([hi, lo], axis=1), gmat2)


def _rope_partner(x, lane):
    fwd = pltpu.roll(x, LANES - 16, axis=1)
    bwd = pltpu.roll(x, 16, axis=1)
    return jnp.where((lane & 16) == 0, fwd, bwd)


def _inproj_kernel(x_ref, g_ref, w_ref, gmat_ref, cos_ref, sin_ref, qg_ref, kg_ref,
                   lng_ref, lnb_ref,
                   qt_ref, k_ref, vt_ref, u_ref, vn_ref, gate_ref, ya_ref, yb_ref):
    i = pl.program_id(0)
    tm = x_ref.shape[0]

    @pl.when(i == 0)
    def _():
        yb_ref[...] = jnp.zeros(yb_ref.shape, F32)

    def compute(y_w, y_r):
        x = x_ref[...]
        ms = jnp.mean(x * x, axis=-1, keepdims=True)
        h = (x * lax.rsqrt(ms + EPS) * g_ref[...]).astype(BF16)
        gmat = gmat_ref[...]
        lane = lax.broadcasted_iota(jnp.int32, (tm, LANES), 1)

        def project(c0, c1):
            y_w[:, c0:c1] = _dot(h, w_ref[:, c0:c1])

        v0 = Q_DIM + KV_DIM
        uv0 = Q_DIM + 2 * KV_DIM
        g0 = uv0 + 2 * GM_DIM
        g1 = g0 + (w_ref.shape[1] - g0) // 2

        n_qk = (Q_DIM + KV_DIM) // LANES
        yqk = [y_r[:, j * LANES:(j + 1) * LANES] for j in range(n_qk)]
        project(0, uv0)
        n_gate = w_ref.shape[1] - g0
        gate_ref[:, :n_gate // 2] = jax.nn.sigmoid(y_r[:, g0:g1])
        ss = [_group_sum(y * y, gmat) for y in yqk]
        project(uv0, g0)

        def norm_rope(y, s2, gain):
            y = y * lax.rsqrt(s2 * (1.0 / HEAD_DIM) + EPS) * gain
            return y * cos_ref[...] + _rope_partner(y, lane) * sin_ref[...]

        row = lax.broadcasted_iota(jnp.int32, (LANES, tm), 0)
        top = row < HEAD_DIM
        half_heads = N_Q_HEADS // 2
        for j in range(Q_DIM // LANES):
            qt = (norm_rope(yqk[j], ss[j], qg_ref[...]) * QK_SCALE_LOG2).T
            zero = jnp.zeros_like(qt)
            qt_ref[j] = jnp.where(top, qt, zero).astype(BF16)
            qt_ref[j + half_heads] = jnp.where(top, zero, qt).astype(BF16)
        k_ref[...] = norm_rope(yqk[n_qk - 1], ss[n_qk - 1], kg_ref[...]).astype(BF16)
        vt = y_r[:, v0:uv0].T.astype(BF16)
        for c in range(tm // TK):
            vt_ref[c] = vt[:, c * TK:(c + 1) * TK]

        gate_ref[:, n_gate // 2:] = jax.nn.sigmoid(y_r[:, g1:])
        u_ref[...] = jax.nn.gelu(y_r[:, uv0:uv0 + GM_DIM])
        n_gm = GM_DIM // LANES
        vv = [jax.nn.gelu(y_r[:, uv0 + GM_DIM + j * LANES:uv0 + GM_DIM + (j + 1) * LANES])
              for j in range(n_gm)]
        project(g0, g1)
        mu = [_group_sum(v, gmat) * (1.0 / GM_GROUP_DIM) for v in vv]
        gq = g1 + (w_ref.shape[1] - g1) // 2
        project(g1, gq)
        dv = [v - m for v, m in zip(vv, mu)]
        var = [_group_sum(d * d, gmat) * (1.0 / GM_GROUP_DIM) for d in dv]
        project(gq, w_ref.shape[1])
        for j in range(n_gm):
            sl = slice(j * LANES, (j + 1) * LANES)
            vn = dv[j] * lax.rsqrt(var[j] + EPS) * lng_ref[:, sl] + lnb_ref[:, sl]
            vn_ref[:, sl] = vn.astype(BF16)

    @pl.when(i % 2 == 0)
    def _():
        compute(ya_ref, yb_ref)

    @pl.when(i % 2 == 1)
    def _():
        compute(yb_ref, ya_ref)


def _sublane_partial_sum(e):
    rows, lanes = e.shape
    return jnp.sum(e.reshape(rows // SUBLANES, SUBLANES, lanes), axis=0)


def _attn_kernel(shift_ref, qt_ref, k_ref, vt_ref, o_ref, sa_ref, sb_ref, m_ref, l_ref, acc_ref,
                 *, fixed_shift):
    n_heads = qt_ref.shape[0]
    group = n_heads // N_KV_HEADS
    n_chunks = k_ref.shape[0] // TK
    if fixed_shift:
        shift = shift_ref[0]
    else:
        m_ref[...] = jnp.full(m_ref.shape, -jnp.inf, F32)
    acc_ref[...] = jnp.zeros(acc_ref.shape, F32)
    l_ref[...] = jnp.zeros(l_ref.shape, F32)

    def keys(c):
        return k_ref[pl.ds(pl.multiple_of(c * TK, TK), TK), :]

    def step(c, cur_ref, next_ref):
        if next_ref is not None:
            k_next = keys(c + 1)
        vt_c = vt_ref[c]
        for h in range(n_heads):
            g = h // group
            if next_ref is not None:
                next_ref[h] = _dot(k_next, qt_ref[h])
            s = cur_ref[h]
            vt_g = vt_c[g * HEAD_DIM:(g + 1) * HEAD_DIM, :]
            if fixed_shift:
                e = jnp.exp2(s - shift)
                l_ref[h] += _sublane_partial_sum(e)
                acc_ref[h] += _dot(vt_g, e.astype(BF16))
            else:
                m_old = m_ref[h]
                m_new = jnp.maximum(m_old, jnp.max(s, axis=0, keepdims=True))
                alpha = jnp.exp2(m_old - m_new)
                e = jnp.exp2(s - m_new)
                m_ref[h] = m_new
                l_ref[h] = alpha * l_ref[h] + _sublane_partial_sum(e)
                acc_ref[h] = alpha * acc_ref[h] + _dot(vt_g, e.astype(BF16))

    k_first = keys(0)
    for h in range(n_heads):
        sa_ref[h] = _dot(k_first, qt_ref[h])

    bufs = (sa_ref, sb_ref)

    def steps(c0, last):
        for j in range(CHUNKS_PER_BODY):
            final = last and j == CHUNKS_PER_BODY - 1
            step(c0 + j, bufs[j % 2], None if final else bufs[(j + 1) % 2])

    def body(i, carry):
        steps(CHUNKS_PER_BODY * i, False)
        return carry

    lax.fori_loop(0, n_chunks // CHUNKS_PER_BODY - 1, body, 0)
    steps(n_chunks - CHUNKS_PER_BODY, True)
    for h in range(n_heads):
        o = acc_ref[h] / jnp.sum(l_ref[h], axis=0, keepdims=True)
        o_ref[h * HEAD_DIM:(h + 1) * HEAD_DIM, :] = o.astype(BF16)


def _merge_ffn_kernel(x_ref, at_ref, u_ref, vn_ref, gate_ref, ws_ref, bs_ref,
                      wpa_ref, wpb_ref, wo_ref, gf_ref, wgu_ref, wd_ref, gfin_ref,
                      o_ref, gm_ref):
    tm = x_ref.shape[0]
    d_model = x_ref.shape[1]
    d_ff = wd_ref.shape[0]
    lane = lax.broadcasted_iota(jnp.int32, (GM_CHUNK, LANES), 1)
    low = lane < GM_GROUP_DIM
    for c in range(tm // GM_CHUNK):
        rs = slice(c * GM_CHUNK, (c + 1) * GM_CHUNK)
        for j in range(GM_DIM // LANES):
            cs = slice(j * LANES, (j + 1) * LANES)
            vblk = vn_ref[rs, cs]
            sv = jnp.where(low, _dot(ws_ref[2 * j], vblk), _dot(ws_ref[2 * j + 1], vblk))
            gm_ref[rs, cs] = (u_ref[rs, cs] * (sv + bs_ref[:, cs])).astype(BF16)

    pa = lax.dot_general(at_ref[...], wpa_ref[...], (((0,), (0,)), ((), ())),
                         preferred_element_type=F32)
    pb = _dot(gm_ref[...], wpb_ref[...])
    mix = gate_ref[:, :d_model] * pa + gate_ref[:, d_model:] * pb
    x1 = x_ref[...] + _dot(mix.astype(BF16), wo_ref[...])

    ms = jnp.mean(x1 * x1, axis=-1, keepdims=True)
    h = (x1 * lax.rsqrt(ms + EPS) * gf_ref[...]).astype(BF16)
    gt = _dot(h, wgu_ref[:, :d_ff])
    up = _dot(h, wgu_ref[:, d_ff:])
    act = (jax.nn.silu(gt) * up).astype(BF16)
    x2 = x1 + _dot(act, wd_ref[...])

    ms2 = jnp.mean(x2 * x2, axis=-1, keepdims=True)
    o_ref[...] = x2 * lax.rsqrt(ms2 + EPS) * gfin_ref[...]


def _const_spec(shape):
    nd = len(shape)
    return pl.BlockSpec(shape, lambda *_: (0,) * nd, pipeline_mode=pl.Buffered(1))


def _rope_tables(seq):
    half = HEAD_DIM // 2
    rows = seq // GRID_W
    row = jnp.repeat(jnp.arange(rows, dtype=jnp.int32), GRID_W)
    col = jnp.tile(jnp.arange(GRID_W, dtype=jnp.int32), rows)
    inv_freq = 1.0 / (ROPE_THETA ** (jnp.arange(0, half, 2, dtype=F32) / half))
    ang_r = row.astype(F32)[:, None] * inv_freq[None, :]
    ang_c = col.astype(F32)[:, None] * inv_freq[None, :]
    cr, sr = jnp.cos(ang_r), jnp.sin(ang_r)
    cc, sc = jnp.cos(ang_c), jnp.sin(ang_c)
    cos64 = jnp.concatenate([cr, cr, cc, cc], axis=-1)
    sin64 = jnp.concatenate([-sr, sr, -sc, sc], axis=-1)
    return jnp.tile(cos64, (1, LANES // HEAD_DIM)), jnp.tile(sin64, (1, LANES // HEAD_DIM))


def kernel(x, norm_mix_g, w_in, q_norm_g, k_norm_g, gm_ln_g, gm_ln_b, w_s, b_s, w_proj_a,
           w_proj_b, w_out, norm_ffn_g, w_gate_up, w_down, norm_final_g):
    B, S, D = x.shape
    in_dim = w_in.shape[-1]
    d_ff = w_down.shape[1]
    assert norm_mix_g.shape[0] == 1, "single-layer block"
    assert CHUNKS_PER_BODY % 2 == 0 and S % (CHUNKS_PER_BODY * TK) == 0
    assert S % TQ == 0 and S % TM_IN == 0 and S % TM_OUT == 0
    assert TM_OUT % GM_CHUNK == 0 and TM_IN % TK == 0

    half_heads = N_Q_HEADS // 2
    perm = jnp.concatenate([
        jnp.arange(HEAD_DIM) + HEAD_DIM * h
        for j in range(half_heads) for h in (j, j + half_heads)])
    w_in_l = w_in[0]
    w_in_b = jnp.concatenate([w_in_l[:, :Q_DIM][:, perm], w_in_l[:, Q_DIM:]], axis=1).astype(BF16)

    cos_t, sin_t = _rope_tables(S)
    reps = LANES // HEAD_DIM
    qg = jnp.tile(q_norm_g[0], reps)[None, :]
    kg = jnp.tile(k_norm_g[0], reps)[None, :]
    lng = gm_ln_g[0].reshape(1, GM_DIM)
    lnb = gm_ln_b[0].reshape(1, GM_DIM)
    gid = jnp.arange(LANES) // HEAD_DIM
    gmat = (gid[:, None] == gid[None, :]).astype(BF16)
    gmat2 = jnp.concatenate([gmat, gmat], axis=0)

    two_axes = ("arbitrary", "arbitrary")
    cparams = pltpu.CompilerParams(dimension_semantics=two_axes,
                                   vmem_limit_bytes=VMEM_LIMIT_BYTES)

    def row_spec(tm, width):
        return pl.BlockSpec((None, tm, width), lambda b, i: (b, i, 0))

    M = B * S
    n_seq = S // TM_IN
    n_blk = M // TM_IN
    ck_in = TM_IN // TK

    def in_blk(i):
        return jnp.minimum(i, n_blk - 1)

    def out_blk(i):
        return jnp.maximum(i - 1, 0)

    def out_rows(width):
        return pl.BlockSpec((TM_IN, width), lambda i: (out_blk(i), 0))

    qt, k, vt, u, vn, gates = pl.pallas_call(
        _inproj_kernel,
        grid=(n_blk + 1,),
        in_specs=[
            pl.BlockSpec((TM_IN, D), lambda i: (in_blk(i), 0)),
            _const_spec((1, D)),
            _const_spec((D, in_dim)),
            _const_spec((2 * LANES, LANES)),
            pl.BlockSpec((TM_IN, LANES), lambda i: (out_blk(i) % n_seq, 0)),
            pl.BlockSpec((TM_IN, LANES), lambda i: (out_blk(i) % n_seq, 0)),
            _const_spec((1, LANES)),
            _const_spec((1, LANES)),
            _const_spec((1, GM_DIM)),
            _const_spec((1, GM_DIM)),
        ],
        out_specs=[
            pl.BlockSpec((None, N_Q_HEADS, LANES, TM_IN),
                         lambda i: (out_blk(i) // n_seq, 0, 0, out_blk(i) % n_seq)),
            out_rows(KV_DIM),
            pl.BlockSpec((ck_in, KV_DIM, TK), lambda i: (out_blk(i), 0, 0)),
            out_rows(GM_DIM), out_rows(GM_DIM), out_rows(2 * D),
        ],
        out_shape=[
            jax.ShapeDtypeStruct((B, N_Q_HEADS, LANES, S), BF16),
            jax.ShapeDtypeStruct((M, KV_DIM), BF16),
            jax.ShapeDtypeStruct((M // TK, KV_DIM, TK), BF16),
            jax.ShapeDtypeStruct((M, GM_DIM), F32),
            jax.ShapeDtypeStruct((M, GM_DIM), BF16),
            jax.ShapeDtypeStruct((M, 2 * D), F32),
        ],
        scratch_shapes=[pltpu.VMEM((TM_IN, in_dim), F32), pltpu.VMEM((TM_IN, in_dim), F32)],
        compiler_params=pltpu.CompilerParams(dimension_semantics=("arbitrary",),
                                             vmem_limit_bytes=VMEM_LIMIT_BYTES),
        name="inproj",
    )(x.reshape(M, D), norm_mix_g, w_in_b, gmat2, cos_t, sin_t, qg, kg, lng, lnb)
    k = k.reshape(B, S, KV_DIM)
    vt = vt.reshape(B, S // TK, KV_DIM, TK)
    u = u.reshape(B, S, GM_DIM)
    vn = vn.reshape(B, S, GM_DIM)
    gates = gates.reshape(B, S, 2 * D)

    def attention(fixed_shift):
        return pl.pallas_call(
            functools.partial(_attn_kernel, fixed_shift=fixed_shift),
            grid=(B, S // TQ),
            in_specs=[
                pl.BlockSpec(memory_space=pltpu.SMEM),
                pl.BlockSpec((None, N_Q_HEADS, LANES, TQ), lambda b, i: (b, 0, 0, i)),
                pl.BlockSpec((None, S, KV_DIM), lambda b, i: (b, 0, 0)),
                pl.BlockSpec((None, S // TK, KV_DIM, TK), lambda b, i: (b, 0, 0, 0)),
            ],
            out_specs=pl.BlockSpec((None, Q_DIM, TQ), lambda b, i: (b, 0, i)),
            out_shape=jax.ShapeDtypeStruct((B, Q_DIM, S), BF16),
            scratch_shapes=[
                pltpu.VMEM((N_Q_HEADS, TK, TQ), F32),
                pltpu.VMEM((N_Q_HEADS, TK, TQ), F32),
                pltpu.VMEM((N_Q_HEADS, 1, TQ), F32),
                pltpu.VMEM((N_Q_HEADS, SUBLANES, TQ), F32),
                pltpu.VMEM((N_Q_HEADS, HEAD_DIM, TQ), F32),
            ],
            compiler_params=cparams,
            name="gqa_attention_shift" if fixed_shift else "gqa_attention_online",
        )

    score_bound = (HEAD_DIM * QK_SCALE_LOG2 * (1.0 + 2.0 ** -6)
                   * jnp.max(jnp.abs(q_norm_g[0])) * jnp.max(jnp.abs(k_norm_g[0])))
    attn_t = lax.cond(
        score_bound <= MAX_FIXED_SHIFT,
        lambda s_, *ops: attention(True)(s_, *ops),
        lambda s_, *ops: attention(False)(s_, *ops),
        score_bound.reshape(1).astype(F32), qt, k, vt)

    bs_full = jnp.repeat(jnp.transpose(b_s[0]), GM_GROUP_DIM, axis=1)
    out = pl.pallas_call(
        _merge_ffn_kernel,
        grid=(B, S // TM_OUT),
        in_specs=[
            row_spec(TM_OUT, D),
            pl.BlockSpec((None, Q_DIM, TM_OUT), lambda b, i: (b, 0, i)),
            row_spec(TM_OUT, GM_DIM), row_spec(TM_OUT, GM_DIM), row_spec(TM_OUT, 2 * D),
            _const_spec((GM_GROUPS, GM_CHUNK, GM_CHUNK)),
            _const_spec((GM_CHUNK, GM_DIM)),
            _const_spec((Q_DIM, D)),
            _const_spec((GM_DIM, D)),
            _const_spec((D, D)),
            _const_spec((1, D)),
            _const_spec((D, 2 * d_ff)),
            _const_spec((d_ff, D)),
            _const_spec((1, D)),
        ],
        out_specs=row_spec(TM_OUT, D),
        out_shape=jax.ShapeDtypeStruct((B, S, D), F32),
        scratch_shapes=[pltpu.VMEM((TM_OUT, GM_DIM), BF16)],
        compiler_params=cparams,
        name="merge_ffn",
    )(x, attn_t, u, vn, gates, w_s[0].astype(BF16), bs_full,
      w_proj_a[0].astype(BF16), w_proj_b[0].astype(BF16), w_out[0].astype(BF16), norm_ffn_g,
      w_gate_up[0].astype(BF16), w_down[0].astype(BF16), norm_final_g[None, :])
    return out
```

```python
import functools
import math

import jax
import jax.numpy as jnp
from jax import lax
from jax.experimental import pallas as pl
from jax.experimental.pallas import tpu as pltpu

HEAD_DIM = 64
N_Q_HEADS = 8
N_KV_HEADS = 2
Q_DIM = N_Q_HEADS * HEAD_DIM
KV_DIM = N_KV_HEADS * HEAD_DIM
GM_GROUPS = 8
GM_GROUP_DIM = 64
GM_DIM = GM_GROUPS * GM_GROUP_DIM
GM_CHUNK = 128
GRID_W = 64
ROPE_THETA = 10000.0
EPS = 1e-6

LANES = 128
SUBLANES = 8
VMEM_LIMIT_BYTES = 56 * 1024 * 1024

TM_IN = 256
TQ = 256
TK = 256
QK_SCALE_LOG2 = math.log2(math.e) / math.sqrt(HEAD_DIM)
MAX_FIXED_SHIFT = 48.0
CHUNKS_PER_BODY = 8
TM_OUT = 256
FF_CHUNK = 768

F32 = jnp.float32
BF16 = jnp.bfloat16


def _dot(a, b):
    return jnp.dot(a, b, preferred_element_type=F32)


def _group_sum(x, gmat2):
    hi = x.astype(BF16)
    lo = (x - hi.astype(F32)).astype(BF16)
    return _dot(jnp.concatenate([hi, lo], axis=1), gmat2)


def _rope_partner(x, lane):
    fwd = pltpu.roll(x, LANES - 16, axis=1)
    bwd = pltpu.roll(x, 16, axis=1)
    return jnp.where((lane & 16) == 0, fwd, bwd)


def _inproj_kernel(x_ref, g_ref, w_ref, gmat_ref, cos_ref, sin_ref, qg_ref, kg_ref,
                   lng_ref, lnb_ref,
                   qt_ref, k_ref, vt_ref, u_ref, vn_ref, gate_ref, ya_ref, yb_ref):
    i = pl.program_id(0)
    tm = x_ref.shape[0]

    @pl.when(i == 0)
    def _():
        yb_ref[...] = jnp.zeros(yb_ref.shape, F32)

    def compute(y_w, y_r):
        x = x_ref[...]
        ms = jnp.mean(x * x, axis=-1, keepdims=True)
        h = (x * lax.rsqrt(ms + EPS) * g_ref[...]).astype(BF16)
        gmat = gmat_ref[...]
        lane = lax.broadcasted_iota(jnp.int32, (tm, LANES), 1)

        def project(c0, c1):
            y_w[:, c0:c1] = _dot(h, w_ref[:, c0:c1])

        v0 = Q_DIM + KV_DIM
        uv0 = Q_DIM + 2 * KV_DIM
        g0 = uv0 + 2 * GM_DIM
        g1 = g0 + (w_ref.shape[1] - g0) // 2

        n_qk = (Q_DIM + KV_DIM) // LANES
        yqk = [y_r[:, j * LANES:(j + 1) * LANES] for j in range(n_qk)]
        project(0, uv0)
        n_gate = w_ref.shape[1] - g0
        gate_ref[:, :n_gate // 2] = jax.nn.sigmoid(y_r[:, g0:g1])
        ss = [_group_sum(y * y, gmat) for y in yqk]
        project(uv0, g0)

        def norm_rope(y, s2, gain):
            y = y * lax.rsqrt(s2 * (1.0 / HEAD_DIM) + EPS) * gain
            return y * cos_ref[...] + _rope_partner(y, lane) * sin_ref[...]

        row = lax.broadcasted_iota(jnp.int32, (LANES, tm), 0)
        top = row < HEAD_DIM
        half_heads = N_Q_HEADS // 2
        for j in range(Q_DIM // LANES):
            qt = (norm_rope(yqk[j], ss[j], qg_ref[...]) * QK_SCALE_LOG2).T
            zero = jnp.zeros_like(qt)
            qt_ref[j] = jnp.where(top, qt, zero).astype(BF16)
            qt_ref[j + half_heads] = jnp.where(top, zero, qt).astype(BF16)
        k_ref[...] = norm_rope(yqk[n_qk - 1], ss[n_qk - 1], kg_ref[...]).astype(BF16)
        vt = y_r[:, v0:uv0].T.astype(BF16)
        for c in range(tm // TK):
            vt_ref[c] = vt[:, c * TK:(c + 1) * TK]

        gate_ref[:, n_gate // 2:] = jax.nn.sigmoid(y_r[:, g1:])
        u_ref[...] = jax.nn.gelu(y_r[:, uv0:uv0 + GM_DIM])
        n_gm = GM_DIM // LANES
        vv = [jax.nn.gelu(y_r[:, uv0 + GM_DIM + j * LANES:uv0 + GM_DIM + (j + 1) * LANES])
              for j in range(n_gm)]
        project(g0, g1)
        mu = [_group_sum(v, gmat) * (1.0 / GM_GROUP_DIM) for v in vv]
        gq = g1 + (w_ref.shape[1] - g1) // 2
        project(g1, gq)
        dv = [v - m for v, m in zip(vv, mu)]
        var = [_group_sum(d * d, gmat) * (1.0 / GM_GROUP_DIM) for d in dv]
        project(gq, w_ref.shape[1])
        for j in range(n_gm):
            sl = slice(j * LANES, (j + 1) * LANES)
            vn = dv[j] * lax.rsqrt(var[j] + EPS) * lng_ref[:, sl] + lnb_ref[:, sl]
            vn_ref[:, sl] = vn.astype(BF16)

    @pl.when(i % 2 == 0)
    def _():
        compute(ya_ref, yb_ref)

    @pl.when(i % 2 == 1)
    def _():
        compute(yb_ref, ya_ref)


def _sublane_partial_sum(e):
    rows, lanes = e.shape
    return jnp.sum(e.reshape(rows // SUBLANES, SUBLANES, lanes), axis=0)


def _attn_kernel(shift_ref, qt_ref, k_ref, vt_ref, o_ref, sa_ref, sb_ref, m_ref, l_ref, acc_ref,
                 *, fixed_shift):
    n_heads = qt_ref.shape[0]
    group = n_heads // N_KV_HEADS
    n_chunks = k_ref.shape[0] // TK
    if fixed_shift:
        shift = shift_ref[0]
    else:
        m_ref[...] = jnp.full(m_ref.shape, -jnp.inf, F32)
    acc_ref[...] = jnp.zeros(acc_ref.shape, F32)
    l_ref[...] = jnp.zeros(l_ref.shape, F32)

    def keys(c):
        return k_ref[pl.ds(pl.multiple_of(c * TK, TK), TK), :]

    def step(c, cur_ref, next_ref):
        if next_ref is not None:
            k_next = keys(c + 1)
        vt_c = vt_ref[c]
        for h in range(n_heads):
            g = h // group
            if next_ref is not None:
                next_ref[h] = _dot(k_next, qt_ref[h])
            s = cur_ref[h]
            vt_g = vt_c[g * HEAD_DIM:(g + 1) * HEAD_DIM, :]
            if fixed_shift:
                e = jnp.exp2(s - shift)
                l_ref[h] += _sublane_partial_sum(e)
                acc_ref[h] += _dot(vt_g, e.astype(BF16))
            else:
                m_old = m_ref[h]
                m_new = jnp.maximum(m_old, jnp.max(s, axis=0, keepdims=True))
                alpha = jnp.exp2(m_old - m_new)
                e = jnp.exp2(s - m_new)
                m_ref[h] = m_new
                l_ref[h] = alpha * l_ref[h] + _sublane_partial_sum(e)
                acc_ref[h] = alpha * acc_ref[h] + _dot(vt_g, e.astype(BF16))

    k_first = keys(0)
    for h in range(n_heads):
        sa_ref[h] = _dot(k_first, qt_ref[h])

    bufs = (sa_ref, sb_ref)

    def steps(c0, last):
        for j in range(CHUNKS_PER_BODY):
            final = last and j == CHUNKS_PER_BODY - 1
            step(c0 + j, bufs[j % 2], None if final else bufs[(j + 1) % 2])

    def body(i, carry):
        steps(CHUNKS_PER_BODY * i, False)
        return carry

    lax.fori_loop(0, n_chunks // CHUNKS_PER_BODY - 1, body, 0)
    steps(n_chunks - CHUNKS_PER_BODY, True)
    for h in range(n_heads):
        o = acc_ref[h] / jnp.sum(l_ref[h], axis=0, keepdims=True)
        o_ref[h * HEAD_DIM:(h + 1) * HEAD_DIM, :] = o.astype(BF16)


def _merge_ffn_kernel(x_ref, at_ref, u_ref, vn_ref, gate_ref, ws_ref, bs_ref,
                      wpa_ref, wpb_ref, wo_ref, gf_ref, wgu_ref, wd_ref, gfin_ref,
                      o_ref, gm_ref, act_ref, xa_ref, xb_ref):
    i = pl.program_id(0)
    tm = x_ref.shape[0]
    d_model = x_ref.shape[1]
    d_ff = wd_ref.shape[0]

    @pl.when(i == 0)
    def _():
        xb_ref[...] = jnp.zeros(xb_ref.shape, F32)

    def compute(x1_w, x1_r):
        x1p = x1_r[...]
        ms = jnp.mean(x1p * x1p, axis=-1, keepdims=True)
        h = (x1p * lax.rsqrt(ms + EPS) * gf_ref[...]).astype(BF16)

        lane = lax.broadcasted_iota(jnp.int32, (GM_CHUNK, LANES), 1)
        low = lane < GM_GROUP_DIM
        for c in range(tm // GM_CHUNK):
            rs = slice(c * GM_CHUNK, (c + 1) * GM_CHUNK)
            for j in range(GM_DIM // LANES):
                cs = slice(j * LANES, (j + 1) * LANES)
                vblk = vn_ref[rs, cs]
                zero = jnp.zeros_like(vblk)
                vsplit = jnp.concatenate([jnp.where(low, vblk, zero), jnp.where(low, zero, vblk)],
                                         axis=0)
                sv = _dot(ws_ref[j], vsplit)
                gm_ref[rs, cs] = (u_ref[rs, cs] * (sv + bs_ref[:, cs])).astype(BF16)

        pa = lax.dot_general(at_ref[...], wpa_ref[...], (((0,), (0,)), ((), ())),
                             preferred_element_type=F32)
        pb = _dot(gm_ref[...], wpb_ref[...])
        mix = (gate_ref[:, :d_model] * pa + gate_ref[:, d_model:] * pb).astype(BF16)

        for c0 in range(0, d_ff, FF_CHUNK):
            c1 = min(c0 + FF_CHUNK, d_ff)
            gt = _dot(h, wgu_ref[:, c0:c1])
            up = _dot(h, wgu_ref[:, d_ff + c0:d_ff + c1])
            act_ref[:, c0:c1] = (jax.nn.silu(gt) * up).astype(BF16)
        x2 = x1p + _dot(act_ref[...], wd_ref[...])
        ms2 = jnp.mean(x2 * x2, axis=-1, keepdims=True)
        o_ref[...] = x2 * lax.rsqrt(ms2 + EPS) * gfin_ref[...]

        x1_w[...] = x_ref[...] + _dot(mix, wo_ref[...])

    @pl.when(i % 2 == 0)
    def _():
        compute(xa_ref, xb_ref)

    @pl.when(i % 2 == 1)
    def _():
        compute(xb_ref, xa_ref)


def _const_spec(shape):
    nd = len(shape)
    return pl.BlockSpec(shape, lambda *_: (0,) * nd, pipeline_mode=pl.Buffered(1))


def _rope_tables(seq):
    half = HEAD_DIM // 2
    rows = seq // GRID_W
    row = jnp.repeat(jnp.arange(rows, dtype=jnp.int32), GRID_W)
    col = jnp.tile(jnp.arange(GRID_W, dtype=jnp.int32), rows)
    inv_freq = 1.0 / (ROPE_THETA ** (jnp.arange(0, half, 2, dtype=F32) / half))
    ang_r = row.astype(F32)[:, None] * inv_freq[None, :]
    ang_c = col.astype(F32)[:, None] * inv_freq[None, :]
    cr, sr = jnp.cos(ang_r), jnp.sin(ang_r)
    cc, sc = jnp.cos(ang_c), jnp.sin(ang_c)
    cos64 = jnp.concatenate([cr, cr, cc, cc], axis=-1)
    sin64 = jnp.concatenate([-sr, sr, -sc, sc], axis=-1)
    return jnp.tile(cos64, (1, LANES // HEAD_DIM)), jnp.tile(sin64, (1, LANES // HEAD_DIM))


def kernel(x, norm_mix_g, w_in, q_norm_g, k_norm_g, gm_ln_g, gm_ln_b, w_s, b_s, w_proj_a,
           w_proj_b, w_out, norm_ffn_g, w_gate_up, w_down, norm_final_g):
    B, S, D = x.shape
    in_dim = w_in.shape[-1]
    d_ff = w_down.shape[1]
    assert norm_mix_g.shape[0] == 1, "single-layer block"
    assert CHUNKS_PER_BODY % 2 == 0 and S % (CHUNKS_PER_BODY * TK) == 0
    assert S % TQ == 0 and S % TM_IN == 0 and S % TM_OUT == 0
    assert TM_OUT % GM_CHUNK == 0 and TM_IN % TK == 0

    half_heads = N_Q_HEADS // 2
    perm = jnp.concatenate([
        jnp.arange(HEAD_DIM) + HEAD_DIM * h
        for j in range(half_heads) for h in (j, j + half_heads)])
    w_in_l = w_in[0]
    w_in_b = jnp.concatenate([w_in_l[:, :Q_DIM][:, perm], w_in_l[:, Q_DIM:]], axis=1).astype(BF16)

    cos_t, sin_t = _rope_tables(S)
    reps = LANES // HEAD_DIM
    qg = jnp.tile(q_norm_g[0], reps)[None, :]
    kg = jnp.tile(k_norm_g[0], reps)[None, :]
    lng = gm_ln_g[0].reshape(1, GM_DIM)
    lnb = gm_ln_b[0].reshape(1, GM_DIM)
    gid = jnp.arange(LANES) // HEAD_DIM
    gmat = (gid[:, None] == gid[None, :]).astype(BF16)
    gmat2 = jnp.concatenate([gmat, gmat], axis=0)

    two_axes = ("arbitrary", "arbitrary")
    cparams = pltpu.CompilerParams(dimension_semantics=two_axes,
                                   vmem_limit_bytes=VMEM_LIMIT_BYTES)

    M = B * S
    n_seq = S // TM_IN
    n_blk = M // TM_IN
    ck_in = TM_IN // TK

    def in_blk(i):
        return jnp.minimum(i, n_blk - 1)

    def out_blk(i):
        return jnp.maximum(i - 1, 0)

    def out_rows(width):
        return pl.BlockSpec((TM_IN, width), lambda i: (out_blk(i), 0))

    qt, k, vt, u, vn, gates = pl.pallas_call(
        _inproj_kernel,
        grid=(n_blk + 1,),
        in_specs=[
            pl.BlockSpec((TM_IN, D), lambda i: (in_blk(i), 0)),
            _const_spec((1, D)),
            _const_spec((D, in_dim)),
            _const_spec((2 * LANES, LANES)),
            pl.BlockSpec((TM_IN, LANES), lambda i: (out_blk(i) % n_seq, 0)),
            pl.BlockSpec((TM_IN, LANES), lambda i: (out_blk(i) % n_seq, 0)),
            _const_spec((1, LANES)),
            _const_spec((1, LANES)),
            _const_spec((1, GM_DIM)),
            _const_spec((1, GM_DIM)),
        ],
        out_specs=[
            pl.BlockSpec((None, N_Q_HEADS, LANES, TM_IN),
                         lambda i: (out_blk(i) // n_seq, 0, 0, out_blk(i) % n_seq)),
            out_rows(KV_DIM),
            pl.BlockSpec((ck_in, KV_DIM, TK), lambda i: (out_blk(i), 0, 0)),
            out_rows(GM_DIM), out_rows(GM_DIM), out_rows(2 * D),
        ],
        out_shape=[
            jax.ShapeDtypeStruct((B, N_Q_HEADS, LANES, S), BF16),
            jax.ShapeDtypeStruct((M, KV_DIM), BF16),
            jax.ShapeDtypeStruct((M // TK, KV_DIM, TK), BF16),
            jax.ShapeDtypeStruct((M, GM_DIM), F32),
            jax.ShapeDtypeStruct((M, GM_DIM), BF16),
            jax.ShapeDtypeStruct((M, 2 * D), F32),
        ],
        scratch_shapes=[pltpu.VMEM((TM_IN, in_dim), F32), pltpu.VMEM((TM_IN, in_dim), F32)],
        compiler_params=pltpu.CompilerParams(dimension_semantics=("arbitrary",),
                                             vmem_limit_bytes=VMEM_LIMIT_BYTES),
        name="inproj",
    )(x.reshape(M, D), norm_mix_g, w_in_b, gmat2, cos_t, sin_t, qg, kg, lng, lnb)
    k = k.reshape(B, S, KV_DIM)
    vt = vt.reshape(B, S // TK, KV_DIM, TK)

    def attention(fixed_shift):
        return pl.pallas_call(
            functools.partial(_attn_kernel, fixed_shift=fixed_shift),
            grid=(B, S // TQ),
            in_specs=[
                pl.BlockSpec(memory_space=pltpu.SMEM),
                pl.BlockSpec((None, N_Q_HEADS, LANES, TQ), lambda b, i: (b, 0, 0, i)),
                pl.BlockSpec((None, S, KV_DIM), lambda b, i: (b, 0, 0)),
                pl.BlockSpec((None, S // TK, KV_DIM, TK), lambda b, i: (b, 0, 0, 0)),
            ],
            out_specs=pl.BlockSpec((None, Q_DIM, TQ), lambda b, i: (b, 0, i)),
            out_shape=jax.ShapeDtypeStruct((B, Q_DIM, S), BF16),
            scratch_shapes=[
                pltpu.VMEM((N_Q_HEADS, TK, TQ), F32),
                pltpu.VMEM((N_Q_HEADS, TK, TQ), F32),
                pltpu.VMEM((N_Q_HEADS, 1, TQ), F32),
                pltpu.VMEM((N_Q_HEADS, SUBLANES, TQ), F32),
                pltpu.VMEM((N_Q_HEADS, HEAD_DIM, TQ), F32),
            ],
            compiler_params=cparams,
            name="gqa_attention_shift" if fixed_shift else "gqa_attention_online",
        )

    score_bound = (HEAD_DIM * QK_SCALE_LOG2 * (1.0 + 2.0 ** -6)
                   * jnp.max(jnp.abs(q_norm_g[0])) * jnp.max(jnp.abs(k_norm_g[0])))
    attn_t = lax.cond(
        score_bound <= MAX_FIXED_SHIFT,
        lambda s_, *ops: attention(True)(s_, *ops),
        lambda s_, *ops: attention(False)(s_, *ops),
        score_bound.reshape(1).astype(F32), qt, k, vt)

    bs_full = jnp.repeat(jnp.transpose(b_s[0]), GM_GROUP_DIM, axis=1)
    ws_pairs = (w_s[0].reshape(GM_GROUPS // 2, 2, GM_CHUNK, GM_CHUNK)
                .transpose(0, 2, 1, 3).reshape(GM_GROUPS // 2, GM_CHUNK, 2 * GM_CHUNK).astype(BF16))
    n_seq_out = S // TM_OUT
    n_out = M // TM_OUT

    def s1_blk(i):
        return jnp.minimum(i, n_out - 1)

    def s1_rows(width):
        return pl.BlockSpec((TM_OUT, width), lambda i: (s1_blk(i), 0))

    out = pl.pallas_call(
        _merge_ffn_kernel,
        grid=(n_out + 1,),
        in_specs=[
            s1_rows(D),
            pl.BlockSpec((None, Q_DIM, TM_OUT),
                         lambda i: (s1_blk(i) // n_seq_out, 0, s1_blk(i) % n_seq_out)),
            s1_rows(GM_DIM), s1_rows(GM_DIM), s1_rows(2 * D),
            _const_spec((GM_GROUPS // 2, GM_CHUNK, 2 * GM_CHUNK)),
            _const_spec((GM_CHUNK, GM_DIM)),
            _const_spec((Q_DIM, D)),
            _const_spec((GM_DIM, D)),
            _const_spec((D, D)),
            _const_spec((1, D)),
            _const_spec((D, 2 * d_ff)),
            _const_spec((d_ff, D)),
            _const_spec((1, D)),
        ],
        out_specs=pl.BlockSpec((TM_OUT, D), lambda i: (jnp.maximum(i - 1, 0), 0)),
        out_shape=jax.ShapeDtypeStruct((M, D), F32),
        scratch_shapes=[
            pltpu.VMEM((TM_OUT, GM_DIM), BF16),
            pltpu.VMEM((TM_OUT, d_ff), BF16),
            pltpu.VMEM((TM_OUT, D), F32),
            pltpu.VMEM((TM_OUT, D), F32),
        ],
        compiler_params=pltpu.CompilerParams(dimension_semantics=("arbitrary",),
                                             vmem_limit_bytes=VMEM_LIMIT_BYTES),
        name="merge_ffn",
    )(x.reshape(M, D), attn_t, u, vn, gates, ws_pairs, bs_full,
      w_proj_a[0].astype(BF16), w_proj_b[0].astype(BF16), w_out[0].astype(BF16), norm_ffn_g,
      w_gate_up[0].astype(BF16), w_down[0].astype(BF16), norm_final_g[None, :])
    return out.reshape(B, S, D)
```

```python
import functools
import math

import jax
import jax.numpy as jnp
from jax import lax
from jax.experimental import pallas as pl
from jax.experimental.pallas import tpu as pltpu

HEAD_DIM = 64
N_Q_HEADS = 8
N_KV_HEADS = 2
Q_DIM = N_Q_HEADS * HEAD_DIM
KV_DIM = N_KV_HEADS * HEAD_DIM
GM_GROUPS = 8
GM_GROUP_DIM = 64
GM_DIM = GM_GROUPS * GM_GROUP_DIM
GM_CHUNK = 128
GRID_W = 64
ROPE_THETA = 10000.0
EPS = 1e-6

LANES = 128
SUBLANES = 8
VMEM_LIMIT_BYTES = 56 * 1024 * 1024

TM_IN = 256
TQ = 256
TK = 256
QK_SCALE_LOG2 = math.log2(math.e) / math.sqrt(HEAD_DIM)
MAX_FIXED_SHIFT = 48.0
CHUNKS_PER_BODY = 16
TM_OUT = 256
FF_CHUNK = 768

F32 = jnp.float32
BF16 = jnp.bfloat16


def _dot(a, b):
    return jnp.dot(a, b, preferred_element_type=F32)


def _group_sum(x, gmat2):
    hi = x.astype(BF16)
    lo = (x - hi.astype(F32)).astype(BF16)
    return _dot(jnp.concatenate([hi, lo], axis=1), gmat2)


def _rope_partner(x, lane):
    fwd = pltpu.roll(x, LANES - 16, axis=1)
    bwd = pltpu.roll(x, 16, axis=1)
    return jnp.where((lane & 16) == 0, fwd, bwd)


def _inproj_kernel(x_ref, g_ref, wq_ref, w_ref, gmat_ref, cos_ref, sin_ref, qg_ref, kg_ref,
                   lng_ref, lnb_ref,
                   qt_ref, k_ref, vt_ref, u_ref, vn_ref, gate_ref, ya_ref, yb_ref):
    i = pl.program_id(0)
    tm = x_ref.shape[0]

    @pl.when(i == 0)
    def _():
        yb_ref[...] = jnp.zeros(yb_ref.shape, F32)

    def compute(y_w, y_r):
        x = x_ref[...]
        ms = jnp.mean(x * x, axis=-1, keepdims=True)
        h = (x * lax.rsqrt(ms + EPS) * g_ref[...]).astype(BF16)
        gmat = gmat_ref[...]
        lane = lax.broadcasted_iota(jnp.int32, (tm, LANES), 1)

        def project(c0, c1):
            y_w[:, c0:c1] = _dot(h, w_ref[:, c0:c1])

        y_w[:, :Q_DIM] = _dot(h, wq_ref[...])

        v0 = Q_DIM + KV_DIM
        uv0 = Q_DIM + 2 * KV_DIM
        g0 = uv0 + 2 * GM_DIM
        g1 = g0 + (w_ref.shape[1] - g0) // 2

        n_qk = (Q_DIM + KV_DIM) // LANES
        yqk = [y_r[:, j * LANES:(j + 1) * LANES] for j in range(n_qk)]
        project(Q_DIM, uv0)
        n_gate = w_ref.shape[1] - g0
        gate_ref[:, :n_gate // 2] = jax.nn.sigmoid(y_r[:, g0:g1])
        ss = [_group_sum(y * y, gmat) for y in yqk]
        project(uv0, g0)

        def norm_rope(y, s2, gain):
            y = y * lax.rsqrt(s2 * (1.0 / HEAD_DIM) + EPS) * gain
            return y * cos_ref[...] + _rope_partner(y, lane) * sin_ref[...]

        row = lax.broadcasted_iota(jnp.int32, (LANES, tm), 0)
        top = row < HEAD_DIM
        half_heads = N_Q_HEADS // 2
        for j in range(Q_DIM // LANES):
            qt = (norm_rope(yqk[j], ss[j], qg_ref[...]) * QK_SCALE_LOG2).T
            zero = jnp.zeros_like(qt)
            qt_ref[j] = jnp.where(top, qt, zero).astype(BF16)
            qt_ref[j + half_heads] = jnp.where(top, zero, qt).astype(BF16)
        k_ref[...] = norm_rope(yqk[n_qk - 1], ss[n_qk - 1], kg_ref[...]).astype(BF16)
        vt = y_r[:, v0:uv0].T.astype(BF16)
        for c in range(tm // TK):
            vt_ref[c] = vt[:, c * TK:(c + 1) * TK]

        gate_ref[:, n_gate // 2:] = jax.nn.sigmoid(y_r[:, g1:])
        u_ref[...] = jax.nn.gelu(y_r[:, uv0:uv0 + GM_DIM])
        n_gm = GM_DIM // LANES
        vv = [jax.nn.gelu(y_r[:, uv0 + GM_DIM + j * LANES:uv0 + GM_DIM + (j + 1) * LANES])
              for j in range(n_gm)]
        project(g0, g1)
        mu = [_group_sum(v, gmat) * (1.0 / GM_GROUP_DIM) for v in vv]
        gq = g1 + (w_ref.shape[1] - g1) // 2
        project(g1, gq)
        dv = [v - m for v, m in zip(vv, mu)]
        var = [_group_sum(d * d, gmat) * (1.0 / GM_GROUP_DIM) for d in dv]
        project(gq, w_ref.shape[1])
        for j in range(n_gm):
            sl = slice(j * LANES, (j + 1) * LANES)
            vn = dv[j] * lax.rsqrt(var[j] + EPS) * lng_ref[:, sl] + lnb_ref[:, sl]
            vn_ref[:, sl] = vn.astype(BF16)

    @pl.when(i % 2 == 0)
    def _():
        compute(ya_ref, yb_ref)

    @pl.when(i % 2 == 1)
    def _():
        compute(yb_ref, ya_ref)


def _sublane_partial_sum(e):
    rows, lanes = e.shape
    return jnp.sum(e.reshape(rows // SUBLANES, SUBLANES, lanes), axis=0)


def _attn_kernel(shift_ref, qt_ref, k_ref, vt_ref, o_ref, sa_ref, sb_ref, m_ref, l_ref, acc_ref,
                 *, fixed_shift):
    n_heads = qt_ref.shape[0]
    group = n_heads // N_KV_HEADS
    n_chunks = k_ref.shape[0] // TK
    if fixed_shift:
        shift = shift_ref[0]
    else:
        m_ref[...] = jnp.full(m_ref.shape, -jnp.inf, F32)
    acc_ref[...] = jnp.zeros(acc_ref.shape, F32)
    l_ref[...] = jnp.zeros(l_ref.shape, F32)

    def keys(c):
        return k_ref[pl.ds(pl.multiple_of(c * TK, TK), TK), :]

    def step(c, cur_ref, next_ref):
        if next_ref is not None:
            k_next = keys(c + 1)
        vt_c = vt_ref[c]
        for h in range(n_heads):
            g = h // group
            if next_ref is not None:
                next_ref[h] = _dot(k_next, qt_ref[h])
            s = cur_ref[h]
            vt_g = vt_c[g * HEAD_DIM:(g + 1) * HEAD_DIM, :]
            if fixed_shift:
                e = jnp.exp2(s - shift)
                l_ref[h] += _sublane_partial_sum(e)
                acc_ref[h] += _dot(vt_g, e.astype(BF16))
            else:
                m_old = m_ref[h]
                m_new = jnp.maximum(m_old, jnp.max(s, axis=0, keepdims=True))
                alpha = jnp.exp2(m_old - m_new)
                e = jnp.exp2(s - m_new)
                m_ref[h] = m_new
                l_ref[h] = alpha * l_ref[h] + _sublane_partial_sum(e)
                acc_ref[h] = alpha * acc_ref[h] + _dot(vt_g, e.astype(BF16))

    k_first = keys(0)
    for h in range(n_heads):
        sa_ref[h] = _dot(k_first, qt_ref[h])

    bufs = (sa_ref, sb_ref)

    def steps(c0, last):
        for j in range(CHUNKS_PER_BODY):
            final = last and j == CHUNKS_PER_BODY - 1
            step(c0 + j, bufs[j % 2], None if final else bufs[(j + 1) % 2])

    def body(i, carry):
        steps(CHUNKS_PER_BODY * i, False)
        return carry

    lax.fori_loop(0, n_chunks // CHUNKS_PER_BODY - 1, body, 0)
    steps(n_chunks - CHUNKS_PER_BODY, True)
    for h in range(n_heads):
        o = acc_ref[h] / jnp.sum(l_ref[h], axis=0, keepdims=True)
        o_ref[h * HEAD_DIM:(h + 1) * HEAD_DIM, :] = o.astype(BF16)


def _merge_ffn_kernel(x_ref, at_ref, u_ref, vn_ref, gate_ref, ws_ref, bs_ref,
                      wpa_ref, wpb_ref, wo_ref, gf_ref, wgu_ref, wd_ref, gfin_ref,
                      o_ref, gm_ref, act_ref, xa_ref, xb_ref):
    i = pl.program_id(0)
    tm = x_ref.shape[0]
    d_model = x_ref.shape[1]
    d_ff = wd_ref.shape[0]

    @pl.when(i == 0)
    def _():
        xb_ref[...] = jnp.zeros(xb_ref.shape, F32)

    def compute(x1_w, x1_r):
        x1p = x1_r[...]
        ms = jnp.mean(x1p * x1p, axis=-1, keepdims=True)
        h = (x1p * lax.rsqrt(ms + EPS) * gf_ref[...]).astype(BF16)

        lane = lax.broadcasted_iota(jnp.int32, (GM_CHUNK, LANES), 1)
        low = lane < GM_GROUP_DIM
        for c in range(tm // GM_CHUNK):
            rs = slice(c * GM_CHUNK, (c + 1) * GM_CHUNK)
            for j in range(GM_DIM // LANES):
                cs = slice(j * LANES, (j + 1) * LANES)
                vblk = vn_ref[rs, cs]
                zero = jnp.zeros_like(vblk)
                vsplit = jnp.concatenate([jnp.where(low, vblk, zero), jnp.where(low, zero, vblk)],
                                         axis=0)
                sv = _dot(ws_ref[j], vsplit)
                gm_ref[rs, cs] = (u_ref[rs, cs] * (sv + bs_ref[:, cs])).astype(BF16)

        pa = lax.dot_general(at_ref[...], wpa_ref[...], (((0,), (0,)), ((), ())),
                             preferred_element_type=F32)
        pb = _dot(gm_ref[...], wpb_ref[...])
        mix = (gate_ref[:, :d_model] * pa + gate_ref[:, d_model:] * pb).astype(BF16)

        for c0 in range(0, d_ff, FF_CHUNK):
            c1 = min(c0 + FF_CHUNK, d_ff)
            gt = _dot(h, wgu_ref[:, c0:c1])
            up = _dot(h, wgu_ref[:, d_ff + c0:d_ff + c1])
            act_ref[:, c0:c1] = (jax.nn.silu(gt) * up).astype(BF16)
        x2 = x1p + _dot(act_ref[...], wd_ref[...])
        ms2 = jnp.mean(x2 * x2, axis=-1, keepdims=True)
        o_ref[...] = x2 * lax.rsqrt(ms2 + EPS) * gfin_ref[...]

        x1_w[...] = x_ref[...] + _dot(mix, wo_ref[...])

    @pl.when(i % 2 == 0)
    def _():
        compute(xa_ref, xb_ref)

    @pl.when(i % 2 == 1)
    def _():
        compute(xb_ref, xa_ref)


def _const_spec(shape):
    nd = len(shape)
    return pl.BlockSpec(shape, lambda *_: (0,) * nd, pipeline_mode=pl.Buffered(1))


def _rope_tables(seq):
    half = HEAD_DIM // 2
    rows = seq // GRID_W
    row = jnp.repeat(jnp.arange(rows, dtype=jnp.int32), GRID_W)
    col = jnp.tile(jnp.arange(GRID_W, dtype=jnp.int32), rows)
    inv_freq = 1.0 / (ROPE_THETA ** (jnp.arange(0, half, 2, dtype=F32) / half))
    ang_r = row.astype(F32)[:, None] * inv_freq[None, :]
    ang_c = col.astype(F32)[:, None] * inv_freq[None, :]
    cr, sr = jnp.cos(ang_r), jnp.sin(ang_r)
    cc, sc = jnp.cos(ang_c), jnp.sin(ang_c)
    cos64 = jnp.concatenate([cr, cr, cc, cc], axis=-1)
    sin64 = jnp.concatenate([-sr, sr, -sc, sc], axis=-1)
    return jnp.tile(cos64, (1, LANES // HEAD_DIM)), jnp.tile(sin64, (1, LANES // HEAD_DIM))


def kernel(x, norm_mix_g, w_in, q_norm_g, k_norm_g, gm_ln_g, gm_ln_b, w_s, b_s, w_proj_a,
           w_proj_b, w_out, norm_ffn_g, w_gate_up, w_down, norm_final_g):
    B, S, D = x.shape
    in_dim = w_in.shape[-1]
    d_ff = w_down.shape[1]
    assert norm_mix_g.shape[0] == 1, "single-layer block"
    assert CHUNKS_PER_BODY % 2 == 0 and S % (CHUNKS_PER_BODY * TK) == 0
    assert S % TQ == 0 and S % TM_IN == 0 and S % TM_OUT == 0
    assert TM_OUT % GM_CHUNK == 0 and TM_IN % TK == 0

    w_in_b = w_in[0].astype(BF16)
    half_heads = N_Q_HEADS // 2
    w_q_b = (w_in_b[:, :Q_DIM].reshape(D, 2, half_heads, HEAD_DIM)
             .transpose(0, 2, 1, 3).reshape(D, Q_DIM))

    cos_t, sin_t = _rope_tables(S)
    reps = LANES // HEAD_DIM
    qg = jnp.tile(q_norm_g[0], reps)[None, :]
    kg = jnp.tile(k_norm_g[0], reps)[None, :]
    lng = gm_ln_g[0].reshape(1, GM_DIM)
    lnb = gm_ln_b[0].reshape(1, GM_DIM)
    gid = jnp.arange(LANES) // HEAD_DIM
    gmat = (gid[:, None] == gid[None, :]).astype(BF16)
    gmat2 = jnp.concatenate([gmat, gmat], axis=0)

    two_axes = ("arbitrary", "arbitrary")
    cparams = pltpu.CompilerParams(dimension_semantics=two_axes,
                                   vmem_limit_bytes=VMEM_LIMIT_BYTES)

    M = B * S
    n_seq = S // TM_IN
    n_blk = M // TM_IN
    ck_in = TM_IN // TK

    def in_blk(i):
        return jnp.minimum(i, n_blk - 1)

    def out_blk(i):
        return jnp.maximum(i - 1, 0)

    def out_rows(width):
        return pl.BlockSpec((TM_IN, width), lambda i: (out_blk(i), 0))

    qt, k, vt, u, vn, gates = pl.pallas_call(
        _inproj_kernel,
        grid=(n_blk + 1,),
        in_specs=[
            pl.BlockSpec((TM_IN, D), lambda i: (in_blk(i), 0)),
            _const_spec((1, D)),
            _const_spec((D, Q_DIM)),
            _const_spec((D, in_dim)),
            _const_spec((2 * LANES, LANES)),
            pl.BlockSpec((TM_IN, LANES), lambda i: (out_blk(i) % n_seq, 0)),
            pl.BlockSpec((TM_IN, LANES), lambda i: (out_blk(i) % n_seq, 0)),
            _const_spec((1, LANES)),
            _const_spec((1, LANES)),
            _const_spec((1, GM_DIM)),
            _const_spec((1, GM_DIM)),
        ],
        out_specs=[
            pl.BlockSpec((None, N_Q_HEADS, LANES, TM_IN),
                         lambda i: (out_blk(i) // n_seq, 0, 0, out_blk(i) % n_seq)),
            out_rows(KV_DIM),
            pl.BlockSpec((ck_in, KV_DIM, TK), lambda i: (out_blk(i), 0, 0)),
            out_rows(GM_DIM), out_rows(GM_DIM), out_rows(2 * D),
        ],
        out_shape=[
            jax.ShapeDtypeStruct((B, N_Q_HEADS, LANES, S), BF16),
            jax.ShapeDtypeStruct((M, KV_DIM), BF16),
            jax.ShapeDtypeStruct((M // TK, KV_DIM, TK), BF16),
            jax.ShapeDtypeStruct((M, GM_DIM), F32),
            jax.ShapeDtypeStruct((M, GM_DIM), BF16),
            jax.ShapeDtypeStruct((M, 2 * D), F32),
        ],
        scratch_shapes=[pltpu.VMEM((TM_IN, in_dim), F32), pltpu.VMEM((TM_IN, in_dim), F32)],
        compiler_params=pltpu.CompilerParams(dimension_semantics=("arbitrary",),
                                             vmem_limit_bytes=VMEM_LIMIT_BYTES),
        name="inproj",
    )(x.reshape(M, D), norm_mix_g, w_q_b, w_in_b, gmat2, cos_t, sin_t, qg, kg, lng, lnb)
    k = k.reshape(B, S, KV_DIM)
    vt = vt.reshape(B, S // TK, KV_DIM, TK)

    def attention(fixed_shift):
        return pl.pallas_call(
            functools.partial(_attn_kernel, fixed_shift=fixed_shift),
            grid=(B, S // TQ),
            in_specs=[
                pl.BlockSpec(memory_space=pltpu.SMEM),
                pl.BlockSpec((None, N_Q_HEADS, LANES, TQ), lambda b, i: (b, 0, 0, i)),
                pl.BlockSpec((None, S, KV_DIM), lambda b, i: (b, 0, 0)),
                pl.BlockSpec((None, S // TK, KV_DIM, TK), lambda b, i: (b, 0, 0, 0)),
            ],
            out_specs=pl.BlockSpec((None, Q_DIM, TQ), lambda b, i: (b, 0, i)),
            out_shape=jax.ShapeDtypeStruct((B, Q_DIM, S), BF16),
            scratch_shapes=[
                pltpu.VMEM((N_Q_HEADS, TK, TQ), F32),
                pltpu.VMEM((N_Q_HEADS, TK, TQ), F32),
                pltpu.VMEM((N_Q_HEADS, 1, TQ), F32),
                pltpu.VMEM((N_Q_HEADS, SUBLANES, TQ), F32),
                pltpu.VMEM((N_Q_HEADS, HEAD_DIM, TQ), F32),
            ],
            compiler_params=cparams,
            name="gqa_attention_shift" if fixed_shift else "gqa_attention_online",
        )

    score_bound = (HEAD_DIM * QK_SCALE_LOG2 * (1.0 + 2.0 ** -6)
                   * jnp.max(jnp.abs(q_norm_g[0])) * jnp.max(jnp.abs(k_norm_g[0])))
    attn_t = lax.cond(
        score_bound <= MAX_FIXED_SHIFT,
        lambda s_, *ops: attention(True)(s_, *ops),
        lambda s_, *ops: attention(False)(s_, *ops),
        score_bound.reshape(1).astype(F32), qt, k, vt)

    bs_full = jnp.repeat(jnp.transpose(b_s[0]), GM_GROUP_DIM, axis=1)
    ws_pairs = (w_s[0].reshape(GM_GROUPS // 2, 2, GM_CHUNK, GM_CHUNK)
                .transpose(0, 2, 1, 3).reshape(GM_GROUPS // 2, GM_CHUNK, 2 * GM_CHUNK).astype(BF16))
    n_seq_out = S // TM_OUT
    n_out = M // TM_OUT

    def s1_blk(i):
        return jnp.minimum(i, n_out - 1)

    def s1_rows(width):
        return pl.BlockSpec((TM_OUT, width), lambda i: (s1_blk(i), 0))

    out = pl.pallas_call(
        _merge_ffn_kernel,
        grid=(n_out + 1,),
        in_specs=[
            s1_rows(D),
            pl.BlockSpec((None, Q_DIM, TM_OUT),
                         lambda i: (s1_blk(i) // n_seq_out, 0, s1_blk(i) % n_seq_out)),
            s1_rows(GM_DIM), s1_rows(GM_DIM), s1_rows(2 * D),
            _const_spec((GM_GROUPS // 2, GM_CHUNK, 2 * GM_CHUNK)),
            _const_spec((GM_CHUNK, GM_DIM)),
            _const_spec((Q_DIM, D)),
            _const_spec((GM_DIM, D)),
            _const_spec((D, D)),
            _const_spec((1, D)),
            _const_spec((D, 2 * d_ff)),
            _const_spec((d_ff, D)),
            _const_spec((1, D)),
        ],
        out_specs=pl.BlockSpec((TM_OUT, D), lambda i: (jnp.maximum(i - 1, 0), 0)),
        out_shape=jax.ShapeDtypeStruct((M, D), F32),
        scratch_shapes=[
            pltpu.VMEM((TM_OUT, GM_DIM), BF16),
            pltpu.VMEM((TM_OUT, d_ff), BF16),
            pltpu.VMEM((TM_OUT, D), F32),
            pltpu.VMEM((TM_OUT, D), F32),
        ],
        compiler_params=pltpu.CompilerParams(dimension_semantics=("arbitrary",),
                                             vmem_limit_bytes=VMEM_LIMIT_BYTES),
        name="merge_ffn",
    )(x.reshape(M, D), attn_t, u, vn, gates, ws_pairs, bs_full,
      w_proj_a[0].astype(BF16), w_proj_b[0].astype(BF16), w_out[0].astype(BF16), norm_ffn_g,
      w_gate_up[0].astype(BF16), w_down[0].astype(BF16), norm_final_g[None, :])
    return out.reshape(B, S, D)
```

```python
import functools
import math

import jax
import jax.numpy as jnp
from jax import lax
from jax.experimental import pallas as pl
from jax.experimental.pallas import tpu as pltpu

HEAD_DIM = 64
N_Q_HEADS = 8
N_KV_HEADS = 2
Q_DIM = N_Q_HEADS * HEAD_DIM
KV_DIM = N_KV_HEADS * HEAD_DIM
GM_GROUPS = 8
GM_GROUP_DIM = 64
GM_DIM = GM_GROUPS * GM_GROUP_DIM
GM_CHUNK = 128
GRID_W = 64
ROPE_THETA = 10000.0
EPS = 1e-6

LANES = 128
SUBLANES = 8
VMEM_LIMIT_BYTES = 56 * 1024 * 1024

TM_IN = 256
TQ = 256
TK = 256
QK_SCALE_LOG2 = math.log2(math.e) / math.sqrt(HEAD_DIM)
MAX_FIXED_SHIFT = 48.0
CHUNKS_PER_BODY = 16
TM_OUT = 256
FF_CHUNK = 768

F32 = jnp.float32
BF16 = jnp.bfloat16


def _dot(a, b):
    return jnp.dot(a, b, preferred_element_type=F32)


def _group_sum(x, gmat2):
    hi = x.astype(BF16)
    lo = (x - hi.astype(F32)).astype(BF16)
    return _dot(jnp.concatenate([hi, lo], axis=1), gmat2)


def _rope_partner(x, lane):
    fwd = pltpu.roll(x, LANES - 16, axis=1)
    bwd = pltpu.roll(x, 16, axis=1)
    return jnp.where((lane & 16) == 0, fwd, bwd)


def _inproj_kernel(x_ref, g_ref, wq_ref, w_ref, gmat_ref, cos_ref, sin_ref, qg_ref, kg_ref,
                   lng_ref, lnb_ref,
                   qt_ref, k_ref, vt_ref, u_ref, vn_ref, gate_ref, ya_ref, yb_ref):
    i = pl.program_id(0)
    tm = x_ref.shape[0]

    @pl.when(i == 0)
    def _():
        yb_ref[...] = jnp.zeros(yb_ref.shape, F32)

    def compute(y_w, y_r):
        x = x_ref[...]
        ms = jnp.mean(x * x, axis=-1, keepdims=True)
        h = (x * lax.rsqrt(ms + EPS) * g_ref[...]).astype(BF16)
        gmat = gmat_ref[...]
        lane = lax.broadcasted_iota(jnp.int32, (tm, LANES), 1)

        def project(c0, c1):
            y_w[:, c0:c1] = _dot(h, w_ref[:, c0:c1])

        y_w[:, :Q_DIM] = _dot(h, wq_ref[...])

        v0 = Q_DIM + KV_DIM
        uv0 = Q_DIM + 2 * KV_DIM
        g0 = uv0 + 2 * GM_DIM
        g1 = g0 + (w_ref.shape[1] - g0) // 2

        n_qk = (Q_DIM + KV_DIM) // LANES
        yqk = [y_r[:, j * LANES:(j + 1) * LANES] for j in range(n_qk)]
        project(Q_DIM, uv0)
        n_gate = w_ref.shape[1] - g0
        gate_ref[:, :n_gate // 2] = jax.nn.sigmoid(y_r[:, g0:g1]).astype(gate_ref.dtype)
        ss = [_group_sum(y * y, gmat) for y in yqk]
        project(uv0, g0)

        def norm_rope(y, s2, gain):
            y = y * lax.rsqrt(s2 * (1.0 / HEAD_DIM) + EPS) * gain
            return y * cos_ref[...] + _rope_partner(y, lane) * sin_ref[...]

        row = lax.broadcasted_iota(jnp.int32, (LANES, tm), 0)
        top = row < HEAD_DIM
        half_heads = N_Q_HEADS // 2
        for j in range(Q_DIM // LANES):
            qt = (norm_rope(yqk[j], ss[j], qg_ref[...]) * QK_SCALE_LOG2).T
            zero = jnp.zeros_like(qt)
            qt_ref[j] = jnp.where(top, qt, zero).astype(BF16)
            qt_ref[j + half_heads] = jnp.where(top, zero, qt).astype(BF16)
        k_ref[...] = norm_rope(yqk[n_qk - 1], ss[n_qk - 1], kg_ref[...]).astype(BF16)
        vt = y_r[:, v0:uv0].T.astype(BF16)
        for c in range(tm // TK):
            vt_ref[c] = vt[:, c * TK:(c + 1) * TK]

        gate_ref[:, n_gate // 2:] = jax.nn.sigmoid(y_r[:, g1:]).astype(gate_ref.dtype)
        u_ref[...] = jax.nn.gelu(y_r[:, uv0:uv0 + GM_DIM]).astype(u_ref.dtype)
        n_gm = GM_DIM // LANES
        vv = [jax.nn.gelu(y_r[:, uv0 + GM_DIM + j * LANES:uv0 + GM_DIM + (j + 1) * LANES])
              for j in range(n_gm)]
        project(g0, g1)
        mu = [_group_sum(v, gmat) * (1.0 / GM_GROUP_DIM) for v in vv]
        gq = g1 + (w_ref.shape[1] - g1) // 2
        project(g1, gq)
        dv = [v - m for v, m in zip(vv, mu)]
        var = [_group_sum(d * d, gmat) * (1.0 / GM_GROUP_DIM) for d in dv]
        project(gq, w_ref.shape[1])
        for j in range(n_gm):
            sl = slice(j * LANES, (j + 1) * LANES)
            vn = dv[j] * lax.rsqrt(var[j] + EPS) * lng_ref[:, sl] + lnb_ref[:, sl]
            vn_ref[:, sl] = vn.astype(BF16)

    @pl.when(i % 2 == 0)
    def _():
        compute(ya_ref, yb_ref)

    @pl.when(i % 2 == 1)
    def _():
        compute(yb_ref, ya_ref)


def _sublane_partial_sum(e):
    rows, lanes = e.shape
    return jnp.sum(e.reshape(rows // SUBLANES, SUBLANES, lanes), axis=0)


def _attn_kernel(shift_ref, qt_ref, k_ref, vt_ref, o_ref, sa_ref, sb_ref, m_ref, l_ref, acc_ref,
                 *, fixed_shift):
    n_heads = qt_ref.shape[0]
    group = n_heads // N_KV_HEADS
    n_chunks = k_ref.shape[0] // TK
    if fixed_shift:
        shift = shift_ref[0]
    else:
        m_ref[...] = jnp.full(m_ref.shape, -jnp.inf, F32)
    acc_ref[...] = jnp.zeros(acc_ref.shape, F32)
    l_ref[...] = jnp.zeros(l_ref.shape, F32)

    def keys(c):
        return k_ref[pl.ds(pl.multiple_of(c * TK, TK), TK), :]

    def step(c, cur_ref, next_ref):
        if next_ref is not None:
            k_next = keys(c + 1)
        vt_c = vt_ref[c]
        for h in range(n_heads):
            g = h // group
            if next_ref is not None:
                next_ref[h] = _dot(k_next, qt_ref[h])
            s = cur_ref[h]
            vt_g = vt_c[g * HEAD_DIM:(g + 1) * HEAD_DIM, :]
            if fixed_shift:
                e = jnp.exp2(s - shift)
                l_ref[h] += _sublane_partial_sum(e)
                acc_ref[h] += _dot(vt_g, e.astype(BF16))
            else:
                m_old = m_ref[h]
                m_new = jnp.maximum(m_old, jnp.max(s, axis=0, keepdims=True))
                alpha = jnp.exp2(m_old - m_new)
                e = jnp.exp2(s - m_new)
                m_ref[h] = m_new
                l_ref[h] = alpha * l_ref[h] + _sublane_partial_sum(e)
                acc_ref[h] = alpha * acc_ref[h] + _dot(vt_g, e.astype(BF16))

    k_first = keys(0)
    for h in range(n_heads):
        sa_ref[h] = _dot(k_first, qt_ref[h])

    bufs = (sa_ref, sb_ref)

    def steps(c0, last):
        for j in range(CHUNKS_PER_BODY):
            final = last and j == CHUNKS_PER_BODY - 1
            step(c0 + j, bufs[j % 2], None if final else bufs[(j + 1) % 2])

    def body(i, carry):
        steps(CHUNKS_PER_BODY * i, False)
        return carry

    lax.fori_loop(0, n_chunks // CHUNKS_PER_BODY - 1, body, 0)
    steps(n_chunks - CHUNKS_PER_BODY, True)
    for h in range(n_heads):
        o = acc_ref[h] / jnp.sum(l_ref[h], axis=0, keepdims=True)
        o_ref[h * HEAD_DIM:(h + 1) * HEAD_DIM, :] = o.astype(BF16)


def _merge_ffn_kernel(x_ref, at_ref, u_ref, vn_ref, gate_ref, ws_ref, bs_ref,
                      wpa_ref, wpb_ref, wo_ref, gf_ref, wgu_ref, wd_ref, gfin_ref,
                      o_ref, gm_ref, act_ref, xa_ref, xb_ref):
    i = pl.program_id(0)
    tm = x_ref.shape[0]
    d_model = x_ref.shape[1]
    d_ff = wd_ref.shape[0]

    @pl.when(i == 0)
    def _():
        xb_ref[...] = jnp.zeros(xb_ref.shape, F32)

    def compute(x1_w, x1_r):
        x1p = x1_r[...]
        ms = jnp.mean(x1p * x1p, axis=-1, keepdims=True)
        h = (x1p * lax.rsqrt(ms + EPS) * gf_ref[...]).astype(BF16)

        lane = lax.broadcasted_iota(jnp.int32, (GM_CHUNK, LANES), 1)
        low = lane < GM_GROUP_DIM
        for c in range(tm // GM_CHUNK):
            rs = slice(c * GM_CHUNK, (c + 1) * GM_CHUNK)
            for j in range(GM_DIM // LANES):
                cs = slice(j * LANES, (j + 1) * LANES)
                vblk = vn_ref[rs, cs]
                zero = jnp.zeros_like(vblk)
                vsplit = jnp.concatenate([jnp.where(low, vblk, zero), jnp.where(low, zero, vblk)],
                                         axis=0)
                sv = _dot(ws_ref[j], vsplit)
                gm_ref[rs, cs] = (u_ref[rs, cs] * (sv + bs_ref[:, cs])).astype(BF16)

        pa = lax.dot_general(at_ref[...], wpa_ref[...], (((0,), (0,)), ((), ())),
                             preferred_element_type=F32)
        pb = _dot(gm_ref[...], wpb_ref[...])
        mix = (gate_ref[:, :d_model] * pa + gate_ref[:, d_model:] * pb).astype(BF16)

        for c0 in range(0, d_ff, FF_CHUNK):
            c1 = min(c0 + FF_CHUNK, d_ff)
            gt = _dot(h, wgu_ref[:, c0:c1])
            up = _dot(h, wgu_ref[:, d_ff + c0:d_ff + c1])
            act_ref[:, c0:c1] = (jax.nn.silu(gt) * up).astype(BF16)
        x2 = x1p + _dot(act_ref[...], wd_ref[...])
        ms2 = jnp.mean(x2 * x2, axis=-1, keepdims=True)
        o_ref[...] = x2 * lax.rsqrt(ms2 + EPS) * gfin_ref[...]

        x1_w[...] = x_ref[...] + _dot(mix, wo_ref[...])

    @pl.when(i % 2 == 0)
    def _():
        compute(xa_ref, xb_ref)

    @pl.when(i % 2 == 1)
    def _():
        compute(xb_ref, xa_ref)


def _const_spec(shape):
    nd = len(shape)
    return pl.BlockSpec(shape, lambda *_: (0,) * nd, pipeline_mode=pl.Buffered(1))


def _rope_tables(seq):
    half = HEAD_DIM // 2
    rows = seq // GRID_W
    row = jnp.repeat(jnp.arange(rows, dtype=jnp.int32), GRID_W)
    col = jnp.tile(jnp.arange(GRID_W, dtype=jnp.int32), rows)
    inv_freq = 1.0 / (ROPE_THETA ** (jnp.arange(0, half, 2, dtype=F32) / half))
    ang_r = row.astype(F32)[:, None] * inv_freq[None, :]
    ang_c = col.astype(F32)[:, None] * inv_freq[None, :]
    cr, sr = jnp.cos(ang_r), jnp.sin(ang_r)
    cc, sc = jnp.cos(ang_c), jnp.sin(ang_c)
    cos64 = jnp.concatenate([cr, cr, cc, cc], axis=-1)
    sin64 = jnp.concatenate([-sr, sr, -sc, sc], axis=-1)
    return jnp.tile(cos64, (1, LANES // HEAD_DIM)), jnp.tile(sin64, (1, LANES // HEAD_DIM))


def kernel(x, norm_mix_g, w_in, q_norm_g, k_norm_g, gm_ln_g, gm_ln_b, w_s, b_s, w_proj_a,
           w_proj_b, w_out, norm_ffn_g, w_gate_up, w_down, norm_final_g):
    B, S, D = x.shape
    in_dim = w_in.shape[-1]
    d_ff = w_down.shape[1]
    assert norm_mix_g.shape[0] == 1, "single-layer block"
    assert CHUNKS_PER_BODY % 2 == 0 and S % (CHUNKS_PER_BODY * TK) == 0
    assert S % TQ == 0 and S % TM_IN == 0 and S % TM_OUT == 0
    assert TM_OUT % GM_CHUNK == 0 and TM_IN % TK == 0

    w_in_b = w_in[0].astype(BF16)
    half_heads = N_Q_HEADS // 2
    w_q_b = (w_in_b[:, :Q_DIM].reshape(D, 2, half_heads, HEAD_DIM)
             .transpose(0, 2, 1, 3).reshape(D, Q_DIM))

    cos_t, sin_t = _rope_tables(S)
    reps = LANES // HEAD_DIM
    qg = jnp.tile(q_norm_g[0], reps)[None, :]
    kg = jnp.tile(k_norm_g[0], reps)[None, :]
    lng = gm_ln_g[0].reshape(1, GM_DIM)
    lnb = gm_ln_b[0].reshape(1, GM_DIM)
    gid = jnp.arange(LANES) // HEAD_DIM
    gmat = (gid[:, None] == gid[None, :]).astype(BF16)
    gmat2 = jnp.concatenate([gmat, gmat], axis=0)

    two_axes = ("arbitrary", "arbitrary")
    cparams = pltpu.CompilerParams(dimension_semantics=two_axes,
                                   vmem_limit_bytes=VMEM_LIMIT_BYTES)

    M = B * S
    n_seq = S // TM_IN
    n_blk = M // TM_IN
    ck_in = TM_IN // TK

    def in_blk(i):
        return jnp.minimum(i, n_blk - 1)

    def out_blk(i):
        return jnp.maximum(i - 1, 0)

    def out_rows(width):
        return pl.BlockSpec((TM_IN, width), lambda i: (out_blk(i), 0))

    qt, k, vt, u, vn, gates = pl.pallas_call(
        _inproj_kernel,
        grid=(n_blk + 1,),
        in_specs=[
            pl.BlockSpec((TM_IN, D), lambda i: (in_blk(i), 0)),
            _const_spec((1, D)),
            _const_spec((D, Q_DIM)),
            _const_spec((D, in_dim)),
            _const_spec((2 * LANES, LANES)),
            pl.BlockSpec((TM_IN, LANES), lambda i: (out_blk(i) % n_seq, 0)),
            pl.BlockSpec((TM_IN, LANES), lambda i: (out_blk(i) % n_seq, 0)),
            _const_spec((1, LANES)),
            _const_spec((1, LANES)),
            _const_spec((1, GM_DIM)),
            _const_spec((1, GM_DIM)),
        ],
        out_specs=[
            pl.BlockSpec((None, N_Q_HEADS, LANES, TM_IN),
                         lambda i: (out_blk(i) // n_seq, 0, 0, out_blk(i) % n_seq)),
            out_rows(KV_DIM),
            pl.BlockSpec((ck_in, KV_DIM, TK), lambda i: (out_blk(i), 0, 0)),
            out_rows(GM_DIM), out_rows(GM_DIM), out_rows(2 * D),
        ],
        out_shape=[
            jax.ShapeDtypeStruct((B, N_Q_HEADS, LANES, S), BF16),
            jax.ShapeDtypeStruct((M, KV_DIM), BF16),
            jax.ShapeDtypeStruct((M // TK, KV_DIM, TK), BF16),
            jax.ShapeDtypeStruct((M, GM_DIM), BF16),
            jax.ShapeDtypeStruct((M, GM_DIM), BF16),
            jax.ShapeDtypeStruct((M, 2 * D), BF16),
        ],
        scratch_shapes=[pltpu.VMEM((TM_IN, in_dim), F32), pltpu.VMEM((TM_IN, in_dim), F32)],
        compiler_params=pltpu.CompilerParams(dimension_semantics=("arbitrary",),
                                             vmem_limit_bytes=VMEM_LIMIT_BYTES),
        name="inproj",
    )(x.reshape(M, D), norm_mix_g, w_q_b, w_in_b, gmat2, cos_t, sin_t, qg, kg, lng, lnb)
    k = k.reshape(B, S, KV_DIM)
    vt = vt.reshape(B, S // TK, KV_DIM, TK)

    def attention(fixed_shift):
        return pl.pallas_call(
            functools.partial(_attn_kernel, fixed_shift=fixed_shift),
            grid=(B, S // TQ),
            in_specs=[
                pl.BlockSpec(memory_space=pltpu.SMEM),
                pl.BlockSpec((None, N_Q_HEADS, LANES, TQ), lambda b, i: (b, 0, 0, i)),
                pl.BlockSpec((None, S, KV_DIM), lambda b, i: (b, 0, 0)),
                pl.BlockSpec((None, S // TK, KV_DIM, TK), lambda b, i: (b, 0, 0, 0)),
            ],
            out_specs=pl.BlockSpec((None, Q_DIM, TQ), lambda b, i: (b, 0, i)),
            out_shape=jax.ShapeDtypeStruct((B, Q_DIM, S), BF16),
            scratch_shapes=[
                pltpu.VMEM((N_Q_HEADS, TK, TQ), F32),
                pltpu.VMEM((N_Q_HEADS, TK, TQ), F32),
                pltpu.VMEM((N_Q_HEADS, 1, TQ), F32),
                pltpu.VMEM((N_Q_HEADS, SUBLANES, TQ), F32),
                pltpu.VMEM((N_Q_HEADS, HEAD_DIM, TQ), F32),
            ],
            compiler_params=cparams,
            name="gqa_attention_shift" if fixed_shift else "gqa_attention_online",
        )

    score_bound = (HEAD_DIM * QK_SCALE_LOG2 * (1.0 + 2.0 ** -6)
                   * jnp.max(jnp.abs(q_norm_g[0])) * jnp.max(jnp.abs(k_norm_g[0])))
    attn_t = lax.cond(
        score_bound <= MAX_FIXED_SHIFT,
        lambda s_, *ops: attention(True)(s_, *ops),
        lambda s_, *ops: attention(False)(s_, *ops),
        score_bound.reshape(1).astype(F32), qt, k, vt)

    bs_full = jnp.repeat(jnp.transpose(b_s[0]), GM_GROUP_DIM, axis=1)
    ws_pairs = (w_s[0].reshape(GM_GROUPS // 2, 2, GM_CHUNK, GM_CHUNK)
                .transpose(0, 2, 1, 3).reshape(GM_GROUPS // 2, GM_CHUNK, 2 * GM_CHUNK).astype(BF16))
    n_seq_out = S // TM_OUT
    n_out = M // TM_OUT

    def s1_blk(i):
        return jnp.minimum(i, n_out - 1)

    def s1_rows(width):
        return pl.BlockSpec((TM_OUT, width), lambda i: (s1_blk(i), 0))

    out = pl.pallas_call(
        _merge_ffn_kernel,
        grid=(n_out + 1,),
        in_specs=[
            s1_rows(D),
            pl.BlockSpec((None, Q_DIM, TM_OUT),
                         lambda i: (s1_blk(i) // n_seq_out, 0, s1_blk(i) % n_seq_out)),
            s1_rows(GM_DIM), s1_rows(GM_DIM), s1_rows(2 * D),
            _const_spec((GM_GROUPS // 2, GM_CHUNK, 2 * GM_CHUNK)),
            _const_spec((GM_CHUNK, GM_DIM)),
            _const_spec((Q_DIM, D)),
            _const_spec((GM_DIM, D)),
            _const_spec((D, D)),
            _const_spec((1, D)),
            _const_spec((D, 2 * d_ff)),
            _const_spec((d_ff, D)),
            _const_spec((1, D)),
        ],
        out_specs=pl.BlockSpec((TM_OUT, D), lambda i: (jnp.maximum(i - 1, 0), 0)),
        out_shape=jax.ShapeDtypeStruct((M, D), F32),
        scratch_shapes=[
            pltpu.VMEM((TM_OUT, GM_DIM), BF16),
            pltpu.VMEM((TM_OUT, d_ff), BF16),
            pltpu.VMEM((TM_OUT, D), F32),
            pltpu.VMEM((TM_OUT, D), F32),
        ],
        compiler_params=pltpu.CompilerParams(dimension_semantics=("arbitrary",),
                                             vmem_limit_bytes=VMEM_LIMIT_BYTES),
        name="merge_ffn",
    )(x.reshape(M, D), attn_t, u, vn, gates, ws_pairs, bs_full,
      w_proj_a[0].astype(BF16), w_proj_b[0].astype(BF16), w_out[0].astype(BF16), norm_ffn_g,
      w_gate_up[0].astype(BF16), w_down[0].astype(BF16), norm_final_g[None, :])
    return out.reshape(B, S, D)
```

```python
import functools
import math

import jax
import jax.numpy as jnp
from jax import lax
from jax.experimental import pallas as pl
from jax.experimental.pallas import tpu as pltpu

HEAD_DIM = 64
N_Q_HEADS = 8
N_KV_HEADS = 2
Q_DIM = N_Q_HEADS * HEAD_DIM
KV_DIM = N_KV_HEADS * HEAD_DIM
GM_GROUPS = 8
GM_GROUP_DIM = 64
GM_DIM = GM_GROUPS * GM_GROUP_DIM
GM_CHUNK = 128
GRID_W = 64
ROPE_THETA = 10000.0
EPS = 1e-6

LANES = 128
SUBLANES = 8
VMEM_LIMIT_BYTES = 56 * 1024 * 1024

TM_IN = 512
TQ = 256
TK = 256
QK_SCALE_LOG2 = math.log2(math.e) / math.sqrt(HEAD_DIM)
MAX_FIXED_SHIFT = 48.0
CHUNKS_PER_BODY = 16
TM_OUT = 256
FF_CHUNK = 768

F32 = jnp.float32
BF16 = jnp.bfloat16


def _dot(a, b):
    return jnp.dot(a, b, preferred_element_type=F32)


def _group_sum(x, gmat2):
    hi = x.astype(BF16)
    lo = (x - hi.astype(F32)).astype(BF16)
    return _dot(jnp.concatenate([hi, lo], axis=1), gmat2)


def _rope_partner(x, lane):
    fwd = pltpu.roll(x, LANES - 16, axis=1)
    bwd = pltpu.roll(x, 16, axis=1)
    return jnp.where((lane & 16) == 0, fwd, bwd)


def _inproj_kernel(x_ref, g_ref, wq_ref, w_ref, gmat_ref, cos_ref, sin_ref, qg_ref, kg_ref,
                   lng_ref, lnb_ref,
                   qt_ref, k_ref, vt_ref, u_ref, vn_ref, gate_ref, ya_ref, yb_ref):
    i = pl.program_id(0)
    tm = x_ref.shape[0]

    @pl.when(i == 0)
    def _():
        yb_ref[...] = jnp.zeros(yb_ref.shape, F32)

    def compute(y_w, y_r):
        x = x_ref[...]
        ms = jnp.mean(x * x, axis=-1, keepdims=True)
        h = (x * lax.rsqrt(ms + EPS) * g_ref[...]).astype(BF16)
        gmat = gmat_ref[...]
        lane = lax.broadcasted_iota(jnp.int32, (tm, LANES), 1)

        def project(c0, c1):
            y_w[:, c0:c1] = _dot(h, w_ref[:, c0:c1])

        y_w[:, :Q_DIM] = _dot(h, wq_ref[...])

        v0 = Q_DIM + KV_DIM
        uv0 = Q_DIM + 2 * KV_DIM
        g0 = uv0 + 2 * GM_DIM
        g1 = g0 + (w_ref.shape[1] - g0) // 2

        n_qk = (Q_DIM + KV_DIM) // LANES
        yqk = [y_r[:, j * LANES:(j + 1) * LANES] for j in range(n_qk)]
        project(Q_DIM, uv0)
        n_gate = w_ref.shape[1] - g0
        gate_ref[:, :n_gate // 2] = jax.nn.sigmoid(y_r[:, g0:g1])
        ss = [_group_sum(y * y, gmat) for y in yqk]
        project(uv0, g0)

        def norm_rope(y, s2, gain):
            y = y * lax.rsqrt(s2 * (1.0 / HEAD_DIM) + EPS) * gain
            return y * cos_ref[...] + _rope_partner(y, lane) * sin_ref[...]

        row = lax.broadcasted_iota(jnp.int32, (LANES, tm), 0)
        top = row < HEAD_DIM
        half_heads = N_Q_HEADS // 2
        for j in range(Q_DIM // LANES):
            qt = (norm_rope(yqk[j], ss[j], qg_ref[...]) * QK_SCALE_LOG2).T
            zero = jnp.zeros_like(qt)
            qt_ref[j] = jnp.where(top, qt, zero).astype(BF16)
            qt_ref[j + half_heads] = jnp.where(top, zero, qt).astype(BF16)
        k_ref[...] = norm_rope(yqk[n_qk - 1], ss[n_qk - 1], kg_ref[...]).astype(BF16)
        vt = y_r[:, v0:uv0].T.astype(BF16)
        for c in range(tm // TK):
            vt_ref[c] = vt[:, c * TK:(c + 1) * TK]

        gate_ref[:, n_gate // 2:] = jax.nn.sigmoid(y_r[:, g1:])
        u_ref[...] = jax.nn.gelu(y_r[:, uv0:uv0 + GM_DIM])
        n_gm = GM_DIM // LANES
        vv = [jax.nn.gelu(y_r[:, uv0 + GM_DIM + j * LANES:uv0 + GM_DIM + (j + 1) * LANES])
              for j in range(n_gm)]
        project(g0, g1)
        mu = [_group_sum(v, gmat) * (1.0 / GM_GROUP_DIM) for v in vv]
        gq = g1 + (w_ref.shape[1] - g1) // 2
        project(g1, gq)
        dv = [v - m for v, m in zip(vv, mu)]
        var = [_group_sum(d * d, gmat) * (1.0 / GM_GROUP_DIM) for d in dv]
        project(gq, w_ref.shape[1])
        for j in range(n_gm):
            sl = slice(j * LANES, (j + 1) * LANES)
            vn = dv[j] * lax.rsqrt(var[j] + EPS) * lng_ref[:, sl] + lnb_ref[:, sl]
            vn_ref[:, sl] = vn.astype(BF16)

    @pl.when(i % 2 == 0)
    def _():
        compute(ya_ref, yb_ref)

    @pl.when(i % 2 == 1)
    def _():
        compute(yb_ref, ya_ref)


def _sublane_partial_sum(e):
    rows, lanes = e.shape
    return jnp.sum(e.reshape(rows // SUBLANES, SUBLANES, lanes), axis=0)


def _attn_kernel(shift_ref, qt_ref, k_ref, vt_ref, o_ref, sa_ref, sb_ref, m_ref, l_ref, acc_ref,
                 *, fixed_shift):
    n_heads = qt_ref.shape[0]
    group = n_heads // N_KV_HEADS
    n_chunks = k_ref.shape[0] // TK
    if fixed_shift:
        shift = shift_ref[0]
    else:
        m_ref[...] = jnp.full(m_ref.shape, -jnp.inf, F32)
    acc_ref[...] = jnp.zeros(acc_ref.shape, F32)
    l_ref[...] = jnp.zeros(l_ref.shape, F32)

    def keys(c):
        return k_ref[pl.ds(pl.multiple_of(c * TK, TK), TK), :]

    def step(c, cur_ref, next_ref):
        if next_ref is not None:
            k_next = keys(c + 1)
        vt_c = vt_ref[c]
        for h in range(n_heads):
            g = h // group
            if next_ref is not None:
                next_ref[h] = _dot(k_next, qt_ref[h])
            s = cur_ref[h]
            vt_g = vt_c[g * HEAD_DIM:(g + 1) * HEAD_DIM, :]
            if fixed_shift:
                e = jnp.exp2(s - shift)
                l_ref[h] += _sublane_partial_sum(e)
                acc_ref[h] += _dot(vt_g, e.astype(BF16))
            else:
                m_old = m_ref[h]
                m_new = jnp.maximum(m_old, jnp.max(s, axis=0, keepdims=True))
                alpha = jnp.exp2(m_old - m_new)
                e = jnp.exp2(s - m_new)
                m_ref[h] = m_new
                l_ref[h] = alpha * l_ref[h] + _sublane_partial_sum(e)
                acc_ref[h] = alpha * acc_ref[h] + _dot(vt_g, e.astype(BF16))

    k_first = keys(0)
    for h in range(n_heads):
        sa_ref[h] = _dot(k_first, qt_ref[h])

    bufs = (sa_ref, sb_ref)

    def steps(c0, last):
        for j in range(CHUNKS_PER_BODY):
            final = last and j == CHUNKS_PER_BODY - 1
            step(c0 + j, bufs[j % 2], None if final else bufs[(j + 1) % 2])

    def body(i, carry):
        steps(CHUNKS_PER_BODY * i, False)
        return carry

    lax.fori_loop(0, n_chunks // CHUNKS_PER_BODY - 1, body, 0)
    steps(n_chunks - CHUNKS_PER_BODY, True)
    for h in range(n_heads):
        o = acc_ref[h] / jnp.sum(l_ref[h], axis=0, keepdims=True)
        o_ref[h * HEAD_DIM:(h + 1) * HEAD_DIM, :] = o.astype(BF16)


def _merge_ffn_kernel(x_ref, at_ref, u_ref, vn_ref, gate_ref, ws_ref, bs_ref,
                      wpa_ref, wpb_ref, wo_ref, gf_ref, wgu_ref, wd_ref, gfin_ref,
                      o_ref, gm_ref, act_ref, xa_ref, xb_ref):
    i = pl.program_id(0)
    tm = x_ref.shape[0]
    d_model = x_ref.shape[1]
    d_ff = wd_ref.shape[0]

    @pl.when(i == 0)
    def _():
        xb_ref[...] = jnp.zeros(xb_ref.shape, F32)

    def compute(x1_w, x1_r):
        x1p = x1_r[...]
        ms = jnp.mean(x1p * x1p, axis=-1, keepdims=True)
        h = (x1p * lax.rsqrt(ms + EPS) * gf_ref[...]).astype(BF16)

        lane = lax.broadcasted_iota(jnp.int32, (GM_CHUNK, LANES), 1)
        low = lane < GM_GROUP_DIM
        for c in range(tm // GM_CHUNK):
            rs = slice(c * GM_CHUNK, (c + 1) * GM_CHUNK)
            for j in range(GM_DIM // LANES):
                cs = slice(j * LANES, (j + 1) * LANES)
                vblk = vn_ref[rs, cs]
                zero = jnp.zeros_like(vblk)
                vsplit = jnp.concatenate([jnp.where(low, vblk, zero), jnp.where(low, zero, vblk)],
                                         axis=0)
                sv = _dot(ws_ref[j], vsplit)
                gm_ref[rs, cs] = (u_ref[rs, cs] * (sv + bs_ref[:, cs])).astype(BF16)

        pa = lax.dot_general(at_ref[...], wpa_ref[...], (((0,), (0,)), ((), ())),
                             preferred_element_type=F32)
        pb = _dot(gm_ref[...], wpb_ref[...])
        mix = (gate_ref[:, :d_model] * pa + gate_ref[:, d_model:] * pb).astype(BF16)

        for c0 in range(0, d_ff, FF_CHUNK):
            c1 = min(c0 + FF_CHUNK, d_ff)
            gt = _dot(h, wgu_ref[:, c0:c1])
            up = _dot(h, wgu_ref[:, d_ff + c0:d_ff + c1])
            act_ref[:, c0:c1] = (jax.nn.silu(gt) * up).astype(BF16)
        x2 = x1p + _dot(act_ref[...], wd_ref[...])
        ms2 = jnp.mean(x2 * x2, axis=-1, keepdims=True)
        o_ref[...] = x2 * lax.rsqrt(ms2 + EPS) * gfin_ref[...]

        x1_w[...] = x_ref[...] + _dot(mix, wo_ref[...])

    @pl.when(i % 2 == 0)
    def _():
        compute(xa_ref, xb_ref)

    @pl.when(i % 2 == 1)
    def _():
        compute(xb_ref, xa_ref)


def _const_spec(shape):
    nd = len(shape)
    return pl.BlockSpec(shape, lambda *_: (0,) * nd, pipeline_mode=pl.Buffered(1))


def _rope_tables(seq):
    half = HEAD_DIM // 2
    rows = seq // GRID_W
    row = jnp.repeat(jnp.arange(rows, dtype=jnp.int32), GRID_W)
    col = jnp.tile(jnp.arange(GRID_W, dtype=jnp.int32), rows)
    inv_freq = 1.0 / (ROPE_THETA ** (jnp.arange(0, half, 2, dtype=F32) / half))
    ang_r = row.astype(F32)[:, None] * inv_freq[None, :]
    ang_c = col.astype(F32)[:, None] * inv_freq[None, :]
    cr, sr = jnp.cos(ang_r), jnp.sin(ang_r)
    cc, sc = jnp.cos(ang_c), jnp.sin(ang_c)
    cos64 = jnp.concatenate([cr, cr, cc, cc], axis=-1)
    sin64 = jnp.concatenate([-sr, sr, -sc, sc], axis=-1)
    return jnp.tile(cos64, (1, LANES // HEAD_DIM)), jnp.tile(sin64, (1, LANES // HEAD_DIM))


def kernel(x, norm_mix_g, w_in, q_norm_g, k_norm_g, gm_ln_g, gm_ln_b, w_s, b_s, w_proj_a,
           w_proj_b, w_out, norm_ffn_g, w_gate_up, w_down, norm_final_g):
    B, S, D = x.shape
    in_dim = w_in.shape[-1]
    d_ff = w_down.shape[1]
    assert norm_mix_g.shape[0] == 1, "single-layer block"
    assert CHUNKS_PER_BODY % 2 == 0 and S % (CHUNKS_PER_BODY * TK) == 0
    assert S % TQ == 0 and S % TM_IN == 0 and S % TM_OUT == 0
    assert TM_OUT % GM_CHUNK == 0 and TM_IN % TK == 0

    w_in_b = w_in[0].astype(BF16)
    half_heads = N_Q_HEADS // 2
    w_q_b = (w_in_b[:, :Q_DIM].reshape(D, 2, half_heads, HEAD_DIM)
             .transpose(0, 2, 1, 3).reshape(D, Q_DIM))

    cos_t, sin_t = _rope_tables(S)
    reps = LANES // HEAD_DIM
    qg = jnp.tile(q_norm_g[0], reps)[None, :]
    kg = jnp.tile(k_norm_g[0], reps)[None, :]
    lng = gm_ln_g[0].reshape(1, GM_DIM)
    lnb = gm_ln_b[0].reshape(1, GM_DIM)
    gid = jnp.arange(LANES) // HEAD_DIM
    gmat = (gid[:, None] == gid[None, :]).astype(BF16)
    gmat2 = jnp.concatenate([gmat, gmat], axis=0)

    two_axes = ("arbitrary", "arbitrary")
    cparams = pltpu.CompilerParams(dimension_semantics=two_axes,
                                   vmem_limit_bytes=VMEM_LIMIT_BYTES)

    M = B * S
    n_seq = S // TM_IN
    n_blk = M // TM_IN
    ck_in = TM_IN // TK

    def in_blk(i):
        return jnp.minimum(i, n_blk - 1)

    def out_blk(i):
        return jnp.maximum(i - 1, 0)

    def out_rows(width):
        return pl.BlockSpec((TM_IN, width), lambda i: (out_blk(i), 0))

    qt, k, vt, u, vn, gates = pl.pallas_call(
        _inproj_kernel,
        grid=(n_blk + 1,),
        in_specs=[
            pl.BlockSpec((TM_IN, D), lambda i: (in_blk(i), 0)),
            _const_spec((1, D)),
            _const_spec((D, Q_DIM)),
            _const_spec((D, in_dim)),
            _const_spec((2 * LANES, LANES)),
            pl.BlockSpec((TM_IN, LANES), lambda i: (out_blk(i) % n_seq, 0)),
            pl.BlockSpec((TM_IN, LANES), lambda i: (out_blk(i) % n_seq, 0)),
            _const_spec((1, LANES)),
            _const_spec((1, LANES)),
            _const_spec((1, GM_DIM)),
            _const_spec((1, GM_DIM)),
        ],
        out_specs=[
            pl.BlockSpec((None, N_Q_HEADS, LANES, TM_IN),
                         lambda i: (out_blk(i) // n_seq, 0, 0, out_blk(i) % n_seq)),
            out_rows(KV_DIM),
            pl.BlockSpec((ck_in, KV_DIM, TK), lambda i: (out_blk(i), 0, 0)),
            out_rows(GM_DIM), out_rows(GM_DIM), out_rows(2 * D),
        ],
        out_shape=[
            jax.ShapeDtypeStruct((B, N_Q_HEADS, LANES, S), BF16),
            jax.ShapeDtypeStruct((M, KV_DIM), BF16),
            jax.ShapeDtypeStruct((M // TK, KV_DIM, TK), BF16),
            jax.ShapeDtypeStruct((M, GM_DIM), F32),
            jax.ShapeDtypeStruct((M, GM_DIM), BF16),
            jax.ShapeDtypeStruct((M, 2 * D), F32),
        ],
        scratch_shapes=[pltpu.VMEM((TM_IN, in_dim), F32), pltpu.VMEM((TM_IN, in_dim), F32)],
        compiler_params=pltpu.CompilerParams(dimension_semantics=("arbitrary",),
                                             vmem_limit_bytes=VMEM_LIMIT_BYTES),
        name="inproj",
    )(x.reshape(M, D), norm_mix_g, w_q_b, w_in_b, gmat2, cos_t, sin_t, qg, kg, lng, lnb)
    k = k.reshape(B, S, KV_DIM)
    vt = vt.reshape(B, S // TK, KV_DIM, TK)

    def attention(fixed_shift):
        return pl.pallas_call(
            functools.partial(_attn_kernel, fixed_shift=fixed_shift),
            grid=(B, S // TQ),
            in_specs=[
                pl.BlockSpec(memory_space=pltpu.SMEM),
                pl.BlockSpec((None, N_Q_HEADS, LANES, TQ), lambda b, i: (b, 0, 0, i)),
                pl.BlockSpec((None, S, KV_DIM), lambda b, i: (b, 0, 0)),
                pl.BlockSpec((None, S // TK, KV_DIM, TK), lambda b, i: (b, 0, 0, 0)),
            ],
            out_specs=pl.BlockSpec((None, Q_DIM, TQ), lambda b, i: (b, 0, i)),
            out_shape=jax.ShapeDtypeStruct((B, Q_DIM, S), BF16),
            scratch_shapes=[
                pltpu.VMEM((N_Q_HEADS, TK, TQ), F32),
                pltpu.VMEM((N_Q_HEADS, TK, TQ), F32),
                pltpu.VMEM((N_Q_HEADS, 1, TQ), F32),
                pltpu.VMEM((N_Q_HEADS, SUBLANES, TQ), F32),
                pltpu.VMEM((N_Q_HEADS, HEAD_DIM, TQ), F32),
            ],
            compiler_params=cparams,
            name="gqa_attention_shift" if fixed_shift else "gqa_attention_online",
        )

    score_bound = (HEAD_DIM * QK_SCALE_LOG2 * (1.0 + 2.0 ** -6)
                   * jnp.max(jnp.abs(q_norm_g[0])) * jnp.max(jnp.abs(k_norm_g[0])))
    attn_t = lax.cond(
        score_bound <= MAX_FIXED_SHIFT,
        lambda s_, *ops: attention(True)(s_, *ops),
        lambda s_, *ops: attention(False)(s_, *ops),
        score_bound.reshape(1).astype(F32), qt, k, vt)

    bs_full = jnp.repeat(jnp.transpose(b_s[0]), GM_GROUP_DIM, axis=1)
    ws_pairs = (w_s[0].reshape(GM_GROUPS // 2, 2, GM_CHUNK, GM_CHUNK)
                .transpose(0, 2, 1, 3).reshape(GM_GROUPS // 2, GM_CHUNK, 2 * GM_CHUNK).astype(BF16))
    n_seq_out = S // TM_OUT
    n_out = M // TM_OUT

    def s1_blk(i):
        return jnp.minimum(i, n_out - 1)

    def s1_rows(width):
        return pl.BlockSpec((TM_OUT, width), lambda i: (s1_blk(i), 0))

    out = pl.pallas_call(
        _merge_ffn_kernel,
        grid=(n_out + 1,),
        in_specs=[
            s1_rows(D),
            pl.BlockSpec((None, Q_DIM, TM_OUT),
                         lambda i: (s1_blk(i) // n_seq_out, 0, s1_blk(i) % n_seq_out)),
            s1_rows(GM_DIM), s1_rows(GM_DIM), s1_rows(2 * D),
            _const_spec((GM_GROUPS // 2, GM_CHUNK, 2 * GM_CHUNK)),
            _const_spec((GM_CHUNK, GM_DIM)),
            _const_spec((Q_DIM, D)),
            _const_spec((GM_DIM, D)),
            _const_spec((D, D)),
            _const_spec((1, D)),
            _const_spec((D, 2 * d_ff)),
            _const_spec((d_ff, D)),
            _const_spec((1, D)),
        ],
        out_specs=pl.BlockSpec((TM_OUT, D), lambda i: (jnp.maximum(i - 1, 0), 0)),
        out_shape=jax.ShapeDtypeStruct((M, D), F32),
        scratch_shapes=[
            pltpu.VMEM((TM_OUT, GM_DIM), BF16),
            pltpu.VMEM((TM_OUT, d_ff), BF16),
            pltpu.VMEM((TM_OUT, D), F32),
            pltpu.VMEM((TM_OUT, D), F32),
        ],
        compiler_params=pltpu.CompilerParams(dimension_semantics=("arbitrary",),
                                             vmem_limit_bytes=VMEM_LIMIT_BYTES),
        name="merge_ffn",
    )(x.reshape(M, D), attn_t, u, vn, gates, ws_pairs, bs_full,
      w_proj_a[0].astype(BF16), w_proj_b[0].astype(BF16), w_out[0].astype(BF16), norm_ffn_g,
      w_gate_up[0].astype(BF16), w_down[0].astype(BF16), norm_final_g[None, :])
    return out.reshape(B, S, D)
```

```python
import functools
import math

import jax
import jax.numpy as jnp
import numpy as np
from jax import lax
from jax.experimental import pallas as pl
from jax.experimental.pallas import tpu as pltpu

HEAD_DIM = 64
N_Q_HEADS = 8
N_KV_HEADS = 2
Q_DIM = N_Q_HEADS * HEAD_DIM
KV_DIM = N_KV_HEADS * HEAD_DIM
GM_GROUPS = 8
GM_GROUP_DIM = 64
GM_DIM = GM_GROUPS * GM_GROUP_DIM
GM_CHUNK = 128
GRID_W = 64
ROPE_THETA = 10000.0
EPS = 1e-6

LANES = 128
SUBLANES = 8
VMEM_LIMIT_BYTES = 56 * 1024 * 1024

TM_IN = 256
TQ = 256
TK = 256
QK_SCALE_LOG2 = math.log2(math.e) / math.sqrt(HEAD_DIM)
MAX_FIXED_SHIFT = 48.0
CHUNKS_PER_BODY = 16
TM_OUT = 256
FF_CHUNK = 768

F32 = jnp.float32
BF16 = jnp.bfloat16


def _dot(a, b):
    return jnp.dot(a, b, preferred_element_type=F32)


def _group_sum(x, gmat2):
    hi = x.astype(BF16)
    lo = (x - hi.astype(F32)).astype(BF16)
    return _dot(jnp.concatenate([hi, lo], axis=1), gmat2)


def _rope_partner(x, lane):
    fwd = pltpu.roll(x, LANES - 16, axis=1)
    bwd = pltpu.roll(x, 16, axis=1)
    return jnp.where((lane & 16) == 0, fwd, bwd)


def _inproj_kernel(x_ref, g_ref, wq_ref, w_ref, gmat_ref, cos_ref, sin_ref, qg_ref, kg_ref,
                   lng_ref, lnb_ref,
                   qt_ref, k_ref, vt_ref, u_ref, vn_ref, gate_ref, ya_ref, yb_ref):
    i = pl.program_id(0)
    tm = x_ref.shape[0]

    @pl.when(i == 0)
    def _():
        yb_ref[...] = jnp.zeros(yb_ref.shape, F32)

    def compute(y_w, y_r):
        x = x_ref[...]
        ms = jnp.mean(x * x, axis=-1, keepdims=True)
        h = (x * lax.rsqrt(ms + EPS) * g_ref[...]).astype(BF16)
        gmat = gmat_ref[...]
        lane = lax.broadcasted_iota(jnp.int32, (tm, LANES), 1)

        def project(c0, c1):
            y_w[:, c0:c1] = _dot(h, w_ref[:, c0:c1])

        y_w[:, :Q_DIM] = _dot(h, wq_ref[...])

        v0 = Q_DIM + KV_DIM
        uv0 = Q_DIM + 2 * KV_DIM
        g0 = uv0 + 2 * GM_DIM
        g1 = g0 + (w_ref.shape[1] - g0) // 2

        n_qk = (Q_DIM + KV_DIM) // LANES
        yqk = [y_r[:, j * LANES:(j + 1) * LANES] for j in range(n_qk)]
        project(Q_DIM, uv0)
        n_gate = w_ref.shape[1] - g0
        gate_ref[:, :n_gate // 2] = jax.nn.sigmoid(y_r[:, g0:g1])
        ss = [_group_sum(y * y, gmat) for y in yqk]
        project(uv0, g0)

        def norm_rope(y, s2, gain):
            y = y * lax.rsqrt(s2 * (1.0 / HEAD_DIM) + EPS) * gain
            return y * cos_ref[...] + _rope_partner(y, lane) * sin_ref[...]

        row = lax.broadcasted_iota(jnp.int32, (LANES, tm), 0)
        top = row < HEAD_DIM
        half_heads = N_Q_HEADS // 2
        for j in range(Q_DIM // LANES):
            qt = (norm_rope(yqk[j], ss[j], qg_ref[...]) * QK_SCALE_LOG2).T
            zero = jnp.zeros_like(qt)
            qt_ref[j] = jnp.where(top, qt, zero).astype(BF16)
            qt_ref[j + half_heads] = jnp.where(top, zero, qt).astype(BF16)
        k_ref[...] = norm_rope(yqk[n_qk - 1], ss[n_qk - 1], kg_ref[...]).astype(BF16)
        vt = y_r[:, v0:uv0].T.astype(BF16)
        for c in range(tm // TK):
            vt_ref[c] = vt[:, c * TK:(c + 1) * TK]

        gate_ref[:, n_gate // 2:] = jax.nn.sigmoid(y_r[:, g1:])
        u_ref[...] = jax.nn.gelu(y_r[:, uv0:uv0 + GM_DIM])
        n_gm = GM_DIM // LANES
        vv = [jax.nn.gelu(y_r[:, uv0 + GM_DIM + j * LANES:uv0 + GM_DIM + (j + 1) * LANES])
              for j in range(n_gm)]
        project(g0, g1)
        mu = [_group_sum(v, gmat) * (1.0 / GM_GROUP_DIM) for v in vv]
        gq = g1 + (w_ref.shape[1] - g1) // 2
        project(g1, gq)
        dv = [v - m for v, m in zip(vv, mu)]
        var = [_group_sum(d * d, gmat) * (1.0 / GM_GROUP_DIM) for d in dv]
        project(gq, w_ref.shape[1])
        for j in range(n_gm):
            sl = slice(j * LANES, (j + 1) * LANES)
            vn = dv[j] * lax.rsqrt(var[j] + EPS) * lng_ref[:, sl] + lnb_ref[:, sl]
            vn_ref[:, sl] = vn.astype(BF16)

    @pl.when(i % 2 == 0)
    def _():
        compute(ya_ref, yb_ref)

    @pl.when(i % 2 == 1)
    def _():
        compute(yb_ref, ya_ref)


def _sublane_partial_sum(e):
    rows, lanes = e.shape
    return jnp.sum(e.reshape(rows // SUBLANES, SUBLANES, lanes), axis=0)


def _attn_kernel(shift_ref, qt_ref, k_ref, vt_ref, o_ref, sa_ref, sb_ref, m_ref, l_ref, acc_ref,
                 *, fixed_shift):
    n_heads = qt_ref.shape[0]
    group = n_heads // N_KV_HEADS
    n_chunks = k_ref.shape[0] // TK
    if fixed_shift:
        shift = shift_ref[0]
    else:
        m_ref[...] = jnp.full(m_ref.shape, -jnp.inf, F32)
    acc_ref[...] = jnp.zeros(acc_ref.shape, F32)
    l_ref[...] = jnp.zeros(l_ref.shape, F32)

    def keys(c):
        return k_ref[pl.ds(pl.multiple_of(c * TK, TK), TK), :]

    def step(c, cur_ref, next_ref):
        if next_ref is not None:
            k_next = keys(c + 1)
        vt_c = vt_ref[c]
        for h in range(n_heads):
            g = h // group
            if next_ref is not None:
                next_ref[h] = _dot(k_next, qt_ref[h])
            s = cur_ref[h]
            vt_g = vt_c[g * HEAD_DIM:(g + 1) * HEAD_DIM, :]
            if fixed_shift:
                e = jnp.exp2(s - shift)
                l_ref[h] += _sublane_partial_sum(e)
                acc_ref[h] += _dot(vt_g, e.astype(BF16))
            else:
                m_old = m_ref[h]
                m_new = jnp.maximum(m_old, jnp.max(s, axis=0, keepdims=True))
                alpha = jnp.exp2(m_old - m_new)
                e = jnp.exp2(s - m_new)
                m_ref[h] = m_new
                l_ref[h] = alpha * l_ref[h] + _sublane_partial_sum(e)
                acc_ref[h] = alpha * acc_ref[h] + _dot(vt_g, e.astype(BF16))

    k_first = keys(0)
    for h in range(n_heads):
        sa_ref[h] = _dot(k_first, qt_ref[h])

    bufs = (sa_ref, sb_ref)

    def steps(c0, last):
        for j in range(CHUNKS_PER_BODY):
            final = last and j == CHUNKS_PER_BODY - 1
            step(c0 + j, bufs[j % 2], None if final else bufs[(j + 1) % 2])

    def body(i, carry):
        steps(CHUNKS_PER_BODY * i, False)
        return carry

    lax.fori_loop(0, n_chunks // CHUNKS_PER_BODY - 1, body, 0)
    steps(n_chunks - CHUNKS_PER_BODY, True)
    for h in range(n_heads):
        o = acc_ref[h] / jnp.sum(l_ref[h], axis=0, keepdims=True)
        o_ref[h * HEAD_DIM:(h + 1) * HEAD_DIM, :] = o.astype(BF16)


def _merge_ffn_kernel(x_ref, at_ref, u_ref, vn_ref, gate_ref, ws_ref, bs_ref,
                      wpa_ref, wpb_ref, wo_ref, gf_ref, wgu_ref, wd_ref, gfin_ref,
                      o_ref, gm_ref, act_ref, xa_ref, xb_ref):
    i = pl.program_id(0)
    tm = x_ref.shape[0]
    d_model = x_ref.shape[1]
    d_ff = wd_ref.shape[0]

    @pl.when(i == 0)
    def _():
        xb_ref[...] = jnp.zeros(xb_ref.shape, F32)

    def compute(x1_w, x1_r):
        x1p = x1_r[...]
        ms = jnp.mean(x1p * x1p, axis=-1, keepdims=True)
        h = (x1p * lax.rsqrt(ms + EPS) * gf_ref[...]).astype(BF16)

        lane = lax.broadcasted_iota(jnp.int32, (GM_CHUNK, LANES), 1)
        low = lane < GM_GROUP_DIM
        for c in range(tm // GM_CHUNK):
            rs = slice(c * GM_CHUNK, (c + 1) * GM_CHUNK)
            for j in range(GM_DIM // LANES):
                cs = slice(j * LANES, (j + 1) * LANES)
                vblk = vn_ref[rs, cs]
                zero = jnp.zeros_like(vblk)
                vsplit = jnp.concatenate([jnp.where(low, vblk, zero), jnp.where(low, zero, vblk)],
                                         axis=0)
                sv = _dot(ws_ref[j], vsplit)
                gm_ref[rs, cs] = (u_ref[rs, cs] * (sv + bs_ref[:, cs])).astype(BF16)

        pa = lax.dot_general(at_ref[...], wpa_ref[...], (((0,), (0,)), ((), ())),
                             preferred_element_type=F32)
        pb = _dot(gm_ref[...], wpb_ref[...])
        mix = (gate_ref[:, :d_model] * pa + gate_ref[:, d_model:] * pb).astype(BF16)

        for c0 in range(0, d_ff, FF_CHUNK):
            c1 = min(c0 + FF_CHUNK, d_ff)
            gt = _dot(h, wgu_ref[:, c0:c1])
            up = _dot(h, wgu_ref[:, d_ff + c0:d_ff + c1])
            act_ref[:, c0:c1] = (jax.nn.silu(gt) * up).astype(BF16)
        x2 = x1p + _dot(act_ref[...], wd_ref[...])
        ms2 = jnp.mean(x2 * x2, axis=-1, keepdims=True)
        o_ref[...] = x2 * lax.rsqrt(ms2 + EPS) * gfin_ref[...]

        x1_w[...] = x_ref[...] + _dot(mix, wo_ref[...])

    @pl.when(i % 2 == 0)
    def _():
        compute(xa_ref, xb_ref)

    @pl.when(i % 2 == 1)
    def _():
        compute(xb_ref, xa_ref)


def _const_spec(shape):
    nd = len(shape)
    return pl.BlockSpec(shape, lambda *_: (0,) * nd, pipeline_mode=pl.Buffered(1))


def _rope_tables(seq):
    half = HEAD_DIM // 2
    rows = seq // GRID_W
    row = np.repeat(np.arange(rows, dtype=np.float64), GRID_W)
    col = np.tile(np.arange(GRID_W, dtype=np.float64), rows)
    inv_freq = 1.0 / (ROPE_THETA ** (np.arange(0, half, 2, dtype=np.float64) / half))
    ang_r = row[:, None] * inv_freq[None, :]
    ang_c = col[:, None] * inv_freq[None, :]
    cr, sr = np.cos(ang_r), np.sin(ang_r)
    cc, sc = np.cos(ang_c), np.sin(ang_c)
    cos64 = np.concatenate([cr, cr, cc, cc], axis=-1)
    sin64 = np.concatenate([-sr, sr, -sc, sc], axis=-1)
    reps = (1, LANES // HEAD_DIM)
    return np.tile(cos64, reps).astype(np.float32), np.tile(sin64, reps).astype(np.float32)


def kernel(x, norm_mix_g, w_in, q_norm_g, k_norm_g, gm_ln_g, gm_ln_b, w_s, b_s, w_proj_a,
           w_proj_b, w_out, norm_ffn_g, w_gate_up, w_down, norm_final_g):
    B, S, D = x.shape
    in_dim = w_in.shape[-1]
    d_ff = w_down.shape[1]
    assert norm_mix_g.shape[0] == 1, "single-layer block"
    assert CHUNKS_PER_BODY % 2 == 0 and S % (CHUNKS_PER_BODY * TK) == 0
    assert S % TQ == 0 and S % TM_IN == 0 and S % TM_OUT == 0
    assert TM_OUT % GM_CHUNK == 0 and TM_IN % TK == 0

    w_in_b = w_in[0].astype(BF16)
    half_heads = N_Q_HEADS // 2
    w_q_b = (w_in_b[:, :Q_DIM].reshape(D, 2, half_heads, HEAD_DIM)
             .transpose(0, 2, 1, 3).reshape(D, Q_DIM))

    cos_t, sin_t = _rope_tables(S)
    reps = LANES // HEAD_DIM
    qg = jnp.tile(q_norm_g[0], reps)[None, :]
    kg = jnp.tile(k_norm_g[0], reps)[None, :]
    lng = gm_ln_g[0].reshape(1, GM_DIM)
    lnb = gm_ln_b[0].reshape(1, GM_DIM)
    gid = np.arange(LANES) // HEAD_DIM
    gmat = (gid[:, None] == gid[None, :]).astype(np.float32)
    gmat2 = jnp.asarray(np.concatenate([gmat, gmat], axis=0), dtype=BF16)

    two_axes = ("arbitrary", "arbitrary")
    cparams = pltpu.CompilerParams(dimension_semantics=two_axes,
                                   vmem_limit_bytes=VMEM_LIMIT_BYTES)

    M = B * S
    n_seq = S // TM_IN
    n_blk = M // TM_IN
    ck_in = TM_IN // TK

    def in_blk(i):
        return jnp.minimum(i, n_blk - 1)

    def out_blk(i):
        return jnp.maximum(i - 1, 0)

    def out_rows(width):
        return pl.BlockSpec((TM_IN, width), lambda i: (out_blk(i), 0))

    qt, k, vt, u, vn, gates = pl.pallas_call(
        _inproj_kernel,
        grid=(n_blk + 1,),
        in_specs=[
            pl.BlockSpec((TM_IN, D), lambda i: (in_blk(i), 0)),
            _const_spec((1, D)),
            _const_spec((D, Q_DIM)),
            _const_spec((D, in_dim)),
            _const_spec((2 * LANES, LANES)),
            pl.BlockSpec((TM_IN, LANES), lambda i: (out_blk(i) % n_seq, 0)),
            pl.BlockSpec((TM_IN, LANES), lambda i: (out_blk(i) % n_seq, 0)),
            _const_spec((1, LANES)),
            _const_spec((1, LANES)),
            _const_spec((1, GM_DIM)),
            _const_spec((1, GM_DIM)),
        ],
        out_specs=[
            pl.BlockSpec((None, N_Q_HEADS, LANES, TM_IN),
                         lambda i: (out_blk(i) // n_seq, 0, 0, out_blk(i) % n_seq)),
            out_rows(KV_DIM),
            pl.BlockSpec((ck_in, KV_DIM, TK), lambda i: (out_blk(i), 0, 0)),
            out_rows(GM_DIM), out_rows(GM_DIM), out_rows(2 * D),
        ],
        out_shape=[
            jax.ShapeDtypeStruct((B, N_Q_HEADS, LANES, S), BF16),
            jax.ShapeDtypeStruct((M, KV_DIM), BF16),
            jax.ShapeDtypeStruct((M // TK, KV_DIM, TK), BF16),
            jax.ShapeDtypeStruct((M, GM_DIM), F32),
            jax.ShapeDtypeStruct((M, GM_DIM), BF16),
            jax.ShapeDtypeStruct((M, 2 * D), F32),
        ],
        scratch_shapes=[pltpu.VMEM((TM_IN, in_dim), F32), pltpu.VMEM((TM_IN, in_dim), F32)],
        compiler_params=pltpu.CompilerParams(dimension_semantics=("arbitrary",),
                                             vmem_limit_bytes=VMEM_LIMIT_BYTES),
        name="inproj",
    )(x.reshape(M, D), norm_mix_g, w_q_b, w_in_b, gmat2, cos_t, sin_t, qg, kg, lng, lnb)
    k = k.reshape(B, S, KV_DIM)
    vt = vt.reshape(B, S // TK, KV_DIM, TK)

    def attention(fixed_shift):
        return pl.pallas_call(
            functools.partial(_attn_kernel, fixed_shift=fixed_shift),
            grid=(B, S // TQ),
            in_specs=[
                pl.BlockSpec(memory_space=pltpu.SMEM),
                pl.BlockSpec((None, N_Q_HEADS, LANES, TQ), lambda b, i: (b, 0, 0, i)),
                pl.BlockSpec((None, S, KV_DIM), lambda b, i: (b, 0, 0)),
                pl.BlockSpec((None, S // TK, KV_DIM, TK), lambda b, i: (b, 0, 0, 0)),
            ],
            out_specs=pl.BlockSpec((None, Q_DIM, TQ), lambda b, i: (b, 0, i)),
            out_shape=jax.ShapeDtypeStruct((B, Q_DIM, S), BF16),
            scratch_shapes=[
                pltpu.VMEM((N_Q_HEADS, TK, TQ), F32),
                pltpu.VMEM((N_Q_HEADS, TK, TQ), F32),
                pltpu.VMEM((N_Q_HEADS, 1, TQ), F32),
                pltpu.VMEM((N_Q_HEADS, SUBLANES, TQ), F32),
                pltpu.VMEM((N_Q_HEADS, HEAD_DIM, TQ), F32),
            ],
            compiler_params=cparams,
            name="gqa_attention_shift" if fixed_shift else "gqa_attention_online",
        )

    score_bound = (HEAD_DIM * QK_SCALE_LOG2 * (1.0 + 2.0 ** -6)
                   * jnp.max(jnp.abs(q_norm_g[0])) * jnp.max(jnp.abs(k_norm_g[0])))
    attn_t = lax.cond(
        score_bound <= MAX_FIXED_SHIFT,
        lambda s_, *ops: attention(True)(s_, *ops),
        lambda s_, *ops: attention(False)(s_, *ops),
        score_bound.reshape(1).astype(F32), qt, k, vt)

    bs_full = jnp.repeat(jnp.transpose(b_s[0]), GM_GROUP_DIM, axis=1)
    ws_pairs = (w_s[0].reshape(GM_GROUPS // 2, 2, GM_CHUNK, GM_CHUNK)
                .transpose(0, 2, 1, 3).reshape(GM_GROUPS // 2, GM_CHUNK, 2 * GM_CHUNK).astype(BF16))
    n_seq_out = S // TM_OUT
    n_out = M // TM_OUT

    def s1_blk(i):
        return jnp.minimum(i, n_out - 1)

    def s1_rows(width):
        return pl.BlockSpec((TM_OUT, width), lambda i: (s1_blk(i), 0))

    out = pl.pallas_call(
        _merge_ffn_kernel,
        grid=(n_out + 1,),
        in_specs=[
            s1_rows(D),
            pl.BlockSpec((None, Q_DIM, TM_OUT),
                         lambda i: (s1_blk(i) // n_seq_out, 0, s1_blk(i) % n_seq_out)),
            s1_rows(GM_DIM), s1_rows(GM_DIM), s1_rows(2 * D),
            _const_spec((GM_GROUPS // 2, GM_CHUNK, 2 * GM_CHUNK)),
            _const_spec((GM_CHUNK, GM_DIM)),
            _const_spec((Q_DIM, D)),
            _const_spec((GM_DIM, D)),
            _const_spec((D, D)),
            _const_spec((1, D)),
            _const_spec((D, 2 * d_ff)),
            _const_spec((d_ff, D)),
            _const_spec((1, D)),
        ],
        out_specs=pl.BlockSpec((TM_OUT, D), lambda i: (jnp.maximum(i - 1, 0), 0)),
        out_shape=jax.ShapeDtypeStruct((M, D), F32),
        scratch_shapes=[
            pltpu.VMEM((TM_OUT, GM_DIM), BF16),
            pltpu.VMEM((TM_OUT, d_ff), BF16),
            pltpu.VMEM((TM_OUT, D), F32),
            pltpu.VMEM((TM_OUT, D), F32),
        ],
        compiler_params=pltpu.CompilerParams(dimension_semantics=("arbitrary",),
                                             vmem_limit_bytes=VMEM_LIMIT_BYTES),
        name="merge_ffn",
    )(x.reshape(M, D), attn_t, u, vn, gates, ws_pairs, bs_full,
      w_proj_a[0].astype(BF16), w_proj_b[0].astype(BF16), w_out[0].astype(BF16), norm_ffn_g,
      w_gate_up[0].astype(BF16), w_down[0].astype(BF16), norm_final_g[None, :])
    return out.reshape(B, S, D)
```

```python
import functools
import math

import jax
import jax.numpy as jnp
import numpy as np
from jax import lax
from jax.experimental import pallas as pl
from jax.experimental.pallas import tpu as pltpu

HEAD_DIM = 64
N_Q_HEADS = 8
N_KV_HEADS = 2
Q_DIM = N_Q_HEADS * HEAD_DIM
KV_DIM = N_KV_HEADS * HEAD_DIM
GM_GROUPS = 8
GM_GROUP_DIM = 64
GM_DIM = GM_GROUPS * GM_GROUP_DIM
GM_CHUNK = 128
GRID_W = 64
ROPE_THETA = 10000.0
EPS = 1e-6

LANES = 128
SUBLANES = 8
VMEM_LIMIT_BYTES = 56 * 1024 * 1024

TM_IN = 256
TQ = 256
TK = 256
QK_SCALE_LOG2 = math.log2(math.e) / math.sqrt(HEAD_DIM)
MAX_FIXED_SHIFT = 48.0
CHUNKS_PER_BODY = 16
TM_OUT = 256
CAST_ROWS = 16
FF_CHUNK = 768

F32 = jnp.float32
BF16 = jnp.bfloat16


def _dot(a, b):
    return jnp.dot(a, b, preferred_element_type=F32)


def _group_sum(x, gmat2):
    hi = x.astype(BF16)
    lo = (x - hi.astype(F32)).astype(BF16)
    return _dot(jnp.concatenate([hi, lo], axis=1), gmat2)


def _rope_partner(x, lane):
    fwd = pltpu.roll(x, LANES - 16, axis=1)
    bwd = pltpu.roll(x, 16, axis=1)
    return jnp.where((lane & 16) == 0, fwd, bwd)


def _inproj_kernel(x_ref, g_ref, wq_ref, w_ref, gmat_ref, cos_ref, sin_ref, qg_ref, kg_ref,
                   lng_ref, lnb_ref,
                   qt_ref, k_ref, vt_ref, u_ref, vn_ref, gate_ref, ya_ref, yb_ref):
    i = pl.program_id(0)
    tm = x_ref.shape[0]

    @pl.when(i == 0)
    def _():
        yb_ref[...] = jnp.zeros(yb_ref.shape, F32)

    def compute(y_w, y_r):
        x = x_ref[...]
        ms = jnp.mean(x * x, axis=-1, keepdims=True)
        h = (x * lax.rsqrt(ms + EPS) * g_ref[...]).astype(BF16)
        gmat = gmat_ref[...]
        lane = lax.broadcasted_iota(jnp.int32, (tm, LANES), 1)

        def project(c0, c1):
            y_w[:, c0:c1] = _dot(h, w_ref[:, c0:c1])

        y_w[:, :Q_DIM] = _dot(h, wq_ref[...])

        v0 = Q_DIM + KV_DIM
        uv0 = Q_DIM + 2 * KV_DIM
        g0 = uv0 + 2 * GM_DIM
        g1 = g0 + (w_ref.shape[1] - g0) // 2

        n_qk = (Q_DIM + KV_DIM) // LANES
        yqk = [y_r[:, j * LANES:(j + 1) * LANES] for j in range(n_qk)]
        project(Q_DIM, uv0)
        n_gate = w_ref.shape[1] - g0
        gate_ref[:, :n_gate // 2] = jax.nn.sigmoid(y_r[:, g0:g1])
        ss = [_group_sum(y * y, gmat) for y in yqk]
        project(uv0, g0)

        def norm_rope(y, s2, gain):
            y = y * lax.rsqrt(s2 * (1.0 / HEAD_DIM) + EPS) * gain
            return y * cos_ref[...] + _rope_partner(y, lane) * sin_ref[...]

        row = lax.broadcasted_iota(jnp.int32, (LANES, tm), 0)
        top = row < HEAD_DIM
        half_heads = N_Q_HEADS // 2
        for j in range(Q_DIM // LANES):
            qt = (norm_rope(yqk[j], ss[j], qg_ref[...]) * QK_SCALE_LOG2).T
            zero = jnp.zeros_like(qt)
            qt_ref[j] = jnp.where(top, qt, zero).astype(BF16)
            qt_ref[j + half_heads] = jnp.where(top, zero, qt).astype(BF16)
        k_ref[...] = norm_rope(yqk[n_qk - 1], ss[n_qk - 1], kg_ref[...]).astype(BF16)
        vt = y_r[:, v0:uv0].T.astype(BF16)
        for c in range(tm // TK):
            vt_ref[c] = vt[:, c * TK:(c + 1) * TK]

        gate_ref[:, n_gate // 2:] = jax.nn.sigmoid(y_r[:, g1:])
        u_ref[...] = jax.nn.gelu(y_r[:, uv0:uv0 + GM_DIM])
        n_gm = GM_DIM // LANES
        vv = [jax.nn.gelu(y_r[:, uv0 + GM_DIM + j * LANES:uv0 + GM_DIM + (j + 1) * LANES])
              for j in range(n_gm)]
        project(g0, g1)
        mu = [_group_sum(v, gmat) * (1.0 / GM_GROUP_DIM) for v in vv]
        gq = g1 + (w_ref.shape[1] - g1) // 2
        project(g1, gq)
        dv = [v - m for v, m in zip(vv, mu)]
        var = [_group_sum(d * d, gmat) * (1.0 / GM_GROUP_DIM) for d in dv]
        project(gq, w_ref.shape[1])
        for j in range(n_gm):
            sl = slice(j * LANES, (j + 1) * LANES)
            vn = dv[j] * lax.rsqrt(var[j] + EPS) * lng_ref[:, sl] + lnb_ref[:, sl]
            vn_ref[:, sl] = vn.astype(BF16)

    @pl.when(i % 2 == 0)
    def _():
        compute(ya_ref, yb_ref)

    @pl.when(i % 2 == 1)
    def _():
        compute(yb_ref, ya_ref)


def _sublane_partial_sum(e):
    rows, lanes = e.shape
    return jnp.sum(e.reshape(rows // SUBLANES, SUBLANES, lanes), axis=0)


def _attn_kernel(shift_ref, qt_ref, k_ref, vt_ref, *refs, fixed_shift, n_cast):
    cast_in, o_ref, cast_out = refs[:n_cast], refs[n_cast], refs[n_cast + 1:2 * n_cast + 1]
    sa_ref, sb_ref, m_ref, l_ref, acc_ref = refs[2 * n_cast + 1:]
    for src, dst in zip(cast_in, cast_out):
        dst[...] = src[...].astype(dst.dtype)

    n_heads = qt_ref.shape[0]
    group = n_heads // N_KV_HEADS
    n_chunks = k_ref.shape[0] // TK
    if fixed_shift:
        shift = shift_ref[0]
    else:
        m_ref[...] = jnp.full(m_ref.shape, -jnp.inf, F32)
    acc_ref[...] = jnp.zeros(acc_ref.shape, F32)
    l_ref[...] = jnp.zeros(l_ref.shape, F32)

    def keys(c):
        return k_ref[pl.ds(pl.multiple_of(c * TK, TK), TK), :]

    def step(c, cur_ref, next_ref):
        if next_ref is not None:
            k_next = keys(c + 1)
        vt_c = vt_ref[c]
        for h in range(n_heads):
            g = h // group
            if next_ref is not None:
                next_ref[h] = _dot(k_next, qt_ref[h])
            s = cur_ref[h]
            vt_g = vt_c[g * HEAD_DIM:(g + 1) * HEAD_DIM, :]
            if fixed_shift:
                e = jnp.exp2(s - shift)
                l_ref[h] += _sublane_partial_sum(e)
                acc_ref[h] += _dot(vt_g, e.astype(BF16))
            else:
                m_old = m_ref[h]
                m_new = jnp.maximum(m_old, jnp.max(s, axis=0, keepdims=True))
                alpha = jnp.exp2(m_old - m_new)
                e = jnp.exp2(s - m_new)
                m_ref[h] = m_new
                l_ref[h] = alpha * l_ref[h] + _sublane_partial_sum(e)
                acc_ref[h] = alpha * acc_ref[h] + _dot(vt_g, e.astype(BF16))

    k_first = keys(0)
    for h in range(n_heads):
        sa_ref[h] = _dot(k_first, qt_ref[h])

    bufs = (sa_ref, sb_ref)

    def steps(c0, last):
        for j in range(CHUNKS_PER_BODY):
            final = last and j == CHUNKS_PER_BODY - 1
            step(c0 + j, bufs[j % 2], None if final else bufs[(j + 1) % 2])

    def body(i, carry):
        steps(CHUNKS_PER_BODY * i, False)
        return carry

    lax.fori_loop(0, n_chunks // CHUNKS_PER_BODY - 1, body, 0)
    steps(n_chunks - CHUNKS_PER_BODY, True)
    for h in range(n_heads):
        o = acc_ref[h] / jnp.sum(l_ref[h], axis=0, keepdims=True)
        o_ref[h * HEAD_DIM:(h + 1) * HEAD_DIM, :] = o.astype(BF16)


def _merge_ffn_kernel(x_ref, at_ref, u_ref, vn_ref, gate_ref, ws_ref, bs_ref,
                      wpa_ref, wpb_ref, wo_ref, gf_ref, wgu_ref, wd_ref, gfin_ref,
                      o_ref, gm_ref, act_ref, xa_ref, xb_ref):
    i = pl.program_id(0)
    tm = x_ref.shape[0]
    d_model = x_ref.shape[1]
    d_ff = wd_ref.shape[0]

    @pl.when(i == 0)
    def _():
        xb_ref[...] = jnp.zeros(xb_ref.shape, F32)

    def compute(x1_w, x1_r):
        x1p = x1_r[...]
        ms = jnp.mean(x1p * x1p, axis=-1, keepdims=True)
        h = (x1p * lax.rsqrt(ms + EPS) * gf_ref[...]).astype(BF16)

        lane = lax.broadcasted_iota(jnp.int32, (GM_CHUNK, LANES), 1)
        low = lane < GM_GROUP_DIM
        for c in range(tm // GM_CHUNK):
            rs = slice(c * GM_CHUNK, (c + 1) * GM_CHUNK)
            for j in range(GM_DIM // LANES):
                cs = slice(j * LANES, (j + 1) * LANES)
                vblk = vn_ref[rs, cs]
                zero = jnp.zeros_like(vblk)
                vsplit = jnp.concatenate([jnp.where(low, vblk, zero), jnp.where(low, zero, vblk)],
                                         axis=0)
                sv = _dot(ws_ref[j], vsplit)
                gm_ref[rs, cs] = (u_ref[rs, cs] * (sv + bs_ref[:, cs])).astype(BF16)

        pa = lax.dot_general(at_ref[...], wpa_ref[...], (((0,), (0,)), ((), ())),
                             preferred_element_type=F32)
        pb = _dot(gm_ref[...], wpb_ref[...])
        mix = (gate_ref[:, :d_model] * pa + gate_ref[:, d_model:] * pb).astype(BF16)

        for c0 in range(0, d_ff, FF_CHUNK):
            c1 = min(c0 + FF_CHUNK, d_ff)
            gt = _dot(h, wgu_ref[:, c0:c1])
            up = _dot(h, wgu_ref[:, d_ff + c0:d_ff + c1])
            act_ref[:, c0:c1] = (jax.nn.silu(gt) * up).astype(BF16)
        x2 = x1p + _dot(act_ref[...], wd_ref[...])
        ms2 = jnp.mean(x2 * x2, axis=-1, keepdims=True)
        o_ref[...] = x2 * lax.rsqrt(ms2 + EPS) * gfin_ref[...]

        x1_w[...] = x_ref[...] + _dot(mix, wo_ref[...])

    @pl.when(i % 2 == 0)
    def _():
        compute(xa_ref, xb_ref)

    @pl.when(i % 2 == 1)
    def _():
        compute(xb_ref, xa_ref)


def _const_spec(shape):
    nd = len(shape)
    return pl.BlockSpec(shape, lambda *_: (0,) * nd, pipeline_mode=pl.Buffered(1))


def _rope_tables(seq):
    half = HEAD_DIM // 2
    rows = seq // GRID_W
    row = np.repeat(np.arange(rows, dtype=np.float64), GRID_W)
    col = np.tile(np.arange(GRID_W, dtype=np.float64), rows)
    inv_freq = 1.0 / (ROPE_THETA ** (np.arange(0, half, 2, dtype=np.float64) / half))
    ang_r = row[:, None] * inv_freq[None, :]
    ang_c = col[:, None] * inv_freq[None, :]
    cr, sr = np.cos(ang_r), np.sin(ang_r)
    cc, sc = np.cos(ang_c), np.sin(ang_c)
    cos64 = np.concatenate([cr, cr, cc, cc], axis=-1)
    sin64 = np.concatenate([-sr, sr, -sc, sc], axis=-1)
    reps = (1, LANES // HEAD_DIM)
    return np.tile(cos64, reps).astype(np.float32), np.tile(sin64, reps).astype(np.float32)


def kernel(x, norm_mix_g, w_in, q_norm_g, k_norm_g, gm_ln_g, gm_ln_b, w_s, b_s, w_proj_a,
           w_proj_b, w_out, norm_ffn_g, w_gate_up, w_down, norm_final_g):
    B, S, D = x.shape
    in_dim = w_in.shape[-1]
    d_ff = w_down.shape[1]
    assert norm_mix_g.shape[0] == 1, "single-layer block"
    assert CHUNKS_PER_BODY % 2 == 0 and S % (CHUNKS_PER_BODY * TK) == 0
    assert S % TQ == 0 and S % TM_IN == 0 and S % TM_OUT == 0
    assert TM_OUT % GM_CHUNK == 0 and TM_IN % TK == 0

    w_in_b = w_in[0].astype(BF16)
    half_heads = N_Q_HEADS // 2
    w_q_b = (w_in_b[:, :Q_DIM].reshape(D, 2, half_heads, HEAD_DIM)
             .transpose(0, 2, 1, 3).reshape(D, Q_DIM))

    cos_t, sin_t = _rope_tables(S)
    reps = LANES // HEAD_DIM
    qg = jnp.tile(q_norm_g[0], reps)[None, :]
    kg = jnp.tile(k_norm_g[0], reps)[None, :]
    lng = gm_ln_g[0].reshape(1, GM_DIM)
    lnb = gm_ln_b[0].reshape(1, GM_DIM)
    gid = np.arange(LANES) // HEAD_DIM
    gmat = (gid[:, None] == gid[None, :]).astype(np.float32)
    gmat2 = jnp.asarray(np.concatenate([gmat, gmat], axis=0), dtype=BF16)

    two_axes = ("arbitrary", "arbitrary")
    cparams = pltpu.CompilerParams(dimension_semantics=two_axes,
                                   vmem_limit_bytes=VMEM_LIMIT_BYTES)

    M = B * S
    n_seq = S // TM_IN
    n_blk = M // TM_IN
    ck_in = TM_IN // TK

    def in_blk(i):
        return jnp.minimum(i, n_blk - 1)

    def out_blk(i):
        return jnp.maximum(i - 1, 0)

    def out_rows(width):
        return pl.BlockSpec((TM_IN, width), lambda i: (out_blk(i), 0))

    qt, k, vt, u, vn, gates = pl.pallas_call(
        _inproj_kernel,
        grid=(n_blk + 1,),
        in_specs=[
            pl.BlockSpec((TM_IN, D), lambda i: (in_blk(i), 0)),
            _const_spec((1, D)),
            _const_spec((D, Q_DIM)),
            _const_spec((D, in_dim)),
            _const_spec((2 * LANES, LANES)),
            pl.BlockSpec((TM_IN, LANES), lambda i: (out_blk(i) % n_seq, 0)),
            pl.BlockSpec((TM_IN, LANES), lambda i: (out_blk(i) % n_seq, 0)),
            _const_spec((1, LANES)),
            _const_spec((1, LANES)),
            _const_spec((1, GM_DIM)),
            _const_spec((1, GM_DIM)),
        ],
        out_specs=[
            pl.BlockSpec((None, N_Q_HEADS, LANES, TM_IN),
                         lambda i: (out_blk(i) // n_seq, 0, 0, out_blk(i) % n_seq)),
            out_rows(KV_DIM),
            pl.BlockSpec((ck_in, KV_DIM, TK), lambda i: (out_blk(i), 0, 0)),
            out_rows(GM_DIM), out_rows(GM_DIM), out_rows(2 * D),
        ],
        out_shape=[
            jax.ShapeDtypeStruct((B, N_Q_HEADS, LANES, S), BF16),
            jax.ShapeDtypeStruct((M, KV_DIM), BF16),
            jax.ShapeDtypeStruct((M // TK, KV_DIM, TK), BF16),
            jax.ShapeDtypeStruct((M, GM_DIM), F32),
            jax.ShapeDtypeStruct((M, GM_DIM), BF16),
            jax.ShapeDtypeStruct((M, 2 * D), F32),
        ],
        scratch_shapes=[pltpu.VMEM((TM_IN, in_dim), F32), pltpu.VMEM((TM_IN, in_dim), F32)],
        compiler_params=pltpu.CompilerParams(dimension_semantics=("arbitrary",),
                                             vmem_limit_bytes=VMEM_LIMIT_BYTES),
        name="inproj",
    )(x.reshape(M, D), norm_mix_g, w_q_b, w_in_b, gmat2, cos_t, sin_t, qg, kg, lng, lnb)
    k = k.reshape(B, S, KV_DIM)
    vt = vt.reshape(B, S // TK, KV_DIM, TK)

    n_q = S // TQ
    cast_rows = CAST_ROWS * B * n_q
    to_cast = [w_gate_up[0], w_down[0], w_out[0], w_proj_a[0], w_proj_b[0]]
    cast_views = [w.reshape(cast_rows, w.size // cast_rows) for w in to_cast]
    cast_specs = [pl.BlockSpec((CAST_ROWS, v.shape[1]), lambda b, i: (b * n_q + i, 0))
                  for v in cast_views]

    def attention(fixed_shift):
        return pl.pallas_call(
            functools.partial(_attn_kernel, fixed_shift=fixed_shift, n_cast=len(cast_views)),
            grid=(B, n_q),
            in_specs=[
                pl.BlockSpec(memory_space=pltpu.SMEM),
                pl.BlockSpec((None, N_Q_HEADS, LANES, TQ), lambda b, i: (b, 0, 0, i)),
                pl.BlockSpec((None, S, KV_DIM), lambda b, i: (b, 0, 0)),
                pl.BlockSpec((None, S // TK, KV_DIM, TK), lambda b, i: (b, 0, 0, 0)),
            ] + cast_specs,
            out_specs=[pl.BlockSpec((None, Q_DIM, TQ), lambda b, i: (b, 0, i))] + cast_specs,
            out_shape=[jax.ShapeDtypeStruct((B, Q_DIM, S), BF16)]
            + [jax.ShapeDtypeStruct(v.shape, BF16) for v in cast_views],
            scratch_shapes=[
                pltpu.VMEM((N_Q_HEADS, TK, TQ), F32),
                pltpu.VMEM((N_Q_HEADS, TK, TQ), F32),
                pltpu.VMEM((N_Q_HEADS, 1, TQ), F32),
                pltpu.VMEM((N_Q_HEADS, SUBLANES, TQ), F32),
                pltpu.VMEM((N_Q_HEADS, HEAD_DIM, TQ), F32),
            ],
            compiler_params=cparams,
            name="gqa_attention_shift" if fixed_shift else "gqa_attention_online",
        )

    score_bound = (HEAD_DIM * QK_SCALE_LOG2 * (1.0 + 2.0 ** -6)
                   * jnp.max(jnp.abs(q_norm_g[0])) * jnp.max(jnp.abs(k_norm_g[0])))
    attn_t, *cast_done = lax.cond(
        score_bound <= MAX_FIXED_SHIFT,
        lambda s_, *ops: attention(True)(s_, *ops),
        lambda s_, *ops: attention(False)(s_, *ops),
        score_bound.reshape(1).astype(F32), qt, k, vt, *cast_views)
    wgu_b, wd_b, wo_b, wpa_b, wpb_b = [c.reshape(w.shape) for c, w in zip(cast_done, to_cast)]

    bs_full = jnp.repeat(jnp.transpose(b_s[0]), GM_GROUP_DIM, axis=1)
    ws_pairs = (w_s[0].reshape(GM_GROUPS // 2, 2, GM_CHUNK, GM_CHUNK)
                .transpose(0, 2, 1, 3).reshape(GM_GROUPS // 2, GM_CHUNK, 2 * GM_CHUNK).astype(BF16))
    n_seq_out = S // TM_OUT
    n_out = M // TM_OUT

    def s1_blk(i):
        return jnp.minimum(i, n_out - 1)

    def s1_rows(width):
        return pl.BlockSpec((TM_OUT, width), lambda i: (s1_blk(i), 0))

    out = pl.pallas_call(
        _merge_ffn_kernel,
        grid=(n_out + 1,),
        in_specs=[
            s1_rows(D),
            pl.BlockSpec((None, Q_DIM, TM_OUT),
                         lambda i: (s1_blk(i) // n_seq_out, 0, s1_blk(i) % n_seq_out)),
            s1_rows(GM_DIM), s1_rows(GM_DIM), s1_rows(2 * D),
            _const_spec((GM_GROUPS // 2, GM_CHUNK, 2 * GM_CHUNK)),
            _const_spec((GM_CHUNK, GM_DIM)),
            _const_spec((Q_DIM, D)),
            _const_spec((GM_DIM, D)),
            _const_spec((D, D)),
            _const_spec((1, D)),
            _const_spec((D, 2 * d_ff)),
            _const_spec((d_ff, D)),
            _const_spec((1, D)),
        ],
        out_specs=pl.BlockSpec((TM_OUT, D), lambda i: (jnp.maximum(i - 1, 0), 0)),
        out_shape=jax.ShapeDtypeStruct((M, D), F32),
        scratch_shapes=[
            pltpu.VMEM((TM_OUT, GM_DIM), BF16),
            pltpu.VMEM((TM_OUT, d_ff), BF16),
            pltpu.VMEM((TM_OUT, D), F32),
            pltpu.VMEM((TM_OUT, D), F32),
        ],
        compiler_params=pltpu.CompilerParams(dimension_semantics=("arbitrary",),
                                             vmem_limit_bytes=VMEM_LIMIT_BYTES),
        name="merge_ffn",
    )(x.reshape(M, D), attn_t, u, vn, gates, ws_pairs, bs_full,
      wpa_b, wpb_b, wo_b, norm_ffn_g, wgu_b, wd_b, norm_final_g[None, :])
    return out.reshape(B, S, D)
```

```python
import functools
import math

import jax
import jax.numpy as jnp
import numpy as np
from jax import lax
from jax.experimental import pallas as pl
from jax.experimental.pallas import tpu as pltpu

HEAD_DIM = 64
N_Q_HEADS = 8
N_KV_HEADS = 2
Q_DIM = N_Q_HEADS * HEAD_DIM
KV_DIM = N_KV_HEADS * HEAD_DIM
GM_GROUPS = 8
GM_GROUP_DIM = 64
GM_DIM = GM_GROUPS * GM_GROUP_DIM
GM_CHUNK = 128
GRID_W = 64
ROPE_THETA = 10000.0
EPS = 1e-6

LANES = 128
SUBLANES = 8
VMEM_LIMIT_BYTES = 56 * 1024 * 1024

TM_IN = 256
TQ = 256
TK = 256
QK_SCALE_LOG2 = math.log2(math.e) / math.sqrt(HEAD_DIM)
MAX_FIXED_SHIFT = 48.0
CHUNKS_PER_BODY = 16
TM_OUT = 256
CAST_ROWS = 16
FF_CHUNK = 768

F32 = jnp.float32
BF16 = jnp.bfloat16


def _dot(a, b):
    return jnp.dot(a, b, preferred_element_type=F32)


def _group_sum(x, gmat2):
    hi = x.astype(BF16)
    lo = (x - hi.astype(F32)).astype(BF16)
    return _dot(jnp.concatenate([hi, lo], axis=1), gmat2)


def _rope_partner(x, lane):
    fwd = pltpu.roll(x, LANES - 16, axis=1)
    bwd = pltpu.roll(x, 16, axis=1)
    return jnp.where((lane & 16) == 0, fwd, bwd)


def _inproj_kernel(x_ref, g_ref, wq_ref, w_ref, gmat_ref, cos_ref, sin_ref, qg_ref, kg_ref,
                   lng_ref, lnb_ref, *refs, cast_steps):
    n_cast = len(cast_steps)
    cast_in = refs[:n_cast]
    qt_ref, k_ref, vt_ref, u_ref, vn_ref, gate_ref = refs[n_cast:n_cast + 6]
    cast_out = refs[n_cast + 6:2 * n_cast + 6]
    ya_ref, yb_ref = refs[2 * n_cast + 6:]
    i = pl.program_id(0)
    tm = x_ref.shape[0]

    @pl.when(i == 0)
    def _():
        yb_ref[...] = jnp.zeros(yb_ref.shape, F32)

    for src, dst, n_steps in zip(cast_in, cast_out, cast_steps):
        @pl.when(i < n_steps)
        def _(src=src, dst=dst):
            dst[...] = src[...].astype(dst.dtype)

    def compute(y_w, y_r):
        x = x_ref[...]
        ms = jnp.mean(x * x, axis=-1, keepdims=True)
        h = (x * lax.rsqrt(ms + EPS) * g_ref[...]).astype(BF16)
        gmat = gmat_ref[...]
        lane = lax.broadcasted_iota(jnp.int32, (tm, LANES), 1)

        def project(c0, c1):
            y_w[:, c0:c1] = _dot(h, w_ref[:, c0:c1])

        y_w[:, :Q_DIM] = _dot(h, wq_ref[...])

        v0 = Q_DIM + KV_DIM
        uv0 = Q_DIM + 2 * KV_DIM
        g0 = uv0 + 2 * GM_DIM
        g1 = g0 + (w_ref.shape[1] - g0) // 2

        n_qk = (Q_DIM + KV_DIM) // LANES
        yqk = [y_r[:, j * LANES:(j + 1) * LANES] for j in range(n_qk)]
        project(Q_DIM, uv0)
        n_gate = w_ref.shape[1] - g0
        gate_ref[:, :n_gate // 2] = jax.nn.sigmoid(y_r[:, g0:g1])
        ss = [_group_sum(y * y, gmat) for y in yqk]
        project(uv0, g0)

        def norm_rope(y, s2, gain):
            y = y * lax.rsqrt(s2 * (1.0 / HEAD_DIM) + EPS) * gain
            return y * cos_ref[...] + _rope_partner(y, lane) * sin_ref[...]

        row = lax.broadcasted_iota(jnp.int32, (LANES, tm), 0)
        top = row < HEAD_DIM
        half_heads = N_Q_HEADS // 2
        for j in range(Q_DIM // LANES):
            qt = (norm_rope(yqk[j], ss[j], qg_ref[...]) * QK_SCALE_LOG2).T
            zero = jnp.zeros_like(qt)
            qt_ref[j] = jnp.where(top, qt, zero).astype(BF16)
            qt_ref[j + half_heads] = jnp.where(top, zero, qt).astype(BF16)
        k_ref[...] = norm_rope(yqk[n_qk - 1], ss[n_qk - 1], kg_ref[...]).astype(BF16)
        vt = y_r[:, v0:uv0].T.astype(BF16)
        for c in range(tm // TK):
            vt_ref[c] = vt[:, c * TK:(c + 1) * TK]

        gate_ref[:, n_gate // 2:] = jax.nn.sigmoid(y_r[:, g1:])
        u_ref[...] = jax.nn.gelu(y_r[:, uv0:uv0 + GM_DIM])
        n_gm = GM_DIM // LANES
        vv = [jax.nn.gelu(y_r[:, uv0 + GM_DIM + j * LANES:uv0 + GM_DIM + (j + 1) * LANES])
              for j in range(n_gm)]
        project(g0, g1)
        mu = [_group_sum(v, gmat) * (1.0 / GM_GROUP_DIM) for v in vv]
        gq = g1 + (w_ref.shape[1] - g1) // 2
        project(g1, gq)
        dv = [v - m for v, m in zip(vv, mu)]
        var = [_group_sum(d * d, gmat) * (1.0 / GM_GROUP_DIM) for d in dv]
        project(gq, w_ref.shape[1])
        for j in range(n_gm):
            sl = slice(j * LANES, (j + 1) * LANES)
            vn = dv[j] * lax.rsqrt(var[j] + EPS) * lng_ref[:, sl] + lnb_ref[:, sl]
            vn_ref[:, sl] = vn.astype(BF16)

    @pl.when(i % 2 == 0)
    def _():
        compute(ya_ref, yb_ref)

    @pl.when(i % 2 == 1)
    def _():
        compute(yb_ref, ya_ref)


def _sublane_partial_sum(e):
    rows, lanes = e.shape
    return jnp.sum(e.reshape(rows // SUBLANES, SUBLANES, lanes), axis=0)


def _attn_kernel(shift_ref, qt_ref, k_ref, vt_ref, o_ref, sa_ref, sb_ref, m_ref, l_ref, acc_ref,
                 *, fixed_shift):
    n_heads = qt_ref.shape[0]
    group = n_heads // N_KV_HEADS
    n_chunks = k_ref.shape[0] // TK
    if fixed_shift:
        shift = shift_ref[0]
    else:
        m_ref[...] = jnp.full(m_ref.shape, -jnp.inf, F32)
    acc_ref[...] = jnp.zeros(acc_ref.shape, F32)
    l_ref[...] = jnp.zeros(l_ref.shape, F32)

    def keys(c):
        return k_ref[pl.ds(pl.multiple_of(c * TK, TK), TK), :]

    def step(c, cur_ref, next_ref):
        if next_ref is not None:
            k_next = keys(c + 1)
        vt_c = vt_ref[c]
        for h in range(n_heads):
            g = h // group
            if next_ref is not None:
                next_ref[h] = _dot(k_next, qt_ref[h])
            s = cur_ref[h]
            vt_g = vt_c[g * HEAD_DIM:(g + 1) * HEAD_DIM, :]
            if fixed_shift:
                e = jnp.exp2(s - shift)
                l_ref[h] += _sublane_partial_sum(e)
                acc_ref[h] += _dot(vt_g, e.astype(BF16))
            else:
                m_old = m_ref[h]
                m_new = jnp.maximum(m_old, jnp.max(s, axis=0, keepdims=True))
                alpha = jnp.exp2(m_old - m_new)
                e = jnp.exp2(s - m_new)
                m_ref[h] = m_new
                l_ref[h] = alpha * l_ref[h] + _sublane_partial_sum(e)
                acc_ref[h] = alpha * acc_ref[h] + _dot(vt_g, e.astype(BF16))

    k_first = keys(0)
    for h in range(n_heads):
        sa_ref[h] = _dot(k_first, qt_ref[h])

    bufs = (sa_ref, sb_ref)

    def steps(c0, last):
        for j in range(CHUNKS_PER_BODY):
            final = last and j == CHUNKS_PER_BODY - 1
            step(c0 + j, bufs[j % 2], None if final else bufs[(j + 1) % 2])

    def body(i, carry):
        steps(CHUNKS_PER_BODY * i, False)
        return carry

    lax.fori_loop(0, n_chunks // CHUNKS_PER_BODY - 1, body, 0)
    steps(n_chunks - CHUNKS_PER_BODY, True)
    for h in range(n_heads):
        o = acc_ref[h] / jnp.sum(l_ref[h], axis=0, keepdims=True)
        o_ref[h * HEAD_DIM:(h + 1) * HEAD_DIM, :] = o.astype(BF16)


def _merge_ffn_kernel(x_ref, at_ref, u_ref, vn_ref, gate_ref, ws_ref, bs_ref,
                      wpa_ref, wpb_ref, wo_ref, gf_ref, wgu_ref, wd_ref, gfin_ref,
                      o_ref, gm_ref, act_ref, xa_ref, xb_ref):
    i = pl.program_id(0)
    tm = x_ref.shape[0]
    d_model = x_ref.shape[1]
    d_ff = wd_ref.shape[0]

    @pl.when(i == 0)
    def _():
        xb_ref[...] = jnp.zeros(xb_ref.shape, F32)

    def compute(x1_w, x1_r):
        x1p = x1_r[...]
        ms = jnp.mean(x1p * x1p, axis=-1, keepdims=True)
        h = (x1p * lax.rsqrt(ms + EPS) * gf_ref[...]).astype(BF16)

        lane = lax.broadcasted_iota(jnp.int32, (GM_CHUNK, LANES), 1)
        low = lane < GM_GROUP_DIM
        for c in range(tm // GM_CHUNK):
            rs = slice(c * GM_CHUNK, (c + 1) * GM_CHUNK)
            for j in range(GM_DIM // LANES):
                cs = slice(j * LANES, (j + 1) * LANES)
                vblk = vn_ref[rs, cs]
                zero = jnp.zeros_like(vblk)
                vsplit = jnp.concatenate([jnp.where(low, vblk, zero), jnp.where(low, zero, vblk)],
                                         axis=0)
                sv = _dot(ws_ref[j], vsplit)
                gm_ref[rs, cs] = (u_ref[rs, cs] * (sv + bs_ref[:, cs])).astype(BF16)

        pa = lax.dot_general(at_ref[...], wpa_ref[...], (((0,), (0,)), ((), ())),
                             preferred_element_type=F32)
        pb = _dot(gm_ref[...], wpb_ref[...])
        mix = (gate_ref[:, :d_model] * pa + gate_ref[:, d_model:] * pb).astype(BF16)

        for c0 in range(0, d_ff, FF_CHUNK):
            c1 = min(c0 + FF_CHUNK, d_ff)
            gt = _dot(h, wgu_ref[:, c0:c1])
            up = _dot(h, wgu_ref[:, d_ff + c0:d_ff + c1])
            act_ref[:, c0:c1] = (jax.nn.silu(gt) * up).astype(BF16)
        x2 = x1p + _dot(act_ref[...], wd_ref[...])
        ms2 = jnp.mean(x2 * x2, axis=-1, keepdims=True)
        o_ref[...] = x2 * lax.rsqrt(ms2 + EPS) * gfin_ref[...]

        x1_w[...] = x_ref[...] + _dot(mix, wo_ref[...])

    @pl.when(i % 2 == 0)
    def _():
        compute(xa_ref, xb_ref)

    @pl.when(i % 2 == 1)
    def _():
        compute(xb_ref, xa_ref)


def _const_spec(shape):
    nd = len(shape)
    return pl.BlockSpec(shape, lambda *_: (0,) * nd, pipeline_mode=pl.Buffered(1))


def _rope_tables(seq):
    half = HEAD_DIM // 2
    rows = seq // GRID_W
    row = np.repeat(np.arange(rows, dtype=np.float64), GRID_W)
    col = np.tile(np.arange(GRID_W, dtype=np.float64), rows)
    inv_freq = 1.0 / (ROPE_THETA ** (np.arange(0, half, 2, dtype=np.float64) / half))
    ang_r = row[:, None] * inv_freq[None, :]
    ang_c = col[:, None] * inv_freq[None, :]
    cr, sr = np.cos(ang_r), np.sin(ang_r)
    cc, sc = np.cos(ang_c), np.sin(ang_c)
    cos64 = np.concatenate([cr, cr, cc, cc], axis=-1)
    sin64 = np.concatenate([-sr, sr, -sc, sc], axis=-1)
    reps = (1, LANES // HEAD_DIM)
    return np.tile(cos64, reps).astype(np.float32), np.tile(sin64, reps).astype(np.float32)


def kernel(x, norm_mix_g, w_in, q_norm_g, k_norm_g, gm_ln_g, gm_ln_b, w_s, b_s, w_proj_a,
           w_proj_b, w_out, norm_ffn_g, w_gate_up, w_down, norm_final_g):
    B, S, D = x.shape
    in_dim = w_in.shape[-1]
    d_ff = w_down.shape[1]
    assert norm_mix_g.shape[0] == 1, "single-layer block"
    assert CHUNKS_PER_BODY % 2 == 0 and S % (CHUNKS_PER_BODY * TK) == 0
    assert S % TQ == 0 and S % TM_IN == 0 and S % TM_OUT == 0
    assert TM_OUT % GM_CHUNK == 0 and TM_IN % TK == 0 and TM_IN == TQ and TM_OUT == TQ

    w_in_b = w_in[0].astype(BF16)
    half_heads = N_Q_HEADS // 2
    w_q_b = (w_in_b[:, :Q_DIM].reshape(D, 2, half_heads, HEAD_DIM)
             .transpose(0, 2, 1, 3).reshape(D, Q_DIM))

    cos_t, sin_t = _rope_tables(S)
    reps = LANES // HEAD_DIM
    qg = jnp.tile(q_norm_g[0], reps)[None, :]
    kg = jnp.tile(k_norm_g[0], reps)[None, :]
    lng = gm_ln_g[0].reshape(1, GM_DIM)
    lnb = gm_ln_b[0].reshape(1, GM_DIM)
    gid = np.arange(LANES) // HEAD_DIM
    gmat = (gid[:, None] == gid[None, :]).astype(np.float32)
    gmat2 = jnp.asarray(np.concatenate([gmat, gmat], axis=0), dtype=BF16)

    two_axes = ("arbitrary", "arbitrary")
    cparams = pltpu.CompilerParams(dimension_semantics=two_axes,
                                   vmem_limit_bytes=VMEM_LIMIT_BYTES)

    M = B * S
    n_seq = S // TM_IN
    n_blk = M // TM_IN
    ck_in = TM_IN // TK

    def in_blk(i):
        return jnp.minimum(i, n_blk - 1)

    def out_blk(i):
        return jnp.maximum(i - 1, 0)

    def out_rows(width):
        return pl.BlockSpec((TM_IN, width), lambda i: (out_blk(i), 0))

    to_cast = [(w_gate_up, n_blk), (w_out, n_blk), (w_down, d_ff // (CAST_ROWS * CAST_ROWS)),
               (w_proj_a, w_proj_a.shape[1] // CAST_ROWS), (w_proj_b, w_proj_b.shape[1] // CAST_ROWS)]
    cast_steps = tuple(n for _, n in to_cast)
    assert all(n <= n_blk and w.shape[1] % n == 0 for w, n in to_cast)
    cast_specs = [pl.BlockSpec((None, w.shape[1] // n, w.shape[2]),
                               lambda i, n=n: (0, jnp.minimum(i, n - 1), 0)) for w, n in to_cast]

    qt, k, vt, u, vn, gates, wgu_b, wo_b, wd_b, wpa_b, wpb_b = pl.pallas_call(
        functools.partial(_inproj_kernel, cast_steps=cast_steps),
        grid=(n_blk + 1,),
        in_specs=[
            pl.BlockSpec((TM_IN, D), lambda i: (in_blk(i), 0)),
            _const_spec((1, D)),
            _const_spec((D, Q_DIM)),
            _const_spec((D, in_dim)),
            _const_spec((2 * LANES, LANES)),
            pl.BlockSpec((TM_IN, LANES), lambda i: (out_blk(i) % n_seq, 0)),
            pl.BlockSpec((TM_IN, LANES), lambda i: (out_blk(i) % n_seq, 0)),
            _const_spec((1, LANES)),
            _const_spec((1, LANES)),
            _const_spec((1, GM_DIM)),
            _const_spec((1, GM_DIM)),
        ] + cast_specs,
        out_specs=[
            pl.BlockSpec((None, N_Q_HEADS, LANES, TM_IN), lambda i: (out_blk(i), 0, 0, 0)),
            out_rows(KV_DIM),
            pl.BlockSpec((ck_in, KV_DIM, TK), lambda i: (out_blk(i), 0, 0)),
            out_rows(GM_DIM), out_rows(GM_DIM), out_rows(2 * D),
        ] + cast_specs,
        out_shape=[
            jax.ShapeDtypeStruct((n_blk, N_Q_HEADS, LANES, TM_IN), BF16),
            jax.ShapeDtypeStruct((M, KV_DIM), BF16),
            jax.ShapeDtypeStruct((M // TK, KV_DIM, TK), BF16),
            jax.ShapeDtypeStruct((M, GM_DIM), F32),
            jax.ShapeDtypeStruct((M, GM_DIM), BF16),
            jax.ShapeDtypeStruct((M, 2 * D), F32),
        ] + [jax.ShapeDtypeStruct(w.shape, BF16) for w, _ in to_cast],
        scratch_shapes=[pltpu.VMEM((TM_IN, in_dim), F32), pltpu.VMEM((TM_IN, in_dim), F32)],
        compiler_params=pltpu.CompilerParams(dimension_semantics=("arbitrary",),
                                             vmem_limit_bytes=VMEM_LIMIT_BYTES),
        name="inproj",
    )(x.reshape(M, D), norm_mix_g, w_q_b, w_in_b, gmat2, cos_t, sin_t, qg, kg, lng, lnb,
      *[w for w, _ in to_cast])
    k = k.reshape(B, S, KV_DIM)
    vt = vt.reshape(B, S // TK, KV_DIM, TK)

    n_q = S // TQ

    def attention(fixed_shift):
        return pl.pallas_call(
            functools.partial(_attn_kernel, fixed_shift=fixed_shift),
            grid=(B, n_q),
            in_specs=[
                pl.BlockSpec(memory_space=pltpu.SMEM),
                pl.BlockSpec((None, N_Q_HEADS, LANES, TQ), lambda b, i: (b * n_q + i, 0, 0, 0)),
                pl.BlockSpec((None, S, KV_DIM), lambda b, i: (b, 0, 0)),
                pl.BlockSpec((None, S // TK, KV_DIM, TK), lambda b, i: (b, 0, 0, 0)),
            ],
            out_specs=pl.BlockSpec((None, Q_DIM, TQ), lambda b, i: (b * n_q + i, 0, 0)),
            out_shape=jax.ShapeDtypeStruct((B * n_q, Q_DIM, TQ), BF16),
            scratch_shapes=[
                pltpu.VMEM((N_Q_HEADS, TK, TQ), F32),
                pltpu.VMEM((N_Q_HEADS, TK, TQ), F32),
                pltpu.VMEM((N_Q_HEADS, 1, TQ), F32),
                pltpu.VMEM((N_Q_HEADS, SUBLANES, TQ), F32),
                pltpu.VMEM((N_Q_HEADS, HEAD_DIM, TQ), F32),
            ],
            compiler_params=cparams,
            name="gqa_attention_shift" if fixed_shift else "gqa_attention_online",
        )

    score_bound = (HEAD_DIM * QK_SCALE_LOG2 * (1.0 + 2.0 ** -6)
                   * jnp.max(jnp.abs(q_norm_g[0])) * jnp.max(jnp.abs(k_norm_g[0])))
    attn_t = lax.cond(
        score_bound <= MAX_FIXED_SHIFT,
        lambda s_, *ops: attention(True)(s_, *ops),
        lambda s_, *ops: attention(False)(s_, *ops),
        score_bound.reshape(1).astype(F32), qt, k, vt)

    bs_full = jnp.repeat(jnp.transpose(b_s[0]), GM_GROUP_DIM, axis=1)
    ws_pairs = (w_s[0].reshape(GM_GROUPS // 2, 2, GM_CHUNK, GM_CHUNK)
                .transpose(0, 2, 1, 3).reshape(GM_GROUPS // 2, GM_CHUNK, 2 * GM_CHUNK).astype(BF16))
    n_out = M // TM_OUT

    def s1_blk(i):
        return jnp.minimum(i, n_out - 1)

    def s1_rows(width):
        return pl.BlockSpec((TM_OUT, width), lambda i: (s1_blk(i), 0))

    out = pl.pallas_call(
        _merge_ffn_kernel,
        grid=(n_out + 1,),
        in_specs=[
            s1_rows(D),
            pl.BlockSpec((None, Q_DIM, TM_OUT), lambda i: (s1_blk(i), 0, 0)),
            s1_rows(GM_DIM), s1_rows(GM_DIM), s1_rows(2 * D),
            _const_spec((GM_GROUPS // 2, GM_CHUNK, 2 * GM_CHUNK)),
            _const_spec((GM_CHUNK, GM_DIM)),
            _const_spec((None, Q_DIM, D)),
            _const_spec((None, GM_DIM, D)),
            _const_spec((None, D, D)),
            _const_spec((1, D)),
            _const_spec((None, D, 2 * d_ff)),
            _const_spec((None, d_ff, D)),
            _const_spec((1, D)),
        ],
        out_specs=pl.BlockSpec((TM_OUT, D), lambda i: (jnp.maximum(i - 1, 0), 0)),
        out_shape=jax.ShapeDtypeStruct((M, D), F32),
        scratch_shapes=[
            pltpu.VMEM((TM_OUT, GM_DIM), BF16),
            pltpu.VMEM((TM_OUT, d_ff), BF16),
            pltpu.VMEM((TM_OUT, D), F32),
            pltpu.VMEM((TM_OUT, D), F32),
        ],
        compiler_params=pltpu.CompilerParams(dimension_semantics=("arbitrary",),
                                             vmem_limit_bytes=VMEM_LIMIT_BYTES),
        name="merge_ffn",
    )(x.reshape(M, D), attn_t, u, vn, gates, ws_pairs, bs_full,
      wpa_b, wpb_b, wo_b, norm_ffn_g, wgu_b, wd_b, norm_final_g[None, :])
    return out.reshape(B, S, D)
```

```python
import functools
import math

import jax
import jax.numpy as jnp
import numpy as np
from jax import lax
from jax.experimental import pallas as pl
from jax.experimental.pallas import tpu as pltpu

HEAD_DIM = 64
N_Q_HEADS = 8
N_KV_HEADS = 2
Q_DIM = N_Q_HEADS * HEAD_DIM
KV_DIM = N_KV_HEADS * HEAD_DIM
GM_GROUPS = 8
GM_GROUP_DIM = 64
GM_DIM = GM_GROUPS * GM_GROUP_DIM
GM_CHUNK = 128
GRID_W = 64
ROPE_THETA = 10000.0
EPS = 1e-6

LANES = 128
SUBLANES = 8
VMEM_LIMIT_BYTES = 56 * 1024 * 1024

TM_IN = 256
TQ = 256
TK = 256
QK_SCALE_LOG2 = math.log2(math.e) / math.sqrt(HEAD_DIM)
MAX_FIXED_SHIFT = 48.0
CHUNKS_PER_BODY = 16
TM_OUT = 256
CAST_ROWS = 16
FF_CHUNK = 768

F32 = jnp.float32
BF16 = jnp.bfloat16


def _dot(a, b):
    return jnp.dot(a, b, preferred_element_type=F32)


def _group_sum(x, gmat2):
    hi = x.astype(BF16)
    lo = (x - hi.astype(F32)).astype(BF16)
    return _dot(jnp.concatenate([hi, lo], axis=1), gmat2)


def _rope_partner(x, lane):
    fwd = pltpu.roll(x, LANES - 16, axis=1)
    bwd = pltpu.roll(x, 16, axis=1)
    return jnp.where((lane & 16) == 0, fwd, bwd)


def _inproj_kernel(x_ref, g_ref, wq_ref, w_ref, gmat_ref, cos_ref, sin_ref, qg_ref, kg_ref,
                   lng_ref, lnb_ref, *refs, cast_steps):
    n_cast = len(cast_steps)
    cast_in = refs[:n_cast]
    qt_ref, k_ref, vt_ref, u_ref, vn_ref, gate_ref = refs[n_cast:n_cast + 6]
    cast_out = refs[n_cast + 6:2 * n_cast + 6]
    ya_ref, yb_ref = refs[2 * n_cast + 6:]
    i = pl.program_id(0)
    tm = x_ref.shape[0]

    @pl.when(i == 0)
    def _():
        yb_ref[...] = jnp.zeros(yb_ref.shape, F32)

    for src, dst, n_steps in zip(cast_in, cast_out, cast_steps):
        @pl.when(i < n_steps)
        def _(src=src, dst=dst):
            dst[...] = src[...].astype(dst.dtype)

    def compute(y_w, y_r):
        x = x_ref[...]
        ms = jnp.mean(x * x, axis=-1, keepdims=True)
        h = (x * lax.rsqrt(ms + EPS) * g_ref[...]).astype(BF16)
        gmat = gmat_ref[...]
        lane = lax.broadcasted_iota(jnp.int32, (tm, LANES), 1)

        def project(c0, c1):
            y_w[:, c0:c1] = _dot(h, w_ref[:, c0:c1])

        y_w[:, :Q_DIM] = _dot(h, wq_ref[...])

        v0 = Q_DIM + KV_DIM
        uv0 = Q_DIM + 2 * KV_DIM
        g0 = uv0 + 2 * GM_DIM
        g1 = g0 + (w_ref.shape[1] - g0) // 2

        n_qk = (Q_DIM + KV_DIM) // LANES
        def yqk(j):
            return y_r[:, j * LANES:(j + 1) * LANES]

        project(Q_DIM, uv0)
        n_gate = w_ref.shape[1] - g0
        gate_ref[:, :n_gate // 2] = jax.nn.sigmoid(y_r[:, g0:g1])
        ss = [_group_sum(yqk(j) * yqk(j), gmat) for j in range(n_qk)]
        project(uv0, g0)

        def norm_rope(y, s2, gain):
            y = y * lax.rsqrt(s2 * (1.0 / HEAD_DIM) + EPS) * gain
            return y * cos_ref[...] + _rope_partner(y, lane) * sin_ref[...]

        row = lax.broadcasted_iota(jnp.int32, (LANES, tm), 0)
        top = row < HEAD_DIM
        half_heads = N_Q_HEADS // 2
        for j in range(Q_DIM // LANES):
            qt = (norm_rope(yqk(j), ss[j], qg_ref[...]) * QK_SCALE_LOG2).T
            zero = jnp.zeros_like(qt)
            qt_ref[j] = jnp.where(top, qt, zero).astype(BF16)
            qt_ref[j + half_heads] = jnp.where(top, zero, qt).astype(BF16)
        k_ref[...] = norm_rope(yqk(n_qk - 1), ss[n_qk - 1], kg_ref[...]).astype(BF16)
        vt = y_r[:, v0:uv0].T.astype(BF16)
        for c in range(tm // TK):
            vt_ref[c] = vt[:, c * TK:(c + 1) * TK]

        gate_ref[:, n_gate // 2:] = jax.nn.sigmoid(y_r[:, g1:])
        u_ref[...] = jax.nn.gelu(y_r[:, uv0:uv0 + GM_DIM])
        n_gm = GM_DIM // LANES
        vv = [jax.nn.gelu(y_r[:, uv0 + GM_DIM + j * LANES:uv0 + GM_DIM + (j + 1) * LANES])
              for j in range(n_gm)]
        project(g0, g1)
        mu = [_group_sum(v, gmat) * (1.0 / GM_GROUP_DIM) for v in vv]
        gq = g1 + (w_ref.shape[1] - g1) // 2
        project(g1, gq)
        dv = [v - m for v, m in zip(vv, mu)]
        var = [_group_sum(d * d, gmat) * (1.0 / GM_GROUP_DIM) for d in dv]
        project(gq, w_ref.shape[1])
        for j in range(n_gm):
            sl = slice(j * LANES, (j + 1) * LANES)
            vn = dv[j] * lax.rsqrt(var[j] + EPS) * lng_ref[:, sl] + lnb_ref[:, sl]
            vn_ref[:, sl] = vn.astype(BF16)

    @pl.when(i % 2 == 0)
    def _():
        compute(ya_ref, yb_ref)

    @pl.when(i % 2 == 1)
    def _():
        compute(yb_ref, ya_ref)


def _sublane_partial_sum(e):
    rows, lanes = e.shape
    return jnp.sum(e.reshape(rows // SUBLANES, SUBLANES, lanes), axis=0)


def _attn_kernel(shift_ref, qt_ref, k_ref, vt_ref, o_ref, sa_ref, sb_ref, m_ref, l_ref, acc_ref,
                 *, fixed_shift):
    n_heads = qt_ref.shape[0]
    group = n_heads // N_KV_HEADS
    n_chunks = k_ref.shape[0] // TK
    if fixed_shift:
        shift = shift_ref[0]
    else:
        m_ref[...] = jnp.full(m_ref.shape, -jnp.inf, F32)
    acc_ref[...] = jnp.zeros(acc_ref.shape, F32)
    l_ref[...] = jnp.zeros(l_ref.shape, F32)

    def keys(c):
        return k_ref[pl.ds(pl.multiple_of(c * TK, TK), TK), :]

    def step(c, cur_ref, next_ref):
        if next_ref is not None:
            k_next = keys(c + 1)
        vt_c = vt_ref[c]
        for h in range(n_heads):
            g = h // group
            if next_ref is not None:
                next_ref[h] = _dot(k_next, qt_ref[h])
            s = cur_ref[h]
            vt_g = vt_c[g * HEAD_DIM:(g + 1) * HEAD_DIM, :]
            if fixed_shift:
                e = jnp.exp2(s - shift)
                l_ref[h] += _sublane_partial_sum(e)
                acc_ref[h] += _dot(vt_g, e.astype(BF16))
            else:
                m_old = m_ref[h]
                m_new = jnp.maximum(m_old, jnp.max(s, axis=0, keepdims=True))
                alpha = jnp.exp2(m_old - m_new)
                e = jnp.exp2(s - m_new)
                m_ref[h] = m_new
                l_ref[h] = alpha * l_ref[h] + _sublane_partial_sum(e)
                acc_ref[h] = alpha * acc_ref[h] + _dot(vt_g, e.astype(BF16))

    k_first = keys(0)
    for h in range(n_heads):
        sa_ref[h] = _dot(k_first, qt_ref[h])

    bufs = (sa_ref, sb_ref)

    def steps(c0, last):
        for j in range(CHUNKS_PER_BODY):
            final = last and j == CHUNKS_PER_BODY - 1
            step(c0 + j, bufs[j % 2], None if final else bufs[(j + 1) % 2])

    def body(i, carry):
        steps(CHUNKS_PER_BODY * i, False)
        return carry

    lax.fori_loop(0, n_chunks // CHUNKS_PER_BODY - 1, body, 0)
    steps(n_chunks - CHUNKS_PER_BODY, True)
    for h in range(n_heads):
        o = acc_ref[h] / jnp.sum(l_ref[h], axis=0, keepdims=True)
        o_ref[h * HEAD_DIM:(h + 1) * HEAD_DIM, :] = o.astype(BF16)


def _merge_ffn_kernel(x_ref, at_ref, u_ref, vn_ref, gate_ref, ws_ref, bs_ref,
                      wpa_ref, wpb_ref, wo_ref, gf_ref, wgu_ref, wd_ref, gfin_ref,
                      o_ref, gm_ref, act_ref, xa_ref, xb_ref):
    i = pl.program_id(0)
    tm = x_ref.shape[0]
    d_model = x_ref.shape[1]
    d_ff = wd_ref.shape[0]

    @pl.when(i == 0)
    def _():
        xb_ref[...] = jnp.zeros(xb_ref.shape, F32)

    def compute(x1_w, x1_r):
        x1p = x1_r[...]
        ms = jnp.mean(x1p * x1p, axis=-1, keepdims=True)
        h = (x1p * lax.rsqrt(ms + EPS) * gf_ref[...]).astype(BF16)

        lane = lax.broadcasted_iota(jnp.int32, (GM_CHUNK, LANES), 1)
        low = lane < GM_GROUP_DIM
        for c in range(tm // GM_CHUNK):
            rs = slice(c * GM_CHUNK, (c + 1) * GM_CHUNK)
            for j in range(GM_DIM // LANES):
                cs = slice(j * LANES, (j + 1) * LANES)
                vblk = vn_ref[rs, cs]
                zero = jnp.zeros_like(vblk)
                vsplit = jnp.concatenate([jnp.where(low, vblk, zero), jnp.where(low, zero, vblk)],
                                         axis=0)
                sv = _dot(ws_ref[j], vsplit)
                gm_ref[rs, cs] = (u_ref[rs, cs] * (sv + bs_ref[:, cs])).astype(BF16)

        pa = lax.dot_general(at_ref[...], wpa_ref[...], (((0,), (0,)), ((), ())),
                             preferred_element_type=F32)
        pb = _dot(gm_ref[...], wpb_ref[...])
        mix = (gate_ref[:, :d_model] * pa + gate_ref[:, d_model:] * pb).astype(BF16)

        for c0 in range(0, d_ff, FF_CHUNK):
            c1 = min(c0 + FF_CHUNK, d_ff)
            gt = _dot(h, wgu_ref[:, c0:c1])
            up = _dot(h, wgu_ref[:, d_ff + c0:d_ff + c1])
            act_ref[:, c0:c1] = (jax.nn.silu(gt) * up).astype(BF16)
        x2 = x1p + _dot(act_ref[...], wd_ref[...])
        ms2 = jnp.mean(x2 * x2, axis=-1, keepdims=True)
        o_ref[...] = x2 * lax.rsqrt(ms2 + EPS) * gfin_ref[...]

        x1_w[...] = x_ref[...] + _dot(mix, wo_ref[...])

    @pl.when(i % 2 == 0)
    def _():
        compute(xa_ref, xb_ref)

    @pl.when(i % 2 == 1)
    def _():
        compute(xb_ref, xa_ref)


def _const_spec(shape):
    nd = len(shape)
    return pl.BlockSpec(shape, lambda *_: (0,) * nd, pipeline_mode=pl.Buffered(1))


def _rope_tables(seq):
    half = HEAD_DIM // 2
    rows = seq // GRID_W
    row = np.repeat(np.arange(rows, dtype=np.float64), GRID_W)
    col = np.tile(np.arange(GRID_W, dtype=np.float64), rows)
    inv_freq = 1.0 / (ROPE_THETA ** (np.arange(0, half, 2, dtype=np.float64) / half))
    ang_r = row[:, None] * inv_freq[None, :]
    ang_c = col[:, None] * inv_freq[None, :]
    cr, sr = np.cos(ang_r), np.sin(ang_r)
    cc, sc = np.cos(ang_c), np.sin(ang_c)
    cos64 = np.concatenate([cr, cr, cc, cc], axis=-1)
    sin64 = np.concatenate([-sr, sr, -sc, sc], axis=-1)
    reps = (1, LANES // HEAD_DIM)
    return np.tile(cos64, reps).astype(np.float32), np.tile(sin64, reps).astype(np.float32)


def kernel(x, norm_mix_g, w_in, q_norm_g, k_norm_g, gm_ln_g, gm_ln_b, w_s, b_s, w_proj_a,
           w_proj_b, w_out, norm_ffn_g, w_gate_up, w_down, norm_final_g):
    B, S, D = x.shape
    in_dim = w_in.shape[-1]
    d_ff = w_down.shape[1]
    assert norm_mix_g.shape[0] == 1, "single-layer block"
    assert CHUNKS_PER_BODY % 2 == 0 and S % (CHUNKS_PER_BODY * TK) == 0
    assert S % TQ == 0 and S % TM_IN == 0 and S % TM_OUT == 0
    assert TM_OUT % GM_CHUNK == 0 and TM_IN % TK == 0 and TM_IN == TQ and TM_OUT == TQ

    w_in_b = w_in[0].astype(BF16)
    half_heads = N_Q_HEADS // 2
    w_q_b = (w_in_b[:, :Q_DIM].reshape(D, 2, half_heads, HEAD_DIM)
             .transpose(0, 2, 1, 3).reshape(D, Q_DIM))

    cos_t, sin_t = _rope_tables(S)
    reps = LANES // HEAD_DIM
    qg = jnp.tile(q_norm_g[0], reps)[None, :]
    kg = jnp.tile(k_norm_g[0], reps)[None, :]
    lng = gm_ln_g[0].reshape(1, GM_DIM)
    lnb = gm_ln_b[0].reshape(1, GM_DIM)
    gid = np.arange(LANES) // HEAD_DIM
    gmat = (gid[:, None] == gid[None, :]).astype(np.float32)
    gmat2 = jnp.asarray(np.concatenate([gmat, gmat], axis=0), dtype=BF16)

    two_axes = ("arbitrary", "arbitrary")
    cparams = pltpu.CompilerParams(dimension_semantics=two_axes,
                                   vmem_limit_bytes=VMEM_LIMIT_BYTES)

    M = B * S
    n_seq = S // TM_IN
    n_blk = M // TM_IN
    ck_in = TM_IN // TK

    def in_blk(i):
        return jnp.minimum(i, n_blk - 1)

    def out_blk(i):
        return jnp.maximum(i - 1, 0)

    def out_rows(width):
        return pl.BlockSpec((TM_IN, width), lambda i: (out_blk(i), 0))

    to_cast = [(w_gate_up, n_blk), (w_out, n_blk), (w_down, d_ff // (CAST_ROWS * CAST_ROWS)),
               (w_proj_a, w_proj_a.shape[1] // CAST_ROWS), (w_proj_b, w_proj_b.shape[1] // CAST_ROWS)]
    cast_steps = tuple(n for _, n in to_cast)
    assert all(n <= n_blk and w.shape[1] % n == 0 for w, n in to_cast)
    cast_specs = [pl.BlockSpec((None, w.shape[1] // n, w.shape[2]),
                               lambda i, n=n: (0, jnp.minimum(i, n - 1), 0)) for w, n in to_cast]

    qt, k, vt, u, vn, gates, wgu_b, wo_b, wd_b, wpa_b, wpb_b = pl.pallas_call(
        functools.partial(_inproj_kernel, cast_steps=cast_steps),
        grid=(n_blk + 1,),
        in_specs=[
            pl.BlockSpec((TM_IN, D), lambda i: (in_blk(i), 0)),
            _const_spec((1, D)),
            _const_spec((D, Q_DIM)),
            _const_spec((D, in_dim)),
            _const_spec((2 * LANES, LANES)),
            pl.BlockSpec((TM_IN, LANES), lambda i: (out_blk(i) % n_seq, 0)),
            pl.BlockSpec((TM_IN, LANES), lambda i: (out_blk(i) % n_seq, 0)),
            _const_spec((1, LANES)),
            _const_spec((1, LANES)),
            _const_spec((1, GM_DIM)),
            _const_spec((1, GM_DIM)),
        ] + cast_specs,
        out_specs=[
            pl.BlockSpec((None, N_Q_HEADS, LANES, TM_IN), lambda i: (out_blk(i), 0, 0, 0)),
            out_rows(KV_DIM),
            pl.BlockSpec((ck_in, KV_DIM, TK), lambda i: (out_blk(i), 0, 0)),
            out_rows(GM_DIM), out_rows(GM_DIM), out_rows(2 * D),
        ] + cast_specs,
        out_shape=[
            jax.ShapeDtypeStruct((n_blk, N_Q_HEADS, LANES, TM_IN), BF16),
            jax.ShapeDtypeStruct((M, KV_DIM), BF16),
            jax.ShapeDtypeStruct((M // TK, KV_DIM, TK), BF16),
            jax.ShapeDtypeStruct((M, GM_DIM), F32),
            jax.ShapeDtypeStruct((M, GM_DIM), BF16),
            jax.ShapeDtypeStruct((M, 2 * D), F32),
        ] + [jax.ShapeDtypeStruct(w.shape, BF16) for w, _ in to_cast],
        scratch_shapes=[pltpu.VMEM((TM_IN, in_dim), F32), pltpu.VMEM((TM_IN, in_dim), F32)],
        compiler_params=pltpu.CompilerParams(dimension_semantics=("arbitrary",),
                                             vmem_limit_bytes=VMEM_LIMIT_BYTES),
        name="inproj",
    )(x.reshape(M, D), norm_mix_g, w_q_b, w_in_b, gmat2, cos_t, sin_t, qg, kg, lng, lnb,
      *[w for w, _ in to_cast])
    k = k.reshape(B, S, KV_DIM)
    vt = vt.reshape(B, S // TK, KV_DIM, TK)

    n_q = S // TQ

    def attention(fixed_shift):
        return pl.pallas_call(
            functools.partial(_attn_kernel, fixed_shift=fixed_shift),
            grid=(B, n_q),
            in_specs=[
                pl.BlockSpec(memory_space=pltpu.SMEM),
                pl.BlockSpec((None, N_Q_HEADS, LANES, TQ), lambda b, i: (b * n_q + i, 0, 0, 0)),
                pl.BlockSpec((None, S, KV_DIM), lambda b, i: (b, 0, 0)),
                pl.BlockSpec((None, S // TK, KV_DIM, TK), lambda b, i: (b, 0, 0, 0)),
            ],
            out_specs=pl.BlockSpec((None, Q_DIM, TQ), lambda b, i: (b * n_q + i, 0, 0)),
            out_shape=jax.ShapeDtypeStruct((B * n_q, Q_DIM, TQ), BF16),
            scratch_shapes=[
                pltpu.VMEM((N_Q_HEADS, TK, TQ), F32),
                pltpu.VMEM((N_Q_HEADS, TK, TQ), F32),
                pltpu.VMEM((N_Q_HEADS, 1, TQ), F32),
                pltpu.VMEM((N_Q_HEADS, SUBLANES, TQ), F32),
                pltpu.VMEM((N_Q_HEADS, HEAD_DIM, TQ), F32),
            ],
            compiler_params=cparams,
            name="gqa_attention_shift" if fixed_shift else "gqa_attention_online",
        )

    score_bound = (HEAD_DIM * QK_SCALE_LOG2 * (1.0 + 2.0 ** -6)
                   * jnp.max(jnp.abs(q_norm_g[0])) * jnp.max(jnp.abs(k_norm_g[0])))
    attn_t = lax.cond(
        score_bound <= MAX_FIXED_SHIFT,
        lambda s_, *ops: attention(True)(s_, *ops),
        lambda s_, *ops: attention(False)(s_, *ops),
        score_bound.reshape(1).astype(F32), qt, k, vt)

    bs_full = jnp.repeat(jnp.transpose(b_s[0]), GM_GROUP_DIM, axis=1)
    ws_pairs = (w_s[0].reshape(GM_GROUPS // 2, 2, GM_CHUNK, GM_CHUNK)
                .transpose(0, 2, 1, 3).reshape(GM_GROUPS // 2, GM_CHUNK, 2 * GM_CHUNK).astype(BF16))
    n_out = M // TM_OUT

    def s1_blk(i):
        return jnp.minimum(i, n_out - 1)

    def s1_rows(width):
        return pl.BlockSpec((TM_OUT, width), lambda i: (s1_blk(i), 0))

    out = pl.pallas_call(
        _merge_ffn_kernel,
        grid=(n_out + 1,),
        in_specs=[
            s1_rows(D),
            pl.BlockSpec((None, Q_DIM, TM_OUT), lambda i: (s1_blk(i), 0, 0)),
            s1_rows(GM_DIM), s1_rows(GM_DIM), s1_rows(2 * D),
            _const_spec((GM_GROUPS // 2, GM_CHUNK, 2 * GM_CHUNK)),
            _const_spec((GM_CHUNK, GM_DIM)),
            _const_spec((None, Q_DIM, D)),
            _const_spec((None, GM_DIM, D)),
            _const_spec((None, D, D)),
            _const_spec((1, D)),
            _const_spec((None, D, 2 * d_ff)),
            _const_spec((None, d_ff, D)),
            _const_spec((1, D)),
        ],
        out_specs=pl.BlockSpec((TM_OUT, D), lambda i: (jnp.maximum(i - 1, 0), 0)),
        out_shape=jax.ShapeDtypeStruct((M, D), F32),
        scratch_shapes=[
            pltpu.VMEM((TM_OUT, GM_DIM), BF16),
            pltpu.VMEM((TM_OUT, d_ff), BF16),
            pltpu.VMEM((TM_OUT, D), F32),
            pltpu.VMEM((TM_OUT, D), F32),
        ],
        compiler_params=pltpu.CompilerParams(dimension_semantics=("arbitrary",),
                                             vmem_limit_bytes=VMEM_LIMIT_BYTES),
        name="merge_ffn",
    )(x.reshape(M, D), attn_t, u, vn, gates, ws_pairs, bs_full,
      wpa_b, wpb_b, wo_b, norm_ffn_g, wgu_b, wd_b, norm_final_g[None, :])
    return out.reshape(B, S, D)
```

```python
import functools
import math

import jax
import jax.numpy as jnp
import numpy as np
from jax import lax
from jax.experimental import pallas as pl
from jax.experimental.pallas import tpu as pltpu

HEAD_DIM = 64
N_Q_HEADS = 8
N_KV_HEADS = 2
Q_DIM = N_Q_HEADS * HEAD_DIM
KV_DIM = N_KV_HEADS * HEAD_DIM
GM_GROUPS = 8
GM_GROUP_DIM = 64
GM_DIM = GM_GROUPS * GM_GROUP_DIM
GM_CHUNK = 128
GRID_W = 64
ROPE_THETA = 10000.0
EPS = 1e-6

LANES = 128
SUBLANES = 8
VMEM_LIMIT_BYTES = 56 * 1024 * 1024

TM_IN = 256
TQ = 256
TK = 256
QK_SCALE_LOG2 = math.log2(math.e) / math.sqrt(HEAD_DIM)
MAX_FIXED_SHIFT = 48.0
CHUNKS_PER_BODY = 16
TM_OUT = 256
CAST_ROWS = 16
FF_CHUNK = 768

F32 = jnp.float32
BF16 = jnp.bfloat16


def _dot(a, b):
    return jnp.dot(a, b, preferred_element_type=F32)


def _group_sum(x, gmat2):
    hi = x.astype(BF16)
    lo = (x - hi.astype(F32)).astype(BF16)
    return _dot(jnp.concatenate([hi, lo], axis=1), gmat2)


def _rope_partner(x, lane):
    fwd = pltpu.roll(x, LANES - 16, axis=1)
    bwd = pltpu.roll(x, 16, axis=1)
    return jnp.where((lane & 16) == 0, fwd, bwd)


def _inproj_kernel(x_ref, g_ref, wq_ref, w_ref, gmat_ref, cos_ref, sin_ref, qg_ref, kg_ref,
                   lng_ref, lnb_ref, *refs, cast_steps):
    n_cast = len(cast_steps)
    cast_in = refs[:n_cast]
    qt_ref, k_ref, vt_ref, u_ref, vn_ref, gate_ref = refs[n_cast:n_cast + 6]
    cast_out = refs[n_cast + 6:2 * n_cast + 6]
    ya_ref, yb_ref = refs[2 * n_cast + 6:]
    i = pl.program_id(0)
    tm = x_ref.shape[0]

    @pl.when(i == 0)
    def _():
        yb_ref[...] = jnp.zeros(yb_ref.shape, F32)

    for src, dst, n_steps in zip(cast_in, cast_out, cast_steps):
        @pl.when(i < n_steps)
        def _(src=src, dst=dst):
            dst[...] = src[...].astype(dst.dtype)

    def compute(y_w, y_r):
        x = x_ref[...]
        ms = jnp.mean(x * x, axis=-1, keepdims=True)
        h = (x * lax.rsqrt(ms + EPS) * g_ref[...]).astype(BF16)
        gmat = gmat_ref[...]
        lane = lax.broadcasted_iota(jnp.int32, (tm, LANES), 1)

        def project(c0, c1):
            y_w[:, c0:c1] = _dot(h, w_ref[:, c0:c1])

        y_w[:, :Q_DIM] = _dot(h, wq_ref[...])

        v0 = Q_DIM + KV_DIM
        uv0 = Q_DIM + 2 * KV_DIM
        g0 = uv0 + 2 * GM_DIM
        g1 = g0 + (w_ref.shape[1] - g0) // 2

        n_qk = (Q_DIM + KV_DIM) // LANES
        yqk = [y_r[:, j * LANES:(j + 1) * LANES] for j in range(n_qk)]
        project(Q_DIM, uv0)
        n_gate = w_ref.shape[1] - g0
        gate_ref[:, :n_gate // 2] = jax.nn.sigmoid(y_r[:, g0:g1])
        ss = [_group_sum(y * y, gmat) for y in yqk]
        project(uv0, g0)

        def norm_rope(y, s2, gain):
            y = y * lax.rsqrt(s2 * (1.0 / HEAD_DIM) + EPS) * gain
            return y * cos_ref[...] + _rope_partner(y, lane) * sin_ref[...]

        row = lax.broadcasted_iota(jnp.int32, (LANES, tm), 0)
        top = row < HEAD_DIM
        half_heads = N_Q_HEADS // 2
        for j in range(Q_DIM // LANES):
            qt = (norm_rope(yqk[j], ss[j], qg_ref[...]) * QK_SCALE_LOG2).T
            zero = jnp.zeros_like(qt)
            qt_ref[j] = jnp.where(top, qt, zero).astype(BF16)
            qt_ref[j + half_heads] = jnp.where(top, zero, qt).astype(BF16)
        k_ref[...] = norm_rope(yqk[n_qk - 1], ss[n_qk - 1], kg_ref[...]).astype(BF16)
        vt = y_r[:, v0:uv0].T.astype(BF16)
        for c in range(tm // TK):
            vt_ref[c] = vt[:, c * TK:(c + 1) * TK]

        gate_ref[:, n_gate // 2:] = jax.nn.sigmoid(y_r[:, g1:])
        u_ref[...] = jax.nn.gelu(y_r[:, uv0:uv0 + GM_DIM])
        n_gm = GM_DIM // LANES
        vv = [jax.nn.gelu(y_r[:, uv0 + GM_DIM + j * LANES:uv0 + GM_DIM + (j + 1) * LANES])
              for j in range(n_gm)]
        project(g0, g1)
        mu = [_group_sum(v, gmat) * (1.0 / GM_GROUP_DIM) for v in vv]
        gq = g1 + (w_ref.shape[1] - g1) // 2
        project(g1, gq)
        dv = [v - m for v, m in zip(vv, mu)]
        var = [_group_sum(d * d, gmat) * (1.0 / GM_GROUP_DIM) for d in dv]
        project(gq, w_ref.shape[1])
        for j in range(n_gm):
            sl = slice(j * LANES, (j + 1) * LANES)
            vn = dv[j] * lax.rsqrt(var[j] + EPS) * lng_ref[:, sl] + lnb_ref[:, sl]
            vn_ref[:, sl] = vn.astype(BF16)

    @pl.when(i % 2 == 0)
    def _():
        compute(ya_ref, yb_ref)

    @pl.when(i % 2 == 1)
    def _():
        compute(yb_ref, ya_ref)


def _sublane_partial_sum(e):
    rows, lanes = e.shape
    return jnp.sum(e.reshape(rows // SUBLANES, SUBLANES, lanes), axis=0)


def _attn_kernel(shift_ref, qt_ref, qtn_ref, k_ref, vt_ref, o_ref, sa_ref, sb_ref, m_ref, l_ref,
                 acc_ref, *, fixed_shift):
    n_heads = qt_ref.shape[0]
    group = n_heads // N_KV_HEADS
    n_chunks = k_ref.shape[0] // TK
    if fixed_shift:
        shift = shift_ref[0]
    else:
        m_ref[...] = jnp.full(m_ref.shape, -jnp.inf, F32)
    acc_ref[...] = jnp.zeros(acc_ref.shape, F32)
    l_ref[...] = jnp.zeros(l_ref.shape, F32)

    def keys(c):
        return k_ref[pl.ds(pl.multiple_of(c * TK, TK), TK), :]

    def step(c, cur_ref, next_ref, k_next, q_next_ref):
        vt_c = vt_ref[c]
        for h in range(n_heads):
            g = h // group
            next_ref[h] = _dot(k_next, q_next_ref[h])
            s = cur_ref[h]
            vt_g = vt_c[g * HEAD_DIM:(g + 1) * HEAD_DIM, :]
            if fixed_shift:
                e = jnp.exp2(s - shift)
                l_ref[h] += _sublane_partial_sum(e)
                acc_ref[h] += _dot(vt_g, e.astype(BF16))
            else:
                m_old = m_ref[h]
                m_new = jnp.maximum(m_old, jnp.max(s, axis=0, keepdims=True))
                alpha = jnp.exp2(m_old - m_new)
                e = jnp.exp2(s - m_new)
                m_ref[h] = m_new
                l_ref[h] = alpha * l_ref[h] + _sublane_partial_sum(e)
                acc_ref[h] = alpha * acc_ref[h] + _dot(vt_g, e.astype(BF16))

    @pl.when(pl.program_id(1) == 0)
    def _():
        k_first = keys(0)
        for h in range(n_heads):
            sa_ref[h] = _dot(k_first, qt_ref[h])

    bufs = (sa_ref, sb_ref)

    def steps(c0, last):
        for j in range(CHUNKS_PER_BODY):
            if last and j == CHUNKS_PER_BODY - 1:
                step(c0 + j, bufs[j % 2], bufs[(j + 1) % 2], keys(0), qtn_ref)
            else:
                step(c0 + j, bufs[j % 2], bufs[(j + 1) % 2], keys(c0 + j + 1), qt_ref)

    def body(i, carry):
        steps(CHUNKS_PER_BODY * i, False)
        return carry

    lax.fori_loop(0, n_chunks // CHUNKS_PER_BODY - 1, body, 0)
    steps(n_chunks - CHUNKS_PER_BODY, True)
    for h in range(n_heads):
        o = acc_ref[h] / jnp.sum(l_ref[h], axis=0, keepdims=True)
        o_ref[h * HEAD_DIM:(h + 1) * HEAD_DIM, :] = o.astype(BF16)


def _merge_ffn_kernel(x_ref, at_ref, u_ref, vn_ref, gate_ref, ws_ref, bs_ref,
                      wpa_ref, wpb_ref, wo_ref, gf_ref, wgu_ref, wd_ref, gfin_ref,
                      o_ref, gm_ref, act_ref, xa_ref, xb_ref):
    i = pl.program_id(0)
    tm = x_ref.shape[0]
    d_model = x_ref.shape[1]
    d_ff = wd_ref.shape[0]

    @pl.when(i == 0)
    def _():
        xb_ref[...] = jnp.zeros(xb_ref.shape, F32)

    def compute(x1_w, x1_r):
        x1p = x1_r[...]
        ms = jnp.mean(x1p * x1p, axis=-1, keepdims=True)
        h = (x1p * lax.rsqrt(ms + EPS) * gf_ref[...]).astype(BF16)

        lane = lax.broadcasted_iota(jnp.int32, (GM_CHUNK, LANES), 1)
        low = lane < GM_GROUP_DIM
        for c in range(tm // GM_CHUNK):
            rs = slice(c * GM_CHUNK, (c + 1) * GM_CHUNK)
            for j in range(GM_DIM // LANES):
                cs = slice(j * LANES, (j + 1) * LANES)
                vblk = vn_ref[rs, cs]
                zero = jnp.zeros_like(vblk)
                vsplit = jnp.concatenate([jnp.where(low, vblk, zero), jnp.where(low, zero, vblk)],
                                         axis=0)
                sv = _dot(ws_ref[j], vsplit)
                gm_ref[rs, cs] = (u_ref[rs, cs] * (sv + bs_ref[:, cs])).astype(BF16)

        pa = lax.dot_general(at_ref[...], wpa_ref[...], (((0,), (0,)), ((), ())),
                             preferred_element_type=F32)
        pb = _dot(gm_ref[...], wpb_ref[...])
        mix = (gate_ref[:, :d_model] * pa + gate_ref[:, d_model:] * pb).astype(BF16)

        for c0 in range(0, d_ff, FF_CHUNK):
            c1 = min(c0 + FF_CHUNK, d_ff)
            gt = _dot(h, wgu_ref[:, c0:c1])
            up = _dot(h, wgu_ref[:, d_ff + c0:d_ff + c1])
            act_ref[:, c0:c1] = (jax.nn.silu(gt) * up).astype(BF16)
        x2 = x1p + _dot(act_ref[...], wd_ref[...])
        ms2 = jnp.mean(x2 * x2, axis=-1, keepdims=True)
        o_ref[...] = x2 * lax.rsqrt(ms2 + EPS) * gfin_ref[...]

        x1_w[...] = x_ref[...] + _dot(mix, wo_ref[...])

    @pl.when(i % 2 == 0)
    def _():
        compute(xa_ref, xb_ref)

    @pl.when(i % 2 == 1)
    def _():
        compute(xb_ref, xa_ref)


def _const_spec(shape):
    nd = len(shape)
    return pl.BlockSpec(shape, lambda *_: (0,) * nd, pipeline_mode=pl.Buffered(1))


def _rope_tables(seq):
    half = HEAD_DIM // 2
    rows = seq // GRID_W
    row = np.repeat(np.arange(rows, dtype=np.float64), GRID_W)
    col = np.tile(np.arange(GRID_W, dtype=np.float64), rows)
    inv_freq = 1.0 / (ROPE_THETA ** (np.arange(0, half, 2, dtype=np.float64) / half))
    ang_r = row[:, None] * inv_freq[None, :]
    ang_c = col[:, None] * inv_freq[None, :]
    cr, sr = np.cos(ang_r), np.sin(ang_r)
    cc, sc = np.cos(ang_c), np.sin(ang_c)
    cos64 = np.concatenate([cr, cr, cc, cc], axis=-1)
    sin64 = np.concatenate([-sr, sr, -sc, sc], axis=-1)
    reps = (1, LANES // HEAD_DIM)
    return np.tile(cos64, reps).astype(np.float32), np.tile(sin64, reps).astype(np.float32)


def kernel(x, norm_mix_g, w_in, q_norm_g, k_norm_g, gm_ln_g, gm_ln_b, w_s, b_s, w_proj_a,
           w_proj_b, w_out, norm_ffn_g, w_gate_up, w_down, norm_final_g):
    B, S, D = x.shape
    in_dim = w_in.shape[-1]
    d_ff = w_down.shape[1]
    assert norm_mix_g.shape[0] == 1, "single-layer block"
    assert CHUNKS_PER_BODY % 2 == 0 and S % (CHUNKS_PER_BODY * TK) == 0
    assert S % TQ == 0 and S % TM_IN == 0 and S % TM_OUT == 0
    assert TM_OUT % GM_CHUNK == 0 and TM_IN % TK == 0 and TM_IN == TQ and TM_OUT == TQ

    w_in_b = w_in[0].astype(BF16)
    half_heads = N_Q_HEADS // 2
    w_q_b = (w_in_b[:, :Q_DIM].reshape(D, 2, half_heads, HEAD_DIM)
             .transpose(0, 2, 1, 3).reshape(D, Q_DIM))

    cos_t, sin_t = _rope_tables(S)
    reps = LANES // HEAD_DIM
    qg = jnp.tile(q_norm_g[0], reps)[None, :]
    kg = jnp.tile(k_norm_g[0], reps)[None, :]
    lng = gm_ln_g[0].reshape(1, GM_DIM)
    lnb = gm_ln_b[0].reshape(1, GM_DIM)
    gid = np.arange(LANES) // HEAD_DIM
    gmat = (gid[:, None] == gid[None, :]).astype(np.float32)
    gmat2 = jnp.asarray(np.concatenate([gmat, gmat], axis=0), dtype=BF16)

    two_axes = ("arbitrary", "arbitrary")
    cparams = pltpu.CompilerParams(dimension_semantics=two_axes,
                                   vmem_limit_bytes=VMEM_LIMIT_BYTES)

    M = B * S
    n_seq = S // TM_IN
    n_blk = M // TM_IN
    ck_in = TM_IN // TK

    def in_blk(i):
        return jnp.minimum(i, n_blk - 1)

    def out_blk(i):
        return jnp.maximum(i - 1, 0)

    def out_rows(width):
        return pl.BlockSpec((TM_IN, width), lambda i: (out_blk(i), 0))

    to_cast = [(w_gate_up, n_blk), (w_out, n_blk), (w_down, d_ff // (CAST_ROWS * CAST_ROWS)),
               (w_proj_a, w_proj_a.shape[1] // CAST_ROWS), (w_proj_b, w_proj_b.shape[1] // CAST_ROWS)]
    cast_steps = tuple(n for _, n in to_cast)
    assert all(n <= n_blk and w.shape[1] % n == 0 for w, n in to_cast)
    cast_specs = [pl.BlockSpec((None, w.shape[1] // n, w.shape[2]),
                               lambda i, n=n: (0, jnp.minimum(i, n - 1), 0)) for w, n in to_cast]

    qt, k, vt, u, vn, gates, wgu_b, wo_b, wd_b, wpa_b, wpb_b = pl.pallas_call(
        functools.partial(_inproj_kernel, cast_steps=cast_steps),
        grid=(n_blk + 1,),
        in_specs=[
            pl.BlockSpec((TM_IN, D), lambda i: (in_blk(i), 0)),
            _const_spec((1, D)),
            _const_spec((D, Q_DIM)),
            _const_spec((D, in_dim)),
            _const_spec((2 * LANES, LANES)),
            pl.BlockSpec((TM_IN, LANES), lambda i: (out_blk(i) % n_seq, 0)),
            pl.BlockSpec((TM_IN, LANES), lambda i: (out_blk(i) % n_seq, 0)),
            _const_spec((1, LANES)),
            _const_spec((1, LANES)),
            _const_spec((1, GM_DIM)),
            _const_spec((1, GM_DIM)),
        ] + cast_specs,
        out_specs=[
            pl.BlockSpec((None, N_Q_HEADS, LANES, TM_IN), lambda i: (out_blk(i), 0, 0, 0)),
            out_rows(KV_DIM),
            pl.BlockSpec((ck_in, KV_DIM, TK), lambda i: (out_blk(i), 0, 0)),
            out_rows(GM_DIM), out_rows(GM_DIM), out_rows(2 * D),
        ] + cast_specs,
        out_shape=[
            jax.ShapeDtypeStruct((n_blk, N_Q_HEADS, LANES, TM_IN), BF16),
            jax.ShapeDtypeStruct((M, KV_DIM), BF16),
            jax.ShapeDtypeStruct((M // TK, KV_DIM, TK), BF16),
            jax.ShapeDtypeStruct((M, GM_DIM), F32),
            jax.ShapeDtypeStruct((M, GM_DIM), BF16),
            jax.ShapeDtypeStruct((M, 2 * D), F32),
        ] + [jax.ShapeDtypeStruct(w.shape, BF16) for w, _ in to_cast],
        scratch_shapes=[pltpu.VMEM((TM_IN, in_dim), F32), pltpu.VMEM((TM_IN, in_dim), F32)],
        compiler_params=pltpu.CompilerParams(dimension_semantics=("arbitrary",),
                                             vmem_limit_bytes=VMEM_LIMIT_BYTES),
        name="inproj",
    )(x.reshape(M, D), norm_mix_g, w_q_b, w_in_b, gmat2, cos_t, sin_t, qg, kg, lng, lnb,
      *[w for w, _ in to_cast])
    k = k.reshape(B, S, KV_DIM)
    vt = vt.reshape(B, S // TK, KV_DIM, TK)

    n_q = S // TQ

    def attention(fixed_shift):
        return pl.pallas_call(
            functools.partial(_attn_kernel, fixed_shift=fixed_shift),
            grid=(B, n_q),
            in_specs=[
                pl.BlockSpec(memory_space=pltpu.SMEM),
                pl.BlockSpec((None, N_Q_HEADS, LANES, TQ), lambda b, i: (b * n_q + i, 0, 0, 0)),
                pl.BlockSpec((None, N_Q_HEADS, LANES, TQ),
                             lambda b, i: (b * n_q + jnp.minimum(i + 1, n_q - 1), 0, 0, 0)),
                pl.BlockSpec((None, S, KV_DIM), lambda b, i: (b, 0, 0)),
                pl.BlockSpec((None, S // TK, KV_DIM, TK), lambda b, i: (b, 0, 0, 0)),
            ],
            out_specs=pl.BlockSpec((None, Q_DIM, TQ), lambda b, i: (b * n_q + i, 0, 0)),
            out_shape=jax.ShapeDtypeStruct((B * n_q, Q_DIM, TQ), BF16),
            scratch_shapes=[
                pltpu.VMEM((N_Q_HEADS, TK, TQ), F32),
                pltpu.VMEM((N_Q_HEADS, TK, TQ), F32),
                pltpu.VMEM((N_Q_HEADS, 1, TQ), F32),
                pltpu.VMEM((N_Q_HEADS, SUBLANES, TQ), F32),
                pltpu.VMEM((N_Q_HEADS, HEAD_DIM, TQ), F32),
            ],
            compiler_params=cparams,
            name="gqa_attention_shift" if fixed_shift else "gqa_attention_online",
        )

    score_bound = (HEAD_DIM * QK_SCALE_LOG2 * (1.0 + 2.0 ** -6)
                   * jnp.max(jnp.abs(q_norm_g[0])) * jnp.max(jnp.abs(k_norm_g[0])))
    attn_t = lax.cond(
        score_bound <= MAX_FIXED_SHIFT,
        lambda s_, *ops: attention(True)(s_, *ops),
        lambda s_, *ops: attention(False)(s_, *ops),
        score_bound.reshape(1).astype(F32), qt, qt, k, vt)

    bs_full = jnp.repeat(jnp.transpose(b_s[0]), GM_GROUP_DIM, axis=1)
    ws_pairs = (w_s[0].reshape(GM_GROUPS // 2, 2, GM_CHUNK, GM_CHUNK)
                .transpose(0, 2, 1, 3).reshape(GM_GROUPS // 2, GM_CHUNK, 2 * GM_CHUNK).astype(BF16))
    n_out = M // TM_OUT

    def s1_blk(i):
        return jnp.minimum(i, n_out - 1)

    def s1_rows(width):
        return pl.BlockSpec((TM_OUT, width), lambda i: (s1_blk(i), 0))

    out = pl.pallas_call(
        _merge_ffn_kernel,
        grid=(n_out + 1,),
        in_specs=[
            s1_rows(D),
            pl.BlockSpec((None, Q_DIM, TM_OUT), lambda i: (s1_blk(i), 0, 0)),
            s1_rows(GM_DIM), s1_rows(GM_DIM), s1_rows(2 * D),
            _const_spec((GM_GROUPS // 2, GM_CHUNK, 2 * GM_CHUNK)),
            _const_spec((GM_CHUNK, GM_DIM)),
            _const_spec((None, Q_DIM, D)),
            _const_spec((None, GM_DIM, D)),
            _const_spec((None, D, D)),
            _const_spec((1, D)),
            _const_spec((None, D, 2 * d_ff)),
            _const_spec((None, d_ff, D)),
            _const_spec((1, D)),
        ],
        out_specs=pl.BlockSpec((TM_OUT, D), lambda i: (jnp.maximum(i - 1, 0), 0)),
        out_shape=jax.ShapeDtypeStruct((M, D), F32),
        scratch_shapes=[
            pltpu.VMEM((TM_OUT, GM_DIM), BF16),
            pltpu.VMEM((TM_OUT, d_ff), BF16),
            pltpu.VMEM((TM_OUT, D), F32),
            pltpu.VMEM((TM_OUT, D), F32),
        ],
        compiler_params=pltpu.CompilerParams(dimension_semantics=("arbitrary",),
                                             vmem_limit_bytes=VMEM_LIMIT_BYTES),
        name="merge_ffn",
    )(x.reshape(M, D), attn_t, u, vn, gates, ws_pairs, bs_full,
      wpa_b, wpb_b, wo_b, norm_ffn_g, wgu_b, wd_b, norm_final_g[None, :])
    return out.reshape(B, S, D)
```

```python
import functools
import math

import jax
import jax.numpy as jnp
import numpy as np
from jax import lax
from jax.experimental import pallas as pl
from jax.experimental.pallas import tpu as pltpu

HEAD_DIM = 64
N_Q_HEADS = 8
N_KV_HEADS = 2
Q_DIM = N_Q_HEADS * HEAD_DIM
KV_DIM = N_KV_HEADS * HEAD_DIM
GM_GROUPS = 8
GM_GROUP_DIM = 64
GM_DIM = GM_GROUPS * GM_GROUP_DIM
GM_CHUNK = 128
GRID_W = 64
ROPE_THETA = 10000.0
EPS = 1e-6

LANES = 128
SUBLANES = 8
VMEM_LIMIT_BYTES = 56 * 1024 * 1024

TM_IN = 256
TQ = 256
TK = 256
QK_SCALE_LOG2 = math.log2(math.e) / math.sqrt(HEAD_DIM)
MAX_FIXED_SHIFT = 48.0
CHUNKS_PER_BODY = 16
TM_OUT = 256
CAST_ROWS = 16
FF_CHUNK = 768

F32 = jnp.float32
BF16 = jnp.bfloat16


def _dot(a, b):
    return jnp.dot(a, b, preferred_element_type=F32)


def _group_sum(x, gmat2):
    hi = x.astype(BF16)
    lo = (x - hi.astype(F32)).astype(BF16)
    return _dot(jnp.concatenate([hi, lo], axis=1), gmat2)


def _rope_partner(x, lane):
    fwd = pltpu.roll(x, LANES - 16, axis=1)
    bwd = pltpu.roll(x, 16, axis=1)
    return jnp.where((lane & 16) == 0, fwd, bwd)


def _inproj_kernel(x_ref, g_ref, wq_ref, w_ref, gmat_ref, cos_ref, sin_ref, qg_ref, kg_ref,
                   lng_ref, lnb_ref, *refs, cast_steps):
    n_cast = len(cast_steps)
    cast_in = refs[:n_cast]
    qt_ref, k_ref, vt_ref, u_ref, vn_ref, gate_ref = refs[n_cast:n_cast + 6]
    cast_out = refs[n_cast + 6:2 * n_cast + 6]
    ya_ref, yb_ref = refs[2 * n_cast + 6:]
    i = pl.program_id(0)
    tm = x_ref.shape[0]

    @pl.when(i == 0)
    def _():
        yb_ref[...] = jnp.zeros(yb_ref.shape, F32)

    for src, dst, n_steps in zip(cast_in, cast_out, cast_steps):
        @pl.when(i < n_steps)
        def _(src=src, dst=dst):
            dst[...] = src[...].astype(dst.dtype)

    def compute(y_w, y_r):
        x = x_ref[...]
        ms = jnp.mean(x * x, axis=-1, keepdims=True)
        h = (x * lax.rsqrt(ms + EPS) * g_ref[...]).astype(BF16)
        gmat = gmat_ref[...]
        lane = lax.broadcasted_iota(jnp.int32, (tm, LANES), 1)

        def project(c0, c1):
            y_w[:, c0:c1] = _dot(h, w_ref[:, c0:c1])

        y_w[:, :Q_DIM] = _dot(h, wq_ref[...])

        v0 = Q_DIM + KV_DIM
        uv0 = Q_DIM + 2 * KV_DIM
        g0 = uv0 + 2 * GM_DIM
        g1 = g0 + (w_ref.shape[1] - g0) // 2

        n_qk = (Q_DIM + KV_DIM) // LANES
        yqk = [y_r[:, j * LANES:(j + 1) * LANES] for j in range(n_qk)]
        project(Q_DIM, uv0)
        n_gate = w_ref.shape[1] - g0
        gate_ref[:, :n_gate // 2] = jax.nn.sigmoid(y_r[:, g0:g1])
        ss = [_group_sum(y * y, gmat) for y in yqk]
        project(uv0, g0)

        n_gm = GM_DIM // LANES
        vv = [jax.nn.gelu(y_r[:, uv0 + GM_DIM + j * LANES:uv0 + GM_DIM + (j + 1) * LANES])
              for j in range(n_gm)]
        project(g0, g1)
        mu = [_group_sum(v, gmat) * (1.0 / GM_GROUP_DIM) for v in vv]

        def norm_rope(y, s2, gain):
            y = y * lax.rsqrt(s2 * (1.0 / HEAD_DIM) + EPS) * gain
            return y * cos_ref[...] + _rope_partner(y, lane) * sin_ref[...]

        row = lax.broadcasted_iota(jnp.int32, (LANES, tm), 0)
        top = row < HEAD_DIM
        half_heads = N_Q_HEADS // 2
        for j in range(Q_DIM // LANES):
            qt = (norm_rope(yqk[j], ss[j], qg_ref[...]) * QK_SCALE_LOG2).T
            zero = jnp.zeros_like(qt)
            qt_ref[j] = jnp.where(top, qt, zero).astype(BF16)
            qt_ref[j + half_heads] = jnp.where(top, zero, qt).astype(BF16)
        k_ref[...] = norm_rope(yqk[n_qk - 1], ss[n_qk - 1], kg_ref[...]).astype(BF16)
        vt = y_r[:, v0:uv0].T.astype(BF16)
        for c in range(tm // TK):
            vt_ref[c] = vt[:, c * TK:(c + 1) * TK]

        gq = g1 + (w_ref.shape[1] - g1) // 2
        project(g1, gq)
        dv = [v - m for v, m in zip(vv, mu)]
        var = [_group_sum(d * d, gmat) * (1.0 / GM_GROUP_DIM) for d in dv]
        gate_ref[:, n_gate // 2:] = jax.nn.sigmoid(y_r[:, g1:])
        u_ref[...] = jax.nn.gelu(y_r[:, uv0:uv0 + GM_DIM])
        project(gq, w_ref.shape[1])
        for j in range(n_gm):
            sl = slice(j * LANES, (j + 1) * LANES)
            vn = dv[j] * lax.rsqrt(var[j] + EPS) * lng_ref[:, sl] + lnb_ref[:, sl]
            vn_ref[:, sl] = vn.astype(BF16)

    @pl.when(i % 2 == 0)
    def _():
        compute(ya_ref, yb_ref)

    @pl.when(i % 2 == 1)
    def _():
        compute(yb_ref, ya_ref)


def _sublane_partial_sum(e):
    rows, lanes = e.shape
    return jnp.sum(e.reshape(rows // SUBLANES, SUBLANES, lanes), axis=0)


def _attn_kernel(shift_ref, qt_ref, k_ref, vt_ref, o_ref, sa_ref, sb_ref, m_ref, l_ref, acc_ref,
                 *, fixed_shift):
    n_heads = qt_ref.shape[0]
    group = n_heads // N_KV_HEADS
    n_chunks = k_ref.shape[0] // TK
    if fixed_shift:
        shift = shift_ref[0]
    else:
        m_ref[...] = jnp.full(m_ref.shape, -jnp.inf, F32)
    acc_ref[...] = jnp.zeros(acc_ref.shape, F32)
    l_ref[...] = jnp.zeros(l_ref.shape, F32)

    def keys(c):
        return k_ref[pl.ds(pl.multiple_of(c * TK, TK), TK), :]

    def step(c, cur_ref, next_ref):
        if next_ref is not None:
            k_next = keys(c + 1)
        vt_c = vt_ref[c]
        for h in range(n_heads):
            g = h // group
            if next_ref is not None:
                next_ref[h] = _dot(k_next, qt_ref[h])
            s = cur_ref[h]
            vt_g = vt_c[g * HEAD_DIM:(g + 1) * HEAD_DIM, :]
            if fixed_shift:
                e = jnp.exp2(s - shift)
                l_ref[h] += _sublane_partial_sum(e)
                acc_ref[h] += _dot(vt_g, e.astype(BF16))
            else:
                m_old = m_ref[h]
                m_new = jnp.maximum(m_old, jnp.max(s, axis=0, keepdims=True))
                alpha = jnp.exp2(m_old - m_new)
                e = jnp.exp2(s - m_new)
                m_ref[h] = m_new
                l_ref[h] = alpha * l_ref[h] + _sublane_partial_sum(e)
                acc_ref[h] = alpha * acc_ref[h] + _dot(vt_g, e.astype(BF16))

    k_first = keys(0)
    for h in range(n_heads):
        sa_ref[h] = _dot(k_first, qt_ref[h])

    bufs = (sa_ref, sb_ref)

    def steps(c0, last):
        for j in range(CHUNKS_PER_BODY):
            final = last and j == CHUNKS_PER_BODY - 1
            step(c0 + j, bufs[j % 2], None if final else bufs[(j + 1) % 2])

    def body(i, carry):
        steps(CHUNKS_PER_BODY * i, False)
        return carry

    lax.fori_loop(0, n_chunks // CHUNKS_PER_BODY - 1, body, 0)
    steps(n_chunks - CHUNKS_PER_BODY, True)
    for h in range(n_heads):
        o = acc_ref[h] / jnp.sum(l_ref[h], axis=0, keepdims=True)
        o_ref[h * HEAD_DIM:(h + 1) * HEAD_DIM, :] = o.astype(BF16)


def _merge_ffn_kernel(x_ref, at_ref, u_ref, vn_ref, gate_ref, ws_ref, bs_ref,
                      wpa_ref, wpb_ref, wo_ref, gf_ref, wgu_ref, wd_ref, gfin_ref,
                      o_ref, gm_ref, act_ref, xa_ref, xb_ref):
    i = pl.program_id(0)
    tm = x_ref.shape[0]
    d_model = x_ref.shape[1]
    d_ff = wd_ref.shape[0]

    @pl.when(i == 0)
    def _():
        xb_ref[...] = jnp.zeros(xb_ref.shape, F32)

    def compute(x1_w, x1_r):
        x1p = x1_r[...]
        ms = jnp.mean(x1p * x1p, axis=-1, keepdims=True)
        h = (x1p * lax.rsqrt(ms + EPS) * gf_ref[...]).astype(BF16)

        lane = lax.broadcasted_iota(jnp.int32, (GM_CHUNK, LANES), 1)
        low = lane < GM_GROUP_DIM
        for c in range(tm // GM_CHUNK):
            rs = slice(c * GM_CHUNK, (c + 1) * GM_CHUNK)
            for j in range(GM_DIM // LANES):
                cs = slice(j * LANES, (j + 1) * LANES)
                vblk = vn_ref[rs, cs]
                zero = jnp.zeros_like(vblk)
                vsplit = jnp.concatenate([jnp.where(low, vblk, zero), jnp.where(low, zero, vblk)],
                                         axis=0)
                sv = _dot(ws_ref[j], vsplit)
                gm_ref[rs, cs] = (u_ref[rs, cs] * (sv + bs_ref[:, cs])).astype(BF16)

        pa = lax.dot_general(at_ref[...], wpa_ref[...], (((0,), (0,)), ((), ())),
                             preferred_element_type=F32)
        pb = _dot(gm_ref[...], wpb_ref[...])
        mix = (gate_ref[:, :d_model] * pa + gate_ref[:, d_model:] * pb).astype(BF16)

        for c0 in range(0, d_ff, FF_CHUNK):
            c1 = min(c0 + FF_CHUNK, d_ff)
            gt = _dot(h, wgu_ref[:, c0:c1])
            up = _dot(h, wgu_ref[:, d_ff + c0:d_ff + c1])
            act_ref[:, c0:c1] = (jax.nn.silu(gt) * up).astype(BF16)
        x2 = x1p + _dot(act_ref[...], wd_ref[...])
        ms2 = jnp.mean(x2 * x2, axis=-1, keepdims=True)
        o_ref[...] = x2 * lax.rsqrt(ms2 + EPS) * gfin_ref[...]

        x1_w[...] = x_ref[...] + _dot(mix, wo_ref[...])

    @pl.when(i % 2 == 0)
    def _():
        compute(xa_ref, xb_ref)

    @pl.when(i % 2 == 1)
    def _():
        compute(xb_ref, xa_ref)


def _const_spec(shape):
    nd = len(shape)
    return pl.BlockSpec(shape, lambda *_: (0,) * nd, pipeline_mode=pl.Buffered(1))


def _rope_tables(seq):
    half = HEAD_DIM // 2
    rows = seq // GRID_W
    row = np.repeat(np.arange(rows, dtype=np.float64), GRID_W)
    col = np.tile(np.arange(GRID_W, dtype=np.float64), rows)
    inv_freq = 1.0 / (ROPE_THETA ** (np.arange(0, half, 2, dtype=np.float64) / half))
    ang_r = row[:, None] * inv_freq[None, :]
    ang_c = col[:, None] * inv_freq[None, :]
    cr, sr = np.cos(ang_r), np.sin(ang_r)
    cc, sc = np.cos(ang_c), np.sin(ang_c)
    cos64 = np.concatenate([cr, cr, cc, cc], axis=-1)
    sin64 = np.concatenate([-sr, sr, -sc, sc], axis=-1)
    reps = (1, LANES // HEAD_DIM)
    return np.tile(cos64, reps).astype(np.float32), np.tile(sin64, reps).astype(np.float32)


def kernel(x, norm_mix_g, w_in, q_norm_g, k_norm_g, gm_ln_g, gm_ln_b, w_s, b_s, w_proj_a,
           w_proj_b, w_out, norm_ffn_g, w_gate_up, w_down, norm_final_g):
    B, S, D = x.shape
    in_dim = w_in.shape[-1]
    d_ff = w_down.shape[1]
    assert norm_mix_g.shape[0] == 1, "single-layer block"
    assert CHUNKS_PER_BODY % 2 == 0 and S % (CHUNKS_PER_BODY * TK) == 0
    assert S % TQ == 0 and S % TM_IN == 0 and S % TM_OUT == 0
    assert TM_OUT % GM_CHUNK == 0 and TM_IN % TK == 0 and TM_IN == TQ and TM_OUT == TQ

    w_in_b = w_in[0].astype(BF16)
    half_heads = N_Q_HEADS // 2
    w_q_b = (w_in_b[:, :Q_DIM].reshape(D, 2, half_heads, HEAD_DIM)
             .transpose(0, 2, 1, 3).reshape(D, Q_DIM))

    cos_t, sin_t = _rope_tables(S)
    reps = LANES // HEAD_DIM
    qg = jnp.tile(q_norm_g[0], reps)[None, :]
    kg = jnp.tile(k_norm_g[0], reps)[None, :]
    lng = gm_ln_g[0].reshape(1, GM_DIM)
    lnb = gm_ln_b[0].reshape(1, GM_DIM)
    gid = np.arange(LANES) // HEAD_DIM
    gmat = (gid[:, None] == gid[None, :]).astype(np.float32)
    gmat2 = jnp.asarray(np.concatenate([gmat, gmat], axis=0), dtype=BF16)

    two_axes = ("arbitrary", "arbitrary")
    cparams = pltpu.CompilerParams(dimension_semantics=two_axes,
                                   vmem_limit_bytes=VMEM_LIMIT_BYTES)

    M = B * S
    n_seq = S // TM_IN
    n_blk = M // TM_IN
    ck_in = TM_IN // TK

    def in_blk(i):
        return jnp.minimum(i, n_blk - 1)

    def out_blk(i):
        return jnp.maximum(i - 1, 0)

    def out_rows(width):
        return pl.BlockSpec((TM_IN, width), lambda i: (out_blk(i), 0))

    to_cast = [(w_gate_up, n_blk), (w_out, n_blk), (w_down, d_ff // (CAST_ROWS * CAST_ROWS)),
               (w_proj_a, w_proj_a.shape[1] // CAST_ROWS), (w_proj_b, w_proj_b.shape[1] // CAST_ROWS)]
    cast_steps = tuple(n for _, n in to_cast)
    assert all(n <= n_blk and w.shape[1] % n == 0 for w, n in to_cast)
    cast_specs = [pl.BlockSpec((None, w.shape[1] // n, w.shape[2]),
                               lambda i, n=n: (0, jnp.minimum(i, n - 1), 0)) for w, n in to_cast]

    qt, k, vt, u, vn, gates, wgu_b, wo_b, wd_b, wpa_b, wpb_b = pl.pallas_call(
        functools.partial(_inproj_kernel, cast_steps=cast_steps),
        grid=(n_blk + 1,),
        in_specs=[
            pl.BlockSpec((TM_IN, D), lambda i: (in_blk(i), 0)),
            _const_spec((1, D)),
            _const_spec((D, Q_DIM)),
            _const_spec((D, in_dim)),
            _const_spec((2 * LANES, LANES)),
            pl.BlockSpec((TM_IN, LANES), lambda i: (out_blk(i) % n_seq, 0)),
            pl.BlockSpec((TM_IN, LANES), lambda i: (out_blk(i) % n_seq, 0)),
            _const_spec((1, LANES)),
            _const_spec((1, LANES)),
            _const_spec((1, GM_DIM)),
            _const_spec((1, GM_DIM)),
        ] + cast_specs,
        out_specs=[
            pl.BlockSpec((None, N_Q_HEADS, LANES, TM_IN), lambda i: (out_blk(i), 0, 0, 0)),
            out_rows(KV_DIM),
            pl.BlockSpec((ck_in, KV_DIM, TK), lambda i: (out_blk(i), 0, 0)),
            out_rows(GM_DIM), out_rows(GM_DIM), out_rows(2 * D),
        ] + cast_specs,
        out_shape=[
            jax.ShapeDtypeStruct((n_blk, N_Q_HEADS, LANES, TM_IN), BF16),
            jax.ShapeDtypeStruct((M, KV_DIM), BF16),
            jax.ShapeDtypeStruct((M // TK, KV_DIM, TK), BF16),
            jax.ShapeDtypeStruct((M, GM_DIM), F32),
            jax.ShapeDtypeStruct((M, GM_DIM), BF16),
            jax.ShapeDtypeStruct((M, 2 * D), F32),
        ] + [jax.ShapeDtypeStruct(w.shape, BF16) for w, _ in to_cast],
        scratch_shapes=[pltpu.VMEM((TM_IN, in_dim), F32), pltpu.VMEM((TM_IN, in_dim), F32)],
        compiler_params=pltpu.CompilerParams(dimension_semantics=("arbitrary",),
                                             vmem_limit_bytes=VMEM_LIMIT_BYTES),
        name="inproj",
    )(x.reshape(M, D), norm_mix_g, w_q_b, w_in_b, gmat2, cos_t, sin_t, qg, kg, lng, lnb,
      *[w for w, _ in to_cast])
    k = k.reshape(B, S, KV_DIM)
    vt = vt.reshape(B, S // TK, KV_DIM, TK)

    n_q = S // TQ

    def attention(fixed_shift):
        return pl.pallas_call(
            functools.partial(_attn_kernel, fixed_shift=fixed_shift),
            grid=(B, n_q),
            in_specs=[
                pl.BlockSpec(memory_space=pltpu.SMEM),
                pl.BlockSpec((None, N_Q_HEADS, LANES, TQ), lambda b, i: (b * n_q + i, 0, 0, 0)),
                pl.BlockSpec((None, S, KV_DIM), lambda b, i: (b, 0, 0)),
                pl.BlockSpec((None, S // TK, KV_DIM, TK), lambda b, i: (b, 0, 0, 0)),
            ],
            out_specs=pl.BlockSpec((None, Q_DIM, TQ), lambda b, i: (b * n_q + i, 0, 0)),
            out_shape=jax.ShapeDtypeStruct((B * n_q, Q_DIM, TQ), BF16),
            scratch_shapes=[
                pltpu.VMEM((N_Q_HEADS, TK, TQ), F32),
                pltpu.VMEM((N_Q_HEADS, TK, TQ), F32),
                pltpu.VMEM((N_Q_HEADS, 1, TQ), F32),
                pltpu.VMEM((N_Q_HEADS, SUBLANES, TQ), F32),
                pltpu.VMEM((N_Q_HEADS, HEAD_DIM, TQ), F32),
            ],
            compiler_params=cparams,
            name="gqa_attention_shift" if fixed_shift else "gqa_attention_online",
        )

    score_bound = (HEAD_DIM * QK_SCALE_LOG2 * (1.0 + 2.0 ** -6)
                   * jnp.max(jnp.abs(q_norm_g[0])) * jnp.max(jnp.abs(k_norm_g[0])))
    attn_t = lax.cond(
        score_bound <= MAX_FIXED_SHIFT,
        lambda s_, *ops: attention(True)(s_, *ops),
        lambda s_, *ops: attention(False)(s_, *ops),
        score_bound.reshape(1).astype(F32), qt, k, vt)

    bs_full = jnp.repeat(jnp.transpose(b_s[0]), GM_GROUP_DIM, axis=1)
    ws_pairs = (w_s[0].reshape(GM_GROUPS // 2, 2, GM_CHUNK, GM_CHUNK)
                .transpose(0, 2, 1, 3).reshape(GM_GROUPS // 2, GM_CHUNK, 2 * GM_CHUNK).astype(BF16))
    n_out = M // TM_OUT

    def s1_blk(i):
        return jnp.minimum(i, n_out - 1)

    def s1_rows(width):
        return pl.BlockSpec((TM_OUT, width), lambda i: (s1_blk(i), 0))

    out = pl.pallas_call(
        _merge_ffn_kernel,
        grid=(n_out + 1,),
        in_specs=[
            s1_rows(D),
            pl.BlockSpec((None, Q_DIM, TM_OUT), lambda i: (s1_blk(i), 0, 0)),
            s1_rows(GM_DIM), s1_rows(GM_DIM), s1_rows(2 * D),
            _const_spec((GM_GROUPS // 2, GM_CHUNK, 2 * GM_CHUNK)),
            _const_spec((GM_CHUNK, GM_DIM)),
            _const_spec((None, Q_DIM, D)),
            _const_spec((None, GM_DIM, D)),
            _const_spec((None, D, D)),
            _const_spec((1, D)),
            _const_spec((None, D, 2 * d_ff)),
            _const_spec((None, d_ff, D)),
            _const_spec((1, D)),
        ],
        out_specs=pl.BlockSpec((TM_OUT, D), lambda i: (jnp.maximum(i - 1, 0), 0)),
        out_shape=jax.ShapeDtypeStruct((M, D), F32),
        scratch_shapes=[
            pltpu.VMEM((TM_OUT, GM_DIM), BF16),
            pltpu.VMEM((TM_OUT, d_ff), BF16),
            pltpu.VMEM((TM_OUT, D), F32),
            pltpu.VMEM((TM_OUT, D), F32),
        ],
        compiler_params=pltpu.CompilerParams(dimension_semantics=("arbitrary",),
                                             vmem_limit_bytes=VMEM_LIMIT_BYTES),
        name="merge_ffn",
    )(x.reshape(M, D), attn_t, u, vn, gates, ws_pairs, bs_full,
      wpa_b, wpb_b, wo_b, norm_ffn_g, wgu_b, wd_b, norm_final_g[None, :])
    return out.reshape(B, S, D)
```

```python
import functools
import math

import jax
import jax.numpy as jnp
import numpy as np
from jax import lax
from jax.experimental import pallas as pl
from jax.experimental.pallas import tpu as pltpu

HEAD_DIM = 64
N_Q_HEADS = 8
N_KV_HEADS = 2
Q_DIM = N_Q_HEADS * HEAD_DIM
KV_DIM = N_KV_HEADS * HEAD_DIM
GM_GROUPS = 8
GM_GROUP_DIM = 64
GM_DIM = GM_GROUPS * GM_GROUP_DIM
GM_CHUNK = 128
GRID_W = 64
ROPE_THETA = 10000.0
EPS = 1e-6

LANES = 128
SUBLANES = 8
GROUP_FOLD = 32
VMEM_LIMIT_BYTES = 56 * 1024 * 1024

TM_IN = 256
TQ = 256
TK = 256
QK_SCALE_LOG2 = math.log2(math.e) / math.sqrt(HEAD_DIM)
MAX_FIXED_SHIFT = 48.0
CHUNKS_PER_BODY = 16
TM_OUT = 256
CAST_ROWS = 16
FF_CHUNK = 768

F32 = jnp.float32
BF16 = jnp.bfloat16


def _dot(a, b):
    return jnp.dot(a, b, preferred_element_type=F32)


def _group_sums(xs, gpair, lane):
    first = (lane & GROUP_FOLD) != 0
    out = []
    for j in range(0, len(xs), 2):
        xa, xb = xs[j], xs[min(j + 1, len(xs) - 1)]
        za = xa + pltpu.roll(xa, GROUP_FOLD, axis=1)
        zb = xb + pltpu.roll(xb, LANES - GROUP_FOLD, axis=1)
        w = jnp.where(first, za, zb)
        hi = w.astype(BF16)
        lo = (w - hi.astype(F32)).astype(BF16)
        r = _dot(jnp.concatenate([hi, lo], axis=1), gpair)
        out += [r[:, :LANES], r[:, LANES:]][:len(xs) - j]
    return out


def _group_pair_matrix():
    lane = np.arange(LANES)
    col = np.arange(2 * LANES)
    same_group = (lane[:, None] // HEAD_DIM) == ((col[None, :] % LANES) // HEAD_DIM)
    first_lane = (lane[:, None] & GROUP_FOLD) != 0
    first_slab = (col[None, :] // LANES) == 0
    m = (same_group & (first_lane == first_slab)).astype(np.float32)
    return np.concatenate([m, m], axis=0)


def _rope_partner(x, lane):
    fwd = pltpu.roll(x, LANES - 16, axis=1)
    bwd = pltpu.roll(x, 16, axis=1)
    return jnp.where((lane & 16) == 0, fwd, bwd)


def _inproj_kernel(x_ref, g_ref, wq_ref, w_ref, gmat_ref, cos_ref, sin_ref, qg_ref, kg_ref,
                   lng_ref, lnb_ref, *refs, cast_steps):
    n_cast = len(cast_steps)
    cast_in = refs[:n_cast]
    qt_ref, k_ref, vt_ref, u_ref, vn_ref, gate_ref = refs[n_cast:n_cast + 6]
    cast_out = refs[n_cast + 6:2 * n_cast + 6]
    ya_ref, yb_ref = refs[2 * n_cast + 6:]
    i = pl.program_id(0)
    tm = x_ref.shape[0]

    @pl.when(i == 0)
    def _():
        yb_ref[...] = jnp.zeros(yb_ref.shape, F32)

    for src, dst, n_steps in zip(cast_in, cast_out, cast_steps):
        @pl.when(i < n_steps)
        def _(src=src, dst=dst):
            dst[...] = src[...].astype(dst.dtype)

    def compute(y_w, y_r):
        x = x_ref[...]
        ms = jnp.mean(x * x, axis=-1, keepdims=True)
        h = (x * lax.rsqrt(ms + EPS) * g_ref[...]).astype(BF16)
        gmat = gmat_ref[...]
        lane = lax.broadcasted_iota(jnp.int32, (tm, LANES), 1)

        def project(c0, c1):
            y_w[:, c0:c1] = _dot(h, w_ref[:, c0:c1])

        y_w[:, :Q_DIM] = _dot(h, wq_ref[...])

        v0 = Q_DIM + KV_DIM
        uv0 = Q_DIM + 2 * KV_DIM
        g0 = uv0 + 2 * GM_DIM
        g1 = g0 + (w_ref.shape[1] - g0) // 2

        n_qk = (Q_DIM + KV_DIM) // LANES
        yqk = [y_r[:, j * LANES:(j + 1) * LANES] for j in range(n_qk)]
        project(Q_DIM, uv0)
        n_gate = w_ref.shape[1] - g0
        gate_ref[:, :n_gate // 2] = jax.nn.sigmoid(y_r[:, g0:g1])
        ss = _group_sums([y * y for y in yqk], gmat, lane)
        project(uv0, g0)

        def norm_rope(y, s2, gain):
            y = y * lax.rsqrt(s2 * (1.0 / HEAD_DIM) + EPS) * gain
            return y * cos_ref[...] + _rope_partner(y, lane) * sin_ref[...]

        row = lax.broadcasted_iota(jnp.int32, (LANES, tm), 0)
        top = row < HEAD_DIM
        half_heads = N_Q_HEADS // 2
        for j in range(Q_DIM // LANES):
            qt = (norm_rope(yqk[j], ss[j], qg_ref[...]) * QK_SCALE_LOG2).T
            zero = jnp.zeros_like(qt)
            qt_ref[j] = jnp.where(top, qt, zero).astype(BF16)
            qt_ref[j + half_heads] = jnp.where(top, zero, qt).astype(BF16)
        k_ref[...] = norm_rope(yqk[n_qk - 1], ss[n_qk - 1], kg_ref[...]).astype(BF16)
        vt = y_r[:, v0:uv0].T.astype(BF16)
        for c in range(tm // TK):
            vt_ref[c] = vt[:, c * TK:(c + 1) * TK]

        gate_ref[:, n_gate // 2:] = jax.nn.sigmoid(y_r[:, g1:])
        u_ref[...] = jax.nn.gelu(y_r[:, uv0:uv0 + GM_DIM])
        n_gm = GM_DIM // LANES
        vv = [jax.nn.gelu(y_r[:, uv0 + GM_DIM + j * LANES:uv0 + GM_DIM + (j + 1) * LANES])
              for j in range(n_gm)]
        project(g0, g1)
        mu = [m * (1.0 / GM_GROUP_DIM) for m in _group_sums(vv, gmat, lane)]
        gq = g1 + (w_ref.shape[1] - g1) // 2
        project(g1, gq)
        dv = [v - m for v, m in zip(vv, mu)]
        var = [m * (1.0 / GM_GROUP_DIM) for m in _group_sums([d * d for d in dv], gmat, lane)]
        project(gq, w_ref.shape[1])
        for j in range(n_gm):
            sl = slice(j * LANES, (j + 1) * LANES)
            vn = dv[j] * lax.rsqrt(var[j] + EPS) * lng_ref[:, sl] + lnb_ref[:, sl]
            vn_ref[:, sl] = vn.astype(BF16)

    @pl.when(i % 2 == 0)
    def _():
        compute(ya_ref, yb_ref)

    @pl.when(i % 2 == 1)
    def _():
        compute(yb_ref, ya_ref)


def _sublane_partial_sum(e):
    rows, lanes = e.shape
    return jnp.sum(e.reshape(rows // SUBLANES, SUBLANES, lanes), axis=0)


def _attn_kernel(shift_ref, qt_ref, k_ref, vt_ref, o_ref, sa_ref, sb_ref, m_ref, l_ref, acc_ref,
                 *, fixed_shift):
    n_heads = qt_ref.shape[0]
    group = n_heads // N_KV_HEADS
    n_chunks = k_ref.shape[0] // TK
    if fixed_shift:
        shift = shift_ref[0]
    else:
        m_ref[...] = jnp.full(m_ref.shape, -jnp.inf, F32)
    acc_ref[...] = jnp.zeros(acc_ref.shape, F32)
    l_ref[...] = jnp.zeros(l_ref.shape, F32)

    def keys(c):
        return k_ref[pl.ds(pl.multiple_of(c * TK, TK), TK), :]

    def step(c, cur_ref, next_ref):
        if next_ref is not None:
            k_next = keys(c + 1)
        vt_c = vt_ref[c]
        for h in range(n_heads):
            g = h // group
            if next_ref is not None:
                next_ref[h] = _dot(k_next, qt_ref[h])
            s = cur_ref[h]
            vt_g = vt_c[g * HEAD_DIM:(g + 1) * HEAD_DIM, :]
            if fixed_shift:
                e = jnp.exp2(s - shift)
                l_ref[h] += _sublane_partial_sum(e)
                acc_ref[h] += _dot(vt_g, e.astype(BF16))
            else:
                m_old = m_ref[h]
                m_new = jnp.maximum(m_old, jnp.max(s, axis=0, keepdims=True))
                alpha = jnp.exp2(m_old - m_new)
                e = jnp.exp2(s - m_new)
                m_ref[h] = m_new
                l_ref[h] = alpha * l_ref[h] + _sublane_partial_sum(e)
                acc_ref[h] = alpha * acc_ref[h] + _dot(vt_g, e.astype(BF16))

    k_first = keys(0)
    for h in range(n_heads):
        sa_ref[h] = _dot(k_first, qt_ref[h])

    bufs = (sa_ref, sb_ref)

    def steps(c0, last):
        for j in range(CHUNKS_PER_BODY):
            final = last and j == CHUNKS_PER_BODY - 1
            step(c0 + j, bufs[j % 2], None if final else bufs[(j + 1) % 2])

    def body(i, carry):
        steps(CHUNKS_PER_BODY * i, False)
        return carry

    lax.fori_loop(0, n_chunks // CHUNKS_PER_BODY - 1, body, 0)
    steps(n_chunks - CHUNKS_PER_BODY, True)
    for h in range(n_heads):
        o = acc_ref[h] / jnp.sum(l_ref[h], axis=0, keepdims=True)
        o_ref[h * HEAD_DIM:(h + 1) * HEAD_DIM, :] = o.astype(BF16)


def _merge_ffn_kernel(x_ref, at_ref, u_ref, vn_ref, gate_ref, ws_ref, bs_ref,
                      wpa_ref, wpb_ref, wo_ref, gf_ref, wgu_ref, wd_ref, gfin_ref,
                      o_ref, gm_ref, act_ref, xa_ref, xb_ref):
    i = pl.program_id(0)
    tm = x_ref.shape[0]
    d_model = x_ref.shape[1]
    d_ff = wd_ref.shape[0]

    @pl.when(i == 0)
    def _():
        xb_ref[...] = jnp.zeros(xb_ref.shape, F32)

    def compute(x1_w, x1_r):
        x1p = x1_r[...]
        ms = jnp.mean(x1p * x1p, axis=-1, keepdims=True)
        h = (x1p * lax.rsqrt(ms + EPS) * gf_ref[...]).astype(BF16)

        lane = lax.broadcasted_iota(jnp.int32, (GM_CHUNK, LANES), 1)
        low = lane < GM_GROUP_DIM
        for c in range(tm // GM_CHUNK):
            rs = slice(c * GM_CHUNK, (c + 1) * GM_CHUNK)
            for j in range(GM_DIM // LANES):
                cs = slice(j * LANES, (j + 1) * LANES)
                vblk = vn_ref[rs, cs]
                zero = jnp.zeros_like(vblk)
                vsplit = jnp.concatenate([jnp.where(low, vblk, zero), jnp.where(low, zero, vblk)],
                                         axis=0)
                sv = _dot(ws_ref[j], vsplit)
                gm_ref[rs, cs] = (u_ref[rs, cs] * (sv + bs_ref[:, cs])).astype(BF16)

        pa = lax.dot_general(at_ref[...], wpa_ref[...], (((0,), (0,)), ((), ())),
                             preferred_element_type=F32)
        pb = _dot(gm_ref[...], wpb_ref[...])
        mix = (gate_ref[:, :d_model] * pa + gate_ref[:, d_model:] * pb).astype(BF16)

        for c0 in range(0, d_ff, FF_CHUNK):
            c1 = min(c0 + FF_CHUNK, d_ff)
            gt = _dot(h, wgu_ref[:, c0:c1])
            up = _dot(h, wgu_ref[:, d_ff + c0:d_ff + c1])
            act_ref[:, c0:c1] = (jax.nn.silu(gt) * up).astype(BF16)
        x2 = x1p + _dot(act_ref[...], wd_ref[...])
        ms2 = jnp.mean(x2 * x2, axis=-1, keepdims=True)
        o_ref[...] = x2 * lax.rsqrt(ms2 + EPS) * gfin_ref[...]

        x1_w[...] = x_ref[...] + _dot(mix, wo_ref[...])

    @pl.when(i % 2 == 0)
    def _():
        compute(xa_ref, xb_ref)

    @pl.when(i % 2 == 1)
    def _():
        compute(xb_ref, xa_ref)


def _const_spec(shape):
    nd = len(shape)
    return pl.BlockSpec(shape, lambda *_: (0,) * nd, pipeline_mode=pl.Buffered(1))


def _rope_tables(seq):
    half = HEAD_DIM // 2
    rows = seq // GRID_W
    row = np.repeat(np.arange(rows, dtype=np.float64), GRID_W)
    col = np.tile(np.arange(GRID_W, dtype=np.float64), rows)
    inv_freq = 1.0 / (ROPE_THETA ** (np.arange(0, half, 2, dtype=np.float64) / half))
    ang_r = row[:, None] * inv_freq[None, :]
    ang_c = col[:, None] * inv_freq[None, :]
    cr, sr = np.cos(ang_r), np.sin(ang_r)
    cc, sc = np.cos(ang_c), np.sin(ang_c)
    cos64 = np.concatenate([cr, cr, cc, cc], axis=-1)
    sin64 = np.concatenate([-sr, sr, -sc, sc], axis=-1)
    reps = (1, LANES // HEAD_DIM)
    return np.tile(cos64, reps).astype(np.float32), np.tile(sin64, reps).astype(np.float32)


def kernel(x, norm_mix_g, w_in, q_norm_g, k_norm_g, gm_ln_g, gm_ln_b, w_s, b_s, w_proj_a,
           w_proj_b, w_out, norm_ffn_g, w_gate_up, w_down, norm_final_g):
    B, S, D = x.shape
    in_dim = w_in.shape[-1]
    d_ff = w_down.shape[1]
    assert norm_mix_g.shape[0] == 1, "single-layer block"
    assert CHUNKS_PER_BODY % 2 == 0 and S % (CHUNKS_PER_BODY * TK) == 0
    assert S % TQ == 0 and S % TM_IN == 0 and S % TM_OUT == 0
    assert TM_OUT % GM_CHUNK == 0 and TM_IN % TK == 0 and TM_IN == TQ and TM_OUT == TQ

    w_in_b = w_in[0].astype(BF16)
    half_heads = N_Q_HEADS // 2
    w_q_b = (w_in_b[:, :Q_DIM].reshape(D, 2, half_heads, HEAD_DIM)
             .transpose(0, 2, 1, 3).reshape(D, Q_DIM))

    cos_t, sin_t = _rope_tables(S)
    reps = LANES // HEAD_DIM
    qg = jnp.tile(q_norm_g[0], reps)[None, :]
    kg = jnp.tile(k_norm_g[0], reps)[None, :]
    lng = gm_ln_g[0].reshape(1, GM_DIM)
    lnb = gm_ln_b[0].reshape(1, GM_DIM)
    gmat2 = jnp.asarray(_group_pair_matrix(), dtype=BF16)

    two_axes = ("arbitrary", "arbitrary")
    cparams = pltpu.CompilerParams(dimension_semantics=two_axes,
                                   vmem_limit_bytes=VMEM_LIMIT_BYTES)

    M = B * S
    n_seq = S // TM_IN
    n_blk = M // TM_IN
    ck_in = TM_IN // TK

    def in_blk(i):
        return jnp.minimum(i, n_blk - 1)

    def out_blk(i):
        return jnp.maximum(i - 1, 0)

    def out_rows(width):
        return pl.BlockSpec((TM_IN, width), lambda i: (out_blk(i), 0))

    to_cast = [(w_gate_up, n_blk), (w_out, n_blk), (w_down, d_ff // (CAST_ROWS * CAST_ROWS)),
               (w_proj_a, w_proj_a.shape[1] // CAST_ROWS), (w_proj_b, w_proj_b.shape[1] // CAST_ROWS)]
    cast_steps = tuple(n for _, n in to_cast)
    assert all(n <= n_blk and w.shape[1] % n == 0 for w, n in to_cast)
    cast_specs = [pl.BlockSpec((None, w.shape[1] // n, w.shape[2]),
                               lambda i, n=n: (0, jnp.minimum(i, n - 1), 0)) for w, n in to_cast]

    qt, k, vt, u, vn, gates, wgu_b, wo_b, wd_b, wpa_b, wpb_b = pl.pallas_call(
        functools.partial(_inproj_kernel, cast_steps=cast_steps),
        grid=(n_blk + 1,),
        in_specs=[
            pl.BlockSpec((TM_IN, D), lambda i: (in_blk(i), 0)),
            _const_spec((1, D)),
            _const_spec((D, Q_DIM)),
            _const_spec((D, in_dim)),
            _const_spec((2 * LANES, 2 * LANES)),
            pl.BlockSpec((TM_IN, LANES), lambda i: (out_blk(i) % n_seq, 0)),
            pl.BlockSpec((TM_IN, LANES), lambda i: (out_blk(i) % n_seq, 0)),
            _const_spec((1, LANES)),
            _const_spec((1, LANES)),
            _const_spec((1, GM_DIM)),
            _const_spec((1, GM_DIM)),
        ] + cast_specs,
        out_specs=[
            pl.BlockSpec((None, N_Q_HEADS, LANES, TM_IN), lambda i: (out_blk(i), 0, 0, 0)),
            out_rows(KV_DIM),
            pl.BlockSpec((ck_in, KV_DIM, TK), lambda i: (out_blk(i), 0, 0)),
            out_rows(GM_DIM), out_rows(GM_DIM), out_rows(2 * D),
        ] + cast_specs,
        out_shape=[
            jax.ShapeDtypeStruct((n_blk, N_Q_HEADS, LANES, TM_IN), BF16),
            jax.ShapeDtypeStruct((M, KV_DIM), BF16),
            jax.ShapeDtypeStruct((M // TK, KV_DIM, TK), BF16),
            jax.ShapeDtypeStruct((M, GM_DIM), F32),
            jax.ShapeDtypeStruct((M, GM_DIM), BF16),
            jax.ShapeDtypeStruct((M, 2 * D), F32),
        ] + [jax.ShapeDtypeStruct(w.shape, BF16) for w, _ in to_cast],
        scratch_shapes=[pltpu.VMEM((TM_IN, in_dim), F32), pltpu.VMEM((TM_IN, in_dim), F32)],
        compiler_params=pltpu.CompilerParams(dimension_semantics=("arbitrary",),
                                             vmem_limit_bytes=VMEM_LIMIT_BYTES),
        name="inproj",
    )(x.reshape(M, D), norm_mix_g, w_q_b, w_in_b, gmat2, cos_t, sin_t, qg, kg, lng, lnb,
      *[w for w, _ in to_cast])
    k = k.reshape(B, S, KV_DIM)
    vt = vt.reshape(B, S // TK, KV_DIM, TK)

    n_q = S // TQ

    def attention(fixed_shift):
        return pl.pallas_call(
            functools.partial(_attn_kernel, fixed_shift=fixed_shift),
            grid=(B, n_q),
            in_specs=[
                pl.BlockSpec(memory_space=pltpu.SMEM),
                pl.BlockSpec((None, N_Q_HEADS, LANES, TQ), lambda b, i: (b * n_q + i, 0, 0, 0)),
                pl.BlockSpec((None, S, KV_DIM), lambda b, i: (b, 0, 0)),
                pl.BlockSpec((None, S // TK, KV_DIM, TK), lambda b, i: (b, 0, 0, 0)),
            ],
            out_specs=pl.BlockSpec((None, Q_DIM, TQ), lambda b, i: (b * n_q + i, 0, 0)),
            out_shape=jax.ShapeDtypeStruct((B * n_q, Q_DIM, TQ), BF16),
            scratch_shapes=[
                pltpu.VMEM((N_Q_HEADS, TK, TQ), F32),
                pltpu.VMEM((N_Q_HEADS, TK, TQ), F32),
                pltpu.VMEM((N_Q_HEADS, 1, TQ), F32),
                pltpu.VMEM((N_Q_HEADS, SUBLANES, TQ), F32),
                pltpu.VMEM((N_Q_HEADS, HEAD_DIM, TQ), F32),
            ],
            compiler_params=cparams,
            name="gqa_attention_shift" if fixed_shift else "gqa_attention_online",
        )

    score_bound = (HEAD_DIM * QK_SCALE_LOG2 * (1.0 + 2.0 ** -6)
                   * jnp.max(jnp.abs(q_norm_g[0])) * jnp.max(jnp.abs(k_norm_g[0])))
    attn_t = lax.cond(
        score_bound <= MAX_FIXED_SHIFT,
        lambda s_, *ops: attention(True)(s_, *ops),
        lambda s_, *ops: attention(False)(s_, *ops),
        score_bound.reshape(1).astype(F32), qt, k, vt)

    bs_full = jnp.repeat(jnp.transpose(b_s[0]), GM_GROUP_DIM, axis=1)
    ws_pairs = (w_s[0].reshape(GM_GROUPS // 2, 2, GM_CHUNK, GM_CHUNK)
                .transpose(0, 2, 1, 3).reshape(GM_GROUPS // 2, GM_CHUNK, 2 * GM_CHUNK).astype(BF16))
    n_out = M // TM_OUT

    def s1_blk(i):
        return jnp.minimum(i, n_out - 1)

    def s1_rows(width):
        return pl.BlockSpec((TM_OUT, width), lambda i: (s1_blk(i), 0))

    out = pl.pallas_call(
        _merge_ffn_kernel,
        grid=(n_out + 1,),
        in_specs=[
            s1_rows(D),
            pl.BlockSpec((None, Q_DIM, TM_OUT), lambda i: (s1_blk(i), 0, 0)),
            s1_rows(GM_DIM), s1_rows(GM_DIM), s1_rows(2 * D),
            _const_spec((GM_GROUPS // 2, GM_CHUNK, 2 * GM_CHUNK)),
            _const_spec((GM_CHUNK, GM_DIM)),
            _const_spec((None, Q_DIM, D)),
            _const_spec((None, GM_DIM, D)),
            _const_spec((None, D, D)),
            _const_spec((1, D)),
            _const_spec((None, D, 2 * d_ff)),
            _const_spec((None, d_ff, D)),
            _const_spec((1, D)),
        ],
        out_specs=pl.BlockSpec((TM_OUT, D), lambda i: (jnp.maximum(i - 1, 0), 0)),
        out_shape=jax.ShapeDtypeStruct((M, D), F32),
        scratch_shapes=[
            pltpu.VMEM((TM_OUT, GM_DIM), BF16),
            pltpu.VMEM((TM_OUT, d_ff), BF16),
            pltpu.VMEM((TM_OUT, D), F32),
            pltpu.VMEM((TM_OUT, D), F32),
        ],
        compiler_params=pltpu.CompilerParams(dimension_semantics=("arbitrary",),
                                             vmem_limit_bytes=VMEM_LIMIT_BYTES),
        name="merge_ffn",
    )(x.reshape(M, D), attn_t, u, vn, gates, ws_pairs, bs_full,
      wpa_b, wpb_b, wo_b, norm_ffn_g, wgu_b, wd_b, norm_final_g[None, :])
    return out.reshape(B, S, D)
```

```python
import functools
import math

import jax
import jax.numpy as jnp
import numpy as np
from jax import lax
from jax.experimental import pallas as pl
from jax.experimental.pallas import tpu as pltpu

HEAD_DIM = 64
N_Q_HEADS = 8
N_KV_HEADS = 2
Q_DIM = N_Q_HEADS * HEAD_DIM
KV_DIM = N_KV_HEADS * HEAD_DIM
GM_GROUPS = 8
GM_GROUP_DIM = 64
GM_DIM = GM_GROUPS * GM_GROUP_DIM
GM_CHUNK = 128
GRID_W = 64
ROPE_THETA = 10000.0
EPS = 1e-6

LANES = 128
SUBLANES = 8
VMEM_LIMIT_BYTES = 56 * 1024 * 1024

TM_IN = 256
TQ = 256
TK = 256
QK_SCALE_LOG2 = math.log2(math.e) / math.sqrt(HEAD_DIM)
MAX_FIXED_SHIFT = 48.0
CHUNKS_PER_BODY = 16
TM_OUT = 256
CAST_ROWS = 16
FF_CHUNK = 768

F32 = jnp.float32
BF16 = jnp.bfloat16


def _dot(a, b):
    return jnp.dot(a, b, preferred_element_type=F32)


def _group_sum(x, gmat2):
    hi = x.astype(BF16)
    lo = (x - hi.astype(F32)).astype(BF16)
    return _dot(jnp.concatenate([hi, lo], axis=1), gmat2)


def _sigmoid(x):
    return 0.5 * jnp.tanh(0.5 * x) + 0.5


def _rope_partner(x, lane):
    fwd = pltpu.roll(x, LANES - 16, axis=1)
    bwd = pltpu.roll(x, 16, axis=1)
    return jnp.where((lane & 16) == 0, fwd, bwd)


def _inproj_kernel(x_ref, g_ref, wq_ref, w_ref, gmat_ref, cos_ref, sin_ref, qg_ref, kg_ref,
                   lng_ref, lnb_ref, *refs, cast_steps):
    n_cast = len(cast_steps)
    cast_in = refs[:n_cast]
    qt_ref, k_ref, vt_ref, u_ref, vn_ref, gate_ref = refs[n_cast:n_cast + 6]
    cast_out = refs[n_cast + 6:2 * n_cast + 6]
    ya_ref, yb_ref = refs[2 * n_cast + 6:]
    i = pl.program_id(0)
    tm = x_ref.shape[0]

    @pl.when(i == 0)
    def _():
        yb_ref[...] = jnp.zeros(yb_ref.shape, F32)

    for src, dst, n_steps in zip(cast_in, cast_out, cast_steps):
        @pl.when(i < n_steps)
        def _(src=src, dst=dst):
            dst[...] = src[...].astype(dst.dtype)

    def compute(y_w, y_r):
        x = x_ref[...]
        ms = jnp.mean(x * x, axis=-1, keepdims=True)
        h = (x * lax.rsqrt(ms + EPS) * g_ref[...]).astype(BF16)
        gmat = gmat_ref[...]
        lane = lax.broadcasted_iota(jnp.int32, (tm, LANES), 1)

        def project(c0, c1):
            y_w[:, c0:c1] = _dot(h, w_ref[:, c0:c1])

        y_w[:, :Q_DIM] = _dot(h, wq_ref[...])

        v0 = Q_DIM + KV_DIM
        uv0 = Q_DIM + 2 * KV_DIM
        g0 = uv0 + 2 * GM_DIM
        g1 = g0 + (w_ref.shape[1] - g0) // 2

        n_qk = (Q_DIM + KV_DIM) // LANES
        yqk = [y_r[:, j * LANES:(j + 1) * LANES] for j in range(n_qk)]
        project(Q_DIM, uv0)
        n_gate = w_ref.shape[1] - g0
        gate_ref[:, :n_gate // 2] = _sigmoid(y_r[:, g0:g1])
        ss = [_group_sum(y * y, gmat) for y in yqk]
        project(uv0, g0)

        def norm_rope(y, s2, gain):
            y = y * lax.rsqrt(s2 * (1.0 / HEAD_DIM) + EPS) * gain
            return y * cos_ref[...] + _rope_partner(y, lane) * sin_ref[...]

        row = lax.broadcasted_iota(jnp.int32, (LANES, tm), 0)
        top = row < HEAD_DIM
        half_heads = N_Q_HEADS // 2
        for j in range(Q_DIM // LANES):
            qt = (norm_rope(yqk[j], ss[j], qg_ref[...]) * QK_SCALE_LOG2).T
            zero = jnp.zeros_like(qt)
            qt_ref[j] = jnp.where(top, qt, zero).astype(BF16)
            qt_ref[j + half_heads] = jnp.where(top, zero, qt).astype(BF16)
        k_ref[...] = norm_rope(yqk[n_qk - 1], ss[n_qk - 1], kg_ref[...]).astype(BF16)
        vt = y_r[:, v0:uv0].T.astype(BF16)
        for c in range(tm // TK):
            vt_ref[c] = vt[:, c * TK:(c + 1) * TK]

        gate_ref[:, n_gate // 2:] = _sigmoid(y_r[:, g1:])
        u_ref[...] = jax.nn.gelu(y_r[:, uv0:uv0 + GM_DIM])
        n_gm = GM_DIM // LANES
        vv = [jax.nn.gelu(y_r[:, uv0 + GM_DIM + j * LANES:uv0 + GM_DIM + (j + 1) * LANES])
              for j in range(n_gm)]
        project(g0, g1)
        mu = [_group_sum(v, gmat) * (1.0 / GM_GROUP_DIM) for v in vv]
        gq = g1 + (w_ref.shape[1] - g1) // 2
        project(g1, gq)
        dv = [v - m for v, m in zip(vv, mu)]
        var = [_group_sum(d * d, gmat) * (1.0 / GM_GROUP_DIM) for d in dv]
        project(gq, w_ref.shape[1])
        for j in range(n_gm):
            sl = slice(j * LANES, (j + 1) * LANES)
            vn = dv[j] * lax.rsqrt(var[j] + EPS) * lng_ref[:, sl] + lnb_ref[:, sl]
            vn_ref[:, sl] = vn.astype(BF16)

    @pl.when(i % 2 == 0)
    def _():
        compute(ya_ref, yb_ref)

    @pl.when(i % 2 == 1)
    def _():
        compute(yb_ref, ya_ref)


def _sublane_partial_sum(e):
    rows, lanes = e.shape
    return jnp.sum(e.reshape(rows // SUBLANES, SUBLANES, lanes), axis=0)


def _attn_kernel(shift_ref, qt_ref, k_ref, vt_ref, o_ref, sa_ref, sb_ref, m_ref, l_ref, acc_ref,
                 *, fixed_shift):
    n_heads = qt_ref.shape[0]
    group = n_heads // N_KV_HEADS
    n_chunks = k_ref.shape[0] // TK
    if fixed_shift:
        shift = shift_ref[0]
    else:
        m_ref[...] = jnp.full(m_ref.shape, -jnp.inf, F32)
    acc_ref[...] = jnp.zeros(acc_ref.shape, F32)
    l_ref[...] = jnp.zeros(l_ref.shape, F32)

    def keys(c):
        return k_ref[pl.ds(pl.multiple_of(c * TK, TK), TK), :]

    def step(c, cur_ref, next_ref):
        if next_ref is not None:
            k_next = keys(c + 1)
        vt_c = vt_ref[c]
        for h in range(n_heads):
            g = h // group
            if next_ref is not None:
                next_ref[h] = _dot(k_next, qt_ref[h])
            s = cur_ref[h]
            vt_g = vt_c[g * HEAD_DIM:(g + 1) * HEAD_DIM, :]
            if fixed_shift:
                e = jnp.exp2(s - shift)
                l_ref[h] += _sublane_partial_sum(e)
                acc_ref[h] += _dot(vt_g, e.astype(BF16))
            else:
                m_old = m_ref[h]
                m_new = jnp.maximum(m_old, jnp.max(s, axis=0, keepdims=True))
                alpha = jnp.exp2(m_old - m_new)
                e = jnp.exp2(s - m_new)
                m_ref[h] = m_new
                l_ref[h] = alpha * l_ref[h] + _sublane_partial_sum(e)
                acc_ref[h] = alpha * acc_ref[h] + _dot(vt_g, e.astype(BF16))

    k_first = keys(0)
    for h in range(n_heads):
        sa_ref[h] = _dot(k_first, qt_ref[h])

    bufs = (sa_ref, sb_ref)

    def steps(c0, last):
        for j in range(CHUNKS_PER_BODY):
            final = last and j == CHUNKS_PER_BODY - 1
            step(c0 + j, bufs[j % 2], None if final else bufs[(j + 1) % 2])

    def body(i, carry):
        steps(CHUNKS_PER_BODY * i, False)
        return carry

    lax.fori_loop(0, n_chunks // CHUNKS_PER_BODY - 1, body, 0)
    steps(n_chunks - CHUNKS_PER_BODY, True)
    for h in range(n_heads):
        o = acc_ref[h] / jnp.sum(l_ref[h], axis=0, keepdims=True)
        o_ref[h * HEAD_DIM:(h + 1) * HEAD_DIM, :] = o.astype(BF16)


def _merge_ffn_kernel(x_ref, at_ref, u_ref, vn_ref, gate_ref, ws_ref, bs_ref,
                      wpa_ref, wpb_ref, wo_ref, gf_ref, wgu_ref, wd_ref, gfin_ref,
                      o_ref, gm_ref, act_ref, xa_ref, xb_ref):
    i = pl.program_id(0)
    tm = x_ref.shape[0]
    d_model = x_ref.shape[1]
    d_ff = wd_ref.shape[0]

    @pl.when(i == 0)
    def _():
        xb_ref[...] = jnp.zeros(xb_ref.shape, F32)

    def compute(x1_w, x1_r):
        x1p = x1_r[...]
        ms = jnp.mean(x1p * x1p, axis=-1, keepdims=True)
        h = (x1p * lax.rsqrt(ms + EPS) * gf_ref[...]).astype(BF16)

        lane = lax.broadcasted_iota(jnp.int32, (GM_CHUNK, LANES), 1)
        low = lane < GM_GROUP_DIM
        for c in range(tm // GM_CHUNK):
            rs = slice(c * GM_CHUNK, (c + 1) * GM_CHUNK)
            for j in range(GM_DIM // LANES):
                cs = slice(j * LANES, (j + 1) * LANES)
                vblk = vn_ref[rs, cs]
                zero = jnp.zeros_like(vblk)
                vsplit = jnp.concatenate([jnp.where(low, vblk, zero), jnp.where(low, zero, vblk)],
                                         axis=0)
                sv = _dot(ws_ref[j], vsplit)
                gm_ref[rs, cs] = (u_ref[rs, cs] * (sv + bs_ref[:, cs])).astype(BF16)

        pa = lax.dot_general(at_ref[...], wpa_ref[...], (((0,), (0,)), ((), ())),
                             preferred_element_type=F32)
        pb = _dot(gm_ref[...], wpb_ref[...])
        mix = (gate_ref[:, :d_model] * pa + gate_ref[:, d_model:] * pb).astype(BF16)

        for c0 in range(0, d_ff, FF_CHUNK):
            c1 = min(c0 + FF_CHUNK, d_ff)
            gt = _dot(h, wgu_ref[:, c0:c1])
            up = _dot(h, wgu_ref[:, d_ff + c0:d_ff + c1])
            act_ref[:, c0:c1] = (jax.nn.silu(gt) * up).astype(BF16)
        x2 = x1p + _dot(act_ref[...], wd_ref[...])
        ms2 = jnp.mean(x2 * x2, axis=-1, keepdims=True)
        o_ref[...] = x2 * lax.rsqrt(ms2 + EPS) * gfin_ref[...]

        x1_w[...] = x_ref[...] + _dot(mix, wo_ref[...])

    @pl.when(i % 2 == 0)
    def _():
        compute(xa_ref, xb_ref)

    @pl.when(i % 2 == 1)
    def _():
        compute(xb_ref, xa_ref)


def _const_spec(shape):
    nd = len(shape)
    return pl.BlockSpec(shape, lambda *_: (0,) * nd, pipeline_mode=pl.Buffered(1))


def _rope_tables(seq):
    half = HEAD_DIM // 2
    rows = seq // GRID_W
    row = np.repeat(np.arange(rows, dtype=np.float64), GRID_W)
    col = np.tile(np.arange(GRID_W, dtype=np.float64), rows)
    inv_freq = 1.0 / (ROPE_THETA ** (np.arange(0, half, 2, dtype=np.float64) / half))
    ang_r = row[:, None] * inv_freq[None, :]
    ang_c = col[:, None] * inv_freq[None, :]
    cr, sr = np.cos(ang_r), np.sin(ang_r)
    cc, sc = np.cos(ang_c), np.sin(ang_c)
    cos64 = np.concatenate([cr, cr, cc, cc], axis=-1)
    sin64 = np.concatenate([-sr, sr, -sc, sc], axis=-1)
    reps = (1, LANES // HEAD_DIM)
    return np.tile(cos64, reps).astype(np.float32), np.tile(sin64, reps).astype(np.float32)


def kernel(x, norm_mix_g, w_in, q_norm_g, k_norm_g, gm_ln_g, gm_ln_b, w_s, b_s, w_proj_a,
           w_proj_b, w_out, norm_ffn_g, w_gate_up, w_down, norm_final_g):
    B, S, D = x.shape
    in_dim = w_in.shape[-1]
    d_ff = w_down.shape[1]
    assert norm_mix_g.shape[0] == 1, "single-layer block"
    assert CHUNKS_PER_BODY % 2 == 0 and S % (CHUNKS_PER_BODY * TK) == 0
    assert S % TQ == 0 and S % TM_IN == 0 and S % TM_OUT == 0
    assert TM_OUT % GM_CHUNK == 0 and TM_IN % TK == 0 and TM_IN == TQ and TM_OUT == TQ

    w_in_b = w_in[0].astype(BF16)
    half_heads = N_Q_HEADS // 2
    w_q_b = (w_in_b[:, :Q_DIM].reshape(D, 2, half_heads, HEAD_DIM)
             .transpose(0, 2, 1, 3).reshape(D, Q_DIM))

    cos_t, sin_t = _rope_tables(S)
    reps = LANES // HEAD_DIM
    qg = jnp.tile(q_norm_g[0], reps)[None, :]
    kg = jnp.tile(k_norm_g[0], reps)[None, :]
    lng = gm_ln_g[0].reshape(1, GM_DIM)
    lnb = gm_ln_b[0].reshape(1, GM_DIM)
    gid = np.arange(LANES) // HEAD_DIM
    gmat = (gid[:, None] == gid[None, :]).astype(np.float32)
    gmat2 = jnp.asarray(np.concatenate([gmat, gmat], axis=0), dtype=BF16)

    two_axes = ("arbitrary", "arbitrary")
    cparams = pltpu.CompilerParams(dimension_semantics=two_axes,
                                   vmem_limit_bytes=VMEM_LIMIT_BYTES)

    M = B * S
    n_seq = S // TM_IN
    n_blk = M // TM_IN
    ck_in = TM_IN // TK

    def in_blk(i):
        return jnp.minimum(i, n_blk - 1)

    def out_blk(i):
        return jnp.maximum(i - 1, 0)

    def out_rows(width):
        return pl.BlockSpec((TM_IN, width), lambda i: (out_blk(i), 0))

    to_cast = [(w_gate_up, n_blk), (w_out, n_blk), (w_down, d_ff // (CAST_ROWS * CAST_ROWS)),
               (w_proj_a, w_proj_a.shape[1] // CAST_ROWS), (w_proj_b, w_proj_b.shape[1] // CAST_ROWS)]
    cast_steps = tuple(n for _, n in to_cast)
    assert all(n <= n_blk and w.shape[1] % n == 0 for w, n in to_cast)
    cast_specs = [pl.BlockSpec((None, w.shape[1] // n, w.shape[2]),
                               lambda i, n=n: (0, jnp.minimum(i, n - 1), 0)) for w, n in to_cast]

    qt, k, vt, u, vn, gates, wgu_b, wo_b, wd_b, wpa_b, wpb_b = pl.pallas_call(
        functools.partial(_inproj_kernel, cast_steps=cast_steps),
        grid=(n_blk + 1,),
        in_specs=[
            pl.BlockSpec((TM_IN, D), lambda i: (in_blk(i), 0)),
            _const_spec((1, D)),
            _const_spec((D, Q_DIM)),
            _const_spec((D, in_dim)),
            _const_spec((2 * LANES, LANES)),
            pl.BlockSpec((TM_IN, LANES), lambda i: (out_blk(i) % n_seq, 0)),
            pl.BlockSpec((TM_IN, LANES), lambda i: (out_blk(i) % n_seq, 0)),
            _const_spec((1, LANES)),
            _const_spec((1, LANES)),
            _const_spec((1, GM_DIM)),
            _const_spec((1, GM_DIM)),
        ] + cast_specs,
        out_specs=[
            pl.BlockSpec((None, N_Q_HEADS, LANES, TM_IN), lambda i: (out_blk(i), 0, 0, 0)),
            out_rows(KV_DIM),
            pl.BlockSpec((ck_in, KV_DIM, TK), lambda i: (out_blk(i), 0, 0)),
            out_rows(GM_DIM), out_rows(GM_DIM), out_rows(2 * D),
        ] + cast_specs,
        out_shape=[
            jax.ShapeDtypeStruct((n_blk, N_Q_HEADS, LANES, TM_IN), BF16),
            jax.ShapeDtypeStruct((M, KV_DIM), BF16),
            jax.ShapeDtypeStruct((M // TK, KV_DIM, TK), BF16),
            jax.ShapeDtypeStruct((M, GM_DIM), F32),
            jax.ShapeDtypeStruct((M, GM_DIM), BF16),
            jax.ShapeDtypeStruct((M, 2 * D), F32),
        ] + [jax.ShapeDtypeStruct(w.shape, BF16) for w, _ in to_cast],
        scratch_shapes=[pltpu.VMEM((TM_IN, in_dim), F32), pltpu.VMEM((TM_IN, in_dim), F32)],
        compiler_params=pltpu.CompilerParams(dimension_semantics=("arbitrary",),
                                             vmem_limit_bytes=VMEM_LIMIT_BYTES),
        name="inproj",
    )(x.reshape(M, D), norm_mix_g, w_q_b, w_in_b, gmat2, cos_t, sin_t, qg, kg, lng, lnb,
      *[w for w, _ in to_cast])
    k = k.reshape(B, S, KV_DIM)
    vt = vt.reshape(B, S // TK, KV_DIM, TK)

    n_q = S // TQ

    def attention(fixed_shift):
        return pl.pallas_call(
            functools.partial(_attn_kernel, fixed_shift=fixed_shift),
            grid=(B, n_q),
            in_specs=[
                pl.BlockSpec(memory_space=pltpu.SMEM),
                pl.BlockSpec((None, N_Q_HEADS, LANES, TQ), lambda b, i: (b * n_q + i, 0, 0, 0)),
                pl.BlockSpec((None, S, KV_DIM), lambda b, i: (b, 0, 0)),
                pl.BlockSpec((None, S // TK, KV_DIM, TK), lambda b, i: (b, 0, 0, 0)),
            ],
            out_specs=pl.BlockSpec((None, Q_DIM, TQ), lambda b, i: (b * n_q + i, 0, 0)),
            out_shape=jax.ShapeDtypeStruct((B * n_q, Q_DIM, TQ), BF16),
            scratch_shapes=[
                pltpu.VMEM((N_Q_HEADS, TK, TQ), F32),
                pltpu.VMEM((N_Q_HEADS, TK, TQ), F32),
                pltpu.VMEM((N_Q_HEADS, 1, TQ), F32),
                pltpu.VMEM((N_Q_HEADS, SUBLANES, TQ), F32),
                pltpu.VMEM((N_Q_HEADS, HEAD_DIM, TQ), F32),
            ],
            compiler_params=cparams,
            name="gqa_attention_shift" if fixed_shift else "gqa_attention_online",
        )

    score_bound = (HEAD_DIM * QK_SCALE_LOG2 * (1.0 + 2.0 ** -6)
                   * jnp.max(jnp.abs(q_norm_g[0])) * jnp.max(jnp.abs(k_norm_g[0])))
    attn_t = lax.cond(
        score_bound <= MAX_FIXED_SHIFT,
        lambda s_, *ops: attention(True)(s_, *ops),
        lambda s_, *ops: attention(False)(s_, *ops),
        score_bound.reshape(1).astype(F32), qt, k, vt)

    bs_full = jnp.repeat(jnp.transpose(b_s[0]), GM_GROUP_DIM, axis=1)
    ws_pairs = (w_s[0].reshape(GM_GROUPS // 2, 2, GM_CHUNK, GM_CHUNK)
                .transpose(0, 2, 1, 3).reshape(GM_GROUPS // 2, GM_CHUNK, 2 * GM_CHUNK).astype(BF16))
    n_out = M // TM_OUT

    def s1_blk(i):
        return jnp.minimum(i, n_out - 1)

    def s1_rows(width):
        return pl.BlockSpec((TM_OUT, width), lambda i: (s1_blk(i), 0))

    out = pl.pallas_call(
        _merge_ffn_kernel,
        grid=(n_out + 1,),
        in_specs=[
            s1_rows(D),
            pl.BlockSpec((None, Q_DIM, TM_OUT), lambda i: (s1_blk(i), 0, 0)),
            s1_rows(GM_DIM), s1_rows(GM_DIM), s1_rows(2 * D),
            _const_spec((GM_GROUPS // 2, GM_CHUNK, 2 * GM_CHUNK)),
            _const_spec((GM_CHUNK, GM_DIM)),
            _const_spec((None, Q_DIM, D)),
            _const_spec((None, GM_DIM, D)),
            _const_spec((None, D, D)),
            _const_spec((1, D)),
            _const_spec((None, D, 2 * d_ff)),
            _const_spec((None, d_ff, D)),
            _const_spec((1, D)),
        ],
        out_specs=pl.BlockSpec((TM_OUT, D), lambda i: (jnp.maximum(i - 1, 0), 0)),
        out_shape=jax.ShapeDtypeStruct((M, D), F32),
        scratch_shapes=[
            pltpu.VMEM((TM_OUT, GM_DIM), BF16),
            pltpu.VMEM((TM_OUT, d_ff), BF16),
            pltpu.VMEM((TM_OUT, D), F32),
            pltpu.VMEM((TM_OUT, D), F32),
        ],
        compiler_params=pltpu.CompilerParams(dimension_semantics=("arbitrary",),
                                             vmem_limit_bytes=VMEM_LIMIT_BYTES),
        name="merge_ffn",
    )(x.reshape(M, D), attn_t, u, vn, gates, ws_pairs, bs_full,
      wpa_b, wpb_b, wo_b, norm_ffn_g, wgu_b, wd_b, norm_final_g[None, :])
    return out.reshape(B, S, D)
```

```python
import functools
import math

import jax
import jax.numpy as jnp
import numpy as np
from jax import lax
from jax.experimental import pallas as pl
from jax.experimental.pallas import tpu as pltpu

HEAD_DIM = 64
N_Q_HEADS = 8
N_KV_HEADS = 2
Q_DIM = N_Q_HEADS * HEAD_DIM
KV_DIM = N_KV_HEADS * HEAD_DIM
GM_GROUPS = 8
GM_GROUP_DIM = 64
GM_DIM = GM_GROUPS * GM_GROUP_DIM
GM_CHUNK = 128
GRID_W = 64
ROPE_THETA = 10000.0
EPS = 1e-6

LANES = 128
SUBLANES = 8
VMEM_LIMIT_BYTES = 56 * 1024 * 1024

TM_IN = 256
TQ = 256
TK = 256
QK_SCALE_LOG2 = math.log2(math.e) / math.sqrt(HEAD_DIM)
MAX_FIXED_SHIFT = 48.0
CHUNKS_PER_BODY = 16
TM_OUT = 256
CAST_ROWS = 16
DOWN_CAST_ROWS = 256
FF_CHUNK = 768

F32 = jnp.float32
BF16 = jnp.bfloat16


def _dot(a, b):
    return jnp.dot(a, b, preferred_element_type=F32)


def _group_sum(x, gmat2):
    hi = x.astype(BF16)
    lo = (x - hi.astype(F32)).astype(BF16)
    return _dot(jnp.concatenate([hi, lo], axis=1), gmat2)


def _rope_partner(x, lane):
    fwd = pltpu.roll(x, LANES - 16, axis=1)
    bwd = pltpu.roll(x, 16, axis=1)
    return jnp.where((lane & 16) == 0, fwd, bwd)


def _inproj_kernel(x_ref, g_ref, wq_ref, w_ref, gmat_ref, cos_ref, sin_ref, qg_ref, kg_ref,
                   lng_ref, lnb_ref, *refs, cast_steps):
    n_cast = len(cast_steps)
    cast_in = refs[:n_cast]
    qt_ref, k_ref, vt_ref, u_ref, vn_ref, gate_ref = refs[n_cast:n_cast + 6]
    cast_out = refs[n_cast + 6:2 * n_cast + 6]
    ya_ref, yb_ref = refs[2 * n_cast + 6:]
    i = pl.program_id(0)
    tm = x_ref.shape[0]

    @pl.when(i == 0)
    def _():
        yb_ref[...] = jnp.zeros(yb_ref.shape, F32)

    for src, dst, n_steps in zip(cast_in, cast_out, cast_steps):
        @pl.when(i < n_steps)
        def _(src=src, dst=dst):
            dst[...] = src[...].astype(dst.dtype)

    def compute(y_w, y_r):
        x = x_ref[...]
        ms = jnp.mean(x * x, axis=-1, keepdims=True)
        h = (x * lax.rsqrt(ms + EPS) * g_ref[...]).astype(BF16)
        gmat = gmat_ref[...]
        lane = lax.broadcasted_iota(jnp.int32, (tm, LANES), 1)

        def project(c0, c1):
            y_w[:, c0:c1] = _dot(h, w_ref[:, c0:c1])

        y_w[:, :Q_DIM] = _dot(h, wq_ref[...])

        v0 = Q_DIM + KV_DIM
        uv0 = Q_DIM + 2 * KV_DIM
        g0 = uv0 + 2 * GM_DIM
        g1 = g0 + (w_ref.shape[1] - g0) // 2

        n_qk = (Q_DIM + KV_DIM) // LANES
        yqk = [y_r[:, j * LANES:(j + 1) * LANES] for j in range(n_qk)]
        project(Q_DIM, uv0)
        n_gate = w_ref.shape[1] - g0
        gate_ref[:, :n_gate // 2] = jax.nn.sigmoid(y_r[:, g0:g1])
        ss = [_group_sum(y * y, gmat) for y in yqk]
        project(uv0, g0)

        def norm_rope(y, s2, gain):
            y = y * lax.rsqrt(s2 * (1.0 / HEAD_DIM) + EPS) * gain
            return y * cos_ref[...] + _rope_partner(y, lane) * sin_ref[...]

        row = lax.broadcasted_iota(jnp.int32, (LANES, tm), 0)
        top = row < HEAD_DIM
        half_heads = N_Q_HEADS // 2
        for j in range(Q_DIM // LANES):
            qt = (norm_rope(yqk[j], ss[j], qg_ref[...]) * QK_SCALE_LOG2).T
            zero = jnp.zeros_like(qt)
            qt_ref[j] = jnp.where(top, qt, zero).astype(BF16)
            qt_ref[j + half_heads] = jnp.where(top, zero, qt).astype(BF16)
        k_ref[...] = norm_rope(yqk[n_qk - 1], ss[n_qk - 1], kg_ref[...]).astype(BF16)
        vt = y_r[:, v0:uv0].T.astype(BF16)
        for c in range(tm // TK):
            vt_ref[c] = vt[:, c * TK:(c + 1) * TK]

        gate_ref[:, n_gate // 2:] = jax.nn.sigmoid(y_r[:, g1:])
        u_ref[...] = jax.nn.gelu(y_r[:, uv0:uv0 + GM_DIM])
        n_gm = GM_DIM // LANES
        vv = [jax.nn.gelu(y_r[:, uv0 + GM_DIM + j * LANES:uv0 + GM_DIM + (j + 1) * LANES])
              for j in range(n_gm)]
        project(g0, g1)
        mu = [_group_sum(v, gmat) * (1.0 / GM_GROUP_DIM) for v in vv]
        gq = g1 + (w_ref.shape[1] - g1) // 2
        project(g1, gq)
        dv = [v - m for v, m in zip(vv, mu)]
        var = [_group_sum(d * d, gmat) * (1.0 / GM_GROUP_DIM) for d in dv]
        project(gq, w_ref.shape[1])
        for j in range(n_gm):
            sl = slice(j * LANES, (j + 1) * LANES)
            vn = dv[j] * lax.rsqrt(var[j] + EPS) * lng_ref[:, sl] + lnb_ref[:, sl]
            vn_ref[:, sl] = vn.astype(BF16)

    @pl.when(i % 2 == 0)
    def _():
        compute(ya_ref, yb_ref)

    @pl.when(i % 2 == 1)
    def _():
        compute(yb_ref, ya_ref)


def _sublane_partial_sum(e):
    rows, lanes = e.shape
    return jnp.sum(e.reshape(rows // SUBLANES, SUBLANES, lanes), axis=0)


def _attn_kernel(shift_ref, qt_ref, k_ref, vt_ref, o_ref, sa_ref, sb_ref, m_ref, l_ref, acc_ref,
                 *, fixed_shift):
    n_heads = qt_ref.shape[0]
    group = n_heads // N_KV_HEADS
    n_chunks = k_ref.shape[0] // TK
    if fixed_shift:
        shift = shift_ref[0]
    else:
        m_ref[...] = jnp.full(m_ref.shape, -jnp.inf, F32)
    acc_ref[...] = jnp.zeros(acc_ref.shape, F32)
    l_ref[...] = jnp.zeros(l_ref.shape, F32)

    def keys(c):
        return k_ref[pl.ds(pl.multiple_of(c * TK, TK), TK), :]

    def step(c, cur_ref, next_ref):
        if next_ref is not None:
            k_next = keys(c + 1)
        vt_c = vt_ref[c]
        for h in range(n_heads):
            g = h // group
            if next_ref is not None:
                next_ref[h] = _dot(k_next, qt_ref[h])
            s = cur_ref[h]
            vt_g = vt_c[g * HEAD_DIM:(g + 1) * HEAD_DIM, :]
            if fixed_shift:
                e = jnp.exp2(s - shift)
                l_ref[h] += _sublane_partial_sum(e)
                acc_ref[h] += _dot(vt_g, e.astype(BF16))
            else:
                m_old = m_ref[h]
                m_new = jnp.maximum(m_old, jnp.max(s, axis=0, keepdims=True))
                alpha = jnp.exp2(m_old - m_new)
                e = jnp.exp2(s - m_new)
                m_ref[h] = m_new
                l_ref[h] = alpha * l_ref[h] + _sublane_partial_sum(e)
                acc_ref[h] = alpha * acc_ref[h] + _dot(vt_g, e.astype(BF16))

    k_first = keys(0)
    for h in range(n_heads):
        sa_ref[h] = _dot(k_first, qt_ref[h])

    bufs = (sa_ref, sb_ref)

    def steps(c0, last):
        for j in range(CHUNKS_PER_BODY):
            final = last and j == CHUNKS_PER_BODY - 1
            step(c0 + j, bufs[j % 2], None if final else bufs[(j + 1) % 2])

    def body(i, carry):
        steps(CHUNKS_PER_BODY * i, False)
        return carry

    lax.fori_loop(0, n_chunks // CHUNKS_PER_BODY - 1, body, 0)
    steps(n_chunks - CHUNKS_PER_BODY, True)
    for h in range(n_heads):
        o = acc_ref[h] / jnp.sum(l_ref[h], axis=0, keepdims=True)
        o_ref[h * HEAD_DIM:(h + 1) * HEAD_DIM, :] = o.astype(BF16)


def _merge_ffn_kernel(x_ref, at_ref, u_ref, vn_ref, gate_ref, ws_ref, bs_ref,
                      wpa_ref, wpb_ref, wo_ref, gf_ref, wgu_ref, wd_ref, gfin_ref,
                      o_ref, gm_ref, act_ref, xa_ref, xb_ref):
    i = pl.program_id(0)
    tm = x_ref.shape[0]
    d_model = x_ref.shape[1]
    d_ff = wd_ref.shape[0]

    @pl.when(i == 0)
    def _():
        xb_ref[...] = jnp.zeros(xb_ref.shape, F32)

    def compute(x1_w, x1_r):
        x1p = x1_r[...]
        ms = jnp.mean(x1p * x1p, axis=-1, keepdims=True)
        h = (x1p * lax.rsqrt(ms + EPS) * gf_ref[...]).astype(BF16)

        lane = lax.broadcasted_iota(jnp.int32, (GM_CHUNK, LANES), 1)
        low = lane < GM_GROUP_DIM
        for c in range(tm // GM_CHUNK):
            rs = slice(c * GM_CHUNK, (c + 1) * GM_CHUNK)
            for j in range(GM_DIM // LANES):
                cs = slice(j * LANES, (j + 1) * LANES)
                vblk = vn_ref[rs, cs]
                zero = jnp.zeros_like(vblk)
                vsplit = jnp.concatenate([jnp.where(low, vblk, zero), jnp.where(low, zero, vblk)],
                                         axis=0)
                sv = _dot(ws_ref[j], vsplit)
                gm_ref[rs, cs] = (u_ref[rs, cs] * (sv + bs_ref[:, cs])).astype(BF16)

        pa = lax.dot_general(at_ref[...], wpa_ref[...], (((0,), (0,)), ((), ())),
                             preferred_element_type=F32)
        pb = _dot(gm_ref[...], wpb_ref[...])
        mix = (gate_ref[:, :d_model] * pa + gate_ref[:, d_model:] * pb).astype(BF16)

        for c0 in range(0, d_ff, FF_CHUNK):
            c1 = min(c0 + FF_CHUNK, d_ff)
            gt = _dot(h, wgu_ref[:, c0:c1])
            up = _dot(h, wgu_ref[:, d_ff + c0:d_ff + c1])
            act_ref[:, c0:c1] = (jax.nn.silu(gt) * up).astype(BF16)
        x2 = x1p + _dot(act_ref[...], wd_ref[...])
        ms2 = jnp.mean(x2 * x2, axis=-1, keepdims=True)
        o_ref[...] = x2 * lax.rsqrt(ms2 + EPS) * gfin_ref[...]

        x1_w[...] = x_ref[...] + _dot(mix, wo_ref[...])

    @pl.when(i % 2 == 0)
    def _():
        compute(xa_ref, xb_ref)

    @pl.when(i % 2 == 1)
    def _():
        compute(xb_ref, xa_ref)


def _const_spec(shape):
    nd = len(shape)
    return pl.BlockSpec(shape, lambda *_: (0,) * nd, pipeline_mode=pl.Buffered(1))


def _rope_tables(seq):
    half = HEAD_DIM // 2
    rows = seq // GRID_W
    row = np.repeat(np.arange(rows, dtype=np.float64), GRID_W)
    col = np.tile(np.arange(GRID_W, dtype=np.float64), rows)
    inv_freq = 1.0 / (ROPE_THETA ** (np.arange(0, half, 2, dtype=np.float64) / half))
    ang_r = row[:, None] * inv_freq[None, :]
    ang_c = col[:, None] * inv_freq[None, :]
    cr, sr = np.cos(ang_r), np.sin(ang_r)
    cc, sc = np.cos(ang_c), np.sin(ang_c)
    cos64 = np.concatenate([cr, cr, cc, cc], axis=-1)
    sin64 = np.concatenate([-sr, sr, -sc, sc], axis=-1)
    reps = (1, LANES // HEAD_DIM)
    return np.tile(cos64, reps).astype(np.float32), np.tile(sin64, reps).astype(np.float32)


def kernel(x, norm_mix_g, w_in, q_norm_g, k_norm_g, gm_ln_g, gm_ln_b, w_s, b_s, w_proj_a,
           w_proj_b, w_out, norm_ffn_g, w_gate_up, w_down, norm_final_g):
    B, S, D = x.shape
    in_dim = w_in.shape[-1]
    d_ff = w_down.shape[1]
    assert norm_mix_g.shape[0] == 1, "single-layer block"
    assert CHUNKS_PER_BODY % 2 == 0 and S % (CHUNKS_PER_BODY * TK) == 0
    assert S % TQ == 0 and S % TM_IN == 0 and S % TM_OUT == 0
    assert TM_OUT % GM_CHUNK == 0 and TM_IN % TK == 0 and TM_IN == TQ and TM_OUT == TQ

    w_in_b = w_in[0].astype(BF16)
    half_heads = N_Q_HEADS // 2
    w_q_b = (w_in_b[:, :Q_DIM].reshape(D, 2, half_heads, HEAD_DIM)
             .transpose(0, 2, 1, 3).reshape(D, Q_DIM))

    cos_t, sin_t = _rope_tables(S)
    reps = LANES // HEAD_DIM
    qg = jnp.tile(q_norm_g[0], reps)[None, :]
    kg = jnp.tile(k_norm_g[0], reps)[None, :]
    lng = gm_ln_g[0].reshape(1, GM_DIM)
    lnb = gm_ln_b[0].reshape(1, GM_DIM)
    gid = np.arange(LANES) // HEAD_DIM
    gmat = (gid[:, None] == gid[None, :]).astype(np.float32)
    gmat2 = jnp.asarray(np.concatenate([gmat, gmat], axis=0), dtype=BF16)

    two_axes = ("arbitrary", "arbitrary")
    cparams = pltpu.CompilerParams(dimension_semantics=two_axes,
                                   vmem_limit_bytes=VMEM_LIMIT_BYTES)

    M = B * S
    n_seq = S // TM_IN
    n_blk = M // TM_IN
    ck_in = TM_IN // TK

    def in_blk(i):
        return jnp.minimum(i, n_blk - 1)

    def out_blk(i):
        return jnp.maximum(i - 1, 0)

    def out_rows(width):
        return pl.BlockSpec((TM_IN, width), lambda i: (out_blk(i), 0))

    to_cast = [(w_gate_up, n_blk), (w_out, n_blk), (w_down, d_ff // DOWN_CAST_ROWS),
               (w_proj_a, w_proj_a.shape[1] // CAST_ROWS), (w_proj_b, w_proj_b.shape[1] // CAST_ROWS)]
    cast_steps = tuple(n for _, n in to_cast)
    assert all(n <= n_blk and w.shape[1] % n == 0 for w, n in to_cast)
    cast_specs = [pl.BlockSpec((None, w.shape[1] // n, w.shape[2]),
                               lambda i, n=n: (0, jnp.minimum(i, n - 1), 0)) for w, n in to_cast]

    qt, k, vt, u, vn, gates, wgu_b, wo_b, wd_b, wpa_b, wpb_b = pl.pallas_call(
        functools.partial(_inproj_kernel, cast_steps=cast_steps),
        grid=(n_blk + 1,),
        in_specs=[
            pl.BlockSpec((TM_IN, D), lambda i: (in_blk(i), 0)),
            _const_spec((1, D)),
            _const_spec((D, Q_DIM)),
            _const_spec((D, in_dim)),
            _const_spec((2 * LANES, LANES)),
            pl.BlockSpec((TM_IN, LANES), lambda i: (out_blk(i) % n_seq, 0)),
            pl.BlockSpec((TM_IN, LANES), lambda i: (out_blk(i) % n_seq, 0)),
            _const_spec((1, LANES)),
            _const_spec((1, LANES)),
            _const_spec((1, GM_DIM)),
            _const_spec((1, GM_DIM)),
        ] + cast_specs,
        out_specs=[
            pl.BlockSpec((None, N_Q_HEADS, LANES, TM_IN), lambda i: (out_blk(i), 0, 0, 0)),
            out_rows(KV_DIM),
            pl.BlockSpec((ck_in, KV_DIM, TK), lambda i: (out_blk(i), 0, 0)),
            out_rows(GM_DIM), out_rows(GM_DIM), out_rows(2 * D),
        ] + cast_specs,
        out_shape=[
            jax.ShapeDtypeStruct((n_blk, N_Q_HEADS, LANES, TM_IN), BF16),
            jax.ShapeDtypeStruct((M, KV_DIM), BF16),
            jax.ShapeDtypeStruct((M // TK, KV_DIM, TK), BF16),
            jax.ShapeDtypeStruct((M, GM_DIM), F32),
            jax.ShapeDtypeStruct((M, GM_DIM), BF16),
            jax.ShapeDtypeStruct((M, 2 * D), F32),
        ] + [jax.ShapeDtypeStruct(w.shape, BF16) for w, _ in to_cast],
        scratch_shapes=[pltpu.VMEM((TM_IN, in_dim), F32), pltpu.VMEM((TM_IN, in_dim), F32)],
        compiler_params=pltpu.CompilerParams(dimension_semantics=("arbitrary",),
                                             vmem_limit_bytes=VMEM_LIMIT_BYTES),
        name="inproj",
    )(x.reshape(M, D), norm_mix_g, w_q_b, w_in_b, gmat2, cos_t, sin_t, qg, kg, lng, lnb,
      *[w for w, _ in to_cast])
    k = k.reshape(B, S, KV_DIM)
    vt = vt.reshape(B, S // TK, KV_DIM, TK)

    n_q = S // TQ

    def attention(fixed_shift):
        return pl.pallas_call(
            functools.partial(_attn_kernel, fixed_shift=fixed_shift),
            grid=(B, n_q),
            in_specs=[
                pl.BlockSpec(memory_space=pltpu.SMEM),
                pl.BlockSpec((None, N_Q_HEADS, LANES, TQ), lambda b, i: (b * n_q + i, 0, 0, 0)),
                pl.BlockSpec((None, S, KV_DIM), lambda b, i: (b, 0, 0)),
                pl.BlockSpec((None, S // TK, KV_DIM, TK), lambda b, i: (b, 0, 0, 0)),
            ],
            out_specs=pl.BlockSpec((None, Q_DIM, TQ), lambda b, i: (b * n_q + i, 0, 0)),
            out_shape=jax.ShapeDtypeStruct((B * n_q, Q_DIM, TQ), BF16),
            scratch_shapes=[
                pltpu.VMEM((N_Q_HEADS, TK, TQ), F32),
                pltpu.VMEM((N_Q_HEADS, TK, TQ), F32),
                pltpu.VMEM((N_Q_HEADS, 1, TQ), F32),
                pltpu.VMEM((N_Q_HEADS, SUBLANES, TQ), F32),
                pltpu.VMEM((N_Q_HEADS, HEAD_DIM, TQ), F32),
            ],
            compiler_params=cparams,
            name="gqa_attention_shift" if fixed_shift else "gqa_attention_online",
        )

    score_bound = (HEAD_DIM * QK_SCALE_LOG2 * (1.0 + 2.0 ** -6)
                   * jnp.max(jnp.abs(q_norm_g[0])) * jnp.max(jnp.abs(k_norm_g[0])))
    attn_t = lax.cond(
        score_bound <= MAX_FIXED_SHIFT,
        lambda s_, *ops: attention(True)(s_, *ops),
        lambda s_, *ops: attention(False)(s_, *ops),
        score_bound.reshape(1).astype(F32), qt, k, vt)

    bs_full = jnp.repeat(jnp.transpose(b_s[0]), GM_GROUP_DIM, axis=1)
    ws_pairs = (w_s[0].reshape(GM_GROUPS // 2, 2, GM_CHUNK, GM_CHUNK)
                .transpose(0, 2, 1, 3).reshape(GM_GROUPS // 2, GM_CHUNK, 2 * GM_CHUNK).astype(BF16))
    n_out = M // TM_OUT

    def s1_blk(i):
        return jnp.minimum(i, n_out - 1)

    def s1_rows(width):
        return pl.BlockSpec((TM_OUT, width), lambda i: (s1_blk(i), 0))

    out = pl.pallas_call(
        _merge_ffn_kernel,
        grid=(n_out + 1,),
        in_specs=[
            s1_rows(D),
            pl.BlockSpec((None, Q_DIM, TM_OUT), lambda i: (s1_blk(i), 0, 0)),
            s1_rows(GM_DIM), s1_rows(GM_DIM), s1_rows(2 * D),
            _const_spec((GM_GROUPS // 2, GM_CHUNK, 2 * GM_CHUNK)),
            _const_spec((GM_CHUNK, GM_DIM)),
            _const_spec((None, Q_DIM, D)),
            _const_spec((None, GM_DIM, D)),
            _const_spec((None, D, D)),
            _const_spec((1, D)),
            _const_spec((None, D, 2 * d_ff)),
            _const_spec((None, d_ff, D)),
            _const_spec((1, D)),
        ],
        out_specs=pl.BlockSpec((TM_OUT, D), lambda i: (jnp.maximum(i - 1, 0), 0)),
        out_shape=jax.ShapeDtypeStruct((M, D), F32),
        scratch_shapes=[
            pltpu.VMEM((TM_OUT, GM_DIM), BF16),
            pltpu.VMEM((TM_OUT, d_ff), BF16),
            pltpu.VMEM((TM_OUT, D), F32),
            pltpu.VMEM((TM_OUT, D), F32),
        ],
        compiler_params=pltpu.CompilerParams(dimension_semantics=("arbitrary",),
                                             vmem_limit_bytes=VMEM_LIMIT_BYTES),
        name="merge_ffn",
    )(x.reshape(M, D), attn_t, u, vn, gates, ws_pairs, bs_full,
      wpa_b, wpb_b, wo_b, norm_ffn_g, wgu_b, wd_b, norm_final_g[None, :])
    return out.reshape(B, S, D)
```

```python
import functools
import math

import jax
import jax.numpy as jnp
import numpy as np
from jax import lax
from jax.experimental import pallas as pl
from jax.experimental.pallas import tpu as pltpu

HEAD_DIM = 64
N_Q_HEADS = 8
N_KV_HEADS = 2
Q_DIM = N_Q_HEADS * HEAD_DIM
KV_DIM = N_KV_HEADS * HEAD_DIM
GM_GROUPS = 8
GM_GROUP_DIM = 64
GM_DIM = GM_GROUPS * GM_GROUP_DIM
GM_CHUNK = 128
GRID_W = 64
ROPE_THETA = 10000.0
EPS = 1e-6

LANES = 128
SUBLANES = 8
VMEM_LIMIT_BYTES = 56 * 1024 * 1024

TM_IN = 256
TQ = 256
TK = 256
QK_SCALE_LOG2 = math.log2(math.e) / math.sqrt(HEAD_DIM)
MAX_FIXED_SHIFT = 48.0
CHUNKS_PER_BODY = 16
TM_OUT = 256
CAST_ROWS = 16
DOWN_CAST_ROWS = 256
FF_CHUNK = 768

F32 = jnp.float32
BF16 = jnp.bfloat16


def _dot(a, b):
    return jnp.dot(a, b, preferred_element_type=F32)


def _group_sum(x, gmat2):
    hi = x.astype(BF16)
    lo = (x - hi.astype(F32)).astype(BF16)
    return _dot(jnp.concatenate([hi, lo], axis=1), gmat2)


def _rope_partner(x, lane):
    fwd = pltpu.roll(x, LANES - 16, axis=1)
    bwd = pltpu.roll(x, 16, axis=1)
    return jnp.where((lane & 16) == 0, fwd, bwd)


def _inproj_kernel(x_ref, g_ref, wq_ref, w_ref, gmat_ref, cos_ref, sin_ref, qg_ref, kg_ref,
                   lng_ref, lnb_ref, *refs, cast_steps):
    n_cast = len(cast_steps)
    cast_in = refs[:n_cast]
    qt_ref, k_ref, vt_ref, u_ref, vn_ref, gate_ref = refs[n_cast:n_cast + 6]
    cast_out = refs[n_cast + 6:2 * n_cast + 6]
    ya_ref, yb_ref = refs[2 * n_cast + 6:]
    i = pl.program_id(0)
    tm = x_ref.shape[0]

    @pl.when(i == 0)
    def _():
        yb_ref[...] = jnp.zeros(yb_ref.shape, F32)

    for src, dst, n_steps in zip(cast_in, cast_out, cast_steps):
        @pl.when(i < n_steps)
        def _(src=src, dst=dst):
            dst[...] = src[...].astype(dst.dtype)

    def compute(y_w, y_r):
        x = x_ref[...]
        ms = jnp.mean(x * x, axis=-1, keepdims=True)
        h = (x * lax.rsqrt(ms + EPS) * g_ref[...]).astype(BF16)
        gmat = gmat_ref[...]
        lane = lax.broadcasted_iota(jnp.int32, (tm, LANES), 1)

        def project(c0, c1):
            y_w[:, c0:c1] = _dot(h, w_ref[:, c0:c1])

        y_w[:, :Q_DIM] = _dot(h, wq_ref[...])

        v0 = Q_DIM + KV_DIM
        uv0 = Q_DIM + 2 * KV_DIM
        g0 = uv0 + 2 * GM_DIM
        g1 = g0 + (w_ref.shape[1] - g0) // 2

        n_qk = (Q_DIM + KV_DIM) // LANES
        yqk = [y_r[:, j * LANES:(j + 1) * LANES] for j in range(n_qk)]
        project(Q_DIM, uv0)
        n_gate = w_ref.shape[1] - g0
        gate_ref[:, :n_gate // 2] = jax.nn.sigmoid(y_r[:, g0:g1])
        ss = [_group_sum(y * y, gmat) for y in yqk]
        project(uv0, g0)

        def norm_rope(y, s2, gain):
            y = y * lax.rsqrt(s2 * (1.0 / HEAD_DIM) + EPS) * gain
            return y * cos_ref[...] + _rope_partner(y, lane) * sin_ref[...]

        row = lax.broadcasted_iota(jnp.int32, (LANES, tm), 0)
        top = row < HEAD_DIM
        half_heads = N_Q_HEADS // 2
        for j in range(Q_DIM // LANES):
            qt = (norm_rope(yqk[j], ss[j], qg_ref[...]) * QK_SCALE_LOG2).T
            zero = jnp.zeros_like(qt)
            qt_ref[j] = jnp.where(top, qt, zero).astype(BF16)
            qt_ref[j + half_heads] = jnp.where(top, zero, qt).astype(BF16)
        k_ref[...] = norm_rope(yqk[n_qk - 1], ss[n_qk - 1], kg_ref[...]).astype(BF16)
        vt = y_r[:, v0:uv0].T.astype(BF16)
        for c in range(tm // TK):
            vt_ref[c] = vt[:, c * TK:(c + 1) * TK]

        gate_ref[:, n_gate // 2:] = jax.nn.sigmoid(y_r[:, g1:])
        u_ref[...] = jax.nn.gelu(y_r[:, uv0:uv0 + GM_DIM])
        n_gm = GM_DIM // LANES
        vv = [jax.nn.gelu(y_r[:, uv0 + GM_DIM + j * LANES:uv0 + GM_DIM + (j + 1) * LANES])
              for j in range(n_gm)]
        project(g0, g1)
        mu = [_group_sum(v, gmat) * (1.0 / GM_GROUP_DIM) for v in vv]
        gq = g1 + (w_ref.shape[1] - g1) // 2
        project(g1, gq)
        dv = [v - m for v, m in zip(vv, mu)]
        var = [_group_sum(d * d, gmat) * (1.0 / GM_GROUP_DIM) for d in dv]
        project(gq, w_ref.shape[1])
        for j in range(n_gm):
            sl = slice(j * LANES, (j + 1) * LANES)
            vn = dv[j] * lax.rsqrt(var[j] + EPS) * lng_ref[:, sl] + lnb_ref[:, sl]
            vn_ref[:, sl] = vn.astype(BF16)

    @pl.when(i % 2 == 0)
    def _():
        compute(ya_ref, yb_ref)

    @pl.when(i % 2 == 1)
    def _():
        compute(yb_ref, ya_ref)


def _sublane_partial_sum(e):
    rows, lanes = e.shape
    return jnp.sum(e.reshape(rows // SUBLANES, SUBLANES, lanes), axis=0)


def _attn_kernel(shift_ref, qt_ref, k_ref, vt_ref, o_ref, sa_ref, sb_ref, m_ref, l_ref, acc_ref,
                 *, fixed_shift):
    n_heads = qt_ref.shape[0]
    group = n_heads // N_KV_HEADS
    n_chunks = k_ref.shape[0] // TK
    if fixed_shift:
        shift = shift_ref[0]
    else:
        m_ref[...] = jnp.full(m_ref.shape, -jnp.inf, F32)
    acc_ref[...] = jnp.zeros(acc_ref.shape, F32)
    l_ref[...] = jnp.zeros(l_ref.shape, F32)

    def keys(c):
        return k_ref[pl.ds(pl.multiple_of(c * TK, TK), TK), :]

    def step(c, cur_ref, next_ref):
        if next_ref is not None:
            k_next = keys(c + 1)
        vt_c = vt_ref[c]

        def softmax_pv(h):
            g = h // group
            s = cur_ref[h]
            vt_g = vt_c[g * HEAD_DIM:(g + 1) * HEAD_DIM, :]
            if fixed_shift:
                e = jnp.exp2(s - shift)
                l_ref[h] += _sublane_partial_sum(e)
                acc_ref[h] += _dot(vt_g, e.astype(BF16))
            else:
                m_old = m_ref[h]
                m_new = jnp.maximum(m_old, jnp.max(s, axis=0, keepdims=True))
                alpha = jnp.exp2(m_old - m_new)
                e = jnp.exp2(s - m_new)
                m_ref[h] = m_new
                l_ref[h] = alpha * l_ref[h] + _sublane_partial_sum(e)
                acc_ref[h] = alpha * acc_ref[h] + _dot(vt_g, e.astype(BF16))

        for h in range(n_heads):
            if next_ref is not None:
                next_ref[h] = _dot(k_next, qt_ref[h])
            if h > 0:
                softmax_pv(h - 1)
        softmax_pv(n_heads - 1)

    k_first = keys(0)
    for h in range(n_heads):
        sa_ref[h] = _dot(k_first, qt_ref[h])

    bufs = (sa_ref, sb_ref)

    def steps(c0, last):
        for j in range(CHUNKS_PER_BODY):
            final = last and j == CHUNKS_PER_BODY - 1
            step(c0 + j, bufs[j % 2], None if final else bufs[(j + 1) % 2])

    def body(i, carry):
        steps(CHUNKS_PER_BODY * i, False)
        return carry

    lax.fori_loop(0, n_chunks // CHUNKS_PER_BODY - 1, body, 0)
    steps(n_chunks - CHUNKS_PER_BODY, True)
    for h in range(n_heads):
        o = acc_ref[h] / jnp.sum(l_ref[h], axis=0, keepdims=True)
        o_ref[h * HEAD_DIM:(h + 1) * HEAD_DIM, :] = o.astype(BF16)


def _merge_ffn_kernel(x_ref, at_ref, u_ref, vn_ref, gate_ref, ws_ref, bs_ref,
                      wpa_ref, wpb_ref, wo_ref, gf_ref, wgu_ref, wd_ref, gfin_ref,
                      o_ref, gm_ref, act_ref, xa_ref, xb_ref):
    i = pl.program_id(0)
    tm = x_ref.shape[0]
    d_model = x_ref.shape[1]
    d_ff = wd_ref.shape[0]

    @pl.when(i == 0)
    def _():
        xb_ref[...] = jnp.zeros(xb_ref.shape, F32)

    def compute(x1_w, x1_r):
        x1p = x1_r[...]
        ms = jnp.mean(x1p * x1p, axis=-1, keepdims=True)
        h = (x1p * lax.rsqrt(ms + EPS) * gf_ref[...]).astype(BF16)

        lane = lax.broadcasted_iota(jnp.int32, (GM_CHUNK, LANES), 1)
        low = lane < GM_GROUP_DIM
        for c in range(tm // GM_CHUNK):
            rs = slice(c * GM_CHUNK, (c + 1) * GM_CHUNK)
            for j in range(GM_DIM // LANES):
                cs = slice(j * LANES, (j + 1) * LANES)
                vblk = vn_ref[rs, cs]
                zero = jnp.zeros_like(vblk)
                vsplit = jnp.concatenate([jnp.where(low, vblk, zero), jnp.where(low, zero, vblk)],
                                         axis=0)
                sv = _dot(ws_ref[j], vsplit)
                gm_ref[rs, cs] = (u_ref[rs, cs] * (sv + bs_ref[:, cs])).astype(BF16)

        pa = lax.dot_general(at_ref[...], wpa_ref[...], (((0,), (0,)), ((), ())),
                             preferred_element_type=F32)
        pb = _dot(gm_ref[...], wpb_ref[...])
        mix = (gate_ref[:, :d_model] * pa + gate_ref[:, d_model:] * pb).astype(BF16)

        for c0 in range(0, d_ff, FF_CHUNK):
            c1 = min(c0 + FF_CHUNK, d_ff)
            gt = _dot(h, wgu_ref[:, c0:c1])
            up = _dot(h, wgu_ref[:, d_ff + c0:d_ff + c1])
            act_ref[:, c0:c1] = (jax.nn.silu(gt) * up).astype(BF16)
        x2 = x1p + _dot(act_ref[...], wd_ref[...])
        ms2 = jnp.mean(x2 * x2, axis=-1, keepdims=True)
        o_ref[...] = x2 * lax.rsqrt(ms2 + EPS) * gfin_ref[...]

        x1_w[...] = x_ref[...] + _dot(mix, wo_ref[...])

    @pl.when(i % 2 == 0)
    def _():
        compute(xa_ref, xb_ref)

    @pl.when(i % 2 == 1)
    def _():
        compute(xb_ref, xa_ref)


def _const_spec(shape):
    nd = len(shape)
    return pl.BlockSpec(shape, lambda *_: (0,) * nd, pipeline_mode=pl.Buffered(1))


def _rope_tables(seq):
    half = HEAD_DIM // 2
    rows = seq // GRID_W
    row = np.repeat(np.arange(rows, dtype=np.float64), GRID_W)
    col = np.tile(np.arange(GRID_W, dtype=np.float64), rows)
    inv_freq = 1.0 / (ROPE_THETA ** (np.arange(0, half, 2, dtype=np.float64) / half))
    ang_r = row[:, None] * inv_freq[None, :]
    ang_c = col[:, None] * inv_freq[None, :]
    cr, sr = np.cos(ang_r), np.sin(ang_r)
    cc, sc = np.cos(ang_c), np.sin(ang_c)
    cos64 = np.concatenate([cr, cr, cc, cc], axis=-1)
    sin64 = np.concatenate([-sr, sr, -sc, sc], axis=-1)
    reps = (1, LANES // HEAD_DIM)
    return np.tile(cos64, reps).astype(np.float32), np.tile(sin64, reps).astype(np.float32)


def kernel(x, norm_mix_g, w_in, q_norm_g, k_norm_g, gm_ln_g, gm_ln_b, w_s, b_s, w_proj_a,
           w_proj_b, w_out, norm_ffn_g, w_gate_up, w_down, norm_final_g):
    B, S, D = x.shape
    in_dim = w_in.shape[-1]
    d_ff = w_down.shape[1]
    assert norm_mix_g.shape[0] == 1, "single-layer block"
    assert CHUNKS_PER_BODY % 2 == 0 and S % (CHUNKS_PER_BODY * TK) == 0
    assert S % TQ == 0 and S % TM_IN == 0 and S % TM_OUT == 0
    assert TM_OUT % GM_CHUNK == 0 and TM_IN % TK == 0 and TM_IN == TQ and TM_OUT == TQ

    w_in_b = w_in[0].astype(BF16)
    half_heads = N_Q_HEADS // 2
    w_q_b = (w_in_b[:, :Q_DIM].reshape(D, 2, half_heads, HEAD_DIM)
             .transpose(0, 2, 1, 3).reshape(D, Q_DIM))

    cos_t, sin_t = _rope_tables(S)
    reps = LANES // HEAD_DIM
    qg = jnp.tile(q_norm_g[0], reps)[None, :]
    kg = jnp.tile(k_norm_g[0], reps)[None, :]
    lng = gm_ln_g[0].reshape(1, GM_DIM)
    lnb = gm_ln_b[0].reshape(1, GM_DIM)
    gid = np.arange(LANES) // HEAD_DIM
    gmat = (gid[:, None] == gid[None, :]).astype(np.float32)
    gmat2 = jnp.asarray(np.concatenate([gmat, gmat], axis=0), dtype=BF16)

    two_axes = ("arbitrary", "arbitrary")
    cparams = pltpu.CompilerParams(dimension_semantics=two_axes,
                                   vmem_limit_bytes=VMEM_LIMIT_BYTES)

    M = B * S
    n_seq = S // TM_IN
    n_blk = M // TM_IN
    ck_in = TM_IN // TK

    def in_blk(i):
        return jnp.minimum(i, n_blk - 1)

    def out_blk(i):
        return jnp.maximum(i - 1, 0)

    def out_rows(width):
        return pl.BlockSpec((TM_IN, width), lambda i: (out_blk(i), 0))

    to_cast = [(w_gate_up, n_blk), (w_out, n_blk), (w_down, d_ff // DOWN_CAST_ROWS),
               (w_proj_a, w_proj_a.shape[1] // CAST_ROWS), (w_proj_b, w_proj_b.shape[1] // CAST_ROWS)]
    cast_steps = tuple(n for _, n in to_cast)
    assert all(n <= n_blk and w.shape[1] % n == 0 for w, n in to_cast)
    cast_specs = [pl.BlockSpec((None, w.shape[1] // n, w.shape[2]),
                               lambda i, n=n: (0, jnp.minimum(i, n - 1), 0)) for w, n in to_cast]

    qt, k, vt, u, vn, gates, wgu_b, wo_b, wd_b, wpa_b, wpb_b = pl.pallas_call(
        functools.partial(_inproj_kernel, cast_steps=cast_steps),
        grid=(n_blk + 1,),
        in_specs=[
            pl.BlockSpec((TM_IN, D), lambda i: (in_blk(i), 0)),
            _const_spec((1, D)),
            _const_spec((D, Q_DIM)),
            _const_spec((D, in_dim)),
            _const_spec((2 * LANES, LANES)),
            pl.BlockSpec((TM_IN, LANES), lambda i: (out_blk(i) % n_seq, 0)),
            pl.BlockSpec((TM_IN, LANES), lambda i: (out_blk(i) % n_seq, 0)),
            _const_spec((1, LANES)),
            _const_spec((1, LANES)),
            _const_spec((1, GM_DIM)),
            _const_spec((1, GM_DIM)),
        ] + cast_specs,
        out_specs=[
            pl.BlockSpec((None, N_Q_HEADS, LANES, TM_IN), lambda i: (out_blk(i), 0, 0, 0)),
            out_rows(KV_DIM),
            pl.BlockSpec((ck_in, KV_DIM, TK), lambda i: (out_blk(i), 0, 0)),
            out_rows(GM_DIM), out_rows(GM_DIM), out_rows(2 * D),
        ] + cast_specs,
        out_shape=[
            jax.ShapeDtypeStruct((n_blk, N_Q_HEADS, LANES, TM_IN), BF16),
            jax.ShapeDtypeStruct((M, KV_DIM), BF16),
            jax.ShapeDtypeStruct((M // TK, KV_DIM, TK), BF16),
            jax.ShapeDtypeStruct((M, GM_DIM), F32),
            jax.ShapeDtypeStruct((M, GM_DIM), BF16),
            jax.ShapeDtypeStruct((M, 2 * D), F32),
        ] + [jax.ShapeDtypeStruct(w.shape, BF16) for w, _ in to_cast],
        scratch_shapes=[pltpu.VMEM((TM_IN, in_dim), F32), pltpu.VMEM((TM_IN, in_dim), F32)],
        compiler_params=pltpu.CompilerParams(dimension_semantics=("arbitrary",),
                                             vmem_limit_bytes=VMEM_LIMIT_BYTES),
        name="inproj",
    )(x.reshape(M, D), norm_mix_g, w_q_b, w_in_b, gmat2, cos_t, sin_t, qg, kg, lng, lnb,
      *[w for w, _ in to_cast])
    k = k.reshape(B, S, KV_DIM)
    vt = vt.reshape(B, S // TK, KV_DIM, TK)

    n_q = S // TQ

    def attention(fixed_shift):
        return pl.pallas_call(
            functools.partial(_attn_kernel, fixed_shift=fixed_shift),
            grid=(B, n_q),
            in_specs=[
                pl.BlockSpec(memory_space=pltpu.SMEM),
                pl.BlockSpec((None, N_Q_HEADS, LANES, TQ), lambda b, i: (b * n_q + i, 0, 0, 0)),
                pl.BlockSpec((None, S, KV_DIM), lambda b, i: (b, 0, 0)),
                pl.BlockSpec((None, S // TK, KV_DIM, TK), lambda b, i: (b, 0, 0, 0)),
            ],
            out_specs=pl.BlockSpec((None, Q_DIM, TQ), lambda b, i: (b * n_q + i, 0, 0)),
            out_shape=jax.ShapeDtypeStruct((B * n_q, Q_DIM, TQ), BF16),
            scratch_shapes=[
                pltpu.VMEM((N_Q_HEADS, TK, TQ), F32),
                pltpu.VMEM((N_Q_HEADS, TK, TQ), F32),
                pltpu.VMEM((N_Q_HEADS, 1, TQ), F32),
                pltpu.VMEM((N_Q_HEADS, SUBLANES, TQ), F32),
                pltpu.VMEM((N_Q_HEADS, HEAD_DIM, TQ), F32),
            ],
            compiler_params=cparams,
            name="gqa_attention_shift" if fixed_shift else "gqa_attention_online",
        )

    score_bound = (HEAD_DIM * QK_SCALE_LOG2 * (1.0 + 2.0 ** -6)
                   * jnp.max(jnp.abs(q_norm_g[0])) * jnp.max(jnp.abs(k_norm_g[0])))
    attn_t = lax.cond(
        score_bound <= MAX_FIXED_SHIFT,
        lambda s_, *ops: attention(True)(s_, *ops),
        lambda s_, *ops: attention(False)(s_, *ops),
        score_bound.reshape(1).astype(F32), qt, k, vt)

    bs_full = jnp.repeat(jnp.transpose(b_s[0]), GM_GROUP_DIM, axis=1)
    ws_pairs = (w_s[0].reshape(GM_GROUPS // 2, 2, GM_CHUNK, GM_CHUNK)
                .transpose(0, 2, 1, 3).reshape(GM_GROUPS // 2, GM_CHUNK, 2 * GM_CHUNK).astype(BF16))
    n_out = M // TM_OUT

    def s1_blk(i):
        return jnp.minimum(i, n_out - 1)

    def s1_rows(width):
        return pl.BlockSpec((TM_OUT, width), lambda i: (s1_blk(i), 0))

    out = pl.pallas_call(
        _merge_ffn_kernel,
        grid=(n_out + 1,),
        in_specs=[
            s1_rows(D),
            pl.BlockSpec((None, Q_DIM, TM_OUT), lambda i: (s1_blk(i), 0, 0)),
            s1_rows(GM_DIM), s1_rows(GM_DIM), s1_rows(2 * D),
            _const_spec((GM_GROUPS // 2, GM_CHUNK, 2 * GM_CHUNK)),
            _const_spec((GM_CHUNK, GM_DIM)),
            _const_spec((None, Q_DIM, D)),
            _const_spec((None, GM_DIM, D)),
            _const_spec((None, D, D)),
            _const_spec((1, D)),
            _const_spec((None, D, 2 * d_ff)),
            _const_spec((None, d_ff, D)),
            _const_spec((1, D)),
        ],
        out_specs=pl.BlockSpec((TM_OUT, D), lambda i: (jnp.maximum(i - 1, 0), 0)),
        out_shape=jax.ShapeDtypeStruct((M, D), F32),
        scratch_shapes=[
            pltpu.VMEM((TM_OUT, GM_DIM), BF16),
            pltpu.VMEM((TM_OUT, d_ff), BF16),
            pltpu.VMEM((TM_OUT, D), F32),
            pltpu.VMEM((TM_OUT, D), F32),
        ],
        compiler_params=pltpu.CompilerParams(dimension_semantics=("arbitrary",),
                                             vmem_limit_bytes=VMEM_LIMIT_BYTES),
        name="merge_ffn",
    )(x.reshape(M, D), attn_t, u, vn, gates, ws_pairs, bs_full,
      wpa_b, wpb_b, wo_b, norm_ffn_g, wgu_b, wd_b, norm_final_g[None, :])
    return out.reshape(B, S, D)
```

```python
import functools
import math

import jax
import jax.numpy as jnp
import numpy as np
from jax import lax
from jax.experimental import pallas as pl
from jax.experimental.pallas import tpu as pltpu

HEAD_DIM = 64
N_Q_HEADS = 8
N_KV_HEADS = 2
Q_DIM = N_Q_HEADS * HEAD_DIM
KV_DIM = N_KV_HEADS * HEAD_DIM
GM_GROUPS = 8
GM_GROUP_DIM = 64
GM_DIM = GM_GROUPS * GM_GROUP_DIM
GM_CHUNK = 128
GRID_W = 64
ROPE_THETA = 10000.0
EPS = 1e-6

LANES = 128
SUBLANES = 8
VMEM_LIMIT_BYTES = 56 * 1024 * 1024

TM_IN = 256
TQ = 256
TK = 256
QK_SCALE_LOG2 = math.log2(math.e) / math.sqrt(HEAD_DIM)
MAX_FIXED_SHIFT = 48.0
CHUNKS_PER_BODY = 16
TM_OUT = 256
CAST_ROWS = 16
DOWN_CAST_ROWS = 256
FF_CHUNK = 1408

F32 = jnp.float32
BF16 = jnp.bfloat16


def _dot(a, b):
    return jnp.dot(a, b, preferred_element_type=F32)


def _group_sum(x, gmat2):
    hi = x.astype(BF16)
    lo = (x - hi.astype(F32)).astype(BF16)
    return _dot(jnp.concatenate([hi, lo], axis=1), gmat2)


def _rope_partner(x, lane):
    fwd = pltpu.roll(x, LANES - 16, axis=1)
    bwd = pltpu.roll(x, 16, axis=1)
    return jnp.where((lane & 16) == 0, fwd, bwd)


def _inproj_kernel(x_ref, g_ref, wq_ref, w_ref, gmat_ref, cos_ref, sin_ref, qg_ref, kg_ref,
                   lng_ref, lnb_ref, *refs, cast_steps):
    n_cast = len(cast_steps)
    cast_in = refs[:n_cast]
    qt_ref, k_ref, vt_ref, u_ref, vn_ref, gate_ref = refs[n_cast:n_cast + 6]
    cast_out = refs[n_cast + 6:2 * n_cast + 6]
    ya_ref, yb_ref = refs[2 * n_cast + 6:]
    i = pl.program_id(0)
    tm = x_ref.shape[0]

    @pl.when(i == 0)
    def _():
        yb_ref[...] = jnp.zeros(yb_ref.shape, F32)

    for src, dst, n_steps in zip(cast_in, cast_out, cast_steps):
        @pl.when(i < n_steps)
        def _(src=src, dst=dst):
            dst[...] = src[...].astype(dst.dtype)

    def compute(y_w, y_r):
        x = x_ref[...]
        ms = jnp.mean(x * x, axis=-1, keepdims=True)
        h = (x * lax.rsqrt(ms + EPS) * g_ref[...]).astype(BF16)
        gmat = gmat_ref[...]
        lane = lax.broadcasted_iota(jnp.int32, (tm, LANES), 1)

        def project(c0, c1):
            y_w[:, c0:c1] = _dot(h, w_ref[:, c0:c1])

        y_w[:, :Q_DIM] = _dot(h, wq_ref[...])

        v0 = Q_DIM + KV_DIM
        uv0 = Q_DIM + 2 * KV_DIM
        g0 = uv0 + 2 * GM_DIM
        g1 = g0 + (w_ref.shape[1] - g0) // 2

        n_qk = (Q_DIM + KV_DIM) // LANES
        yqk = [y_r[:, j * LANES:(j + 1) * LANES] for j in range(n_qk)]
        project(Q_DIM, uv0)
        n_gate = w_ref.shape[1] - g0
        gate_ref[:, :n_gate // 2] = jax.nn.sigmoid(y_r[:, g0:g1])
        ss = [_group_sum(y * y, gmat) for y in yqk]
        project(uv0, g0)

        def norm_rope(y, s2, gain):
            y = y * lax.rsqrt(s2 * (1.0 / HEAD_DIM) + EPS) * gain
            return y * cos_ref[...] + _rope_partner(y, lane) * sin_ref[...]

        row = lax.broadcasted_iota(jnp.int32, (LANES, tm), 0)
        top = row < HEAD_DIM
        half_heads = N_Q_HEADS // 2
        for j in range(Q_DIM // LANES):
            qt = (norm_rope(yqk[j], ss[j], qg_ref[...]) * QK_SCALE_LOG2).T
            zero = jnp.zeros_like(qt)
            qt_ref[j] = jnp.where(top, qt, zero).astype(BF16)
            qt_ref[j + half_heads] = jnp.where(top, zero, qt).astype(BF16)
        k_ref[...] = norm_rope(yqk[n_qk - 1], ss[n_qk - 1], kg_ref[...]).astype(BF16)
        vt = y_r[:, v0:uv0].T.astype(BF16)
        for c in range(tm // TK):
            vt_ref[c] = vt[:, c * TK:(c + 1) * TK]

        gate_ref[:, n_gate // 2:] = jax.nn.sigmoid(y_r[:, g1:])
        u_ref[...] = jax.nn.gelu(y_r[:, uv0:uv0 + GM_DIM])
        n_gm = GM_DIM // LANES
        vv = [jax.nn.gelu(y_r[:, uv0 + GM_DIM + j * LANES:uv0 + GM_DIM + (j + 1) * LANES])
              for j in range(n_gm)]
        project(g0, g1)
        mu = [_group_sum(v, gmat) * (1.0 / GM_GROUP_DIM) for v in vv]
        gq = g1 + (w_ref.shape[1] - g1) // 2
        project(g1, gq)
        dv = [v - m for v, m in zip(vv, mu)]
        var = [_group_sum(d * d, gmat) * (1.0 / GM_GROUP_DIM) for d in dv]
        project(gq, w_ref.shape[1])
        for j in range(n_gm):
            sl = slice(j * LANES, (j + 1) * LANES)
            vn = dv[j] * lax.rsqrt(var[j] + EPS) * lng_ref[:, sl] + lnb_ref[:, sl]
            vn_ref[:, sl] = vn.astype(BF16)

    @pl.when(i % 2 == 0)
    def _():
        compute(ya_ref, yb_ref)

    @pl.when(i % 2 == 1)
    def _():
        compute(yb_ref, ya_ref)


def _sublane_partial_sum(e):
    rows, lanes = e.shape
    return jnp.sum(e.reshape(rows // SUBLANES, SUBLANES, lanes), axis=0)


def _attn_kernel(shift_ref, qt_ref, k_ref, vt_ref, o_ref, sa_ref, sb_ref, m_ref, l_ref, acc_ref,
                 *, fixed_shift):
    n_heads = qt_ref.shape[0]
    group = n_heads // N_KV_HEADS
    n_chunks = k_ref.shape[0] // TK
    if fixed_shift:
        shift = shift_ref[0]
    else:
        m_ref[...] = jnp.full(m_ref.shape, -jnp.inf, F32)
    acc_ref[...] = jnp.zeros(acc_ref.shape, F32)
    l_ref[...] = jnp.zeros(l_ref.shape, F32)

    def keys(c):
        return k_ref[pl.ds(pl.multiple_of(c * TK, TK), TK), :]

    def step(c, cur_ref, next_ref):
        if next_ref is not None:
            k_next = keys(c + 1)
        vt_c = vt_ref[c]
        for h in range(n_heads):
            g = h // group
            if next_ref is not None:
                next_ref[h] = _dot(k_next, qt_ref[h])
            s = cur_ref[h]
            vt_g = vt_c[g * HEAD_DIM:(g + 1) * HEAD_DIM, :]
            if fixed_shift:
                e = jnp.exp2(s - shift)
                l_ref[h] += _sublane_partial_sum(e)
                acc_ref[h] += _dot(vt_g, e.astype(BF16))
            else:
                m_old = m_ref[h]
                m_new = jnp.maximum(m_old, jnp.max(s, axis=0, keepdims=True))
                alpha = jnp.exp2(m_old - m_new)
                e = jnp.exp2(s - m_new)
                m_ref[h] = m_new
                l_ref[h] = alpha * l_ref[h] + _sublane_partial_sum(e)
                acc_ref[h] = alpha * acc_ref[h] + _dot(vt_g, e.astype(BF16))

    k_first = keys(0)
    for h in range(n_heads):
        sa_ref[h] = _dot(k_first, qt_ref[h])

    bufs = (sa_ref, sb_ref)

    def steps(c0, last):
        for j in range(CHUNKS_PER_BODY):
            final = last and j == CHUNKS_PER_BODY - 1
            step(c0 + j, bufs[j % 2], None if final else bufs[(j + 1) % 2])

    def body(i, carry):
        steps(CHUNKS_PER_BODY * i, False)
        return carry

    lax.fori_loop(0, n_chunks // CHUNKS_PER_BODY - 1, body, 0)
    steps(n_chunks - CHUNKS_PER_BODY, True)
    for h in range(n_heads):
        o = acc_ref[h] / jnp.sum(l_ref[h], axis=0, keepdims=True)
        o_ref[h * HEAD_DIM:(h + 1) * HEAD_DIM, :] = o.astype(BF16)


def _merge_ffn_kernel(x_ref, at_ref, u_ref, vn_ref, gate_ref, ws_ref, bs_ref,
                      wpa_ref, wpb_ref, wo_ref, gf_ref, wgu_ref, wd_ref, gfin_ref,
                      o_ref, gm_ref, act_ref, xa_ref, xb_ref):
    i = pl.program_id(0)
    tm = x_ref.shape[0]
    d_model = x_ref.shape[1]
    d_ff = wd_ref.shape[0]

    @pl.when(i == 0)
    def _():
        xb_ref[...] = jnp.zeros(xb_ref.shape, F32)

    def compute(x1_w, x1_r):
        x1p = x1_r[...]
        ms = jnp.mean(x1p * x1p, axis=-1, keepdims=True)
        h = (x1p * lax.rsqrt(ms + EPS) * gf_ref[...]).astype(BF16)

        lane = lax.broadcasted_iota(jnp.int32, (GM_CHUNK, LANES), 1)
        low = lane < GM_GROUP_DIM
        for c in range(tm // GM_CHUNK):
            rs = slice(c * GM_CHUNK, (c + 1) * GM_CHUNK)
            for j in range(GM_DIM // LANES):
                cs = slice(j * LANES, (j + 1) * LANES)
                vblk = vn_ref[rs, cs]
                zero = jnp.zeros_like(vblk)
                vsplit = jnp.concatenate([jnp.where(low, vblk, zero), jnp.where(low, zero, vblk)],
                                         axis=0)
                sv = _dot(ws_ref[j], vsplit)
                gm_ref[rs, cs] = (u_ref[rs, cs] * (sv + bs_ref[:, cs])).astype(BF16)

        pa = lax.dot_general(at_ref[...], wpa_ref[...], (((0,), (0,)), ((), ())),
                             preferred_element_type=F32)
        pb = _dot(gm_ref[...], wpb_ref[...])
        mix = (gate_ref[:, :d_model] * pa + gate_ref[:, d_model:] * pb).astype(BF16)

        for c0 in range(0, d_ff, FF_CHUNK):
            c1 = min(c0 + FF_CHUNK, d_ff)
            gt = _dot(h, wgu_ref[:, c0:c1])
            up = _dot(h, wgu_ref[:, d_ff + c0:d_ff + c1])
            act_ref[:, c0:c1] = (jax.nn.silu(gt) * up).astype(BF16)
        x2 = x1p + _dot(act_ref[...], wd_ref[...])
        ms2 = jnp.mean(x2 * x2, axis=-1, keepdims=True)
        o_ref[...] = x2 * lax.rsqrt(ms2 + EPS) * gfin_ref[...]

        x1_w[...] = x_ref[...] + _dot(mix, wo_ref[...])

    @pl.when(i % 2 == 0)
    def _():
        compute(xa_ref, xb_ref)

    @pl.when(i % 2 == 1)
    def _():
        compute(xb_ref, xa_ref)


def _const_spec(shape):
    nd = len(shape)
    return pl.BlockSpec(shape, lambda *_: (0,) * nd, pipeline_mode=pl.Buffered(1))


def _rope_tables(seq):
    half = HEAD_DIM // 2
    rows = seq // GRID_W
    row = np.repeat(np.arange(rows, dtype=np.float64), GRID_W)
    col = np.tile(np.arange(GRID_W, dtype=np.float64), rows)
    inv_freq = 1.0 / (ROPE_THETA ** (np.arange(0, half, 2, dtype=np.float64) / half))
    ang_r = row[:, None] * inv_freq[None, :]
    ang_c = col[:, None] * inv_freq[None, :]
    cr, sr = np.cos(ang_r), np.sin(ang_r)
    cc, sc = np.cos(ang_c), np.sin(ang_c)
    cos64 = np.concatenate([cr, cr, cc, cc], axis=-1)
    sin64 = np.concatenate([-sr, sr, -sc, sc], axis=-1)
    reps = (1, LANES // HEAD_DIM)
    return np.tile(cos64, reps).astype(np.float32), np.tile(sin64, reps).astype(np.float32)


def kernel(x, norm_mix_g, w_in, q_norm_g, k_norm_g, gm_ln_g, gm_ln_b, w_s, b_s, w_proj_a,
           w_proj_b, w_out, norm_ffn_g, w_gate_up, w_down, norm_final_g):
    B, S, D = x.shape
    in_dim = w_in.shape[-1]
    d_ff = w_down.shape[1]
    assert norm_mix_g.shape[0] == 1, "single-layer block"
    assert CHUNKS_PER_BODY % 2 == 0 and S % (CHUNKS_PER_BODY * TK) == 0
    assert S % TQ == 0 and S % TM_IN == 0 and S % TM_OUT == 0
    assert TM_OUT % GM_CHUNK == 0 and TM_IN % TK == 0 and TM_IN == TQ and TM_OUT == TQ

    w_in_b = w_in[0].astype(BF16)
    half_heads = N_Q_HEADS // 2
    w_q_b = (w_in_b[:, :Q_DIM].reshape(D, 2, half_heads, HEAD_DIM)
             .transpose(0, 2, 1, 3).reshape(D, Q_DIM))

    cos_t, sin_t = _rope_tables(S)
    reps = LANES // HEAD_DIM
    qg = jnp.tile(q_norm_g[0], reps)[None, :]
    kg = jnp.tile(k_norm_g[0], reps)[None, :]
    lng = gm_ln_g[0].reshape(1, GM_DIM)
    lnb = gm_ln_b[0].reshape(1, GM_DIM)
    gid = np.arange(LANES) // HEAD_DIM
    gmat = (gid[:, None] == gid[None, :]).astype(np.float32)
    gmat2 = jnp.asarray(np.concatenate([gmat, gmat], axis=0), dtype=BF16)

    two_axes = ("arbitrary", "arbitrary")
    cparams = pltpu.CompilerParams(dimension_semantics=two_axes,
                                   vmem_limit_bytes=VMEM_LIMIT_BYTES)

    M = B * S
    n_seq = S // TM_IN
    n_blk = M // TM_IN
    ck_in = TM_IN // TK

    def in_blk(i):
        return jnp.minimum(i, n_blk - 1)

    def out_blk(i):
        return jnp.maximum(i - 1, 0)

    def out_rows(width):
        return pl.BlockSpec((TM_IN, width), lambda i: (out_blk(i), 0))

    to_cast = [(w_gate_up, n_blk), (w_out, n_blk), (w_down, d_ff // DOWN_CAST_ROWS),
               (w_proj_a, w_proj_a.shape[1] // CAST_ROWS), (w_proj_b, w_proj_b.shape[1] // CAST_ROWS)]
    cast_steps = tuple(n for _, n in to_cast)
    assert all(n <= n_blk and w.shape[1] % n == 0 for w, n in to_cast)
    cast_specs = [pl.BlockSpec((None, w.shape[1] // n, w.shape[2]),
                               lambda i, n=n: (0, jnp.minimum(i, n - 1), 0)) for w, n in to_cast]

    qt, k, vt, u, vn, gates, wgu_b, wo_b, wd_b, wpa_b, wpb_b = pl.pallas_call(
        functools.partial(_inproj_kernel, cast_steps=cast_steps),
        grid=(n_blk + 1,),
        in_specs=[
            pl.BlockSpec((TM_IN, D), lambda i: (in_blk(i), 0)),
            _const_spec((1, D)),
            _const_spec((D, Q_DIM)),
            _const_spec((D, in_dim)),
            _const_spec((2 * LANES, LANES)),
            pl.BlockSpec((TM_IN, LANES), lambda i: (out_blk(i) % n_seq, 0)),
            pl.BlockSpec((TM_IN, LANES), lambda i: (out_blk(i) % n_seq, 0)),
            _const_spec((1, LANES)),
            _const_spec((1, LANES)),
            _const_spec((1, GM_DIM)),
            _const_spec((1, GM_DIM)),
        ] + cast_specs,
        out_specs=[
            pl.BlockSpec((None, N_Q_HEADS, LANES, TM_IN), lambda i: (out_blk(i), 0, 0, 0)),
            out_rows(KV_DIM),
            pl.BlockSpec((ck_in, KV_DIM, TK), lambda i: (out_blk(i), 0, 0)),
            out_rows(GM_DIM), out_rows(GM_DIM), out_rows(2 * D),
        ] + cast_specs,
        out_shape=[
            jax.ShapeDtypeStruct((n_blk, N_Q_HEADS, LANES, TM_IN), BF16),
            jax.ShapeDtypeStruct((M, KV_DIM), BF16),
            jax.ShapeDtypeStruct((M // TK, KV_DIM, TK), BF16),
            jax.ShapeDtypeStruct((M, GM_DIM), F32),
            jax.ShapeDtypeStruct((M, GM_DIM), BF16),
            jax.ShapeDtypeStruct((M, 2 * D), F32),
        ] + [jax.ShapeDtypeStruct(w.shape, BF16) for w, _ in to_cast],
        scratch_shapes=[pltpu.VMEM((TM_IN, in_dim), F32), pltpu.VMEM((TM_IN, in_dim), F32)],
        compiler_params=pltpu.CompilerParams(dimension_semantics=("arbitrary",),
                                             vmem_limit_bytes=VMEM_LIMIT_BYTES),
        name="inproj",
    )(x.reshape(M, D), norm_mix_g, w_q_b, w_in_b, gmat2, cos_t, sin_t, qg, kg, lng, lnb,
      *[w for w, _ in to_cast])
    k = k.reshape(B, S, KV_DIM)
    vt = vt.reshape(B, S // TK, KV_DIM, TK)

    n_q = S // TQ

    def attention(fixed_shift):
        return pl.pallas_call(
            functools.partial(_attn_kernel, fixed_shift=fixed_shift),
            grid=(B, n_q),
            in_specs=[
                pl.BlockSpec(memory_space=pltpu.SMEM),
                pl.BlockSpec((None, N_Q_HEADS, LANES, TQ), lambda b, i: (b * n_q + i, 0, 0, 0)),
                pl.BlockSpec((None, S, KV_DIM), lambda b, i: (b, 0, 0)),
                pl.BlockSpec((None, S // TK, KV_DIM, TK), lambda b, i: (b, 0, 0, 0)),
            ],
            out_specs=pl.BlockSpec((None, Q_DIM, TQ), lambda b, i: (b * n_q + i, 0, 0)),
            out_shape=jax.ShapeDtypeStruct((B * n_q, Q_DIM, TQ), BF16),
            scratch_shapes=[
                pltpu.VMEM((N_Q_HEADS, TK, TQ), F32),
                pltpu.VMEM((N_Q_HEADS, TK, TQ), F32),
                pltpu.VMEM((N_Q_HEADS, 1, TQ), F32),
                pltpu.VMEM((N_Q_HEADS, SUBLANES, TQ), F32),
                pltpu.VMEM((N_Q_HEADS, HEAD_DIM, TQ), F32),
            ],
            compiler_params=cparams,
            name="gqa_attention_shift" if fixed_shift else "gqa_attention_online",
        )

    score_bound = (HEAD_DIM * QK_SCALE_LOG2 * (1.0 + 2.0 ** -6)
                   * jnp.max(jnp.abs(q_norm_g[0])) * jnp.max(jnp.abs(k_norm_g[0])))
    attn_t = lax.cond(
        score_bound <= MAX_FIXED_SHIFT,
        lambda s_, *ops: attention(True)(s_, *ops),
        lambda s_, *ops: attention(False)(s_, *ops),
        score_bound.reshape(1).astype(F32), qt, k, vt)

    bs_full = jnp.repeat(jnp.transpose(b_s[0]), GM_GROUP_DIM, axis=1)
    ws_pairs = (w_s[0].reshape(GM_GROUPS // 2, 2, GM_CHUNK, GM_CHUNK)
                .transpose(0, 2, 1, 3).reshape(GM_GROUPS // 2, GM_CHUNK, 2 * GM_CHUNK).astype(BF16))
    n_out = M // TM_OUT

    def s1_blk(i):
        return jnp.minimum(i, n_out - 1)

    def s1_rows(width):
        return pl.BlockSpec((TM_OUT, width), lambda i: (s1_blk(i), 0))

    out = pl.pallas_call(
        _merge_ffn_kernel,
        grid=(n_out + 1,),
        in_specs=[
            s1_rows(D),
            pl.BlockSpec((None, Q_DIM, TM_OUT), lambda i: (s1_blk(i), 0, 0)),
            s1_rows(GM_DIM), s1_rows(GM_DIM), s1_rows(2 * D),
            _const_spec((GM_GROUPS // 2, GM_CHUNK, 2 * GM_CHUNK)),
            _const_spec((GM_CHUNK, GM_DIM)),
            _const_spec((None, Q_DIM, D)),
            _const_spec((None, GM_DIM, D)),
            _const_spec((None, D, D)),
            _const_spec((1, D)),
            _const_spec((None, D, 2 * d_ff)),
            _const_spec((None, d_ff, D)),
            _const_spec((1, D)),
        ],
        out_specs=pl.BlockSpec((TM_OUT, D), lambda i: (jnp.maximum(i - 1, 0), 0)),
        out_shape=jax.ShapeDtypeStruct((M, D), F32),
        scratch_shapes=[
            pltpu.VMEM((TM_OUT, GM_DIM), BF16),
            pltpu.VMEM((TM_OUT, d_ff), BF16),
            pltpu.VMEM((TM_OUT, D), F32),
            pltpu.VMEM((TM_OUT, D), F32),
        ],
        compiler_params=pltpu.CompilerParams(dimension_semantics=("arbitrary",),
                                             vmem_limit_bytes=VMEM_LIMIT_BYTES),
        name="merge_ffn",
    )(x.reshape(M, D), attn_t, u, vn, gates, ws_pairs, bs_full,
      wpa_b, wpb_b, wo_b, norm_ffn_g, wgu_b, wd_b, norm_final_g[None, :])
    return out.reshape(B, S, D)
```

```python
import functools
import math

import jax
import jax.numpy as jnp
import numpy as np
from jax import lax
from jax.experimental import pallas as pl
from jax.experimental.pallas import tpu as pltpu

HEAD_DIM = 64
N_Q_HEADS = 8
N_KV_HEADS = 2
Q_DIM = N_Q_HEADS * HEAD_DIM
KV_DIM = N_KV_HEADS * HEAD_DIM
GM_GROUPS = 8
GM_GROUP_DIM = 64
GM_DIM = GM_GROUPS * GM_GROUP_DIM
GM_CHUNK = 128
GRID_W = 64
ROPE_THETA = 10000.0
EPS = 1e-6

LANES = 128
SUBLANES = 8
VMEM_LIMIT_BYTES = 56 * 1024 * 1024

TM_IN = 256
TQ = 256
TK = 256
QK_SCALE_LOG2 = math.log2(math.e) / math.sqrt(HEAD_DIM)
MAX_FIXED_SHIFT = 48.0
CHUNKS_PER_BODY = 16
TM_OUT = 256
CAST_ROWS = 16
DOWN_CAST_ROWS = 256
FF_CHUNK = 512

F32 = jnp.float32
BF16 = jnp.bfloat16


def _dot(a, b):
    return jnp.dot(a, b, preferred_element_type=F32)


def _group_sum(x, gmat2):
    hi = x.astype(BF16)
    lo = (x - hi.astype(F32)).astype(BF16)
    return _dot(jnp.concatenate([hi, lo], axis=1), gmat2)


def _rope_partner(x, lane):
    fwd = pltpu.roll(x, LANES - 16, axis=1)
    bwd = pltpu.roll(x, 16, axis=1)
    return jnp.where((lane & 16) == 0, fwd, bwd)


def _inproj_kernel(x_ref, g_ref, wq_ref, w_ref, gmat_ref, cos_ref, sin_ref, qg_ref, kg_ref,
                   lng_ref, lnb_ref, *refs, cast_steps):
    n_cast = len(cast_steps)
    cast_in = refs[:n_cast]
    qt_ref, k_ref, vt_ref, u_ref, vn_ref, gate_ref = refs[n_cast:n_cast + 6]
    cast_out = refs[n_cast + 6:2 * n_cast + 6]
    ya_ref, yb_ref = refs[2 * n_cast + 6:]
    i = pl.program_id(0)
    tm = x_ref.shape[0]

    @pl.when(i == 0)
    def _():
        yb_ref[...] = jnp.zeros(yb_ref.shape, F32)

    for src, dst, n_steps in zip(cast_in, cast_out, cast_steps):
        @pl.when(i < n_steps)
        def _(src=src, dst=dst):
            dst[...] = src[...].astype(dst.dtype)

    def compute(y_w, y_r):
        x = x_ref[...]
        ms = jnp.mean(x * x, axis=-1, keepdims=True)
        h = (x * lax.rsqrt(ms + EPS) * g_ref[...]).astype(BF16)
        gmat = gmat_ref[...]
        lane = lax.broadcasted_iota(jnp.int32, (tm, LANES), 1)

        def project(c0, c1):
            y_w[:, c0:c1] = _dot(h, w_ref[:, c0:c1])

        y_w[:, :Q_DIM] = _dot(h, wq_ref[...])

        v0 = Q_DIM + KV_DIM
        uv0 = Q_DIM + 2 * KV_DIM
        g0 = uv0 + 2 * GM_DIM
        g1 = g0 + (w_ref.shape[1] - g0) // 2

        n_qk = (Q_DIM + KV_DIM) // LANES
        yqk = [y_r[:, j * LANES:(j + 1) * LANES] for j in range(n_qk)]
        project(Q_DIM, uv0)
        n_gate = w_ref.shape[1] - g0
        gate_ref[:, :n_gate // 2] = jax.nn.sigmoid(y_r[:, g0:g1])
        ss = [_group_sum(y * y, gmat) for y in yqk]
        project(uv0, g0)

        def norm_rope(y, s2, gain):
            y = y * lax.rsqrt(s2 * (1.0 / HEAD_DIM) + EPS) * gain
            return y * cos_ref[...] + _rope_partner(y, lane) * sin_ref[...]

        row = lax.broadcasted_iota(jnp.int32, (LANES, tm), 0)
        top = row < HEAD_DIM
        half_heads = N_Q_HEADS // 2
        for j in range(Q_DIM // LANES):
            qt = (norm_rope(yqk[j], ss[j], qg_ref[...]) * QK_SCALE_LOG2).T
            zero = jnp.zeros_like(qt)
            qt_ref[j] = jnp.where(top, qt, zero).astype(BF16)
            qt_ref[j + half_heads] = jnp.where(top, zero, qt).astype(BF16)
        k_ref[...] = norm_rope(yqk[n_qk - 1], ss[n_qk - 1], kg_ref[...]).astype(BF16)
        vt = y_r[:, v0:uv0].T.astype(BF16)
        for c in range(tm // TK):
            vt_ref[c] = vt[:, c * TK:(c + 1) * TK]

        gate_ref[:, n_gate // 2:] = jax.nn.sigmoid(y_r[:, g1:])
        u_ref[...] = jax.nn.gelu(y_r[:, uv0:uv0 + GM_DIM])
        n_gm = GM_DIM // LANES
        vv = [jax.nn.gelu(y_r[:, uv0 + GM_DIM + j * LANES:uv0 + GM_DIM + (j + 1) * LANES])
              for j in range(n_gm)]
        project(g0, g1)
        mu = [_group_sum(v, gmat) * (1.0 / GM_GROUP_DIM) for v in vv]
        gq = g1 + (w_ref.shape[1] - g1) // 2
        project(g1, gq)
        dv = [v - m for v, m in zip(vv, mu)]
        var = [_group_sum(d * d, gmat) * (1.0 / GM_GROUP_DIM) for d in dv]
        project(gq, w_ref.shape[1])
        for j in range(n_gm):
            sl = slice(j * LANES, (j + 1) * LANES)
            vn = dv[j] * lax.rsqrt(var[j] + EPS) * lng_ref[:, sl] + lnb_ref[:, sl]
            vn_ref[:, sl] = vn.astype(BF16)

    @pl.when(i % 2 == 0)
    def _():
        compute(ya_ref, yb_ref)

    @pl.when(i % 2 == 1)
    def _():
        compute(yb_ref, ya_ref)


def _sublane_partial_sum(e):
    rows, lanes = e.shape
    return jnp.sum(e.reshape(rows // SUBLANES, SUBLANES, lanes), axis=0)


def _attn_kernel(shift_ref, qt_ref, k_ref, vt_ref, o_ref, sa_ref, sb_ref, m_ref, l_ref, acc_ref,
                 *, fixed_shift):
    n_heads = qt_ref.shape[0]
    group = n_heads // N_KV_HEADS
    n_chunks = k_ref.shape[0] // TK
    if fixed_shift:
        shift = shift_ref[0]
    else:
        m_ref[...] = jnp.full(m_ref.shape, -jnp.inf, F32)
    acc_ref[...] = jnp.zeros(acc_ref.shape, F32)
    l_ref[...] = jnp.zeros(l_ref.shape, F32)

    def keys(c):
        return k_ref[pl.ds(pl.multiple_of(c * TK, TK), TK), :]

    def step(c, cur_ref, next_ref):
        if next_ref is not None:
            k_next = keys(c + 1)
        vt_c = vt_ref[c]
        for h in range(n_heads):
            g = h // group
            if next_ref is not None:
                next_ref[h] = _dot(k_next, qt_ref[h])
            s = cur_ref[h]
            vt_g = vt_c[g * HEAD_DIM:(g + 1) * HEAD_DIM, :]
            if fixed_shift:
                e = jnp.exp2(s - shift)
                l_ref[h] += _sublane_partial_sum(e)
                acc_ref[h] += _dot(vt_g, e.astype(BF16))
            else:
                m_old = m_ref[h]
                m_new = jnp.maximum(m_old, jnp.max(s, axis=0, keepdims=True))
                alpha = jnp.exp2(m_old - m_new)
                e = jnp.exp2(s - m_new)
                m_ref[h] = m_new
                l_ref[h] = alpha * l_ref[h] + _sublane_partial_sum(e)
                acc_ref[h] = alpha * acc_ref[h] + _dot(vt_g, e.astype(BF16))

    k_first = keys(0)
    for h in range(n_heads):
        sa_ref[h] = _dot(k_first, qt_ref[h])

    bufs = (sa_ref, sb_ref)

    def steps(c0, last):
        for j in range(CHUNKS_PER_BODY):
            final = last and j == CHUNKS_PER_BODY - 1
            step(c0 + j, bufs[j % 2], None if final else bufs[(j + 1) % 2])

    def body(i, carry):
        steps(CHUNKS_PER_BODY * i, False)
        return carry

    lax.fori_loop(0, n_chunks // CHUNKS_PER_BODY - 1, body, 0)
    steps(n_chunks - CHUNKS_PER_BODY, True)
    for h in range(n_heads):
        o = acc_ref[h] / jnp.sum(l_ref[h], axis=0, keepdims=True)
        o_ref[h * HEAD_DIM:(h + 1) * HEAD_DIM, :] = o.astype(BF16)


def _merge_ffn_kernel(x_ref, at_ref, u_ref, vn_ref, gate_ref, ws_ref, bs_ref,
                      wpa_ref, wpb_ref, wo_ref, gf_ref, wgu_ref, wd_ref, gfin_ref,
                      o_ref, gm_ref, act_ref, xa_ref, xb_ref):
    i = pl.program_id(0)
    tm = x_ref.shape[0]
    d_model = x_ref.shape[1]
    d_ff = wd_ref.shape[0]

    @pl.when(i == 0)
    def _():
        xb_ref[...] = jnp.zeros(xb_ref.shape, F32)

    def compute(x1_w, x1_r):
        x1p = x1_r[...]
        ms = jnp.mean(x1p * x1p, axis=-1, keepdims=True)
        h = (x1p * lax.rsqrt(ms + EPS) * gf_ref[...]).astype(BF16)

        lane = lax.broadcasted_iota(jnp.int32, (GM_CHUNK, LANES), 1)
        low = lane < GM_GROUP_DIM
        for c in range(tm // GM_CHUNK):
            rs = slice(c * GM_CHUNK, (c + 1) * GM_CHUNK)
            for j in range(GM_DIM // LANES):
                cs = slice(j * LANES, (j + 1) * LANES)
                vblk = vn_ref[rs, cs]
                zero = jnp.zeros_like(vblk)
                vsplit = jnp.concatenate([jnp.where(low, vblk, zero), jnp.where(low, zero, vblk)],
                                         axis=0)
                sv = _dot(ws_ref[j], vsplit)
                gm_ref[rs, cs] = (u_ref[rs, cs] * (sv + bs_ref[:, cs])).astype(BF16)

        pa = lax.dot_general(at_ref[...], wpa_ref[...], (((0,), (0,)), ((), ())),
                             preferred_element_type=F32)
        pb = _dot(gm_ref[...], wpb_ref[...])
        mix = (gate_ref[:, :d_model] * pa + gate_ref[:, d_model:] * pb).astype(BF16)

        for c0 in range(0, d_ff, FF_CHUNK):
            c1 = min(c0 + FF_CHUNK, d_ff)
            gt = _dot(h, wgu_ref[:, c0:c1])
            up = _dot(h, wgu_ref[:, d_ff + c0:d_ff + c1])
            act_ref[:, c0:c1] = (jax.nn.silu(gt) * up).astype(BF16)
        x2 = x1p + _dot(act_ref[...], wd_ref[...])
        ms2 = jnp.mean(x2 * x2, axis=-1, keepdims=True)
        o_ref[...] = x2 * lax.rsqrt(ms2 + EPS) * gfin_ref[...]

        x1_w[...] = x_ref[...] + _dot(mix, wo_ref[...])

    @pl.when(i % 2 == 0)
    def _():
        compute(xa_ref, xb_ref)

    @pl.when(i % 2 == 1)
    def _():
        compute(xb_ref, xa_ref)


def _const_spec(shape):
    nd = len(shape)
    return pl.BlockSpec(shape, lambda *_: (0,) * nd, pipeline_mode=pl.Buffered(1))


def _rope_tables(seq):
    half = HEAD_DIM // 2
    rows = seq // GRID_W
    row = np.repeat(np.arange(rows, dtype=np.float64), GRID_W)
    col = np.tile(np.arange(GRID_W, dtype=np.float64), rows)
    inv_freq = 1.0 / (ROPE_THETA ** (np.arange(0, half, 2, dtype=np.float64) / half))
    ang_r = row[:, None] * inv_freq[None, :]
    ang_c = col[:, None] * inv_freq[None, :]
    cr, sr = np.cos(ang_r), np.sin(ang_r)
    cc, sc = np.cos(ang_c), np.sin(ang_c)
    cos64 = np.concatenate([cr, cr, cc, cc], axis=-1)
    sin64 = np.concatenate([-sr, sr, -sc, sc], axis=-1)
    reps = (1, LANES // HEAD_DIM)
    return np.tile(cos64, reps).astype(np.float32), np.tile(sin64, reps).astype(np.float32)


def kernel(x, norm_mix_g, w_in, q_norm_g, k_norm_g, gm_ln_g, gm_ln_b, w_s, b_s, w_proj_a,
           w_proj_b, w_out, norm_ffn_g, w_gate_up, w_down, norm_final_g):
    B, S, D = x.shape
    in_dim = w_in.shape[-1]
    d_ff = w_down.shape[1]
    assert norm_mix_g.shape[0] == 1, "single-layer block"
    assert CHUNKS_PER_BODY % 2 == 0 and S % (CHUNKS_PER_BODY * TK) == 0
    assert S % TQ == 0 and S % TM_IN == 0 and S % TM_OUT == 0
    assert TM_OUT % GM_CHUNK == 0 and TM_IN % TK == 0 and TM_IN == TQ and TM_OUT == TQ

    w_in_b = w_in[0].astype(BF16)
    half_heads = N_Q_HEADS // 2
    w_q_b = (w_in_b[:, :Q_DIM].reshape(D, 2, half_heads, HEAD_DIM)
             .transpose(0, 2, 1, 3).reshape(D, Q_DIM))

    cos_t, sin_t = _rope_tables(S)
    reps = LANES // HEAD_DIM
    qg = jnp.tile(q_norm_g[0], reps)[None, :]
    kg = jnp.tile(k_norm_g[0], reps)[None, :]
    lng = gm_ln_g[0].reshape(1, GM_DIM)
    lnb = gm_ln_b[0].reshape(1, GM_DIM)
    gid = np.arange(LANES) // HEAD_DIM
    gmat = (gid[:, None] == gid[None, :]).astype(np.float32)
    gmat2 = jnp.asarray(np.concatenate([gmat, gmat], axis=0), dtype=BF16)

    two_axes = ("arbitrary", "arbitrary")
    cparams = pltpu.CompilerParams(dimension_semantics=two_axes,
                                   vmem_limit_bytes=VMEM_LIMIT_BYTES)

    M = B * S
    n_seq = S // TM_IN
    n_blk = M // TM_IN
    ck_in = TM_IN // TK

    def in_blk(i):
        return jnp.minimum(i, n_blk - 1)

    def out_blk(i):
        return jnp.maximum(i - 1, 0)

    def out_rows(width):
        return pl.BlockSpec((TM_IN, width), lambda i: (out_blk(i), 0))

    to_cast = [(w_gate_up, n_blk), (w_out, n_blk), (w_down, d_ff // DOWN_CAST_ROWS),
               (w_proj_a, w_proj_a.shape[1] // CAST_ROWS), (w_proj_b, w_proj_b.shape[1] // CAST_ROWS)]
    cast_steps = tuple(n for _, n in to_cast)
    assert all(n <= n_blk and w.shape[1] % n == 0 for w, n in to_cast)
    cast_specs = [pl.BlockSpec((None, w.shape[1] // n, w.shape[2]),
                               lambda i, n=n: (0, jnp.minimum(i, n - 1), 0)) for w, n in to_cast]

    qt, k, vt, u, vn, gates, wgu_b, wo_b, wd_b, wpa_b, wpb_b = pl.pallas_call(
        functools.partial(_inproj_kernel, cast_steps=cast_steps),
        grid=(n_blk + 1,),
        in_specs=[
            pl.BlockSpec((TM_IN, D), lambda i: (in_blk(i), 0)),
            _const_spec((1, D)),
            _const_spec((D, Q_DIM)),
            _const_spec((D, in_dim)),
            _const_spec((2 * LANES, LANES)),
            pl.BlockSpec((TM_IN, LANES), lambda i: (out_blk(i) % n_seq, 0)),
            pl.BlockSpec((TM_IN, LANES), lambda i: (out_blk(i) % n_seq, 0)),
            _const_spec((1, LANES)),
            _const_spec((1, LANES)),
            _const_spec((1, GM_DIM)),
            _const_spec((1, GM_DIM)),
        ] + cast_specs,
        out_specs=[
            pl.BlockSpec((None, N_Q_HEADS, LANES, TM_IN), lambda i: (out_blk(i), 0, 0, 0)),
            out_rows(KV_DIM),
            pl.BlockSpec((ck_in, KV_DIM, TK), lambda i: (out_blk(i), 0, 0)),
            out_rows(GM_DIM), out_rows(GM_DIM), out_rows(2 * D),
        ] + cast_specs,
        out_shape=[
            jax.ShapeDtypeStruct((n_blk, N_Q_HEADS, LANES, TM_IN), BF16),
            jax.ShapeDtypeStruct((M, KV_DIM), BF16),
            jax.ShapeDtypeStruct((M // TK, KV_DIM, TK), BF16),
            jax.ShapeDtypeStruct((M, GM_DIM), F32),
            jax.ShapeDtypeStruct((M, GM_DIM), BF16),
            jax.ShapeDtypeStruct((M, 2 * D), F32),
        ] + [jax.ShapeDtypeStruct(w.shape, BF16) for w, _ in to_cast],
        scratch_shapes=[pltpu.VMEM((TM_IN, in_dim), F32), pltpu.VMEM((TM_IN, in_dim), F32)],
        compiler_params=pltpu.CompilerParams(dimension_semantics=("arbitrary",),
                                             vmem_limit_bytes=VMEM_LIMIT_BYTES),
        name="inproj",
    )(x.reshape(M, D), norm_mix_g, w_q_b, w_in_b, gmat2, cos_t, sin_t, qg, kg, lng, lnb,
      *[w for w, _ in to_cast])
    k = k.reshape(B, S, KV_DIM)
    vt = vt.reshape(B, S // TK, KV_DIM, TK)

    n_q = S // TQ

    def attention(fixed_shift):
        return pl.pallas_call(
            functools.partial(_attn_kernel, fixed_shift=fixed_shift),
            grid=(B, n_q),
            in_specs=[
                pl.BlockSpec(memory_space=pltpu.SMEM),
                pl.BlockSpec((None, N_Q_HEADS, LANES, TQ), lambda b, i: (b * n_q + i, 0, 0, 0)),
                pl.BlockSpec((None, S, KV_DIM), lambda b, i: (b, 0, 0)),
                pl.BlockSpec((None, S // TK, KV_DIM, TK), lambda b, i: (b, 0, 0, 0)),
            ],
            out_specs=pl.BlockSpec((None, Q_DIM, TQ), lambda b, i: (b * n_q + i, 0, 0)),
            out_shape=jax.ShapeDtypeStruct((B * n_q, Q_DIM, TQ), BF16),
            scratch_shapes=[
                pltpu.VMEM((N_Q_HEADS, TK, TQ), F32),
                pltpu.VMEM((N_Q_HEADS, TK, TQ), F32),
                pltpu.VMEM((N_Q_HEADS, 1, TQ), F32),
                pltpu.VMEM((N_Q_HEADS, SUBLANES, TQ), F32),
                pltpu.VMEM((N_Q_HEADS, HEAD_DIM, TQ), F32),
            ],
            compiler_params=cparams,
            name="gqa_attention_shift" if fixed_shift else "gqa_attention_online",
        )

    score_bound = (HEAD_DIM * QK_SCALE_LOG2 * (1.0 + 2.0 ** -6)
                   * jnp.max(jnp.abs(q_norm_g[0])) * jnp.max(jnp.abs(k_norm_g[0])))
    attn_t = lax.cond(
        score_bound <= MAX_FIXED_SHIFT,
        lambda s_, *ops: attention(True)(s_, *ops),
        lambda s_, *ops: attention(False)(s_, *ops),
        score_bound.reshape(1).astype(F32), qt, k, vt)

    bs_full = jnp.repeat(jnp.transpose(b_s[0]), GM_GROUP_DIM, axis=1)
    ws_pairs = (w_s[0].reshape(GM_GROUPS // 2, 2, GM_CHUNK, GM_CHUNK)
                .transpose(0, 2, 1, 3).reshape(GM_GROUPS // 2, GM_CHUNK, 2 * GM_CHUNK).astype(BF16))
    n_out = M // TM_OUT

    def s1_blk(i):
        return jnp.minimum(i, n_out - 1)

    def s1_rows(width):
        return pl.BlockSpec((TM_OUT, width), lambda i: (s1_blk(i), 0))

    out = pl.pallas_call(
        _merge_ffn_kernel,
        grid=(n_out + 1,),
        in_specs=[
            s1_rows(D),
            pl.BlockSpec((None, Q_DIM, TM_OUT), lambda i: (s1_blk(i), 0, 0)),
            s1_rows(GM_DIM), s1_rows(GM_DIM), s1_rows(2 * D),
            _const_spec((GM_GROUPS // 2, GM_CHUNK, 2 * GM_CHUNK)),
            _const_spec((GM_CHUNK, GM_DIM)),
            _const_spec((None, Q_DIM, D)),
            _const_spec((None, GM_DIM, D)),
            _const_spec((None, D, D)),
            _const_spec((1, D)),
            _const_spec((None, D, 2 * d_ff)),
            _const_spec((None, d_ff, D)),
            _const_spec((1, D)),
        ],
        out_specs=pl.BlockSpec((TM_OUT, D), lambda i: (jnp.maximum(i - 1, 0), 0)),
        out_shape=jax.ShapeDtypeStruct((M, D), F32),
        scratch_shapes=[
            pltpu.VMEM((TM_OUT, GM_DIM), BF16),
            pltpu.VMEM((TM_OUT, d_ff), BF16),
            pltpu.VMEM((TM_OUT, D), F32),
            pltpu.VMEM((TM_OUT, D), F32),
        ],
        compiler_params=pltpu.CompilerParams(dimension_semantics=("arbitrary",),
                                             vmem_limit_bytes=VMEM_LIMIT_BYTES),
        name="merge_ffn",
    )(x.reshape(M, D), attn_t, u, vn, gates, ws_pairs, bs_full,
      wpa_b, wpb_b, wo_b, norm_ffn_g, wgu_b, wd_b, norm_final_g[None, :])
    return out.reshape(B, S, D)
```

```python
import functools
import math

import jax
import jax.numpy as jnp
import numpy as np
from jax import lax
from jax.experimental import pallas as pl
from jax.experimental.pallas import tpu as pltpu

HEAD_DIM = 64
N_Q_HEADS = 8
N_KV_HEADS = 2
Q_DIM = N_Q_HEADS * HEAD_DIM
KV_DIM = N_KV_HEADS * HEAD_DIM
GM_GROUPS = 8
GM_GROUP_DIM = 64
GM_DIM = GM_GROUPS * GM_GROUP_DIM
GM_CHUNK = 128
GRID_W = 64
ROPE_THETA = 10000.0
EPS = 1e-6

LANES = 128
SUBLANES = 8
VMEM_LIMIT_BYTES = 56 * 1024 * 1024

TM_IN = 256
TQ = 256
TK = 256
QK_SCALE_LOG2 = math.log2(math.e) / math.sqrt(HEAD_DIM)
MAX_FIXED_SHIFT = 48.0
CHUNKS_PER_BODY = 16
TM_OUT = 256
CAST_ROWS = 16
DOWN_CAST_ROWS = 256
FF_CHUNK = 768

F32 = jnp.float32
BF16 = jnp.bfloat16


def _dot(a, b):
    return jnp.dot(a, b, preferred_element_type=F32)


def _group_sum(x, gmat2):
    hi = x.astype(BF16)
    lo = (x - hi.astype(F32)).astype(BF16)
    return _dot(jnp.concatenate([hi, lo], axis=1), gmat2)


def _rope_partner(x, lane):
    fwd = pltpu.roll(x, LANES - 16, axis=1)
    bwd = pltpu.roll(x, 16, axis=1)
    return jnp.where((lane & 16) == 0, fwd, bwd)


def _inproj_kernel(x_ref, g_ref, wq_ref, w_ref, gmat_ref, cos_ref, sin_ref, qg_ref, kg_ref,
                   lng_ref, lnb_ref, *refs, cast_steps):
    n_cast = len(cast_steps)
    cast_in = refs[:n_cast]
    qt_ref, k_ref, vt_ref, u_ref, vn_ref, gate_ref = refs[n_cast:n_cast + 6]
    cast_out = refs[n_cast + 6:2 * n_cast + 6]
    ya_ref, yb_ref = refs[2 * n_cast + 6:]
    i = pl.program_id(0)
    tm = x_ref.shape[0]

    @pl.when(i == 0)
    def _():
        yb_ref[...] = jnp.zeros(yb_ref.shape, F32)

    for src, dst, n_steps in zip(cast_in, cast_out, cast_steps):
        @pl.when(i < n_steps)
        def _(src=src, dst=dst):
            dst[...] = src[...].astype(dst.dtype)

    def compute(y_w, y_r):
        x = x_ref[...]
        ms = jnp.mean(x * x, axis=-1, keepdims=True)
        h = (x * lax.rsqrt(ms + EPS) * g_ref[...]).astype(BF16)
        gmat = gmat_ref[...]
        lane = lax.broadcasted_iota(jnp.int32, (tm, LANES), 1)

        def project(c0, c1):
            y_w[:, c0:c1] = _dot(h, w_ref[:, c0:c1])

        y_w[:, :Q_DIM] = _dot(h, wq_ref[...])

        v0 = Q_DIM + KV_DIM
        uv0 = Q_DIM + 2 * KV_DIM
        g0 = uv0 + 2 * GM_DIM
        g1 = g0 + (w_ref.shape[1] - g0) // 2

        n_qk = (Q_DIM + KV_DIM) // LANES
        yqk = [y_r[:, j * LANES:(j + 1) * LANES] for j in range(n_qk)]
        project(Q_DIM, uv0)
        n_gate = w_ref.shape[1] - g0
        gate_ref[:, :n_gate // 2] = jax.nn.sigmoid(y_r[:, g0:g1])
        ss = [_group_sum(y * y, gmat) for y in yqk]
        project(uv0, g0)

        def norm_rope(y, s2, gain):
            y = y * lax.rsqrt(s2 * (1.0 / HEAD_DIM) + EPS) * gain
            return y * cos_ref[...] + _rope_partner(y, lane) * sin_ref[...]

        row = lax.broadcasted_iota(jnp.int32, (LANES, tm), 0)
        top = row < HEAD_DIM
        half_heads = N_Q_HEADS // 2
        for j in range(Q_DIM // LANES):
            qt = (norm_rope(yqk[j], ss[j], qg_ref[...]) * QK_SCALE_LOG2).T
            zero = jnp.zeros_like(qt)
            qt_ref[j] = jnp.where(top, qt, zero).astype(BF16)
            qt_ref[j + half_heads] = jnp.where(top, zero, qt).astype(BF16)
        k_ref[...] = norm_rope(yqk[n_qk - 1], ss[n_qk - 1], kg_ref[...]).astype(BF16)
        vt = y_r[:, v0:uv0].T.astype(BF16)
        for c in range(tm // TK):
            vt_ref[c] = vt[:, c * TK:(c + 1) * TK]

        gate_ref[:, n_gate // 2:] = jax.nn.sigmoid(y_r[:, g1:])
        u_ref[...] = jax.nn.gelu(y_r[:, uv0:uv0 + GM_DIM])
        n_gm = GM_DIM // LANES
        vv = [jax.nn.gelu(y_r[:, uv0 + GM_DIM + j * LANES:uv0 + GM_DIM + (j + 1) * LANES])
              for j in range(n_gm)]
        project(g0, g1)
        mu = [_group_sum(v, gmat) * (1.0 / GM_GROUP_DIM) for v in vv]
        gq = g1 + (w_ref.shape[1] - g1) // 2
        project(g1, gq)
        dv = [v - m for v, m in zip(vv, mu)]
        var = [_group_sum(d * d, gmat) * (1.0 / GM_GROUP_DIM) for d in dv]
        project(gq, w_ref.shape[1])
        for j in range(n_gm):
            sl = slice(j * LANES, (j + 1) * LANES)
            vn = dv[j] * lax.rsqrt(var[j] + EPS) * lng_ref[:, sl] + lnb_ref[:, sl]
            vn_ref[:, sl] = vn.astype(BF16)

    @pl.when(i % 2 == 0)
    def _():
        compute(ya_ref, yb_ref)

    @pl.when(i % 2 == 1)
    def _():
        compute(yb_ref, ya_ref)


def _sublane_partial_sum(e):
    rows, lanes = e.shape
    return jnp.sum(e.reshape(rows // SUBLANES, SUBLANES, lanes), axis=0)


def _attn_kernel(shift_ref, qt_ref, k_ref, vt_ref, o_ref, sa_ref, sb_ref, m_ref, l_ref, acc_ref,
                 *, fixed_shift):
    n_heads = qt_ref.shape[0]
    group = n_heads // N_KV_HEADS
    n_chunks = k_ref.shape[0] // TK
    if fixed_shift:
        shift = shift_ref[0]
    else:
        m_ref[...] = jnp.full(m_ref.shape, -jnp.inf, F32)
    acc_ref[...] = jnp.zeros(acc_ref.shape, F32)
    l_ref[...] = jnp.zeros(l_ref.shape, F32)

    def keys(c):
        return k_ref[pl.ds(pl.multiple_of(c * TK, TK), TK), :]

    def step(c, cur_ref, next_ref):
        if next_ref is not None:
            k_next = keys(c + 1)
        vt_c = vt_ref[c]
        for h in range(n_heads):
            g = h // group
            if next_ref is not None:
                next_ref[h] = _dot(k_next, qt_ref[h])
            s = cur_ref[h]
            vt_g = vt_c[g * HEAD_DIM:(g + 1) * HEAD_DIM, :]
            if fixed_shift:
                e = jnp.exp2(s - shift)
                l_ref[h] += _sublane_partial_sum(e)
                acc_ref[h] += _dot(vt_g, e.astype(BF16))
            else:
                m_old = m_ref[h]
                m_new = jnp.maximum(m_old, jnp.max(s, axis=0, keepdims=True))
                alpha = jnp.exp2(m_old - m_new)
                e = jnp.exp2(s - m_new)
                m_ref[h] = m_new
                l_ref[h] = alpha * l_ref[h] + _sublane_partial_sum(e)
                acc_ref[h] = alpha * acc_ref[h] + _dot(vt_g, e.astype(BF16))

    k_first = keys(0)
    for h in range(n_heads):
        sa_ref[h] = _dot(k_first, qt_ref[h])

    bufs = (sa_ref, sb_ref)

    def steps(c0, last):
        for j in range(CHUNKS_PER_BODY):
            final = last and j == CHUNKS_PER_BODY - 1
            step(c0 + j, bufs[j % 2], None if final else bufs[(j + 1) % 2])

    def body(i, carry):
        steps(CHUNKS_PER_BODY * i, False)
        return carry

    lax.fori_loop(0, n_chunks // CHUNKS_PER_BODY - 1, body, 0)
    steps(n_chunks - CHUNKS_PER_BODY, True)
    for h in range(n_heads):
        o = acc_ref[h] / jnp.sum(l_ref[h], axis=0, keepdims=True)
        o_ref[h * HEAD_DIM:(h + 1) * HEAD_DIM, :] = o.astype(BF16)


def _merge_ffn_kernel(x_ref, at_ref, u_ref, vn_ref, gate_ref, ws_ref, bs_ref,
                      wpa_ref, wpb_ref, wo_ref, gf_ref, wgu_ref, wd_ref, gfin_ref,
                      o_ref, gm_ref, act_ref, xa_ref, xb_ref):
    i = pl.program_id(0)
    tm = x_ref.shape[0]
    d_model = x_ref.shape[1]
    d_ff = wd_ref.shape[0]

    @pl.when(i == 0)
    def _():
        xb_ref[...] = jnp.zeros(xb_ref.shape, F32)

    def compute(x1_w, x1_r):
        x1p = x1_r[...]
        ms = jnp.mean(x1p * x1p, axis=-1, keepdims=True)
        h = (x1p * lax.rsqrt(ms + EPS) * gf_ref[...]).astype(BF16)

        lane = lax.broadcasted_iota(jnp.int32, (GM_CHUNK, LANES), 1)
        low = lane < GM_GROUP_DIM
        for c in range(tm // GM_CHUNK):
            rs = slice(c * GM_CHUNK, (c + 1) * GM_CHUNK)
            for j in range(GM_DIM // LANES):
                cs = slice(j * LANES, (j + 1) * LANES)
                vblk = vn_ref[rs, cs]
                zero = jnp.zeros_like(vblk)
                vsplit = jnp.concatenate([jnp.where(low, vblk, zero), jnp.where(low, zero, vblk)],
                                         axis=0)
                sv = _dot(ws_ref[j], vsplit)
                gm_ref[rs, cs] = (u_ref[rs, cs] * (sv + bs_ref[:, cs])).astype(BF16)

        pa = lax.dot_general(at_ref[...], wpa_ref[...], (((0,), (0,)), ((), ())),
                             preferred_element_type=F32)
        pb = _dot(gm_ref[...], wpb_ref[...])
        mix = (gate_ref[:, :d_model] * pa + gate_ref[:, d_model:] * pb).astype(BF16)

        for c0 in range(0, d_ff, FF_CHUNK):
            c1 = min(c0 + FF_CHUNK, d_ff)
            gt = _dot(h, wgu_ref[:, c0:c1])
            up = _dot(h, wgu_ref[:, d_ff + c0:d_ff + c1])
            act_ref[:, c0:c1] = (jax.nn.silu(gt) * up).astype(BF16)
        x1_w[...] = x_ref[...] + _dot(mix, wo_ref[...])

        x2 = x1p + _dot(act_ref[...], wd_ref[...])
        ms2 = jnp.mean(x2 * x2, axis=-1, keepdims=True)
        o_ref[...] = x2 * lax.rsqrt(ms2 + EPS) * gfin_ref[...]

    @pl.when(i % 2 == 0)
    def _():
        compute(xa_ref, xb_ref)

    @pl.when(i % 2 == 1)
    def _():
        compute(xb_ref, xa_ref)


def _const_spec(shape):
    nd = len(shape)
    return pl.BlockSpec(shape, lambda *_: (0,) * nd, pipeline_mode=pl.Buffered(1))


def _rope_tables(seq):
    half = HEAD_DIM // 2
    rows = seq // GRID_W
    row = np.repeat(np.arange(rows, dtype=np.float64), GRID_W)
    col = np.tile(np.arange(GRID_W, dtype=np.float64), rows)
    inv_freq = 1.0 / (ROPE_THETA ** (np.arange(0, half, 2, dtype=np.float64) / half))
    ang_r = row[:, None] * inv_freq[None, :]
    ang_c = col[:, None] * inv_freq[None, :]
    cr, sr = np.cos(ang_r), np.sin(ang_r)
    cc, sc = np.cos(ang_c), np.sin(ang_c)
    cos64 = np.concatenate([cr, cr, cc, cc], axis=-1)
    sin64 = np.concatenate([-sr, sr, -sc, sc], axis=-1)
    reps = (1, LANES // HEAD_DIM)
    return np.tile(cos64, reps).astype(np.float32), np.tile(sin64, reps).astype(np.float32)


def kernel(x, norm_mix_g, w_in, q_norm_g, k_norm_g, gm_ln_g, gm_ln_b, w_s, b_s, w_proj_a,
           w_proj_b, w_out, norm_ffn_g, w_gate_up, w_down, norm_final_g):
    B, S, D = x.shape
    in_dim = w_in.shape[-1]
    d_ff = w_down.shape[1]
    assert norm_mix_g.shape[0] == 1, "single-layer block"
    assert CHUNKS_PER_BODY % 2 == 0 and S % (CHUNKS_PER_BODY * TK) == 0
    assert S % TQ == 0 and S % TM_IN == 0 and S % TM_OUT == 0
    assert TM_OUT % GM_CHUNK == 0 and TM_IN % TK == 0 and TM_IN == TQ and TM_OUT == TQ

    w_in_b = w_in[0].astype(BF16)
    half_heads = N_Q_HEADS // 2
    w_q_b = (w_in_b[:, :Q_DIM].reshape(D, 2, half_heads, HEAD_DIM)
             .transpose(0, 2, 1, 3).reshape(D, Q_DIM))

    cos_t, sin_t = _rope_tables(S)
    reps = LANES // HEAD_DIM
    qg = jnp.tile(q_norm_g[0], reps)[None, :]
    kg = jnp.tile(k_norm_g[0], reps)[None, :]
    lng = gm_ln_g[0].reshape(1, GM_DIM)
    lnb = gm_ln_b[0].reshape(1, GM_DIM)
    gid = np.arange(LANES) // HEAD_DIM
    gmat = (gid[:, None] == gid[None, :]).astype(np.float32)
    gmat2 = jnp.asarray(np.concatenate([gmat, gmat], axis=0), dtype=BF16)

    two_axes = ("arbitrary", "arbitrary")
    cparams = pltpu.CompilerParams(dimension_semantics=two_axes,
                                   vmem_limit_bytes=VMEM_LIMIT_BYTES)

    M = B * S
    n_seq = S // TM_IN
    n_blk = M // TM_IN
    ck_in = TM_IN // TK

    def in_blk(i):
        return jnp.minimum(i, n_blk - 1)

    def out_blk(i):
        return jnp.maximum(i - 1, 0)

    def out_rows(width):
        return pl.BlockSpec((TM_IN, width), lambda i: (out_blk(i), 0))

    to_cast = [(w_gate_up, n_blk), (w_out, n_blk), (w_down, d_ff // DOWN_CAST_ROWS),
               (w_proj_a, w_proj_a.shape[1] // CAST_ROWS), (w_proj_b, w_proj_b.shape[1] // CAST_ROWS)]
    cast_steps = tuple(n for _, n in to_cast)
    assert all(n <= n_blk and w.shape[1] % n == 0 for w, n in to_cast)
    cast_specs = [pl.BlockSpec((None, w.shape[1] // n, w.shape[2]),
                               lambda i, n=n: (0, jnp.minimum(i, n - 1), 0)) for w, n in to_cast]

    qt, k, vt, u, vn, gates, wgu_b, wo_b, wd_b, wpa_b, wpb_b = pl.pallas_call(
        functools.partial(_inproj_kernel, cast_steps=cast_steps),
        grid=(n_blk + 1,),
        in_specs=[
            pl.BlockSpec((TM_IN, D), lambda i: (in_blk(i), 0)),
            _const_spec((1, D)),
            _const_spec((D, Q_DIM)),
            _const_spec((D, in_dim)),
            _const_spec((2 * LANES, LANES)),
            pl.BlockSpec((TM_IN, LANES), lambda i: (out_blk(i) % n_seq, 0)),
            pl.BlockSpec((TM_IN, LANES), lambda i: (out_blk(i) % n_seq, 0)),
            _const_spec((1, LANES)),
            _const_spec((1, LANES)),
            _const_spec((1, GM_DIM)),
            _const_spec((1, GM_DIM)),
        ] + cast_specs,
        out_specs=[
            pl.BlockSpec((None, N_Q_HEADS, LANES, TM_IN), lambda i: (out_blk(i), 0, 0, 0)),
            out_rows(KV_DIM),
            pl.BlockSpec((ck_in, KV_DIM, TK), lambda i: (out_blk(i), 0, 0)),
            out_rows(GM_DIM), out_rows(GM_DIM), out_rows(2 * D),
        ] + cast_specs,
        out_shape=[
            jax.ShapeDtypeStruct((n_blk, N_Q_HEADS, LANES, TM_IN), BF16),
            jax.ShapeDtypeStruct((M, KV_DIM), BF16),
            jax.ShapeDtypeStruct((M // TK, KV_DIM, TK), BF16),
            jax.ShapeDtypeStruct((M, GM_DIM), F32),
            jax.ShapeDtypeStruct((M, GM_DIM), BF16),
            jax.ShapeDtypeStruct((M, 2 * D), F32),
        ] + [jax.ShapeDtypeStruct(w.shape, BF16) for w, _ in to_cast],
        scratch_shapes=[pltpu.VMEM((TM_IN, in_dim), F32), pltpu.VMEM((TM_IN, in_dim), F32)],
        compiler_params=pltpu.CompilerParams(dimension_semantics=("arbitrary",),
                                             vmem_limit_bytes=VMEM_LIMIT_BYTES),
        name="inproj",
    )(x.reshape(M, D), norm_mix_g, w_q_b, w_in_b, gmat2, cos_t, sin_t, qg, kg, lng, lnb,
      *[w for w, _ in to_cast])
    k = k.reshape(B, S, KV_DIM)
    vt = vt.reshape(B, S // TK, KV_DIM, TK)

    n_q = S // TQ

    def attention(fixed_shift):
        return pl.pallas_call(
            functools.partial(_attn_kernel, fixed_shift=fixed_shift),
            grid=(B, n_q),
            in_specs=[
                pl.BlockSpec(memory_space=pltpu.SMEM),
                pl.BlockSpec((None, N_Q_HEADS, LANES, TQ), lambda b, i: (b * n_q + i, 0, 0, 0)),
                pl.BlockSpec((None, S, KV_DIM), lambda b, i: (b, 0, 0)),
                pl.BlockSpec((None, S // TK, KV_DIM, TK), lambda b, i: (b, 0, 0, 0)),
            ],
            out_specs=pl.BlockSpec((None, Q_DIM, TQ), lambda b, i: (b * n_q + i, 0, 0)),
            out_shape=jax.ShapeDtypeStruct((B * n_q, Q_DIM, TQ), BF16),
            scratch_shapes=[
                pltpu.VMEM((N_Q_HEADS, TK, TQ), F32),
                pltpu.VMEM((N_Q_HEADS, TK, TQ), F32),
                pltpu.VMEM((N_Q_HEADS, 1, TQ), F32),
                pltpu.VMEM((N_Q_HEADS, SUBLANES, TQ), F32),
                pltpu.VMEM((N_Q_HEADS, HEAD_DIM, TQ), F32),
            ],
            compiler_params=cparams,
            name="gqa_attention_shift" if fixed_shift else "gqa_attention_online",
        )

    score_bound = (HEAD_DIM * QK_SCALE_LOG2 * (1.0 + 2.0 ** -6)
                   * jnp.max(jnp.abs(q_norm_g[0])) * jnp.max(jnp.abs(k_norm_g[0])))
    attn_t = lax.cond(
        score_bound <= MAX_FIXED_SHIFT,
        lambda s_, *ops: attention(True)(s_, *ops),
        lambda s_, *ops: attention(False)(s_, *ops),
        score_bound.reshape(1).astype(F32), qt, k, vt)

    bs_full = jnp.repeat(jnp.transpose(b_s[0]), GM_GROUP_DIM, axis=1)
    ws_pairs = (w_s[0].reshape(GM_GROUPS // 2, 2, GM_CHUNK, GM_CHUNK)
                .transpose(0, 2, 1, 3).reshape(GM_GROUPS // 2, GM_CHUNK, 2 * GM_CHUNK).astype(BF16))
    n_out = M // TM_OUT

    def s1_blk(i):
        return jnp.minimum(i, n_out - 1)

    def s1_rows(width):
        return pl.BlockSpec((TM_OUT, width), lambda i: (s1_blk(i), 0))

    out = pl.pallas_call(
        _merge_ffn_kernel,
        grid=(n_out + 1,),
        in_specs=[
            s1_rows(D),
            pl.BlockSpec((None, Q_DIM, TM_OUT), lambda i: (s1_blk(i), 0, 0)),
            s1_rows(GM_DIM), s1_rows(GM_DIM), s1_rows(2 * D),
            _const_spec((GM_GROUPS // 2, GM_CHUNK, 2 * GM_CHUNK)),
            _const_spec((GM_CHUNK, GM_DIM)),
            _const_spec((None, Q_DIM, D)),
            _const_spec((None, GM_DIM, D)),
            _const_spec((None, D, D)),
            _const_spec((1, D)),
            _const_spec((None, D, 2 * d_ff)),
            _const_spec((None, d_ff, D)),
            _const_spec((1, D)),
        ],
        out_specs=pl.BlockSpec((TM_OUT, D), lambda i: (jnp.maximum(i - 1, 0), 0)),
        out_shape=jax.ShapeDtypeStruct((M, D), F32),
        scratch_shapes=[
            pltpu.VMEM((TM_OUT, GM_DIM), BF16),
            pltpu.VMEM((TM_OUT, d_ff), BF16),
            pltpu.VMEM((TM_OUT, D), F32),
            pltpu.VMEM((TM_OUT, D), F32),
        ],
        compiler_params=pltpu.CompilerParams(dimension_semantics=("arbitrary",),
                                             vmem_limit_bytes=VMEM_LIMIT_BYTES),
        name="merge_ffn",
    )(x.reshape(M, D), attn_t, u, vn, gates, ws_pairs, bs_full,
      wpa_b, wpb_b, wo_b, norm_ffn_g, wgu_b, wd_b, norm_final_g[None, :])
    return out.reshape(B, S, D)
```

```python
import functools
import math

import jax
import jax.numpy as jnp
import numpy as np
from jax import lax
from jax.experimental import pallas as pl
from jax.experimental.pallas import tpu as pltpu

HEAD_DIM = 64
N_Q_HEADS = 8
N_KV_HEADS = 2
Q_DIM = N_Q_HEADS * HEAD_DIM
KV_DIM = N_KV_HEADS * HEAD_DIM
GM_GROUPS = 8
GM_GROUP_DIM = 64
GM_DIM = GM_GROUPS * GM_GROUP_DIM
GM_CHUNK = 128
GRID_W = 64
ROPE_THETA = 10000.0
EPS = 1e-6

LANES = 128
SUBLANES = 8
VMEM_LIMIT_BYTES = 56 * 1024 * 1024

TM_IN = 256
TQ = 256
TK = 256
QK_SCALE_LOG2 = math.log2(math.e) / math.sqrt(HEAD_DIM)
MAX_FIXED_SHIFT = 48.0
CHUNKS_PER_BODY = 16
TM_OUT = 256
CAST_ROWS = 16
DOWN_CAST_ROWS = 256
FF_CHUNK = 768

F32 = jnp.float32
BF16 = jnp.bfloat16


def _dot(a, b):
    return jnp.dot(a, b, preferred_element_type=F32)


def _group_sum(x, gmat2):
    hi = x.astype(BF16)
    lo = (x - hi.astype(F32)).astype(BF16)
    return _dot(jnp.concatenate([hi, lo], axis=1), gmat2)


def _rope_partner(x, lane):
    fwd = pltpu.roll(x, LANES - 16, axis=1)
    bwd = pltpu.roll(x, 16, axis=1)
    return jnp.where((lane & 16) == 0, fwd, bwd)


def _inproj_kernel(x_ref, g_ref, wq_ref, w_ref, gmat_ref, cos_ref, sin_ref, qg_ref, kg_ref,
                   lng_ref, lnb_ref, *refs, cast_steps):
    n_cast = len(cast_steps)
    cast_in = refs[:n_cast]
    qt_ref, k_ref, vt_ref, u_ref, vn_ref, gate_ref = refs[n_cast:n_cast + 6]
    cast_out = refs[n_cast + 6:2 * n_cast + 6]
    ya_ref, yb_ref = refs[2 * n_cast + 6:]
    i = pl.program_id(0)
    tm = x_ref.shape[0]

    @pl.when(i == 0)
    def _():
        yb_ref[...] = jnp.zeros(yb_ref.shape, F32)

    for src, dst, n_steps in zip(cast_in, cast_out, cast_steps):
        @pl.when(i < n_steps)
        def _(src=src, dst=dst):
            dst[...] = src[...].astype(dst.dtype)

    def compute(y_w, y_r):
        x = x_ref[...]
        ms = jnp.mean(x * x, axis=-1, keepdims=True)
        h = (x * lax.rsqrt(ms + EPS) * g_ref[...]).astype(BF16)
        gmat = gmat_ref[...]
        lane = lax.broadcasted_iota(jnp.int32, (tm, LANES), 1)

        def project(c0, c1):
            y_w[:, c0:c1] = _dot(h, w_ref[:, c0:c1])

        y_w[:, :Q_DIM] = _dot(h, wq_ref[...])

        v0 = Q_DIM + KV_DIM
        uv0 = Q_DIM + 2 * KV_DIM
        g0 = uv0 + 2 * GM_DIM
        g1 = g0 + (w_ref.shape[1] - g0) // 2

        n_qk = (Q_DIM + KV_DIM) // LANES
        yqk = [y_r[:, j * LANES:(j + 1) * LANES] for j in range(n_qk)]
        project(Q_DIM, uv0)
        n_gate = w_ref.shape[1] - g0
        gate_ref[:, :n_gate // 2] = jax.nn.sigmoid(y_r[:, g0:g1])
        ss = [_group_sum(y * y, gmat) for y in yqk]
        project(uv0, g0)

        def norm_rope(y, s2, gain):
            y = y * lax.rsqrt(s2 * (1.0 / HEAD_DIM) + EPS) * gain
            return y * cos_ref[...] + _rope_partner(y, lane) * sin_ref[...]

        row = lax.broadcasted_iota(jnp.int32, (LANES, tm), 0)
        top = row < HEAD_DIM
        half_heads = N_Q_HEADS // 2
        for j in range(Q_DIM // LANES):
            qt = (norm_rope(yqk[j], ss[j], qg_ref[...]) * QK_SCALE_LOG2).T
            zero = jnp.zeros_like(qt)
            qt_ref[j] = jnp.where(top, qt, zero).astype(BF16)
            qt_ref[j + half_heads] = jnp.where(top, zero, qt).astype(BF16)
        k_ref[...] = norm_rope(yqk[n_qk - 1], ss[n_qk - 1], kg_ref[...]).astype(BF16)
        vt = y_r[:, v0:uv0].T.astype(BF16)
        for c in range(tm // TK):
            vt_ref[c] = vt[:, c * TK:(c + 1) * TK]

        gate_ref[:, n_gate // 2:] = jax.nn.sigmoid(y_r[:, g1:])
        u_ref[...] = jax.nn.gelu(y_r[:, uv0:uv0 + GM_DIM])
        n_gm = GM_DIM // LANES
        vv = [jax.nn.gelu(y_r[:, uv0 + GM_DIM + j * LANES:uv0 + GM_DIM + (j + 1) * LANES])
              for j in range(n_gm)]
        project(g0, g1)
        mu = [_group_sum(v, gmat) * (1.0 / GM_GROUP_DIM) for v in vv]
        gq = g1 + (w_ref.shape[1] - g1) // 2
        project(g1, gq)
        dv = [v - m for v, m in zip(vv, mu)]
        var = [_group_sum(d * d, gmat) * (1.0 / GM_GROUP_DIM) for d in dv]
        project(gq, w_ref.shape[1])
        for j in range(n_gm):
            sl = slice(j * LANES, (j + 1) * LANES)
            vn = dv[j] * lax.rsqrt(var[j] + EPS) * lng_ref[:, sl] + lnb_ref[:, sl]
            vn_ref[:, sl] = vn.astype(BF16)

    @pl.when(i % 2 == 0)
    def _():
        compute(ya_ref, yb_ref)

    @pl.when(i % 2 == 1)
    def _():
        compute(yb_ref, ya_ref)


def _sublane_partial_sum(e):
    rows, lanes = e.shape
    return jnp.sum(e.reshape(rows // SUBLANES, SUBLANES, lanes), axis=0)


def _attn_kernel(shift_ref, qt_ref, k_ref, vt_ref, o_ref, sa_ref, sb_ref, m_ref, l_ref, acc_ref,
                 *, fixed_shift):
    n_heads = qt_ref.shape[0]
    group = n_heads // N_KV_HEADS
    n_chunks = k_ref.shape[0] // TK
    if fixed_shift:
        shift = shift_ref[0]
    else:
        m_ref[...] = jnp.full(m_ref.shape, -jnp.inf, F32)
    acc_ref[...] = jnp.zeros(acc_ref.shape, F32)
    l_ref[...] = jnp.zeros(l_ref.shape, F32)

    def keys(c):
        return k_ref[pl.ds(pl.multiple_of(c * TK, TK), TK), :]

    def step(c, cur_ref, next_ref):
        if next_ref is not None:
            k_next = keys(c + 1)
        vt_c = vt_ref[c]
        for h in range(n_heads):
            g = h // group
            if next_ref is not None:
                next_ref[h] = _dot(k_next, qt_ref[h])
            s = cur_ref[h]
            vt_g = vt_c[g * HEAD_DIM:(g + 1) * HEAD_DIM, :]
            if fixed_shift:
                e = jnp.exp2(s - shift)
                l_ref[h] += _sublane_partial_sum(e)
                acc_ref[h] += _dot(vt_g, e.astype(BF16))
            else:
                m_old = m_ref[h]
                m_new = jnp.maximum(m_old, jnp.max(s, axis=0, keepdims=True))
                alpha = jnp.exp2(m_old - m_new)
                e = jnp.exp2(s - m_new)
                m_ref[h] = m_new
                l_ref[h] = alpha * l_ref[h] + _sublane_partial_sum(e)
                acc_ref[h] = alpha * acc_ref[h] + _dot(vt_g, e.astype(BF16))

    k_first = keys(0)
    for h in range(n_heads):
        sa_ref[h] = _dot(k_first, qt_ref[h])

    bufs = (sa_ref, sb_ref)

    def steps(c0, last):
        for j in range(CHUNKS_PER_BODY):
            final = last and j == CHUNKS_PER_BODY - 1
            step(c0 + j, bufs[j % 2], None if final else bufs[(j + 1) % 2])

    def body(i, carry):
        steps(CHUNKS_PER_BODY * i, False)
        return carry

    lax.fori_loop(0, n_chunks // CHUNKS_PER_BODY - 1, body, 0)
    steps(n_chunks - CHUNKS_PER_BODY, True)
    for h in range(n_heads):
        o = acc_ref[h] / jnp.sum(l_ref[h], axis=0, keepdims=True)
        o_ref[h * HEAD_DIM:(h + 1) * HEAD_DIM, :] = o.astype(BF16)


def _merge_ffn_kernel(x_ref, at_ref, u_ref, vn_ref, gate_ref, ws_ref, bs_ref,
                      wpa_ref, wpb_ref, wo_ref, gf_ref, wgu_ref, wd_ref, gfin_ref,
                      o_ref, gm_ref, act_ref, xa_ref, xb_ref):
    i = pl.program_id(0)
    tm = x_ref.shape[0]
    d_model = x_ref.shape[1]
    d_ff = wd_ref.shape[0]

    @pl.when(i == 0)
    def _():
        xb_ref[...] = jnp.zeros(xb_ref.shape, F32)

    def compute(x1_w, x1_r):
        x1p = x1_r[...]
        ms = jnp.mean(x1p * x1p, axis=-1, keepdims=True)
        h = (x1p * lax.rsqrt(ms + EPS) * gf_ref[...]).astype(BF16)

        lane = lax.broadcasted_iota(jnp.int32, (GM_CHUNK, LANES), 1)
        low = lane < GM_GROUP_DIM
        for c in range(tm // GM_CHUNK):
            rs = slice(c * GM_CHUNK, (c + 1) * GM_CHUNK)
            for j in range(GM_DIM // LANES):
                cs = slice(j * LANES, (j + 1) * LANES)
                vblk = vn_ref[rs, cs]
                zero = jnp.zeros_like(vblk)
                vsplit = jnp.concatenate([jnp.where(low, vblk, zero), jnp.where(low, zero, vblk)],
                                         axis=0)
                sv = _dot(ws_ref[j], vsplit)
                gm_ref[rs, cs] = (u_ref[rs, cs] * (sv + bs_ref[:, cs])).astype(BF16)

        pa = lax.dot_general(at_ref[...], wpa_ref[...], (((0,), (0,)), ((), ())),
                             preferred_element_type=F32)
        pb = _dot(gm_ref[...], wpb_ref[...])
        mix = (gate_ref[:, :d_model] * pa + gate_ref[:, d_model:] * pb).astype(BF16)

        for c0 in range(0, d_ff, FF_CHUNK):
            c1 = min(c0 + FF_CHUNK, d_ff)
            gt = _dot(h, wgu_ref[:, c0:c1])
            up = _dot(h, wgu_ref[:, d_ff + c0:d_ff + c1])
            act_ref[:, c0:c1] = (jax.nn.silu(gt) * up).astype(BF16)
        x2 = x1p + _dot(act_ref[...], wd_ref[...])
        ms2 = jnp.mean(x2 * x2, axis=-1, keepdims=True)
        o_ref[...] = x2 * lax.rsqrt(ms2 + EPS) * gfin_ref[...]

        x1_w[...] = x_ref[...] + _dot(mix, wo_ref[...])

    @pl.when(i % 2 == 0)
    def _():
        compute(xa_ref, xb_ref)

    @pl.when(i % 2 == 1)
    def _():
        compute(xb_ref, xa_ref)


def _const_spec(shape):
    nd = len(shape)
    return pl.BlockSpec(shape, lambda *_: (0,) * nd, pipeline_mode=pl.Buffered(1))


def _rope_tables(seq):
    half = HEAD_DIM // 2
    rows = seq // GRID_W
    row = np.repeat(np.arange(rows, dtype=np.float64), GRID_W)
    col = np.tile(np.arange(GRID_W, dtype=np.float64), rows)
    inv_freq = 1.0 / (ROPE_THETA ** (np.arange(0, half, 2, dtype=np.float64) / half))
    ang_r = row[:, None] * inv_freq[None, :]
    ang_c = col[:, None] * inv_freq[None, :]
    cr, sr = np.cos(ang_r), np.sin(ang_r)
    cc, sc = np.cos(ang_c), np.sin(ang_c)
    cos64 = np.concatenate([cr, cr, cc, cc], axis=-1)
    sin64 = np.concatenate([-sr, sr, -sc, sc], axis=-1)
    reps = (1, LANES // HEAD_DIM)
    return np.tile(cos64, reps).astype(np.float32), np.tile(sin64, reps).astype(np.float32)


def kernel(x, norm_mix_g, w_in, q_norm_g, k_norm_g, gm_ln_g, gm_ln_b, w_s, b_s, w_proj_a,
           w_proj_b, w_out, norm_ffn_g, w_gate_up, w_down, norm_final_g):
    B, S, D = x.shape
    in_dim = w_in.shape[-1]
    d_ff = w_down.shape[1]
    assert norm_mix_g.shape[0] == 1, "single-layer block"
    assert CHUNKS_PER_BODY % 2 == 0 and S % (CHUNKS_PER_BODY * TK) == 0
    assert S % TQ == 0 and S % TM_IN == 0 and S % TM_OUT == 0
    assert TM_OUT % GM_CHUNK == 0 and TM_IN % TK == 0 and TM_IN == TQ and TM_OUT == TQ

    w_in_b = w_in[0].astype(BF16)
    half_heads = N_Q_HEADS // 2
    w_q_b = (w_in_b[:, :Q_DIM].reshape(D, 2, half_heads, HEAD_DIM)
             .transpose(0, 2, 1, 3).reshape(D, Q_DIM))

    cos_t, sin_t = _rope_tables(S)
    reps = LANES // HEAD_DIM
    qg = jnp.tile(q_norm_g[0], reps)[None, :]
    kg = jnp.tile(k_norm_g[0], reps)[None, :]
    lng = gm_ln_g[0].reshape(1, GM_DIM)
    lnb = gm_ln_b[0].reshape(1, GM_DIM)
    gid = np.arange(LANES) // HEAD_DIM
    gmat = (gid[:, None] == gid[None, :]).astype(np.float32)
    gmat2 = jnp.asarray(np.concatenate([gmat, gmat], axis=0), dtype=BF16)

    two_axes = ("arbitrary", "arbitrary")
    cparams = pltpu.CompilerParams(dimension_semantics=two_axes,
                                   vmem_limit_bytes=VMEM_LIMIT_BYTES)

    M = B * S
    n_seq = S // TM_IN
    n_blk = M // TM_IN
    ck_in = TM_IN // TK

    def in_blk(i):
        return jnp.minimum(i, n_blk - 1)

    def out_blk(i):
        return jnp.maximum(i - 1, 0)

    def out_rows(width):
        return pl.BlockSpec((TM_IN, width), lambda i: (out_blk(i), 0))

    to_cast = [(w_gate_up, n_blk // 4), (w_out, n_blk // 4), (w_down, d_ff // DOWN_CAST_ROWS),
               (w_proj_a, w_proj_a.shape[1] // CAST_ROWS), (w_proj_b, w_proj_b.shape[1] // CAST_ROWS)]
    cast_steps = tuple(n for _, n in to_cast)
    assert all(n <= n_blk and w.shape[1] % n == 0 for w, n in to_cast)
    cast_specs = [pl.BlockSpec((None, w.shape[1] // n, w.shape[2]),
                               lambda i, n=n: (0, jnp.minimum(i, n - 1), 0)) for w, n in to_cast]

    qt, k, vt, u, vn, gates, wgu_b, wo_b, wd_b, wpa_b, wpb_b = pl.pallas_call(
        functools.partial(_inproj_kernel, cast_steps=cast_steps),
        grid=(n_blk + 1,),
        in_specs=[
            pl.BlockSpec((TM_IN, D), lambda i: (in_blk(i), 0)),
            _const_spec((1, D)),
            _const_spec((D, Q_DIM)),
            _const_spec((D, in_dim)),
            _const_spec((2 * LANES, LANES)),
            pl.BlockSpec((TM_IN, LANES), lambda i: (out_blk(i) % n_seq, 0)),
            pl.BlockSpec((TM_IN, LANES), lambda i: (out_blk(i) % n_seq, 0)),
            _const_spec((1, LANES)),
            _const_spec((1, LANES)),
            _const_spec((1, GM_DIM)),
            _const_spec((1, GM_DIM)),
        ] + cast_specs,
        out_specs=[
            pl.BlockSpec((None, N_Q_HEADS, LANES, TM_IN), lambda i: (out_blk(i), 0, 0, 0)),
            out_rows(KV_DIM),
            pl.BlockSpec((ck_in, KV_DIM, TK), lambda i: (out_blk(i), 0, 0)),
            out_rows(GM_DIM), out_rows(GM_DIM), out_rows(2 * D),
        ] + cast_specs,
        out_shape=[
            jax.ShapeDtypeStruct((n_blk, N_Q_HEADS, LANES, TM_IN), BF16),
            jax.ShapeDtypeStruct((M, KV_DIM), BF16),
            jax.ShapeDtypeStruct((M // TK, KV_DIM, TK), BF16),
            jax.ShapeDtypeStruct((M, GM_DIM), F32),
            jax.ShapeDtypeStruct((M, GM_DIM), BF16),
            jax.ShapeDtypeStruct((M, 2 * D), F32),
        ] + [jax.ShapeDtypeStruct(w.shape, BF16) for w, _ in to_cast],
        scratch_shapes=[pltpu.VMEM((TM_IN, in_dim), F32), pltpu.VMEM((TM_IN, in_dim), F32)],
        compiler_params=pltpu.CompilerParams(dimension_semantics=("arbitrary",),
                                             vmem_limit_bytes=VMEM_LIMIT_BYTES),
        name="inproj",
    )(x.reshape(M, D), norm_mix_g, w_q_b, w_in_b, gmat2, cos_t, sin_t, qg, kg, lng, lnb,
      *[w for w, _ in to_cast])
    k = k.reshape(B, S, KV_DIM)
    vt = vt.reshape(B, S // TK, KV_DIM, TK)

    n_q = S // TQ

    def attention(fixed_shift):
        return pl.pallas_call(
            functools.partial(_attn_kernel, fixed_shift=fixed_shift),
            grid=(B, n_q),
            in_specs=[
                pl.BlockSpec(memory_space=pltpu.SMEM),
                pl.BlockSpec((None, N_Q_HEADS, LANES, TQ), lambda b, i: (b * n_q + i, 0, 0, 0)),
                pl.BlockSpec((None, S, KV_DIM), lambda b, i: (b, 0, 0)),
                pl.BlockSpec((None, S // TK, KV_DIM, TK), lambda b, i: (b, 0, 0, 0)),
            ],
            out_specs=pl.BlockSpec((None, Q_DIM, TQ), lambda b, i: (b * n_q + i, 0, 0)),
            out_shape=jax.ShapeDtypeStruct((B * n_q, Q_DIM, TQ), BF16),
            scratch_shapes=[
                pltpu.VMEM((N_Q_HEADS, TK, TQ), F32),
                pltpu.VMEM((N_Q_HEADS, TK, TQ), F32),
                pltpu.VMEM((N_Q_HEADS, 1, TQ), F32),
                pltpu.VMEM((N_Q_HEADS, SUBLANES, TQ), F32),
                pltpu.VMEM((N_Q_HEADS, HEAD_DIM, TQ), F32),
            ],
            compiler_params=cparams,
            name="gqa_attention_shift" if fixed_shift else "gqa_attention_online",
        )

    score_bound = (HEAD_DIM * QK_SCALE_LOG2 * (1.0 + 2.0 ** -6)
                   * jnp.max(jnp.abs(q_norm_g[0])) * jnp.max(jnp.abs(k_norm_g[0])))
    attn_t = lax.cond(
        score_bound <= MAX_FIXED_SHIFT,
        lambda s_, *ops: attention(True)(s_, *ops),
        lambda s_, *ops: attention(False)(s_, *ops),
        score_bound.reshape(1).astype(F32), qt, k, vt)

    bs_full = jnp.repeat(jnp.transpose(b_s[0]), GM_GROUP_DIM, axis=1)
    ws_pairs = (w_s[0].reshape(GM_GROUPS // 2, 2, GM_CHUNK, GM_CHUNK)
                .transpose(0, 2, 1, 3).reshape(GM_GROUPS // 2, GM_CHUNK, 2 * GM_CHUNK).astype(BF16))
    n_out = M // TM_OUT

    def s1_blk(i):
        return jnp.minimum(i, n_out - 1)

    def s1_rows(width):
        return pl.BlockSpec((TM_OUT, width), lambda i: (s1_blk(i), 0))

    out = pl.pallas_call(
        _merge_ffn_kernel,
        grid=(n_out + 1,),
        in_specs=[
            s1_rows(D),
            pl.BlockSpec((None, Q_DIM, TM_OUT), lambda i: (s1_blk(i), 0, 0)),
            s1_rows(GM_DIM), s1_rows(GM_DIM), s1_rows(2 * D),
            _const_spec((GM_GROUPS // 2, GM_CHUNK, 2 * GM_CHUNK)),
            _const_spec((GM_CHUNK, GM_DIM)),
            _const_spec((None, Q_DIM, D)),
            _const_spec((None, GM_DIM, D)),
            _const_spec((None, D, D)),
            _const_spec((1, D)),
            _const_spec((None, D, 2 * d_ff)),
            _const_spec((None, d_ff, D)),
            _const_spec((1, D)),
        ],
        out_specs=pl.BlockSpec((TM_OUT, D), lambda i: (jnp.maximum(i - 1, 0), 0)),
        out_shape=jax.ShapeDtypeStruct((M, D), F32),
        scratch_shapes=[
            pltpu.VMEM((TM_OUT, GM_DIM), BF16),
            pltpu.VMEM((TM_OUT, d_ff), BF16),
            pltpu.VMEM((TM_OUT, D), F32),
            pltpu.VMEM((TM_OUT, D), F32),
        ],
        compiler_params=pltpu.CompilerParams(dimension_semantics=("arbitrary",),
                                             vmem_limit_bytes=VMEM_LIMIT_BYTES),
        name="merge_ffn",
    )(x.reshape(M, D), attn_t, u, vn, gates, ws_pairs, bs_full,
      wpa_b, wpb_b, wo_b, norm_ffn_g, wgu_b, wd_b, norm_final_g[None, :])
    return out.reshape(B, S, D)
```

```python
import functools
import math

import jax
import jax.numpy as jnp
import numpy as np
from jax import lax
from jax.experimental import pallas as pl
from jax.experimental.pallas import tpu as pltpu

HEAD_DIM = 64
N_Q_HEADS = 8
N_KV_HEADS = 2
Q_DIM = N_Q_HEADS * HEAD_DIM
KV_DIM = N_KV_HEADS * HEAD_DIM
GM_GROUPS = 8
GM_GROUP_DIM = 64
GM_DIM = GM_GROUPS * GM_GROUP_DIM
GM_CHUNK = 128
GRID_W = 64
ROPE_THETA = 10000.0
EPS = 1e-6

LANES = 128
SUBLANES = 8
VMEM_LIMIT_BYTES = 56 * 1024 * 1024

TM_IN = 256
TQ = 256
TK = 256
QK_SCALE_LOG2 = math.log2(math.e) / math.sqrt(HEAD_DIM)
MAX_FIXED_SHIFT = 48.0
CHUNKS_PER_BODY = 16
TM_OUT = 256
CAST_ROWS = 16
DOWN_CAST_ROWS = 256
FF_CHUNK = 768

F32 = jnp.float32
BF16 = jnp.bfloat16


def _dot(a, b):
    return jnp.dot(a, b, preferred_element_type=F32)


def _group_sum(x, gmat2):
    hi = x.astype(BF16)
    lo = (x - hi.astype(F32)).astype(BF16)
    return _dot(jnp.concatenate([hi, lo], axis=1), gmat2)


def _rope_partner(x, lane):
    fwd = pltpu.roll(x, LANES - 16, axis=1)
    bwd = pltpu.roll(x, 16, axis=1)
    return jnp.where((lane & 16) == 0, fwd, bwd)


def _inproj_kernel(x_ref, g_ref, wq_ref, w_ref, gmat_ref, cos_ref, sin_ref, qg_ref, kg_ref,
                   lng_ref, lnb_ref, *refs, cast_steps):
    n_cast = len(cast_steps)
    cast_in = refs[:n_cast]
    qt_ref, k_ref, vt_ref, u_ref, vn_ref, gate_ref = refs[n_cast:n_cast + 6]
    cast_out = refs[n_cast + 6:2 * n_cast + 6]
    ya_ref, yb_ref = refs[2 * n_cast + 6:]
    i = pl.program_id(0)
    tm = x_ref.shape[0]

    @pl.when(i == 0)
    def _():
        yb_ref[...] = jnp.zeros(yb_ref.shape, F32)

    for src, dst, n_steps in zip(cast_in, cast_out, cast_steps):
        @pl.when(i < n_steps)
        def _(src=src, dst=dst):
            dst[...] = src[...].astype(dst.dtype)

    def compute(y_w, y_r):
        x = x_ref[...]
        ms = jnp.mean(x * x, axis=-1, keepdims=True)
        h = (x * lax.rsqrt(ms + EPS) * g_ref[...]).astype(BF16)
        gmat = gmat_ref[...]
        lane = lax.broadcasted_iota(jnp.int32, (tm, LANES), 1)

        def project(c0, c1):
            y_w[:, c0:c1] = _dot(h, w_ref[:, c0:c1])

        v0 = Q_DIM + KV_DIM
        uv0 = Q_DIM + 2 * KV_DIM
        g0 = uv0 + 2 * GM_DIM
        g1 = g0 + (w_ref.shape[1] - g0) // 2

        n_qk = (Q_DIM + KV_DIM) // LANES
        yqk = [y_r[:, j * LANES:(j + 1) * LANES] for j in range(n_qk)]
        project(Q_DIM, uv0)
        n_gate = w_ref.shape[1] - g0
        gate_ref[:, :n_gate // 2] = jax.nn.sigmoid(y_r[:, g0:g1])
        ss = [_group_sum(y * y, gmat) for y in yqk]
        project(uv0, g0)

        def norm_rope(y, s2, gain):
            y = y * lax.rsqrt(s2 * (1.0 / HEAD_DIM) + EPS) * gain
            return y * cos_ref[...] + _rope_partner(y, lane) * sin_ref[...]

        row = lax.broadcasted_iota(jnp.int32, (LANES, tm), 0)
        top = row < HEAD_DIM
        half_heads = N_Q_HEADS // 2
        for j in range(Q_DIM // LANES):
            qt = (norm_rope(yqk[j], ss[j], qg_ref[...]) * QK_SCALE_LOG2).T
            zero = jnp.zeros_like(qt)
            qt_ref[j] = jnp.where(top, qt, zero).astype(BF16)
            qt_ref[j + half_heads] = jnp.where(top, zero, qt).astype(BF16)
        k_ref[...] = norm_rope(yqk[n_qk - 1], ss[n_qk - 1], kg_ref[...]).astype(BF16)
        vt = y_r[:, v0:uv0].T.astype(BF16)
        for c in range(tm // TK):
            vt_ref[c] = vt[:, c * TK:(c + 1) * TK]

        gate_ref[:, n_gate // 2:] = jax.nn.sigmoid(y_r[:, g1:])
        u_ref[...] = jax.nn.gelu(y_r[:, uv0:uv0 + GM_DIM])
        n_gm = GM_DIM // LANES
        vv = [jax.nn.gelu(y_r[:, uv0 + GM_DIM + j * LANES:uv0 + GM_DIM + (j + 1) * LANES])
              for j in range(n_gm)]
        project(g0, g1)
        mu = [_group_sum(v, gmat) * (1.0 / GM_GROUP_DIM) for v in vv]
        gq = g1 + (w_ref.shape[1] - g1) // 2
        project(g1, gq)
        dv = [v - m for v, m in zip(vv, mu)]
        var = [_group_sum(d * d, gmat) * (1.0 / GM_GROUP_DIM) for d in dv]
        project(gq, w_ref.shape[1])
        y_w[:, :Q_DIM] = _dot(h, wq_ref[...])
        for j in range(n_gm):
            sl = slice(j * LANES, (j + 1) * LANES)
            vn = dv[j] * lax.rsqrt(var[j] + EPS) * lng_ref[:, sl] + lnb_ref[:, sl]
            vn_ref[:, sl] = vn.astype(BF16)

    @pl.when(i % 2 == 0)
    def _():
        compute(ya_ref, yb_ref)

    @pl.when(i % 2 == 1)
    def _():
        compute(yb_ref, ya_ref)


def _sublane_partial_sum(e):
    rows, lanes = e.shape
    return jnp.sum(e.reshape(rows // SUBLANES, SUBLANES, lanes), axis=0)


def _attn_kernel(shift_ref, qt_ref, k_ref, vt_ref, o_ref, sa_ref, sb_ref, m_ref, l_ref, acc_ref,
                 *, fixed_shift):
    n_heads = qt_ref.shape[0]
    group = n_heads // N_KV_HEADS
    n_chunks = k_ref.shape[0] // TK
    if fixed_shift:
        shift = shift_ref[0]
    else:
        m_ref[...] = jnp.full(m_ref.shape, -jnp.inf, F32)
    acc_ref[...] = jnp.zeros(acc_ref.shape, F32)
    l_ref[...] = jnp.zeros(l_ref.shape, F32)

    def keys(c):
        return k_ref[pl.ds(pl.multiple_of(c * TK, TK), TK), :]

    def step(c, cur_ref, next_ref):
        if next_ref is not None:
            k_next = keys(c + 1)
        vt_c = vt_ref[c]
        for h in range(n_heads):
            g = h // group
            if next_ref is not None:
                next_ref[h] = _dot(k_next, qt_ref[h])
            s = cur_ref[h]
            vt_g = vt_c[g * HEAD_DIM:(g + 1) * HEAD_DIM, :]
            if fixed_shift:
                e = jnp.exp2(s - shift)
                l_ref[h] += _sublane_partial_sum(e)
                acc_ref[h] += _dot(vt_g, e.astype(BF16))
            else:
                m_old = m_ref[h]
                m_new = jnp.maximum(m_old, jnp.max(s, axis=0, keepdims=True))
                alpha = jnp.exp2(m_old - m_new)
                e = jnp.exp2(s - m_new)
                m_ref[h] = m_new
                l_ref[h] = alpha * l_ref[h] + _sublane_partial_sum(e)
                acc_ref[h] = alpha * acc_ref[h] + _dot(vt_g, e.astype(BF16))

    k_first = keys(0)
    for h in range(n_heads):
        sa_ref[h] = _dot(k_first, qt_ref[h])

    bufs = (sa_ref, sb_ref)

    def steps(c0, last):
        for j in range(CHUNKS_PER_BODY):
            final = last and j == CHUNKS_PER_BODY - 1
            step(c0 + j, bufs[j % 2], None if final else bufs[(j + 1) % 2])

    def body(i, carry):
        steps(CHUNKS_PER_BODY * i, False)
        return carry

    lax.fori_loop(0, n_chunks // CHUNKS_PER_BODY - 1, body, 0)
    steps(n_chunks - CHUNKS_PER_BODY, True)
    for h in range(n_heads):
        o = acc_ref[h] / jnp.sum(l_ref[h], axis=0, keepdims=True)
        o_ref[h * HEAD_DIM:(h + 1) * HEAD_DIM, :] = o.astype(BF16)


def _merge_ffn_kernel(x_ref, at_ref, u_ref, vn_ref, gate_ref, ws_ref, bs_ref,
                      wpa_ref, wpb_ref, wo_ref, gf_ref, wgu_ref, wd_ref, gfin_ref,
                      o_ref, gm_ref, act_ref, xa_ref, xb_ref):
    i = pl.program_id(0)
    tm = x_ref.shape[0]
    d_model = x_ref.shape[1]
    d_ff = wd_ref.shape[0]

    @pl.when(i == 0)
    def _():
        xb_ref[...] = jnp.zeros(xb_ref.shape, F32)

    def compute(x1_w, x1_r):
        x1p = x1_r[...]
        ms = jnp.mean(x1p * x1p, axis=-1, keepdims=True)
        h = (x1p * lax.rsqrt(ms + EPS) * gf_ref[...]).astype(BF16)

        lane = lax.broadcasted_iota(jnp.int32, (GM_CHUNK, LANES), 1)
        low = lane < GM_GROUP_DIM
        for c in range(tm // GM_CHUNK):
            rs = slice(c * GM_CHUNK, (c + 1) * GM_CHUNK)
            for j in range(GM_DIM // LANES):
                cs = slice(j * LANES, (j + 1) * LANES)
                vblk = vn_ref[rs, cs]
                zero = jnp.zeros_like(vblk)
                vsplit = jnp.concatenate([jnp.where(low, vblk, zero), jnp.where(low, zero, vblk)],
                                         axis=0)
                sv = _dot(ws_ref[j], vsplit)
                gm_ref[rs, cs] = (u_ref[rs, cs] * (sv + bs_ref[:, cs])).astype(BF16)

        pa = lax.dot_general(at_ref[...], wpa_ref[...], (((0,), (0,)), ((), ())),
                             preferred_element_type=F32)
        pb = _dot(gm_ref[...], wpb_ref[...])
        mix = (gate_ref[:, :d_model] * pa + gate_ref[:, d_model:] * pb).astype(BF16)

        for c0 in range(0, d_ff, FF_CHUNK):
            c1 = min(c0 + FF_CHUNK, d_ff)
            gt = _dot(h, wgu_ref[:, c0:c1])
            up = _dot(h, wgu_ref[:, d_ff + c0:d_ff + c1])
            act_ref[:, c0:c1] = (jax.nn.silu(gt) * up).astype(BF16)
        x2 = x1p + _dot(act_ref[...], wd_ref[...])
        ms2 = jnp.mean(x2 * x2, axis=-1, keepdims=True)
        o_ref[...] = x2 * lax.rsqrt(ms2 + EPS) * gfin_ref[...]

        x1_w[...] = x_ref[...] + _dot(mix, wo_ref[...])

    @pl.when(i % 2 == 0)
    def _():
        compute(xa_ref, xb_ref)

    @pl.when(i % 2 == 1)
    def _():
        compute(xb_ref, xa_ref)


def _const_spec(shape):
    nd = len(shape)
    return pl.BlockSpec(shape, lambda *_: (0,) * nd, pipeline_mode=pl.Buffered(1))


def _rope_tables(seq):
    half = HEAD_DIM // 2
    rows = seq // GRID_W
    row = np.repeat(np.arange(rows, dtype=np.float64), GRID_W)
    col = np.tile(np.arange(GRID_W, dtype=np.float64), rows)
    inv_freq = 1.0 / (ROPE_THETA ** (np.arange(0, half, 2, dtype=np.float64) / half))
    ang_r = row[:, None] * inv_freq[None, :]
    ang_c = col[:, None] * inv_freq[None, :]
    cr, sr = np.cos(ang_r), np.sin(ang_r)
    cc, sc = np.cos(ang_c), np.sin(ang_c)
    cos64 = np.concatenate([cr, cr, cc, cc], axis=-1)
    sin64 = np.concatenate([-sr, sr, -sc, sc], axis=-1)
    reps = (1, LANES // HEAD_DIM)
    return np.tile(cos64, reps).astype(np.float32), np.tile(sin64, reps).astype(np.float32)


def kernel(x, norm_mix_g, w_in, q_norm_g, k_norm_g, gm_ln_g, gm_ln_b, w_s, b_s, w_proj_a,
           w_proj_b, w_out, norm_ffn_g, w_gate_up, w_down, norm_final_g):
    B, S, D = x.shape
    in_dim = w_in.shape[-1]
    d_ff = w_down.shape[1]
    assert norm_mix_g.shape[0] == 1, "single-layer block"
    assert CHUNKS_PER_BODY % 2 == 0 and S % (CHUNKS_PER_BODY * TK) == 0
    assert S % TQ == 0 and S % TM_IN == 0 and S % TM_OUT == 0
    assert TM_OUT % GM_CHUNK == 0 and TM_IN % TK == 0 and TM_IN == TQ and TM_OUT == TQ

    w_in_b = w_in[0].astype(BF16)
    half_heads = N_Q_HEADS // 2
    w_q_b = (w_in_b[:, :Q_DIM].reshape(D, 2, half_heads, HEAD_DIM)
             .transpose(0, 2, 1, 3).reshape(D, Q_DIM))

    cos_t, sin_t = _rope_tables(S)
    reps = LANES // HEAD_DIM
    qg = jnp.tile(q_norm_g[0], reps)[None, :]
    kg = jnp.tile(k_norm_g[0], reps)[None, :]
    lng = gm_ln_g[0].reshape(1, GM_DIM)
    lnb = gm_ln_b[0].reshape(1, GM_DIM)
    gid = np.arange(LANES) // HEAD_DIM
    gmat = (gid[:, None] == gid[None, :]).astype(np.float32)
    gmat2 = jnp.asarray(np.concatenate([gmat, gmat], axis=0), dtype=BF16)

    two_axes = ("arbitrary", "arbitrary")
    cparams = pltpu.CompilerParams(dimension_semantics=two_axes,
                                   vmem_limit_bytes=VMEM_LIMIT_BYTES)

    M = B * S
    n_seq = S // TM_IN
    n_blk = M // TM_IN
    ck_in = TM_IN // TK

    def in_blk(i):
        return jnp.minimum(i, n_blk - 1)

    def out_blk(i):
        return jnp.maximum(i - 1, 0)

    def out_rows(width):
        return pl.BlockSpec((TM_IN, width), lambda i: (out_blk(i), 0))

    to_cast = [(w_gate_up, n_blk), (w_out, n_blk), (w_down, d_ff // DOWN_CAST_ROWS),
               (w_proj_a, w_proj_a.shape[1] // CAST_ROWS), (w_proj_b, w_proj_b.shape[1] // CAST_ROWS)]
    cast_steps = tuple(n for _, n in to_cast)
    assert all(n <= n_blk and w.shape[1] % n == 0 for w, n in to_cast)
    cast_specs = [pl.BlockSpec((None, w.shape[1] // n, w.shape[2]),
                               lambda i, n=n: (0, jnp.minimum(i, n - 1), 0)) for w, n in to_cast]

    qt, k, vt, u, vn, gates, wgu_b, wo_b, wd_b, wpa_b, wpb_b = pl.pallas_call(
        functools.partial(_inproj_kernel, cast_steps=cast_steps),
        grid=(n_blk + 1,),
        in_specs=[
            pl.BlockSpec((TM_IN, D), lambda i: (in_blk(i), 0)),
            _const_spec((1, D)),
            _const_spec((D, Q_DIM)),
            _const_spec((D, in_dim)),
            _const_spec((2 * LANES, LANES)),
            pl.BlockSpec((TM_IN, LANES), lambda i: (out_blk(i) % n_seq, 0)),
            pl.BlockSpec((TM_IN, LANES), lambda i: (out_blk(i) % n_seq, 0)),
            _const_spec((1, LANES)),
            _const_spec((1, LANES)),
            _const_spec((1, GM_DIM)),
            _const_spec((1, GM_DIM)),
        ] + cast_specs,
        out_specs=[
            pl.BlockSpec((None, N_Q_HEADS, LANES, TM_IN), lambda i: (out_blk(i), 0, 0, 0)),
            out_rows(KV_DIM),
            pl.BlockSpec((ck_in, KV_DIM, TK), lambda i: (out_blk(i), 0, 0)),
            out_rows(GM_DIM), out_rows(GM_DIM), out_rows(2 * D),
        ] + cast_specs,
        out_shape=[
            jax.ShapeDtypeStruct((n_blk, N_Q_HEADS, LANES, TM_IN), BF16),
            jax.ShapeDtypeStruct((M, KV_DIM), BF16),
            jax.ShapeDtypeStruct((M // TK, KV_DIM, TK), BF16),
            jax.ShapeDtypeStruct((M, GM_DIM), F32),
            jax.ShapeDtypeStruct((M, GM_DIM), BF16),
            jax.ShapeDtypeStruct((M, 2 * D), F32),
        ] + [jax.ShapeDtypeStruct(w.shape, BF16) for w, _ in to_cast],
        scratch_shapes=[pltpu.VMEM((TM_IN, in_dim), F32), pltpu.VMEM((TM_IN, in_dim), F32)],
        compiler_params=pltpu.CompilerParams(dimension_semantics=("arbitrary",),
                                             vmem_limit_bytes=VMEM_LIMIT_BYTES),
        name="inproj",
    )(x.reshape(M, D), norm_mix_g, w_q_b, w_in_b, gmat2, cos_t, sin_t, qg, kg, lng, lnb,
      *[w for w, _ in to_cast])
    k = k.reshape(B, S, KV_DIM)
    vt = vt.reshape(B, S // TK, KV_DIM, TK)

    n_q = S // TQ

    def attention(fixed_shift):
        return pl.pallas_call(
            functools.partial(_attn_kernel, fixed_shift=fixed_shift),
            grid=(B, n_q),
            in_specs=[
                pl.BlockSpec(memory_space=pltpu.SMEM),
                pl.BlockSpec((None, N_Q_HEADS, LANES, TQ), lambda b, i: (b * n_q + i, 0, 0, 0)),
                pl.BlockSpec((None, S, KV_DIM), lambda b, i: (b, 0, 0)),
                pl.BlockSpec((None, S // TK, KV_DIM, TK), lambda b, i: (b, 0, 0, 0)),
            ],
            out_specs=pl.BlockSpec((None, Q_DIM, TQ), lambda b, i: (b * n_q + i, 0, 0)),
            out_shape=jax.ShapeDtypeStruct((B * n_q, Q_DIM, TQ), BF16),
            scratch_shapes=[
                pltpu.VMEM((N_Q_HEADS, TK, TQ), F32),
                pltpu.VMEM((N_Q_HEADS, TK, TQ), F32),
                pltpu.VMEM((N_Q_HEADS, 1, TQ), F32),
                pltpu.VMEM((N_Q_HEADS, SUBLANES, TQ), F32),
                pltpu.VMEM((N_Q_HEADS, HEAD_DIM, TQ), F32),
            ],
            compiler_params=cparams,
            name="gqa_attention_shift" if fixed_shift else "gqa_attention_online",
        )

    score_bound = (HEAD_DIM * QK_SCALE_LOG2 * (1.0 + 2.0 ** -6)
                   * jnp.max(jnp.abs(q_norm_g[0])) * jnp.max(jnp.abs(k_norm_g[0])))
    attn_t = lax.cond(
        score_bound <= MAX_FIXED_SHIFT,
        lambda s_, *ops: attention(True)(s_, *ops),
        lambda s_, *ops: attention(False)(s_, *ops),
        score_bound.reshape(1).astype(F32), qt, k, vt)

    bs_full = jnp.repeat(jnp.transpose(b_s[0]), GM_GROUP_DIM, axis=1)
    ws_pairs = (w_s[0].reshape(GM_GROUPS // 2, 2, GM_CHUNK, GM_CHUNK)
                .transpose(0, 2, 1, 3).reshape(GM_GROUPS // 2, GM_CHUNK, 2 * GM_CHUNK).astype(BF16))
    n_out = M // TM_OUT

    def s1_blk(i):
        return jnp.minimum(i, n_out - 1)

    def s1_rows(width):
        return pl.BlockSpec((TM_OUT, width), lambda i: (s1_blk(i), 0))

    out = pl.pallas_call(
        _merge_ffn_kernel,
        grid=(n_out + 1,),
        in_specs=[
            s1_rows(D),
            pl.BlockSpec((None, Q_DIM, TM_OUT), lambda i: (s1_blk(i), 0, 0)),
            s1_rows(GM_DIM), s1_rows(GM_DIM), s1_rows(2 * D),
            _const_spec((GM_GROUPS // 2, GM_CHUNK, 2 * GM_CHUNK)),
            _const_spec((GM_CHUNK, GM_DIM)),
            _const_spec((None, Q_DIM, D)),
            _const_spec((None, GM_DIM, D)),
            _const_spec((None, D, D)),
            _const_spec((1, D)),
            _const_spec((None, D, 2 * d_ff)),
            _const_spec((None, d_ff, D)),
            _const_spec((1, D)),
        ],
        out_specs=pl.BlockSpec((TM_OUT, D), lambda i: (jnp.maximum(i - 1, 0), 0)),
        out_shape=jax.ShapeDtypeStruct((M, D), F32),
        scratch_shapes=[
            pltpu.VMEM((TM_OUT, GM_DIM), BF16),
            pltpu.VMEM((TM_OUT, d_ff), BF16),
            pltpu.VMEM((TM_OUT, D), F32),
            pltpu.VMEM((TM_OUT, D), F32),
        ],
        compiler_params=pltpu.CompilerParams(dimension_semantics=("arbitrary",),
                                             vmem_limit_bytes=VMEM_LIMIT_BYTES),
        name="merge_ffn",
    )(x.reshape(M, D), attn_t, u, vn, gates, ws_pairs, bs_full,
      wpa_b, wpb_b, wo_b, norm_ffn_g, wgu_b, wd_b, norm_final_g[None, :])
    return out.reshape(B, S, D)
```

```python
import functools
import math

import jax
import jax.numpy as jnp
import numpy as np
from jax import lax
from jax.experimental import pallas as pl
from jax.experimental.pallas import tpu as pltpu

HEAD_DIM = 64
N_Q_HEADS = 8
N_KV_HEADS = 2
Q_DIM = N_Q_HEADS * HEAD_DIM
KV_DIM = N_KV_HEADS * HEAD_DIM
GM_GROUPS = 8
GM_GROUP_DIM = 64
GM_DIM = GM_GROUPS * GM_GROUP_DIM
GM_CHUNK = 128
GRID_W = 64
ROPE_THETA = 10000.0
EPS = 1e-6

LANES = 128
SUBLANES = 8
VMEM_LIMIT_BYTES = 56 * 1024 * 1024

TM_IN = 256
TQ = 256
TK = 256
QK_SCALE_LOG2 = math.log2(math.e) / math.sqrt(HEAD_DIM)
MAX_FIXED_SHIFT = 48.0
CHUNKS_PER_BODY = 16
TM_OUT = 256
CAST_ROWS = 16
DOWN_CAST_ROWS = 256
FF_CHUNK = 768

F32 = jnp.float32
BF16 = jnp.bfloat16


def _dot(a, b):
    return jnp.dot(a, b, preferred_element_type=F32)


def _group_sum(x, gmat2):
    hi = x.astype(BF16)
    lo = (x - hi.astype(F32)).astype(BF16)
    return _dot(jnp.concatenate([hi, lo], axis=1), gmat2)


def _rope_partner(x, lane):
    fwd = pltpu.roll(x, LANES - 16, axis=1)
    bwd = pltpu.roll(x, 16, axis=1)
    return jnp.where((lane & 16) == 0, fwd, bwd)


def _inproj_kernel(x_ref, g_ref, wq_ref, w_ref, gmat_ref, cos_ref, sin_ref, qg_ref, kg_ref,
                   lng_ref, lnb_ref, *refs, cast_steps):
    n_cast = len(cast_steps)
    cast_in = refs[:n_cast]
    qt_ref, k_ref, vt_ref, u_ref, vn_ref, gate_ref = refs[n_cast:n_cast + 6]
    cast_out = refs[n_cast + 6:2 * n_cast + 6]
    ya_ref, yb_ref = refs[2 * n_cast + 6:]
    i = pl.program_id(0)
    tm = x_ref.shape[0]

    @pl.when(i == 0)
    def _():
        yb_ref[...] = jnp.zeros(yb_ref.shape, F32)

    for src, dst, n_steps in zip(cast_in, cast_out, cast_steps):
        @pl.when(i < n_steps)
        def _(src=src, dst=dst):
            dst[...] = src[...].astype(dst.dtype)

    def compute(y_w, y_r):
        x = x_ref[...]
        ms = jnp.mean(x * x, axis=-1, keepdims=True)
        h = (x * lax.rsqrt(ms + EPS) * g_ref[...]).astype(BF16)
        gmat = gmat_ref[...]
        lane = lax.broadcasted_iota(jnp.int32, (tm, LANES), 1)

        def project(c0, c1):
            y_w[:, c0:c1] = _dot(h, w_ref[:, c0:c1])

        y_w[:, :Q_DIM] = _dot(h, wq_ref[...])

        v0 = Q_DIM + KV_DIM
        uv0 = Q_DIM + 2 * KV_DIM
        g0 = uv0 + 2 * GM_DIM
        g1 = g0 + (w_ref.shape[1] - g0) // 2

        n_qk = (Q_DIM + KV_DIM) // LANES
        yqk = [y_r[:, j * LANES:(j + 1) * LANES] for j in range(n_qk)]
        project(Q_DIM, uv0)
        n_gate = w_ref.shape[1] - g0
        gate_ref[:, :n_gate // 2] = jax.nn.sigmoid(y_r[:, g0:g1])
        ss = [_group_sum(y * y, gmat) for y in yqk]
        project(uv0, g0)

        def norm_rope(y, s2, gain):
            y = y * lax.rsqrt(s2 * (1.0 / HEAD_DIM) + EPS) * gain
            return y * cos_ref[...] + _rope_partner(y, lane) * sin_ref[...]

        row = lax.broadcasted_iota(jnp.int32, (LANES, tm), 0)
        top = row < HEAD_DIM
        half_heads = N_Q_HEADS // 2
        for j in range(Q_DIM // LANES):
            qt = (norm_rope(yqk[j], ss[j], qg_ref[...]) * QK_SCALE_LOG2).T
            zero = jnp.zeros_like(qt)
            qt_ref[j] = jnp.where(top, qt, zero).astype(BF16)
            qt_ref[j + half_heads] = jnp.where(top, zero, qt).astype(BF16)
        k_ref[...] = norm_rope(yqk[n_qk - 1], ss[n_qk - 1], kg_ref[...]).astype(BF16)
        vt = y_r[:, v0:uv0].T.astype(BF16)
        for c in range(tm // TK):
            vt_ref[c] = vt[:, c * TK:(c + 1) * TK]

        gate_ref[:, n_gate // 2:] = jax.nn.sigmoid(y_r[:, g1:])
        u_ref[...] = jax.nn.gelu(y_r[:, uv0:uv0 + GM_DIM])
        n_gm = GM_DIM // LANES
        vv = [jax.nn.gelu(y_r[:, uv0 + GM_DIM + j * LANES:uv0 + GM_DIM + (j + 1) * LANES])
              for j in range(n_gm)]
        project(g0, g1)
        mu = [_group_sum(v, gmat) * (1.0 / GM_GROUP_DIM) for v in vv]
        gq = g1 + (w_ref.shape[1] - g1) // 2
        project(g1, gq)
        dv = [v - m for v, m in zip(vv, mu)]
        var = [_group_sum(d * d, gmat) * (1.0 / GM_GROUP_DIM) for d in dv]
        project(gq, w_ref.shape[1])
        for j in range(n_gm):
            sl = slice(j * LANES, (j + 1) * LANES)
            vn = dv[j] * lax.rsqrt(var[j] + EPS) * lng_ref[:, sl] + lnb_ref[:, sl]
            vn_ref[:, sl] = vn.astype(BF16)

    @pl.when(i % 2 == 0)
    def _():
        compute(ya_ref, yb_ref)

    @pl.when(i % 2 == 1)
    def _():
        compute(yb_ref, ya_ref)


def _sublane_partial_sum(e):
    rows, lanes = e.shape
    return jnp.sum(e.reshape(rows // SUBLANES, SUBLANES, lanes), axis=0)


def _attn_kernel(shift_ref, qt_ref, k_ref, vt_ref, o_ref, sa_ref, sb_ref, m_ref, l_ref, acc_ref,
                 *, fixed_shift):
    n_heads = qt_ref.shape[0]
    group = n_heads // N_KV_HEADS
    n_chunks = k_ref.shape[0] // TK
    if fixed_shift:
        shift = shift_ref[0]
    else:
        m_ref[...] = jnp.full(m_ref.shape, -jnp.inf, F32)
    acc_ref[...] = jnp.zeros(acc_ref.shape, F32)
    l_ref[...] = jnp.zeros(l_ref.shape, F32)

    def keys(c):
        return k_ref[pl.ds(pl.multiple_of(c * TK, TK), TK), :]

    def step(c, cur_ref, next_ref):
        if next_ref is not None:
            k_next = keys(c + 1)
        vt_c = vt_ref[c]
        for h in range(n_heads):
            g = h // group
            if next_ref is not None:
                next_ref[h] = _dot(k_next, qt_ref[h])
            s = cur_ref[h]
            vt_g = vt_c[g * HEAD_DIM:(g + 1) * HEAD_DIM, :]
            if fixed_shift:
                e = jnp.exp2(s - shift)
                l_ref[h] += _sublane_partial_sum(e)
                acc_ref[h] += _dot(vt_g, e.astype(BF16))
            else:
                m_old = m_ref[h]
                m_new = jnp.maximum(m_old, jnp.max(s, axis=0, keepdims=True))
                alpha = jnp.exp2(m_old - m_new)
                e = jnp.exp2(s - m_new)
                m_ref[h] = m_new
                l_ref[h] = alpha * l_ref[h] + _sublane_partial_sum(e)
                acc_ref[h] = alpha * acc_ref[h] + _dot(vt_g, e.astype(BF16))

    k_first = keys(0)
    for h in range(n_heads):
        sa_ref[h] = _dot(k_first, qt_ref[h])

    bufs = (sa_ref, sb_ref)

    def steps(c0, last):
        for j in range(CHUNKS_PER_BODY):
            final = last and j == CHUNKS_PER_BODY - 1
            step(c0 + j, bufs[j % 2], None if final else bufs[(j + 1) % 2])

    def body(i, carry):
        steps(CHUNKS_PER_BODY * i, False)
        return carry

    lax.fori_loop(0, n_chunks // CHUNKS_PER_BODY - 1, body, 0)
    steps(n_chunks - CHUNKS_PER_BODY, True)
    for h in range(n_heads):
        o = acc_ref[h] * (1.0 / jnp.sum(l_ref[h], axis=0, keepdims=True))
        o_ref[h * HEAD_DIM:(h + 1) * HEAD_DIM, :] = o.astype(BF16)


def _merge_ffn_kernel(x_ref, at_ref, u_ref, vn_ref, gate_ref, ws_ref, bs_ref,
                      wpa_ref, wpb_ref, wo_ref, gf_ref, wgu_ref, wd_ref, gfin_ref,
                      o_ref, gm_ref, act_ref, xa_ref, xb_ref):
    i = pl.program_id(0)
    tm = x_ref.shape[0]
    d_model = x_ref.shape[1]
    d_ff = wd_ref.shape[0]

    @pl.when(i == 0)
    def _():
        xb_ref[...] = jnp.zeros(xb_ref.shape, F32)

    def compute(x1_w, x1_r):
        x1p = x1_r[...]
        ms = jnp.mean(x1p * x1p, axis=-1, keepdims=True)
        h = (x1p * lax.rsqrt(ms + EPS) * gf_ref[...]).astype(BF16)

        lane = lax.broadcasted_iota(jnp.int32, (GM_CHUNK, LANES), 1)
        low = lane < GM_GROUP_DIM
        for c in range(tm // GM_CHUNK):
            rs = slice(c * GM_CHUNK, (c + 1) * GM_CHUNK)
            for j in range(GM_DIM // LANES):
                cs = slice(j * LANES, (j + 1) * LANES)
                vblk = vn_ref[rs, cs]
                zero = jnp.zeros_like(vblk)
                vsplit = jnp.concatenate([jnp.where(low, vblk, zero), jnp.where(low, zero, vblk)],
                                         axis=0)
                sv = _dot(ws_ref[j], vsplit)
                gm_ref[rs, cs] = (u_ref[rs, cs] * (sv + bs_ref[:, cs])).astype(BF16)

        pa = lax.dot_general(at_ref[...], wpa_ref[...], (((0,), (0,)), ((), ())),
                             preferred_element_type=F32)
        pb = _dot(gm_ref[...], wpb_ref[...])
        mix = (gate_ref[:, :d_model] * pa + gate_ref[:, d_model:] * pb).astype(BF16)

        for c0 in range(0, d_ff, FF_CHUNK):
            c1 = min(c0 + FF_CHUNK, d_ff)
            gt = _dot(h, wgu_ref[:, c0:c1])
            up = _dot(h, wgu_ref[:, d_ff + c0:d_ff + c1])
            act_ref[:, c0:c1] = (jax.nn.silu(gt) * up).astype(BF16)
        x2 = x1p + _dot(act_ref[...], wd_ref[...])
        ms2 = jnp.mean(x2 * x2, axis=-1, keepdims=True)
        o_ref[...] = x2 * lax.rsqrt(ms2 + EPS) * gfin_ref[...]

        x1_w[...] = x_ref[...] + _dot(mix, wo_ref[...])

    @pl.when(i % 2 == 0)
    def _():
        compute(xa_ref, xb_ref)

    @pl.when(i % 2 == 1)
    def _():
        compute(xb_ref, xa_ref)


def _const_spec(shape):
    nd = len(shape)
    return pl.BlockSpec(shape, lambda *_: (0,) * nd, pipeline_mode=pl.Buffered(1))


def _rope_tables(seq):
    half = HEAD_DIM // 2
    rows = seq // GRID_W
    row = np.repeat(np.arange(rows, dtype=np.float64), GRID_W)
    col = np.tile(np.arange(GRID_W, dtype=np.float64), rows)
    inv_freq = 1.0 / (ROPE_THETA ** (np.arange(0, half, 2, dtype=np.float64) / half))
    ang_r = row[:, None] * inv_freq[None, :]
    ang_c = col[:, None] * inv_freq[None, :]
    cr, sr = np.cos(ang_r), np.sin(ang_r)
    cc, sc = np.cos(ang_c), np.sin(ang_c)
    cos64 = np.concatenate([cr, cr, cc, cc], axis=-1)
    sin64 = np.concatenate([-sr, sr, -sc, sc], axis=-1)
    reps = (1, LANES // HEAD_DIM)
    return np.tile(cos64, reps).astype(np.float32), np.tile(sin64, reps).astype(np.float32)


def kernel(x, norm_mix_g, w_in, q_norm_g, k_norm_g, gm_ln_g, gm_ln_b, w_s, b_s, w_proj_a,
           w_proj_b, w_out, norm_ffn_g, w_gate_up, w_down, norm_final_g):
    B, S, D = x.shape
    in_dim = w_in.shape[-1]
    d_ff = w_down.shape[1]
    assert norm_mix_g.shape[0] == 1, "single-layer block"
    assert CHUNKS_PER_BODY % 2 == 0 and S % (CHUNKS_PER_BODY * TK) == 0
    assert S % TQ == 0 and S % TM_IN == 0 and S % TM_OUT == 0
    assert TM_OUT % GM_CHUNK == 0 and TM_IN % TK == 0 and TM_IN == TQ and TM_OUT == TQ

    w_in_b = w_in[0].astype(BF16)
    half_heads = N_Q_HEADS // 2
    w_q_b = (w_in_b[:, :Q_DIM].reshape(D, 2, half_heads, HEAD_DIM)
             .transpose(0, 2, 1, 3).reshape(D, Q_DIM))

    cos_t, sin_t = _rope_tables(S)
    reps = LANES // HEAD_DIM
    qg = jnp.tile(q_norm_g[0], reps)[None, :]
    kg = jnp.tile(k_norm_g[0], reps)[None, :]
    lng = gm_ln_g[0].reshape(1, GM_DIM)
    lnb = gm_ln_b[0].reshape(1, GM_DIM)
    gid = np.arange(LANES) // HEAD_DIM
    gmat = (gid[:, None] == gid[None, :]).astype(np.float32)
    gmat2 = jnp.asarray(np.concatenate([gmat, gmat], axis=0), dtype=BF16)

    two_axes = ("arbitrary", "arbitrary")
    cparams = pltpu.CompilerParams(dimension_semantics=two_axes,
                                   vmem_limit_bytes=VMEM_LIMIT_BYTES)

    M = B * S
    n_seq = S // TM_IN
    n_blk = M // TM_IN
    ck_in = TM_IN // TK

    def in_blk(i):
        return jnp.minimum(i, n_blk - 1)

    def out_blk(i):
        return jnp.maximum(i - 1, 0)

    def out_rows(width):
        return pl.BlockSpec((TM_IN, width), lambda i: (out_blk(i), 0))

    to_cast = [(w_gate_up, n_blk), (w_out, n_blk), (w_down, d_ff // DOWN_CAST_ROWS),
               (w_proj_a, w_proj_a.shape[1] // CAST_ROWS), (w_proj_b, w_proj_b.shape[1] // CAST_ROWS)]
    cast_steps = tuple(n for _, n in to_cast)
    assert all(n <= n_blk and w.shape[1] % n == 0 for w, n in to_cast)
    cast_specs = [pl.BlockSpec((None, w.shape[1] // n, w.shape[2]),
                               lambda i, n=n: (0, jnp.minimum(i, n - 1), 0)) for w, n in to_cast]

    qt, k, vt, u, vn, gates, wgu_b, wo_b, wd_b, wpa_b, wpb_b = pl.pallas_call(
        functools.partial(_inproj_kernel, cast_steps=cast_steps),
        grid=(n_blk + 1,),
        in_specs=[
            pl.BlockSpec((TM_IN, D), lambda i: (in_blk(i), 0)),
            _const_spec((1, D)),
            _const_spec((D, Q_DIM)),
            _const_spec((D, in_dim)),
            _const_spec((2 * LANES, LANES)),
            pl.BlockSpec((TM_IN, LANES), lambda i: (out_blk(i) % n_seq, 0)),
            pl.BlockSpec((TM_IN, LANES), lambda i: (out_blk(i) % n_seq, 0)),
            _const_spec((1, LANES)),
            _const_spec((1, LANES)),
            _const_spec((1, GM_DIM)),
            _const_spec((1, GM_DIM)),
        ] + cast_specs,
        out_specs=[
            pl.BlockSpec((None, N_Q_HEADS, LANES, TM_IN), lambda i: (out_blk(i), 0, 0, 0)),
            out_rows(KV_DIM),
            pl.BlockSpec((ck_in, KV_DIM, TK), lambda i: (out_blk(i), 0, 0)),
            out_rows(GM_DIM), out_rows(GM_DIM), out_rows(2 * D),
        ] + cast_specs,
        out_shape=[
            jax.ShapeDtypeStruct((n_blk, N_Q_HEADS, LANES, TM_IN), BF16),
            jax.ShapeDtypeStruct((M, KV_DIM), BF16),
            jax.ShapeDtypeStruct((M // TK, KV_DIM, TK), BF16),
            jax.ShapeDtypeStruct((M, GM_DIM), F32),
            jax.ShapeDtypeStruct((M, GM_DIM), BF16),
            jax.ShapeDtypeStruct((M, 2 * D), F32),
        ] + [jax.ShapeDtypeStruct(w.shape, BF16) for w, _ in to_cast],
        scratch_shapes=[pltpu.VMEM((TM_IN, in_dim), F32), pltpu.VMEM((TM_IN, in_dim), F32)],
        compiler_params=pltpu.CompilerParams(dimension_semantics=("arbitrary",),
                                             vmem_limit_bytes=VMEM_LIMIT_BYTES),
        name="inproj",
    )(x.reshape(M, D), norm_mix_g, w_q_b, w_in_b, gmat2, cos_t, sin_t, qg, kg, lng, lnb,
      *[w for w, _ in to_cast])
    k = k.reshape(B, S, KV_DIM)
    vt = vt.reshape(B, S // TK, KV_DIM, TK)

    n_q = S // TQ

    def attention(fixed_shift):
        return pl.pallas_call(
            functools.partial(_attn_kernel, fixed_shift=fixed_shift),
            grid=(B, n_q),
            in_specs=[
                pl.BlockSpec(memory_space=pltpu.SMEM),
                pl.BlockSpec((None, N_Q_HEADS, LANES, TQ), lambda b, i: (b * n_q + i, 0, 0, 0)),
                pl.BlockSpec((None, S, KV_DIM), lambda b, i: (b, 0, 0)),
                pl.BlockSpec((None, S // TK, KV_DIM, TK), lambda b, i: (b, 0, 0, 0)),
            ],
            out_specs=pl.BlockSpec((None, Q_DIM, TQ), lambda b, i: (b * n_q + i, 0, 0)),
            out_shape=jax.ShapeDtypeStruct((B * n_q, Q_DIM, TQ), BF16),
            scratch_shapes=[
                pltpu.VMEM((N_Q_HEADS, TK, TQ), F32),
                pltpu.VMEM((N_Q_HEADS, TK, TQ), F32),
                pltpu.VMEM((N_Q_HEADS, 1, TQ), F32),
                pltpu.VMEM((N_Q_HEADS, SUBLANES, TQ), F32),
                pltpu.VMEM((N_Q_HEADS, HEAD_DIM, TQ), F32),
            ],
            compiler_params=cparams,
            name="gqa_attention_shift" if fixed_shift else "gqa_attention_online",
        )

    score_bound = (HEAD_DIM * QK_SCALE_LOG2 * (1.0 + 2.0 ** -6)
                   * jnp.max(jnp.abs(q_norm_g[0])) * jnp.max(jnp.abs(k_norm_g[0])))
    attn_t = lax.cond(
        score_bound <= MAX_FIXED_SHIFT,
        lambda s_, *ops: attention(True)(s_, *ops),
        lambda s_, *ops: attention(False)(s_, *ops),
        score_bound.reshape(1).astype(F32), qt, k, vt)

    bs_full = jnp.repeat(jnp.transpose(b_s[0]), GM_GROUP_DIM, axis=1)
    ws_pairs = (w_s[0].reshape(GM_GROUPS // 2, 2, GM_CHUNK, GM_CHUNK)
                .transpose(0, 2, 1, 3).reshape(GM_GROUPS // 2, GM_CHUNK, 2 * GM_CHUNK).astype(BF16))
    n_out = M // TM_OUT

    def s1_blk(i):
        return jnp.minimum(i, n_out - 1)

    def s1_rows(width):
        return pl.BlockSpec((TM_OUT, width), lambda i: (s1_blk(i), 0))

    out = pl.pallas_call(
        _merge_ffn_kernel,
        grid=(n_out + 1,),
        in_specs=[
            s1_rows(D),
            pl.BlockSpec((None, Q_DIM, TM_OUT), lambda i: (s1_blk(i), 0, 0)),
            s1_rows(GM_DIM), s1_rows(GM_DIM), s1_rows(2 * D),
            _const_spec((GM_GROUPS // 2, GM_CHUNK, 2 * GM_CHUNK)),
            _const_spec((GM_CHUNK, GM_DIM)),
            _const_spec((None, Q_DIM, D)),
            _const_spec((None, GM_DIM, D)),
            _const_spec((None, D, D)),
            _const_spec((1, D)),
            _const_spec((None, D, 2 * d_ff)),
            _const_spec((None, d_ff, D)),
            _const_spec((1, D)),
        ],
        out_specs=pl.BlockSpec((TM_OUT, D), lambda i: (jnp.maximum(i - 1, 0), 0)),
        out_shape=jax.ShapeDtypeStruct((M, D), F32),
        scratch_shapes=[
            pltpu.VMEM((TM_OUT, GM_DIM), BF16),
            pltpu.VMEM((TM_OUT, d_ff), BF16),
            pltpu.VMEM((TM_OUT, D), F32),
            pltpu.VMEM((TM_OUT, D), F32),
        ],
        compiler_params=pltpu.CompilerParams(dimension_semantics=("arbitrary",),
                                             vmem_limit_bytes=VMEM_LIMIT_BYTES),
        name="merge_ffn",
    )(x.reshape(M, D), attn_t, u, vn, gates, ws_pairs, bs_full,
      wpa_b, wpb_b, wo_b, norm_ffn_g, wgu_b, wd_b, norm_final_g[None, :])
    return out.reshape(B, S, D)
```

```python
import functools
import math

import jax
import jax.numpy as jnp
import numpy as np
from jax import lax
from jax.experimental import pallas as pl
from jax.experimental.pallas import tpu as pltpu

HEAD_DIM = 64
N_Q_HEADS = 8
N_KV_HEADS = 2
Q_DIM = N_Q_HEADS * HEAD_DIM
KV_DIM = N_KV_HEADS * HEAD_DIM
GM_GROUPS = 8
GM_GROUP_DIM = 64
GM_DIM = GM_GROUPS * GM_GROUP_DIM
GM_CHUNK = 128
GRID_W = 64
ROPE_THETA = 10000.0
EPS = 1e-6

LANES = 128
SUBLANES = 8
VMEM_LIMIT_BYTES = 56 * 1024 * 1024

TM_IN = 256
TQ = 256
TK = 256
QK_SCALE_LOG2 = math.log2(math.e) / math.sqrt(HEAD_DIM)
MAX_FIXED_SHIFT = 48.0
CHUNKS_PER_BODY = 16
TM_OUT = 256
CAST_ROWS = 16
DOWN_CAST_ROWS = 256
FF_CHUNK = 768

F32 = jnp.float32
BF16 = jnp.bfloat16


def _dot(a, b):
    return jnp.dot(a, b, preferred_element_type=F32)


def _group_sum(x, gmat2):
    hi = x.astype(BF16)
    lo = (x - hi.astype(F32)).astype(BF16)
    return _dot(jnp.concatenate([hi, lo], axis=1), gmat2)


def _rope_partner(x, lane):
    fwd = pltpu.roll(x, LANES - 16, axis=1)
    bwd = pltpu.roll(x, 16, axis=1)
    return jnp.where((lane & 16) == 0, fwd, bwd)


def _inproj_kernel(x_ref, g_ref, wq_ref, w_ref, gmat_ref, cos_ref, sin_ref, qg_ref, kg_ref,
                   lng_ref, lnb_ref, *refs, cast_steps):
    n_cast = len(cast_steps)
    cast_in = refs[:n_cast]
    qt_ref, k_ref, vt_ref, u_ref, vn_ref, gate_ref = refs[n_cast:n_cast + 6]
    cast_out = refs[n_cast + 6:2 * n_cast + 6]
    ya_ref, yb_ref = refs[2 * n_cast + 6:]
    i = pl.program_id(0)
    tm = x_ref.shape[0]

    @pl.when(i == 0)
    def _():
        yb_ref[...] = jnp.zeros(yb_ref.shape, F32)

    def compute(y_w, y_r):
        x = x_ref[...]
        ms = jnp.mean(x * x, axis=-1, keepdims=True)
        h = (x * lax.rsqrt(ms + EPS) * g_ref[...]).astype(BF16)
        gmat = gmat_ref[...]
        lane = lax.broadcasted_iota(jnp.int32, (tm, LANES), 1)

        def project(c0, c1):
            y_w[:, c0:c1] = _dot(h, w_ref[:, c0:c1])

        y_w[:, :Q_DIM] = _dot(h, wq_ref[...])

        v0 = Q_DIM + KV_DIM
        uv0 = Q_DIM + 2 * KV_DIM
        g0 = uv0 + 2 * GM_DIM
        g1 = g0 + (w_ref.shape[1] - g0) // 2

        n_qk = (Q_DIM + KV_DIM) // LANES
        yqk = [y_r[:, j * LANES:(j + 1) * LANES] for j in range(n_qk)]
        project(Q_DIM, uv0)
        n_gate = w_ref.shape[1] - g0
        gate_ref[:, :n_gate // 2] = jax.nn.sigmoid(y_r[:, g0:g1])
        ss = [_group_sum(y * y, gmat) for y in yqk]
        project(uv0, g0)

        def norm_rope(y, s2, gain):
            y = y * lax.rsqrt(s2 * (1.0 / HEAD_DIM) + EPS) * gain
            return y * cos_ref[...] + _rope_partner(y, lane) * sin_ref[...]

        row = lax.broadcasted_iota(jnp.int32, (LANES, tm), 0)
        top = row < HEAD_DIM
        half_heads = N_Q_HEADS // 2
        for j in range(Q_DIM // LANES):
            qt = (norm_rope(yqk[j], ss[j], qg_ref[...]) * QK_SCALE_LOG2).T
            zero = jnp.zeros_like(qt)
            qt_ref[j] = jnp.where(top, qt, zero).astype(BF16)
            qt_ref[j + half_heads] = jnp.where(top, zero, qt).astype(BF16)
        k_ref[...] = norm_rope(yqk[n_qk - 1], ss[n_qk - 1], kg_ref[...]).astype(BF16)
        vt = y_r[:, v0:uv0].T.astype(BF16)
        for c in range(tm // TK):
            vt_ref[c] = vt[:, c * TK:(c + 1) * TK]

        gate_ref[:, n_gate // 2:] = jax.nn.sigmoid(y_r[:, g1:])
        u_ref[...] = jax.nn.gelu(y_r[:, uv0:uv0 + GM_DIM])
        n_gm = GM_DIM // LANES
        vv = [jax.nn.gelu(y_r[:, uv0 + GM_DIM + j * LANES:uv0 + GM_DIM + (j + 1) * LANES])
              for j in range(n_gm)]
        project(g0, g1)
        mu = [_group_sum(v, gmat) * (1.0 / GM_GROUP_DIM) for v in vv]
        gq = g1 + (w_ref.shape[1] - g1) // 2
        project(g1, gq)
        dv = [v - m for v, m in zip(vv, mu)]
        var = [_group_sum(d * d, gmat) * (1.0 / GM_GROUP_DIM) for d in dv]
        project(gq, w_ref.shape[1])
        for j in range(n_gm):
            sl = slice(j * LANES, (j + 1) * LANES)
            vn = dv[j] * lax.rsqrt(var[j] + EPS) * lng_ref[:, sl] + lnb_ref[:, sl]
            vn_ref[:, sl] = vn.astype(BF16)

    @pl.when(i % 2 == 0)
    def _():
        compute(ya_ref, yb_ref)

    @pl.when(i % 2 == 1)
    def _():
        compute(yb_ref, ya_ref)

    for src, dst, n_steps in zip(cast_in, cast_out, cast_steps):
        @pl.when(i < n_steps)
        def _(src=src, dst=dst):
            dst[...] = src[...].astype(dst.dtype)


def _sublane_partial_sum(e):
    rows, lanes = e.shape
    return jnp.sum(e.reshape(rows // SUBLANES, SUBLANES, lanes), axis=0)


def _attn_kernel(shift_ref, qt_ref, k_ref, vt_ref, o_ref, sa_ref, sb_ref, m_ref, l_ref, acc_ref,
                 *, fixed_shift):
    n_heads = qt_ref.shape[0]
    group = n_heads // N_KV_HEADS
    n_chunks = k_ref.shape[0] // TK
    if fixed_shift:
        shift = shift_ref[0]
    else:
        m_ref[...] = jnp.full(m_ref.shape, -jnp.inf, F32)
    acc_ref[...] = jnp.zeros(acc_ref.shape, F32)
    l_ref[...] = jnp.zeros(l_ref.shape, F32)

    def keys(c):
        return k_ref[pl.ds(pl.multiple_of(c * TK, TK), TK), :]

    def step(c, cur_ref, next_ref):
        if next_ref is not None:
            k_next = keys(c + 1)
        vt_c = vt_ref[c]
        for h in range(n_heads):
            g = h // group
            if next_ref is not None:
                next_ref[h] = _dot(k_next, qt_ref[h])
            s = cur_ref[h]
            vt_g = vt_c[g * HEAD_DIM:(g + 1) * HEAD_DIM, :]
            if fixed_shift:
                e = jnp.exp2(s - shift)
                l_ref[h] += _sublane_partial_sum(e)
                acc_ref[h] += _dot(vt_g, e.astype(BF16))
            else:
                m_old = m_ref[h]
                m_new = jnp.maximum(m_old, jnp.max(s, axis=0, keepdims=True))
                alpha = jnp.exp2(m_old - m_new)
                e = jnp.exp2(s - m_new)
                m_ref[h] = m_new
                l_ref[h] = alpha * l_ref[h] + _sublane_partial_sum(e)
                acc_ref[h] = alpha * acc_ref[h] + _dot(vt_g, e.astype(BF16))

    k_first = keys(0)
    for h in range(n_heads):
        sa_ref[h] = _dot(k_first, qt_ref[h])

    bufs = (sa_ref, sb_ref)

    def steps(c0, last):
        for j in range(CHUNKS_PER_BODY):
            final = last and j == CHUNKS_PER_BODY - 1
            step(c0 + j, bufs[j % 2], None if final else bufs[(j + 1) % 2])

    def body(i, carry):
        steps(CHUNKS_PER_BODY * i, False)
        return carry

    lax.fori_loop(0, n_chunks // CHUNKS_PER_BODY - 1, body, 0)
    steps(n_chunks - CHUNKS_PER_BODY, True)
    for h in range(n_heads):
        o = acc_ref[h] / jnp.sum(l_ref[h], axis=0, keepdims=True)
        o_ref[h * HEAD_DIM:(h + 1) * HEAD_DIM, :] = o.astype(BF16)


def _merge_ffn_kernel(x_ref, at_ref, u_ref, vn_ref, gate_ref, ws_ref, bs_ref,
                      wpa_ref, wpb_ref, wo_ref, gf_ref, wgu_ref, wd_ref, gfin_ref,
                      o_ref, gm_ref, act_ref, xa_ref, xb_ref):
    i = pl.program_id(0)
    tm = x_ref.shape[0]
    d_model = x_ref.shape[1]
    d_ff = wd_ref.shape[0]

    @pl.when(i == 0)
    def _():
        xb_ref[...] = jnp.zeros(xb_ref.shape, F32)

    def compute(x1_w, x1_r):
        x1p = x1_r[...]
        ms = jnp.mean(x1p * x1p, axis=-1, keepdims=True)
        h = (x1p * lax.rsqrt(ms + EPS) * gf_ref[...]).astype(BF16)

        lane = lax.broadcasted_iota(jnp.int32, (GM_CHUNK, LANES), 1)
        low = lane < GM_GROUP_DIM
        for c in range(tm // GM_CHUNK):
            rs = slice(c * GM_CHUNK, (c + 1) * GM_CHUNK)
            for j in range(GM_DIM // LANES):
                cs = slice(j * LANES, (j + 1) * LANES)
                vblk = vn_ref[rs, cs]
                zero = jnp.zeros_like(vblk)
                vsplit = jnp.concatenate([jnp.where(low, vblk, zero), jnp.where(low, zero, vblk)],
                                         axis=0)
                sv = _dot(ws_ref[j], vsplit)
                gm_ref[rs, cs] = (u_ref[rs, cs] * (sv + bs_ref[:, cs])).astype(BF16)

        pa = lax.dot_general(at_ref[...], wpa_ref[...], (((0,), (0,)), ((), ())),
                             preferred_element_type=F32)
        pb = _dot(gm_ref[...], wpb_ref[...])
        mix = (gate_ref[:, :d_model] * pa + gate_ref[:, d_model:] * pb).astype(BF16)

        for c0 in range(0, d_ff, FF_CHUNK):
            c1 = min(c0 + FF_CHUNK, d_ff)
            gt = _dot(h, wgu_ref[:, c0:c1])
            up = _dot(h, wgu_ref[:, d_ff + c0:d_ff + c1])
            act_ref[:, c0:c1] = (jax.nn.silu(gt) * up).astype(BF16)
        x2 = x1p + _dot(act_ref[...], wd_ref[...])
        ms2 = jnp.mean(x2 * x2, axis=-1, keepdims=True)
        o_ref[...] = x2 * lax.rsqrt(ms2 + EPS) * gfin_ref[...]

        x1_w[...] = x_ref[...] + _dot(mix, wo_ref[...])

    @pl.when(i % 2 == 0)
    def _():
        compute(xa_ref, xb_ref)

    @pl.when(i % 2 == 1)
    def _():
        compute(xb_ref, xa_ref)


def _const_spec(shape):
    nd = len(shape)
    return pl.BlockSpec(shape, lambda *_: (0,) * nd, pipeline_mode=pl.Buffered(1))


def _rope_tables(seq):
    half = HEAD_DIM // 2
    rows = seq // GRID_W
    row = np.repeat(np.arange(rows, dtype=np.float64), GRID_W)
    col = np.tile(np.arange(GRID_W, dtype=np.float64), rows)
    inv_freq = 1.0 / (ROPE_THETA ** (np.arange(0, half, 2, dtype=np.float64) / half))
    ang_r = row[:, None] * inv_freq[None, :]
    ang_c = col[:, None] * inv_freq[None, :]
    cr, sr = np.cos(ang_r), np.sin(ang_r)
    cc, sc = np.cos(ang_c), np.sin(ang_c)
    cos64 = np.concatenate([cr, cr, cc, cc], axis=-1)
    sin64 = np.concatenate([-sr, sr, -sc, sc], axis=-1)
    reps = (1, LANES // HEAD_DIM)
    return np.tile(cos64, reps).astype(np.float32), np.tile(sin64, reps).astype(np.float32)


def kernel(x, norm_mix_g, w_in, q_norm_g, k_norm_g, gm_ln_g, gm_ln_b, w_s, b_s, w_proj_a,
           w_proj_b, w_out, norm_ffn_g, w_gate_up, w_down, norm_final_g):
    B, S, D = x.shape
    in_dim = w_in.shape[-1]
    d_ff = w_down.shape[1]
    assert norm_mix_g.shape[0] == 1, "single-layer block"
    assert CHUNKS_PER_BODY % 2 == 0 and S % (CHUNKS_PER_BODY * TK) == 0
    assert S % TQ == 0 and S % TM_IN == 0 and S % TM_OUT == 0
    assert TM_OUT % GM_CHUNK == 0 and TM_IN % TK == 0 and TM_IN == TQ and TM_OUT == TQ

    w_in_b = w_in[0].astype(BF16)
    half_heads = N_Q_HEADS // 2
    w_q_b = (w_in_b[:, :Q_DIM].reshape(D, 2, half_heads, HEAD_DIM)
             .transpose(0, 2, 1, 3).reshape(D, Q_DIM))

    cos_t, sin_t = _rope_tables(S)
    reps = LANES // HEAD_DIM
    qg = jnp.tile(q_norm_g[0], reps)[None, :]
    kg = jnp.tile(k_norm_g[0], reps)[None, :]
    lng = gm_ln_g[0].reshape(1, GM_DIM)
    lnb = gm_ln_b[0].reshape(1, GM_DIM)
    gid = np.arange(LANES) // HEAD_DIM
    gmat = (gid[:, None] == gid[None, :]).astype(np.float32)
    gmat2 = jnp.asarray(np.concatenate([gmat, gmat], axis=0), dtype=BF16)

    two_axes = ("arbitrary", "arbitrary")
    cparams = pltpu.CompilerParams(dimension_semantics=two_axes,
                                   vmem_limit_bytes=VMEM_LIMIT_BYTES)

    M = B * S
    n_seq = S // TM_IN
    n_blk = M // TM_IN
    ck_in = TM_IN // TK

    def in_blk(i):
        return jnp.minimum(i, n_blk - 1)

    def out_blk(i):
        return jnp.maximum(i - 1, 0)

    def out_rows(width):
        return pl.BlockSpec((TM_IN, width), lambda i: (out_blk(i), 0))

    to_cast = [(w_gate_up, n_blk), (w_out, n_blk), (w_down, d_ff // DOWN_CAST_ROWS),
               (w_proj_a, w_proj_a.shape[1] // CAST_ROWS), (w_proj_b, w_proj_b.shape[1] // CAST_ROWS)]
    cast_steps = tuple(n for _, n in to_cast)
    assert all(n <= n_blk and w.shape[1] % n == 0 for w, n in to_cast)
    cast_specs = [pl.BlockSpec((None, w.shape[1] // n, w.shape[2]),
                               lambda i, n=n: (0, jnp.minimum(i, n - 1), 0)) for w, n in to_cast]

    qt, k, vt, u, vn, gates, wgu_b, wo_b, wd_b, wpa_b, wpb_b = pl.pallas_call(
        functools.partial(_inproj_kernel, cast_steps=cast_steps),
        grid=(n_blk + 1,),
        in_specs=[
            pl.BlockSpec((TM_IN, D), lambda i: (in_blk(i), 0)),
            _const_spec((1, D)),
            _const_spec((D, Q_DIM)),
            _const_spec((D, in_dim)),
            _const_spec((2 * LANES, LANES)),
            pl.BlockSpec((TM_IN, LANES), lambda i: (out_blk(i) % n_seq, 0)),
            pl.BlockSpec((TM_IN, LANES), lambda i: (out_blk(i) % n_seq, 0)),
            _const_spec((1, LANES)),
            _const_spec((1, LANES)),
            _const_spec((1, GM_DIM)),
            _const_spec((1, GM_DIM)),
        ] + cast_specs,
        out_specs=[
            pl.BlockSpec((None, N_Q_HEADS, LANES, TM_IN), lambda i: (out_blk(i), 0, 0, 0)),
            out_rows(KV_DIM),
            pl.BlockSpec((ck_in, KV_DIM, TK), lambda i: (out_blk(i), 0, 0)),
            out_rows(GM_DIM), out_rows(GM_DIM), out_rows(2 * D),
        ] + cast_specs,
        out_shape=[
            jax.ShapeDtypeStruct((n_blk, N_Q_HEADS, LANES, TM_IN), BF16),
            jax.ShapeDtypeStruct((M, KV_DIM), BF16),
            jax.ShapeDtypeStruct((M // TK, KV_DIM, TK), BF16),
            jax.ShapeDtypeStruct((M, GM_DIM), F32),
            jax.ShapeDtypeStruct((M, GM_DIM), BF16),
            jax.ShapeDtypeStruct((M, 2 * D), F32),
        ] + [jax.ShapeDtypeStruct(w.shape, BF16) for w, _ in to_cast],
        scratch_shapes=[pltpu.VMEM((TM_IN, in_dim), F32), pltpu.VMEM((TM_IN, in_dim), F32)],
        compiler_params=pltpu.CompilerParams(dimension_semantics=("arbitrary",),
                                             vmem_limit_bytes=VMEM_LIMIT_BYTES),
        name="inproj",
    )(x.reshape(M, D), norm_mix_g, w_q_b, w_in_b, gmat2, cos_t, sin_t, qg, kg, lng, lnb,
      *[w for w, _ in to_cast])
    k = k.reshape(B, S, KV_DIM)
    vt = vt.reshape(B, S // TK, KV_DIM, TK)

    n_q = S // TQ

    def attention(fixed_shift):
        return pl.pallas_call(
            functools.partial(_attn_kernel, fixed_shift=fixed_shift),
            grid=(B, n_q),
            in_specs=[
                pl.BlockSpec(memory_space=pltpu.SMEM),
                pl.BlockSpec((None, N_Q_HEADS, LANES, TQ), lambda b, i: (b * n_q + i, 0, 0, 0)),
                pl.BlockSpec((None, S, KV_DIM), lambda b, i: (b, 0, 0)),
                pl.BlockSpec((None, S // TK, KV_DIM, TK), lambda b, i: (b, 0, 0, 0)),
            ],
            out_specs=pl.BlockSpec((None, Q_DIM, TQ), lambda b, i: (b * n_q + i, 0, 0)),
            out_shape=jax.ShapeDtypeStruct((B * n_q, Q_DIM, TQ), BF16),
            scratch_shapes=[
                pltpu.VMEM((N_Q_HEADS, TK, TQ), F32),
                pltpu.VMEM((N_Q_HEADS, TK, TQ), F32),
                pltpu.VMEM((N_Q_HEADS, 1, TQ), F32),
                pltpu.VMEM((N_Q_HEADS, SUBLANES, TQ), F32),
                pltpu.VMEM((N_Q_HEADS, HEAD_DIM, TQ), F32),
            ],
            compiler_params=cparams,
            name="gqa_attention_shift" if fixed_shift else "gqa_attention_online",
        )

    score_bound = (HEAD_DIM * QK_SCALE_LOG2 * (1.0 + 2.0 ** -6)
                   * jnp.max(jnp.abs(q_norm_g[0])) * jnp.max(jnp.abs(k_norm_g[0])))
    attn_t = lax.cond(
        score_bound <= MAX_FIXED_SHIFT,
        lambda s_, *ops: attention(True)(s_, *ops),
        lambda s_, *ops: attention(False)(s_, *ops),
        score_bound.reshape(1).astype(F32), qt, k, vt)

    bs_full = jnp.repeat(jnp.transpose(b_s[0]), GM_GROUP_DIM, axis=1)
    ws_pairs = (w_s[0].reshape(GM_GROUPS // 2, 2, GM_CHUNK, GM_CHUNK)
                .transpose(0, 2, 1, 3).reshape(GM_GROUPS // 2, GM_CHUNK, 2 * GM_CHUNK).astype(BF16))
    n_out = M // TM_OUT

    def s1_blk(i):
        return jnp.minimum(i, n_out - 1)

    def s1_rows(width):
        return pl.BlockSpec((TM_OUT, width), lambda i: (s1_blk(i), 0))

    out = pl.pallas_call(
        _merge_ffn_kernel,
        grid=(n_out + 1,),
        in_specs=[
            s1_rows(D),
            pl.BlockSpec((None, Q_DIM, TM_OUT), lambda i: (s1_blk(i), 0, 0)),
            s1_rows(GM_DIM), s1_rows(GM_DIM), s1_rows(2 * D),
            _const_spec((GM_GROUPS // 2, GM_CHUNK, 2 * GM_CHUNK)),
            _const_spec((GM_CHUNK, GM_DIM)),
            _const_spec((None, Q_DIM, D)),
            _const_spec((None, GM_DIM, D)),
            _const_spec((None, D, D)),
            _const_spec((1, D)),
            _const_spec((None, D, 2 * d_ff)),
            _const_spec((None, d_ff, D)),
            _const_spec((1, D)),
        ],
        out_specs=pl.BlockSpec((TM_OUT, D), lambda i: (jnp.maximum(i - 1, 0), 0)),
        out_shape=jax.ShapeDtypeStruct((M, D), F32),
        scratch_shapes=[
            pltpu.VMEM((TM_OUT, GM_DIM), BF16),
            pltpu.VMEM((TM_OUT, d_ff), BF16),
            pltpu.VMEM((TM_OUT, D), F32),
            pltpu.VMEM((TM_OUT, D), F32),
        ],
        compiler_params=pltpu.CompilerParams(dimension_semantics=("arbitrary",),
                                             vmem_limit_bytes=VMEM_LIMIT_BYTES),
        name="merge_ffn",
    )(x.reshape(M, D), attn_t, u, vn, gates, ws_pairs, bs_full,
      wpa_b, wpb_b, wo_b, norm_ffn_g, wgu_b, wd_b, norm_final_g[None, :])
    return out.reshape(B, S, D)
```

```python
import functools
import math

import jax
import jax.numpy as jnp
import numpy as np
from jax import lax
from jax.experimental import pallas as pl
from jax.experimental.pallas import tpu as pltpu

HEAD_DIM = 64
N_Q_HEADS = 8
N_KV_HEADS = 2
Q_DIM = N_Q_HEADS * HEAD_DIM
KV_DIM = N_KV_HEADS * HEAD_DIM
GM_GROUPS = 8
GM_GROUP_DIM = 64
GM_DIM = GM_GROUPS * GM_GROUP_DIM
GM_CHUNK = 128
GRID_W = 64
ROPE_THETA = 10000.0
EPS = 1e-6

LANES = 128
SUBLANES = 8
VMEM_LIMIT_BYTES = 56 * 1024 * 1024

TM_IN = 256
TQ = 256
TK = 256
QK_SCALE_LOG2 = math.log2(math.e) / math.sqrt(HEAD_DIM)
MAX_FIXED_SHIFT = 48.0
CHUNKS_PER_BODY = 16
TM_OUT = 256
CAST_ROWS = 16
DOWN_CAST_ROWS = 256
FF_CHUNK = 768

F32 = jnp.float32
BF16 = jnp.bfloat16


def _dot(a, b):
    return jnp.dot(a, b, preferred_element_type=F32)


def _group_sum(x, gmat2):
    hi = x.astype(BF16)
    lo = (x - hi.astype(F32)).astype(BF16)
    return _dot(jnp.concatenate([hi, lo], axis=1), gmat2)


def _rope_partner(x, lane):
    fwd = pltpu.roll(x, LANES - 16, axis=1)
    bwd = pltpu.roll(x, 16, axis=1)
    return jnp.where((lane & 16) == 0, fwd, bwd)


def _inproj_kernel(x_ref, g_ref, wq_ref, w_ref, gmat_ref, cos_ref, sin_ref, qg_ref, kg_ref,
                   lng_ref, lnb_ref, *refs, cast_steps):
    n_cast = len(cast_steps)
    cast_in = refs[:n_cast]
    qt_ref, k_ref, vt_ref, u_ref, vn_ref, gate_ref = refs[n_cast:n_cast + 6]
    cast_out = refs[n_cast + 6:2 * n_cast + 6]
    ya_ref, yb_ref = refs[2 * n_cast + 6:]
    i = pl.program_id(0)
    tm = x_ref.shape[0]

    @pl.when(i == 0)
    def _():
        yb_ref[...] = jnp.zeros(yb_ref.shape, F32)

    for src, dst, n_steps in zip(cast_in, cast_out, cast_steps):
        @pl.when(i < n_steps)
        def _(src=src, dst=dst):
            dst[...] = src[...].astype(dst.dtype)

    def compute(y_w, y_r):
        x = x_ref[...]
        ms = jnp.mean(x * x, axis=-1, keepdims=True)
        h = (x * lax.rsqrt(ms + EPS) * g_ref[...]).astype(BF16)
        gmat = gmat_ref[...]
        lane = lax.broadcasted_iota(jnp.int32, (tm, LANES), 1)

        def project(c0, c1):
            y_w[:, c0:c1] = _dot(h, w_ref[:, c0:c1])

        y_w[:, :Q_DIM] = _dot(h, wq_ref[...])

        v0 = Q_DIM + KV_DIM
        uv0 = Q_DIM + 2 * KV_DIM
        g0 = uv0 + 2 * GM_DIM
        g1 = g0 + (w_ref.shape[1] - g0) // 2

        n_qk = (Q_DIM + KV_DIM) // LANES
        yqk = [y_r[:, j * LANES:(j + 1) * LANES] for j in range(n_qk)]
        project(Q_DIM, uv0)
        n_gate = w_ref.shape[1] - g0
        gate_ref[:, :n_gate // 2] = jax.nn.sigmoid(y_r[:, g0:g1])
        ss = [_group_sum(y * y, gmat) for y in yqk]
        project(uv0, g0)

        def norm_rope(y, s2, gain):
            y = y * lax.rsqrt(s2 * (1.0 / HEAD_DIM) + EPS) * gain
            return y * cos_ref[...] + _rope_partner(y, lane) * sin_ref[...]

        row = lax.broadcasted_iota(jnp.int32, (LANES, tm), 0)
        top = row < HEAD_DIM
        half_heads = N_Q_HEADS // 2
        for j in range(Q_DIM // LANES):
            qt = (norm_rope(yqk[j], ss[j], qg_ref[...]) * QK_SCALE_LOG2).T
            zero = jnp.zeros_like(qt)
            qt_ref[j] = jnp.where(top, qt, zero).astype(BF16)
            qt_ref[j + half_heads] = jnp.where(top, zero, qt).astype(BF16)
        k_ref[...] = norm_rope(yqk[n_qk - 1], ss[n_qk - 1], kg_ref[...]).astype(BF16)
        vt = y_r[:, v0:uv0].T.astype(BF16)
        for c in range(tm // TK):
            vt_ref[c] = vt[:, c * TK:(c + 1) * TK]

        gate_ref[:, n_gate // 2:] = jax.nn.sigmoid(y_r[:, g1:])
        u_ref[...] = jax.nn.gelu(y_r[:, uv0:uv0 + GM_DIM])
        n_gm = GM_DIM // LANES
        vv = [jax.nn.gelu(y_r[:, uv0 + GM_DIM + j * LANES:uv0 + GM_DIM + (j + 1) * LANES])
              for j in range(n_gm)]
        project(g0, g1)
        mu = [_group_sum(v, gmat) * (1.0 / GM_GROUP_DIM) for v in vv]
        gq = g1 + (w_ref.shape[1] - g1) // 2
        project(g1, gq)
        dv = [v - m for v, m in zip(vv, mu)]
        var = [_group_sum(d * d, gmat) * (1.0 / GM_GROUP_DIM) for d in dv]
        project(gq, w_ref.shape[1])
        for j in range(n_gm):
            sl = slice(j * LANES, (j + 1) * LANES)
            vn = dv[j] * lax.rsqrt(var[j] + EPS) * lng_ref[:, sl] + lnb_ref[:, sl]
            vn_ref[:, sl] = vn.astype(BF16)

    @pl.when(i % 2 == 0)
    def _():
        compute(ya_ref, yb_ref)

    @pl.when(i % 2 == 1)
    def _():
        compute(yb_ref, ya_ref)


def _sublane_partial_sum(e):
    rows, lanes = e.shape
    return jnp.sum(e.reshape(rows // SUBLANES, SUBLANES, lanes), axis=0)


def _attn_kernel(shift_ref, qt_ref, k_ref, vt_ref, o_ref, sa_ref, sb_ref, m_ref, l_ref, acc_ref,
                 *, fixed_shift):
    n_heads = qt_ref.shape[0]
    group = n_heads // N_KV_HEADS
    n_chunks = k_ref.shape[0] // TK
    if fixed_shift:
        shift = shift_ref[0]
    else:
        m_ref[...] = jnp.full(m_ref.shape, -jnp.inf, F32)
    acc_ref[...] = jnp.zeros(acc_ref.shape, F32)
    l_ref[...] = jnp.zeros(l_ref.shape, F32)

    def keys(c):
        return k_ref[pl.ds(pl.multiple_of(c * TK, TK), TK), :]

    def step(c, cur_ref, next_ref):
        if next_ref is not None:
            k_next = keys(c + 1)
        vt_c = vt_ref[c]
        for h in range(n_heads):
            g = h // group
            if next_ref is not None:
                next_ref[h] = _dot(k_next, qt_ref[h])
            s = cur_ref[h]
            vt_g = vt_c[g * HEAD_DIM:(g + 1) * HEAD_DIM, :]
            if fixed_shift:
                e = jnp.exp2(s - shift)
                l_ref[h] += _sublane_partial_sum(e)
                acc_ref[h] += _dot(vt_g, e.astype(BF16))
            else:
                m_old = m_ref[h]
                m_new = jnp.maximum(m_old, jnp.max(s, axis=0, keepdims=True))
                alpha = jnp.exp2(m_old - m_new)
                e = jnp.exp2(s - m_new)
                m_ref[h] = m_new
                l_ref[h] = alpha * l_ref[h] + _sublane_partial_sum(e)
                acc_ref[h] = alpha * acc_ref[h] + _dot(vt_g, e.astype(BF16))

    k_first = keys(0)
    for h in range(n_heads):
        sa_ref[h] = _dot(k_first, qt_ref[h])

    bufs = (sa_ref, sb_ref)

    def steps(c0, last):
        for j in range(CHUNKS_PER_BODY):
            final = last and j == CHUNKS_PER_BODY - 1
            step(c0 + j, bufs[j % 2], None if final else bufs[(j + 1) % 2])

    def body(i, carry):
        steps(CHUNKS_PER_BODY * i, False)
        return carry

    lax.fori_loop(0, n_chunks // CHUNKS_PER_BODY - 1, body, 0)
    steps(n_chunks - CHUNKS_PER_BODY, True)
    for h in range(n_heads):
        o = acc_ref[h] / jnp.sum(l_ref[h], axis=0, keepdims=True)
        o_ref[h * HEAD_DIM:(h + 1) * HEAD_DIM, :] = o.astype(BF16)


def _merge_ffn_kernel(x_ref, at_ref, u_ref, vn_ref, gate_ref, ws_ref, bs_ref,
                      wpa_ref, wpb_ref, wo_ref, gf_ref, wgu_ref, wd_ref, gfin_ref,
                      o_ref, gm_ref, act_ref, xa_ref, xb_ref, *, last_step):
    i = pl.program_id(0)
    tm = x_ref.shape[0]
    d_model = x_ref.shape[1]
    d_ff = wd_ref.shape[0]

    def compute(x1_w, x1_r):
        if x1_r is not None:
            x1p = x1_r[...]
            ms = jnp.mean(x1p * x1p, axis=-1, keepdims=True)
            h = (x1p * lax.rsqrt(ms + EPS) * gf_ref[...]).astype(BF16)
        if x1_w is not None:
            mix = stage1_mix()
        if x1_r is not None:
            stage2(x1p, h)
        if x1_w is not None:
            x1_w[...] = x_ref[...] + _dot(mix, wo_ref[...])

    def stage1_mix():
        lane = lax.broadcasted_iota(jnp.int32, (GM_CHUNK, LANES), 1)
        low = lane < GM_GROUP_DIM
        for c in range(tm // GM_CHUNK):
            rs = slice(c * GM_CHUNK, (c + 1) * GM_CHUNK)
            for j in range(GM_DIM // LANES):
                cs = slice(j * LANES, (j + 1) * LANES)
                vblk = vn_ref[rs, cs]
                zero = jnp.zeros_like(vblk)
                vsplit = jnp.concatenate([jnp.where(low, vblk, zero), jnp.where(low, zero, vblk)],
                                         axis=0)
                sv = _dot(ws_ref[j], vsplit)
                gm_ref[rs, cs] = (u_ref[rs, cs] * (sv + bs_ref[:, cs])).astype(BF16)

        pa = lax.dot_general(at_ref[...], wpa_ref[...], (((0,), (0,)), ((), ())),
                             preferred_element_type=F32)
        pb = _dot(gm_ref[...], wpb_ref[...])
        return (gate_ref[:, :d_model] * pa + gate_ref[:, d_model:] * pb).astype(BF16)

    def stage2(x1p, h):
        for c0 in range(0, d_ff, FF_CHUNK):
            c1 = min(c0 + FF_CHUNK, d_ff)
            gt = _dot(h, wgu_ref[:, c0:c1])
            up = _dot(h, wgu_ref[:, d_ff + c0:d_ff + c1])
            act_ref[:, c0:c1] = (jax.nn.silu(gt) * up).astype(BF16)
        x2 = x1p + _dot(act_ref[...], wd_ref[...])
        ms2 = jnp.mean(x2 * x2, axis=-1, keepdims=True)
        o_ref[...] = x2 * lax.rsqrt(ms2 + EPS) * gfin_ref[...]

    bufs = (xa_ref, xb_ref)
    interior = jnp.logical_and(i > 0, i < last_step)

    @pl.when(i == 0)
    def _():
        o_ref[...] = jnp.zeros(o_ref.shape, o_ref.dtype)
        compute(bufs[0], None)

    @pl.when(jnp.logical_and(interior, i % 2 == 0))
    def _():
        compute(bufs[0], bufs[1])

    @pl.when(jnp.logical_and(interior, i % 2 == 1))
    def _():
        compute(bufs[1], bufs[0])

    @pl.when(i == last_step)
    def _():
        compute(None, bufs[(last_step - 1) % 2])


def _const_spec(shape):
    nd = len(shape)
    return pl.BlockSpec(shape, lambda *_: (0,) * nd, pipeline_mode=pl.Buffered(1))


def _rope_tables(seq):
    half = HEAD_DIM // 2
    rows = seq // GRID_W
    row = np.repeat(np.arange(rows, dtype=np.float64), GRID_W)
    col = np.tile(np.arange(GRID_W, dtype=np.float64), rows)
    inv_freq = 1.0 / (ROPE_THETA ** (np.arange(0, half, 2, dtype=np.float64) / half))
    ang_r = row[:, None] * inv_freq[None, :]
    ang_c = col[:, None] * inv_freq[None, :]
    cr, sr = np.cos(ang_r), np.sin(ang_r)
    cc, sc = np.cos(ang_c), np.sin(ang_c)
    cos64 = np.concatenate([cr, cr, cc, cc], axis=-1)
    sin64 = np.concatenate([-sr, sr, -sc, sc], axis=-1)
    reps = (1, LANES // HEAD_DIM)
    return np.tile(cos64, reps).astype(np.float32), np.tile(sin64, reps).astype(np.float32)


def kernel(x, norm_mix_g, w_in, q_norm_g, k_norm_g, gm_ln_g, gm_ln_b, w_s, b_s, w_proj_a,
           w_proj_b, w_out, norm_ffn_g, w_gate_up, w_down, norm_final_g):
    B, S, D = x.shape
    in_dim = w_in.shape[-1]
    d_ff = w_down.shape[1]
    assert norm_mix_g.shape[0] == 1, "single-layer block"
    assert CHUNKS_PER_BODY % 2 == 0 and S % (CHUNKS_PER_BODY * TK) == 0
    assert S % TQ == 0 and S % TM_IN == 0 and S % TM_OUT == 0
    assert TM_OUT % GM_CHUNK == 0 and TM_IN % TK == 0 and TM_IN == TQ and TM_OUT == TQ

    w_in_b = w_in[0].astype(BF16)
    half_heads = N_Q_HEADS // 2
    w_q_b = (w_in_b[:, :Q_DIM].reshape(D, 2, half_heads, HEAD_DIM)
             .transpose(0, 2, 1, 3).reshape(D, Q_DIM))

    cos_t, sin_t = _rope_tables(S)
    reps = LANES // HEAD_DIM
    qg = jnp.tile(q_norm_g[0], reps)[None, :]
    kg = jnp.tile(k_norm_g[0], reps)[None, :]
    lng = gm_ln_g[0].reshape(1, GM_DIM)
    lnb = gm_ln_b[0].reshape(1, GM_DIM)
    gid = np.arange(LANES) // HEAD_DIM
    gmat = (gid[:, None] == gid[None, :]).astype(np.float32)
    gmat2 = jnp.asarray(np.concatenate([gmat, gmat], axis=0), dtype=BF16)

    two_axes = ("arbitrary", "arbitrary")
    cparams = pltpu.CompilerParams(dimension_semantics=two_axes,
                                   vmem_limit_bytes=VMEM_LIMIT_BYTES)

    M = B * S
    n_seq = S // TM_IN
    n_blk = M // TM_IN
    ck_in = TM_IN // TK

    def in_blk(i):
        return jnp.minimum(i, n_blk - 1)

    def out_blk(i):
        return jnp.maximum(i - 1, 0)

    def out_rows(width):
        return pl.BlockSpec((TM_IN, width), lambda i: (out_blk(i), 0))

    to_cast = [(w_gate_up, n_blk), (w_out, n_blk), (w_down, d_ff // DOWN_CAST_ROWS),
               (w_proj_a, w_proj_a.shape[1] // CAST_ROWS), (w_proj_b, w_proj_b.shape[1] // CAST_ROWS)]
    cast_steps = tuple(n for _, n in to_cast)
    assert all(n <= n_blk and w.shape[1] % n == 0 for w, n in to_cast)
    cast_specs = [pl.BlockSpec((None, w.shape[1] // n, w.shape[2]),
                               lambda i, n=n: (0, jnp.minimum(i, n - 1), 0)) for w, n in to_cast]

    qt, k, vt, u, vn, gates, wgu_b, wo_b, wd_b, wpa_b, wpb_b = pl.pallas_call(
        functools.partial(_inproj_kernel, cast_steps=cast_steps),
        grid=(n_blk + 1,),
        in_specs=[
            pl.BlockSpec((TM_IN, D), lambda i: (in_blk(i), 0)),
            _const_spec((1, D)),
            _const_spec((D, Q_DIM)),
            _const_spec((D, in_dim)),
            _const_spec((2 * LANES, LANES)),
            pl.BlockSpec((TM_IN, LANES), lambda i: (out_blk(i) % n_seq, 0)),
            pl.BlockSpec((TM_IN, LANES), lambda i: (out_blk(i) % n_seq, 0)),
            _const_spec((1, LANES)),
            _const_spec((1, LANES)),
            _const_spec((1, GM_DIM)),
            _const_spec((1, GM_DIM)),
        ] + cast_specs,
        out_specs=[
            pl.BlockSpec((None, N_Q_HEADS, LANES, TM_IN), lambda i: (out_blk(i), 0, 0, 0)),
            out_rows(KV_DIM),
            pl.BlockSpec((ck_in, KV_DIM, TK), lambda i: (out_blk(i), 0, 0)),
            out_rows(GM_DIM), out_rows(GM_DIM), out_rows(2 * D),
        ] + cast_specs,
        out_shape=[
            jax.ShapeDtypeStruct((n_blk, N_Q_HEADS, LANES, TM_IN), BF16),
            jax.ShapeDtypeStruct((M, KV_DIM), BF16),
            jax.ShapeDtypeStruct((M // TK, KV_DIM, TK), BF16),
            jax.ShapeDtypeStruct((M, GM_DIM), F32),
            jax.ShapeDtypeStruct((M, GM_DIM), BF16),
            jax.ShapeDtypeStruct((M, 2 * D), F32),
        ] + [jax.ShapeDtypeStruct(w.shape, BF16) for w, _ in to_cast],
        scratch_shapes=[pltpu.VMEM((TM_IN, in_dim), F32), pltpu.VMEM((TM_IN, in_dim), F32)],
        compiler_params=pltpu.CompilerParams(dimension_semantics=("arbitrary",),
                                             vmem_limit_bytes=VMEM_LIMIT_BYTES),
        name="inproj",
    )(x.reshape(M, D), norm_mix_g, w_q_b, w_in_b, gmat2, cos_t, sin_t, qg, kg, lng, lnb,
      *[w for w, _ in to_cast])
    k = k.reshape(B, S, KV_DIM)
    vt = vt.reshape(B, S // TK, KV_DIM, TK)

    n_q = S // TQ

    def attention(fixed_shift):
        return pl.pallas_call(
            functools.partial(_attn_kernel, fixed_shift=fixed_shift),
            grid=(B, n_q),
            in_specs=[
                pl.BlockSpec(memory_space=pltpu.SMEM),
                pl.BlockSpec((None, N_Q_HEADS, LANES, TQ), lambda b, i: (b * n_q + i, 0, 0, 0)),
                pl.BlockSpec((None, S, KV_DIM), lambda b, i: (b, 0, 0)),
                pl.BlockSpec((None, S // TK, KV_DIM, TK), lambda b, i: (b, 0, 0, 0)),
            ],
            out_specs=pl.BlockSpec((None, Q_DIM, TQ), lambda b, i: (b * n_q + i, 0, 0)),
            out_shape=jax.ShapeDtypeStruct((B * n_q, Q_DIM, TQ), BF16),
            scratch_shapes=[
                pltpu.VMEM((N_Q_HEADS, TK, TQ), F32),
                pltpu.VMEM((N_Q_HEADS, TK, TQ), F32),
                pltpu.VMEM((N_Q_HEADS, 1, TQ), F32),
                pltpu.VMEM((N_Q_HEADS, SUBLANES, TQ), F32),
                pltpu.VMEM((N_Q_HEADS, HEAD_DIM, TQ), F32),
            ],
            compiler_params=cparams,
            name="gqa_attention_shift" if fixed_shift else "gqa_attention_online",
        )

    score_bound = (HEAD_DIM * QK_SCALE_LOG2 * (1.0 + 2.0 ** -6)
                   * jnp.max(jnp.abs(q_norm_g[0])) * jnp.max(jnp.abs(k_norm_g[0])))
    attn_t = lax.cond(
        score_bound <= MAX_FIXED_SHIFT,
        lambda s_, *ops: attention(True)(s_, *ops),
        lambda s_, *ops: attention(False)(s_, *ops),
        score_bound.reshape(1).astype(F32), qt, k, vt)

    bs_full = jnp.repeat(jnp.transpose(b_s[0]), GM_GROUP_DIM, axis=1)
    ws_pairs = (w_s[0].reshape(GM_GROUPS // 2, 2, GM_CHUNK, GM_CHUNK)
                .transpose(0, 2, 1, 3).reshape(GM_GROUPS // 2, GM_CHUNK, 2 * GM_CHUNK).astype(BF16))
    n_out = M // TM_OUT

    def s1_blk(i):
        return jnp.minimum(i, n_out - 1)

    def s1_rows(width):
        return pl.BlockSpec((TM_OUT, width), lambda i: (s1_blk(i), 0))

    out = pl.pallas_call(
        functools.partial(_merge_ffn_kernel, last_step=n_out),
        grid=(n_out + 1,),
        in_specs=[
            s1_rows(D),
            pl.BlockSpec((None, Q_DIM, TM_OUT), lambda i: (s1_blk(i), 0, 0)),
            s1_rows(GM_DIM), s1_rows(GM_DIM), s1_rows(2 * D),
            _const_spec((GM_GROUPS // 2, GM_CHUNK, 2 * GM_CHUNK)),
            _const_spec((GM_CHUNK, GM_DIM)),
            _const_spec((None, Q_DIM, D)),
            _const_spec((None, GM_DIM, D)),
            _const_spec((None, D, D)),
            _const_spec((1, D)),
            _const_spec((None, D, 2 * d_ff)),
            _const_spec((None, d_ff, D)),
            _const_spec((1, D)),
        ],
        out_specs=pl.BlockSpec((TM_OUT, D), lambda i: (jnp.maximum(i - 1, 0), 0)),
        out_shape=jax.ShapeDtypeStruct((M, D), F32),
        scratch_shapes=[
            pltpu.VMEM((TM_OUT, GM_DIM), BF16),
            pltpu.VMEM((TM_OUT, d_ff), BF16),
            pltpu.VMEM((TM_OUT, D), F32),
            pltpu.VMEM((TM_OUT, D), F32),
        ],
        compiler_params=pltpu.CompilerParams(dimension_semantics=("arbitrary",),
                                             vmem_limit_bytes=VMEM_LIMIT_BYTES),
        name="merge_ffn",
    )(x.reshape(M, D), attn_t, u, vn, gates, ws_pairs, bs_full,
      wpa_b, wpb_b, wo_b, norm_ffn_g, wgu_b, wd_b, norm_final_g[None, :])
    return out.reshape(B, S, D)
```
